```python
import math
import jax
import jax.numpy as jnp
from jax import lax
import numpy as np

D_MODEL = 1024
BATCH = 16
SEQ = 256
DEPTH = 2
DEC_BATCH = 2
DEC_SEQ = 1024
PAST_LEN = 512

GRID_W = 64
Q_BLOCK = 128
CHUNK = 32
ROPE_THETA = 10000.0
EPS = 1e-6
LB_FLOOR = 1e-30
N_MOD = 6
N_BRANCH = 4
BRANCH_W = D_MODEL // 4
A_HEADS = 4
A_KV_HEADS = 2
A_HEAD_DIM = BRANCH_W // A_HEADS
C_HEADS = 4
C_HEAD_DIM = BRANCH_W // (2 * C_HEADS)
B_HEADS = 4
B_KEY_DIM = BRANCH_W // B_HEADS
B_VAL_DIM = BRANCH_W // B_HEADS
D_HEADS = 4
D_KEY_DIM = BRANCH_W // (2 * D_HEADS)
D_VAL_DIM = BRANCH_W // D_HEADS
D_GATE_RANK = 16
D_GATE_TAU = 16.0
D_FF = 11 * D_MODEL // 4
CONV_WIDTH = 3
IN_SPLITS = (
    A_HEADS * A_HEAD_DIM, A_KV_HEADS * A_HEAD_DIM, A_KV_HEADS * A_HEAD_DIM,
    2 * C_HEADS * C_HEAD_DIM, 2 * C_HEADS * C_HEAD_DIM, 2 * C_HEADS * C_HEAD_DIM,
    B_HEADS * B_KEY_DIM, B_HEADS * B_VAL_DIM, B_HEADS * B_KEY_DIM, B_HEADS * B_KEY_DIM, B_HEADS * B_VAL_DIM,
    D_HEADS * D_KEY_DIM, D_HEADS * D_KEY_DIM, D_HEADS * D_VAL_DIM, D_HEADS * D_VAL_DIM, D_GATE_RANK, D_GATE_RANK,
    N_BRANCH * D_MODEL,
)
IN_WIDTH = sum(IN_SPLITS)

kernel_name = 'hybrid_diffusion_prefix_trunk_step'


def rms_norm(x, g):
    xf = x.astype(jnp.float32)
    y = xf * lax.rsqrt(jnp.mean(xf * xf, axis=-1, keepdims=True) + EPS)
    return (y * g.astype(jnp.float32)).astype(x.dtype)


def axial_rope_tables(n_tokens, head_dim):
    rows = n_tokens // GRID_W
    row = jnp.repeat(jnp.arange(rows), GRID_W).astype(jnp.float32)
    col = jnp.tile(jnp.arange(GRID_W), rows).astype(jnp.float32)
    half = head_dim // 2
    freqs = ROPE_THETA ** (-jnp.arange(0, half, 2, dtype=jnp.float32) / half)
    ang_r = row[:, None] * freqs
    ang_c = col[:, None] * freqs
    ang = jnp.concatenate([ang_r, ang_r, ang_c, ang_c], axis=-1)
    return jnp.cos(ang), jnp.sin(ang)


def apply_rope(x, cos, sin):
    q4 = x.shape[-1] // 4
    xs = x.reshape(x.shape[:-1] + (2, 2, q4))
    rot = jnp.concatenate([-xs[..., 1:2, :], xs[..., 0:1, :]], axis=-2).reshape(x.shape)
    y = x.astype(jnp.float32) * cos[:, None, :] + rot.astype(jnp.float32) * sin[:, None, :]
    return y.astype(x.dtype)


def attend(q, k, v, scale):
    b, tq = q.shape[:2]
    nb = tq // Q_BLOCK
    qb = jnp.moveaxis(q.reshape((b, nb, Q_BLOCK) + q.shape[2:]), 1, 0)
    kf = k.astype(jnp.float32)
    vf = v.astype(jnp.float32)

    def block(qblk):
        s = jnp.einsum('bqgrd,bkgd->bgrqk', qblk.astype(jnp.float32), kf) * scale
        p = jax.nn.softmax(s, axis=-1)
        return jnp.einsum('bgrqk,bkgv->bqgrv', p, vf)

    o = lax.map(block, qb)
    return jnp.moveaxis(o, 0, 1).reshape((b, tq) + o.shape[3:]).astype(q.dtype)


def chunked_gla(q, k, v, log_a, s0):
    bsz, t, h, dk = q.shape
    dv = v.shape[-1]
    n = t // CHUNK
    q, k, log_a = [a.astype(jnp.float32).reshape(bsz, n, CHUNK, h, dk) for a in (q, k, log_a)]
    v = v.astype(jnp.float32).reshape(bsz, n, CHUNK, h, dv)
    cum = jnp.cumsum(log_a, axis=2)
    cum_last = cum[:, :, -1]
    mask = jnp.tril(jnp.ones((CHUNK, CHUNK), dtype=bool))[:, :, None, None]
    diff = cum[:, :, :, None] - cum[:, :, None, :]
    decay = jnp.where(mask, jnp.exp(jnp.where(mask, diff, 0.0)), 0.0)
    scores = jnp.einsum('bnthk,bnshk,bntshk->bnhts', q, k, decay)
    o_intra = jnp.einsum('bnhts,bnshv->bnthv', scores, v)
    q_in = q * jnp.exp(cum)
    k_out = k * jnp.exp(cum_last[:, :, None] - cum)

    def step(s, xs):
        qi, ko, vc, cl = xs
        o = jnp.einsum('bthk,bhkv->bthv', qi, s)
        s = jnp.exp(cl)[..., None] * s + jnp.einsum('bthk,bthv->bhkv', ko, vc)
        return s, o

    xs = tuple(jnp.moveaxis(a, 1, 0) for a in (q_in, k_out, v, cum_last))
    s_final, o_inter = lax.scan(step, s0.astype(jnp.float32), xs)
    o = o_intra + jnp.moveaxis(o_inter, 0, 1)
    return o.reshape(bsz, t, h, dv), s_final


def bidir_gla(q, k_f, k_b, v, la_f, la_b, s0_f, s0_b):
    o_f, s_f = chunked_gla(q, k_f, v, la_f, s0_f)
    flip = lambda a: jnp.flip(a, axis=1)
    o_b, s_b = chunked_gla(flip(q), flip(k_b), flip(v), flip(la_b), s0_b)
    return o_f + flip(o_b), s_f, s_b


def hgrn_forget(z, lb):
    z = z.astype(jnp.float32)
    log_f = jnp.logaddexp(jnp.log(jnp.maximum(lb, LB_FLOOR)), jnp.log1p(-lb) + jax.nn.log_sigmoid(z))
    k = (1.0 - lb) * jax.nn.sigmoid(-z)
    return k, log_f


def conv_ffn(h, w_up, conv_w, conv_b, w_down):
    u = h @ w_up
    pad = CONV_WIDTH // 2
    t = u.shape[1]
    up = jnp.pad(u, ((0, 0), (pad, pad), (0, 0)))
    u = sum(up[:, j:j + t] * conv_w[j] for j in range(CONV_WIDTH)) + conv_b
    a, g = jnp.split(u, 2, axis=-1)
    return (jax.nn.silu(g) * a) @ w_down


def trunk_layer(x, cond, ctx, p, lam_init):
    bsz, t, _ = x.shape
    dt = x.dtype
    mod = jax.nn.silu(cond) @ p['w_ada'] + p['b_ada']
    sh1, sc1, g1, sh2, sc2, g2 = [m[:, None, :] for m in jnp.split(mod, N_MOD, axis=-1)]
    h = rms_norm(x, p['norm1_g']) * (1 + sc1) + sh1
    z = h @ p['w_in']
    (a_q, a_k, a_v, c_q, c_k, c_v, b_q, b_i, b_ff, b_fb, b_g,
     d_q, d_k, d_v, d_g, d_rf, d_rb, merge_logits) = jnp.split(z, np.cumsum(IN_SPLITS)[:-1].tolist(), axis=-1)

    aq = rms_norm(a_q.reshape(bsz, t, A_HEADS, A_HEAD_DIM), p['a_qn_g'])
    ak = rms_norm(a_k.reshape(bsz, t, A_KV_HEADS, A_HEAD_DIM), p['a_kn_g'])
    av = a_v.reshape(bsz, t, A_KV_HEADS, A_HEAD_DIM)
    cq = rms_norm(c_q.reshape(bsz, t, 2 * C_HEADS, C_HEAD_DIM), p['c_qn_g'])
    ck = rms_norm(c_k.reshape(bsz, t, 2 * C_HEADS, C_HEAD_DIM), p['c_kn_g'])
    cv = c_v.reshape(bsz, t, C_HEADS, 2 * C_HEAD_DIM)

    if ctx is None:
        ctx_out = [ak, av, ck.reshape(bsz, t, C_HEADS, 2, C_HEAD_DIM), cv]
        s_hgrn = jnp.zeros((bsz, 2, B_HEADS, B_KEY_DIM, B_VAL_DIM), jnp.float32)
        s_gla = jnp.zeros((bsz, 2, D_HEADS, D_KEY_DIM, D_VAL_DIM), jnp.float32)
    else:
        ctx_ak, ctx_av, ctx_ck, ctx_cv, s_hgrn, s_gla = ctx
        cos_a, sin_a = axial_rope_tables(t, A_HEAD_DIM)
        cos_c, sin_c = axial_rope_tables(t, C_HEAD_DIM)
        aq = apply_rope(aq, cos_a, sin_a)
        ak = jnp.concatenate([ctx_ak, apply_rope(ak, cos_a, sin_a)], axis=1)
        av = jnp.concatenate([ctx_av, av], axis=1)
        cq = apply_rope(cq, cos_c, sin_c)
        ck = jnp.concatenate([ctx_ck.reshape(bsz, -1, 2 * C_HEADS, C_HEAD_DIM), apply_rope(ck, cos_c, sin_c)], axis=1)
        cv = jnp.concatenate([ctx_cv, cv], axis=1)

    y_a = attend(aq.reshape(bsz, t, A_KV_HEADS, A_HEADS // A_KV_HEADS, A_HEAD_DIM), ak, av,
                 A_HEAD_DIM ** -0.5).reshape(bsz, t, BRANCH_W)

    cq = cq.reshape(bsz, t, C_HEADS, 2, C_HEAD_DIM)
    ck = ck.reshape(bsz, -1, C_HEADS, 2, C_HEAD_DIM)
    o1 = attend(cq[:, :, :, 0:1], ck[:, :, :, 0], cv, C_HEAD_DIM ** -0.5)
    o2 = attend(cq[:, :, :, 1:2], ck[:, :, :, 1], cv, C_HEAD_DIM ** -0.5)
    lq1, lk1, lq2, lk2 = p['c_lambda'].astype(jnp.float32)
    lam = jnp.exp(jnp.sum(lq1 * lk1)) - jnp.exp(jnp.sum(lq2 * lk2)) + lam_init
    y_c = (rms_norm((o1 - lam * o2)[:, :, :, 0], p['c_subln_g']) * (1.0 - lam_init)).reshape(bsz, t, BRANCH_W)

    bq = jax.nn.silu(b_q).reshape(bsz, t, B_HEADS, B_KEY_DIM)
    bi = b_i.reshape(bsz, t, B_HEADS, B_VAL_DIM)
    lb = p['b_lb'].reshape(2, B_HEADS, B_KEY_DIM)
    kb_f, lf_f = hgrn_forget(b_ff.reshape(bsz, t, B_HEADS, B_KEY_DIM), lb[0])
    kb_b, lf_b = hgrn_forget(b_fb.reshape(bsz, t, B_HEADS, B_KEY_DIM), lb[1])
    o_b, sb_f, sb_b = bidir_gla(bq, kb_f, kb_b, bi, lf_f, lf_b, s_hgrn[:, 0], s_hgrn[:, 1])
    y_b = (rms_norm(o_b, p['b_norm_g']) * jax.nn.silu(b_g.reshape(bsz, t, B_HEADS, B_VAL_DIM))).reshape(bsz, t, BRANCH_W)

    dq = d_q.reshape(bsz, t, D_HEADS, D_KEY_DIM) * D_KEY_DIM ** -0.5
    dk = d_k.reshape(bsz, t, D_HEADS, D_KEY_DIM)
    dv = d_v.reshape(bsz, t, D_HEADS, D_VAL_DIM)
    la_f, la_b = [(jax.nn.log_sigmoid((r @ p['d_alpha_w'][i] + p['d_alpha_b'][i]).astype(jnp.float32))
                   / D_GATE_TAU).reshape(bsz, t, D_HEADS, D_KEY_DIM) for i, r in enumerate((d_rf, d_rb))]
    o_d, sd_f, sd_b = bidir_gla(dq, dk, dk, dv, la_f, la_b, s_gla[:, 0], s_gla[:, 1])
    y_d = (rms_norm(o_d, p['d_norm_g']) * jax.nn.silu(d_g.reshape(bsz, t, D_HEADS, D_VAL_DIM))).reshape(bsz, t, BRANCH_W)

    branches = jnp.stack([y_a, y_b, y_c, y_d], axis=2).astype(dt)
    proj = jnp.einsum('btnw,nwd->btnd', branches, p['w_branch'])
    gates = jax.nn.sigmoid(merge_logits.reshape(bsz, t, N_BRANCH, D_MODEL))
    mixed = jnp.sum(gates * proj, axis=2) @ p['w_out']
    x = x + g1 * mixed

    h2 = rms_norm(x, p['norm2_g']) * (1 + sc2) + sh2
    x = x + g2 * conv_ffn(h2, p['w_up'], p['conv_w'], p['conv_b'], p['w_down'])
    if ctx is None:
        ctx_out = ctx_out + [jnp.stack([sb_f, sb_b], axis=1), jnp.stack([sd_f, sd_b], axis=1)]
        return x, ctx_out
    return x, None


def setup_inputs(seed: int = 0) -> dict:
    key = jax.random.key(seed)
    ks = iter(jax.random.split(key, 40))

    def nrm(shape, scale):
        return jax.random.normal(next(ks), shape, jnp.float32) * scale

    def gain(shape):
        return 1.0 + nrm(shape, 0.02)

    return {
        'x_prompt': nrm((BATCH, SEQ, D_MODEL), 1.0),
        'x_sample': nrm((DEC_BATCH, DEC_SEQ, D_MODEL), 1.0),
        'cache_a_k': nrm((DEC_BATCH, DEPTH, PAST_LEN, A_KV_HEADS, A_HEAD_DIM), 1.0),
        'cache_a_v': nrm((DEC_BATCH, DEPTH, PAST_LEN, A_KV_HEADS, A_HEAD_DIM), 1.0),
        'cache_c_k': nrm((DEC_BATCH, DEPTH, PAST_LEN, C_HEADS, 2, C_HEAD_DIM), 1.0),
        'cache_c_v': nrm((DEC_BATCH, DEPTH, PAST_LEN, C_HEADS, 2 * C_HEAD_DIM), 1.0),
        'state_hgrn': nrm((DEC_BATCH, DEPTH, 2, B_HEADS, B_KEY_DIM, B_VAL_DIM), 0.5),
        'state_gla': nrm((DEC_BATCH, DEPTH, 2, D_HEADS, D_KEY_DIM, D_VAL_DIM), 0.5),
        'c': nrm((DEC_BATCH, D_MODEL), 1.0),
        'c_ctx': nrm((D_MODEL,), 1.0),
        'w_ada': nrm((DEPTH, D_MODEL, N_MOD * D_MODEL), 0.5 * D_MODEL ** -0.5),
        'b_ada': nrm((DEPTH, N_MOD * D_MODEL), 0.02),
        'norm1_g': gain((DEPTH, D_MODEL)),
        'norm2_g': gain((DEPTH, D_MODEL)),
        'w_in': nrm((DEPTH, D_MODEL, IN_WIDTH), D_MODEL ** -0.5),
        'a_qn_g': gain((DEPTH, A_HEAD_DIM)),
        'a_kn_g': gain((DEPTH, A_HEAD_DIM)),
        'c_qn_g': gain((DEPTH, C_HEAD_DIM)),
        'c_kn_g': gain((DEPTH, C_HEAD_DIM)),
        'c_lambda': nrm((DEPTH, 4, C_HEAD_DIM), 0.1),
        'c_subln_g': gain((DEPTH, 2 * C_HEAD_DIM)),
        'b_lb_logits': nrm((2, DEPTH, B_HEADS * B_KEY_DIM), 0.5),
        'b_norm_g': gain((DEPTH, B_VAL_DIM)),
        'd_alpha_w': nrm((DEPTH, 2, D_GATE_RANK, D_HEADS * D_KEY_DIM), D_GATE_RANK ** -0.5),
        'd_alpha_b': nrm((DEPTH, 2, D_HEADS * D_KEY_DIM), 0.1),
        'd_norm_g': gain((DEPTH, D_VAL_DIM)),
        'w_branch': nrm((DEPTH, N_BRANCH, BRANCH_W, D_MODEL), BRANCH_W ** -0.5),
        'w_out': nrm((DEPTH, D_MODEL, D_MODEL), D_MODEL ** -0.5),
        'w_up': nrm((DEPTH, D_MODEL, 2 * D_FF), D_MODEL ** -0.5),
        'conv_w': nrm((DEPTH, CONV_WIDTH, 2 * D_FF), CONV_WIDTH ** -0.5),
        'conv_b': nrm((DEPTH, 2 * D_FF), 0.01),
        'w_down': nrm((DEPTH, D_FF, D_MODEL), D_FF ** -0.5),
    }


def reference(x_prompt, x_sample, cache_a_k, cache_a_v, cache_c_k, cache_c_v, state_hgrn, state_gla,
              c, c_ctx, w_ada, b_ada, norm1_g, norm2_g, w_in, a_qn_g, a_kn_g, c_qn_g, c_kn_g,
              c_lambda, c_subln_g, b_lb_logits, b_norm_g, d_alpha_w, d_alpha_b, d_norm_g,
              w_branch, w_out, w_up, conv_w, conv_b, w_down):
    lb_p = jax.nn.softmax(b_lb_logits.astype(jnp.float32), axis=1)
    lb_all = jnp.cumsum(lb_p, axis=1) - lb_p[:, :1]
    y_prompt = x_prompt
    y_sample = x_sample
    collected = [[] for _ in range(6)]
    for l in range(DEPTH):
        p = {
            'w_ada': w_ada[l], 'b_ada': b_ada[l], 'norm1_g': norm1_g[l], 'norm2_g': norm2_g[l],
            'w_in': w_in[l], 'a_qn_g': a_qn_g[l], 'a_kn_g': a_kn_g[l], 'c_qn_g': c_qn_g[l],
            'c_kn_g': c_kn_g[l], 'c_lambda': c_lambda[l], 'c_subln_g': c_subln_g[l],
            'b_lb': lb_all[:, l], 'b_norm_g': b_norm_g[l], 'd_alpha_w': d_alpha_w[l],
            'd_alpha_b': d_alpha_b[l], 'd_norm_g': d_norm_g[l], 'w_branch': w_branch[l],
            'w_out': w_out[l], 'w_up': w_up[l], 'conv_w': conv_w[l], 'conv_b': conv_b[l],
            'w_down': w_down[l],
        }
        lam_init = 0.8 - 0.6 * math.exp(-0.3 * l)
        y_prompt, ctx_l = trunk_layer(y_prompt, c_ctx[None, :], None, p, lam_init)
        cached = (cache_a_k[:, l], cache_a_v[:, l], cache_c_k[:, l], cache_c_v[:, l],
                  state_hgrn[:, l], state_gla[:, l])
        y_sample, _ = trunk_layer(y_sample, c, cached, p, lam_init)
        for acc, arr in zip(collected, ctx_l):
            acc.append(arr)
    new_cache_a_k, new_cache_a_v, new_cache_c_k, new_cache_c_v, new_state_hgrn, new_state_gla = [
        jnp.stack(acc, axis=1) for acc in collected]
    return (y_prompt, y_sample, new_cache_a_k, new_cache_a_v, new_cache_c_k, new_cache_c_v, new_state_hgrn, new_state_gla)
```

```python
import functools
import math

import numpy as np
import jax
import jax.numpy as jnp
from jax import lax
from jax.experimental import pallas as pl
from jax.experimental.pallas import tpu as pltpu

F32 = jnp.float32
BF = jnp.bfloat16

D_MODEL = 1024
BATCH = 16
SEQ = 256
DEPTH = 2
DEC_BATCH = 2
DEC_SEQ = 1024
PAST_LEN = 512
GRID_W = 64
ROPE_THETA = 10000.0
EPS = 1e-6
LB_FLOOR = 1e-30
N_MOD = 6
N_BRANCH = 4
BRANCH_W = 256
A_HEADS, A_KV_HEADS, A_HEAD_DIM = 4, 2, 64
C_HEADS, C_HEAD_DIM = 4, 32
B_HEADS, B_KEY_DIM, B_VAL_DIM = 4, 64, 64
D_HEADS, D_KEY_DIM, D_VAL_DIM = 4, 32, 64
D_GATE_RANK = 16
D_GATE_TAU = 16.0
D_FF = 2816
CONV_WIDTH = 3

N_PROMPT = BATCH * SEQ
N_SAMPLE = DEC_BATCH * DEC_SEQ
N_TOK = N_PROMPT + N_SAMPLE
TM = 256
N_TILES = N_TOK // TM
PROMPT_TILES = N_PROMPT // TM
SAMPLE_TILES_PER_SEQ = DEC_SEQ // TM
N_GROUPS = 1 + DEC_BATCH

AC_W = 1280
BD_W = 2080
BD_PAD = 2176
SMALL_W = AC_W + BD_PAD
MERGE_OFF = AC_W + BD_W
HALO = 8
FF_CHUNK = 256
VMEM_LIMIT = 56 * 1024 * 1024
LEVELS = (1, 2, 4, 8, 16, 32, 64, 128)


def _dot(a, b):
    return jnp.dot(a, b, preferred_element_type=F32)


def _dot_nt(a, b):
    return lax.dot_general(a, b, (((1,), (1,)), ((), ())), preferred_element_type=F32)


def _dot_tn(a, b):
    return lax.dot_general(a, b, (((0,), (0,)), ((), ())), preferred_element_type=F32)


def _silu(x):
    return x * jax.nn.sigmoid(x)


def _log_sigmoid(x):
    return jnp.minimum(x, 0.0) - jnp.log1p(jnp.exp(-jnp.abs(x)))


def _rms(x, g):
    return x * lax.rsqrt(jnp.mean(x * x, axis=-1, keepdims=True) + EPS) * g


def _head_rms(x, head_dim, g):
    w = x.shape[-1]
    sh = int(math.log2(head_dim))
    r = lax.shift_right_logical(lax.broadcasted_iota(jnp.int32, (w, w), 0), sh)
    c = lax.shift_right_logical(lax.broadcasted_iota(jnp.int32, (w, w), 1), sh)
    bd = jnp.where(r == c, 1.0 / head_dim, 0.0).astype(BF)
    x2 = x * x
    hi = x2.astype(BF)
    lo = (x2 - hi.astype(F32)).astype(BF)
    ms = _dot(hi, bd) + _dot(lo, bd)
    return x * lax.rsqrt(ms + EPS) * g


def _group_of_tile(i):
    return jnp.where(i < PROMPT_TILES, 0, 1 + jnp.maximum(i - PROMPT_TILES, 0) // SAMPLE_TILES_PER_SEQ)


def _params(sem):
    return pltpu.CompilerParams(dimension_semantics=sem, vmem_limit_bytes=VMEM_LIMIT)


def _mod_kernel(cond_ref, w_ref, b_ref, o_ref):
    s = _silu(cond_ref[...])
    o_ref[...] = _dot(s.astype(BF), w_ref[...].astype(BF)) + b_ref[...]


def _modulation(cond8, w_ada, b_ada):
    nb = 1536
    return pl.pallas_call(
        _mod_kernel,
        grid=(DEPTH, N_MOD * D_MODEL // nb),
        in_specs=[
            pl.BlockSpec((8, D_MODEL), lambda l, j: (0, 0)),
            pl.BlockSpec((None, D_MODEL, nb), lambda l, j: (l, 0, j)),
            pl.BlockSpec((None, 1, nb), lambda l, j: (l, 0, j)),
        ],
        out_specs=pl.BlockSpec((None, 8, nb), lambda l, j: (l, 0, j)),
        out_shape=jax.ShapeDtypeStruct((DEPTH, 8, N_MOD * D_MODEL), F32),
        compiler_params=_params(("arbitrary", "arbitrary")),
        name="modulation",
    )(cond8, w_ada, b_ada.reshape(DEPTH, 1, N_MOD * D_MODEL))


def _inproj_kernel(x_ref, mod_ref, g_ref, w_ref, zac_ref, zbd_ref, h_ref):
    m = mod_ref[...]
    h = _rms(x_ref[...], g_ref[...]) * (1.0 + m[1:2]) + m[0:1]
    hb = h.astype(BF)
    h_ref[...] = hb
    zac_ref[...] = _dot(hb, w_ref[:, :AC_W])
    zbd_ref[...] = _dot(hb, w_ref[:, AC_W:])


def _inproj(l, x, mod, norm_g, w_small):
    return pl.pallas_call(
        _inproj_kernel,
        grid=(N_TILES,),
        in_specs=[
            pl.BlockSpec((TM, D_MODEL), lambda i: (i, 0)),
            pl.BlockSpec((None, None, N_MOD, D_MODEL), lambda i: (l, _group_of_tile(i), 0, 0)),
            pl.BlockSpec((None, 1, D_MODEL), lambda i: (l, 0, 0)),
            pl.BlockSpec((None, D_MODEL, SMALL_W), lambda i: (l, 0, 0)),
        ],
        out_specs=[
            pl.BlockSpec((TM, AC_W), lambda i: (i, 0)),
            pl.BlockSpec((TM, BD_PAD), lambda i: (i, 0)),
            pl.BlockSpec((TM, D_MODEL), lambda i: (i, 0)),
        ],
        out_shape=[
            jax.ShapeDtypeStruct((N_TOK, AC_W), F32),
            jax.ShapeDtypeStruct((N_TOK, BD_PAD), F32),
            jax.ShapeDtypeStruct((N_TOK, D_MODEL), BF),
        ],
        compiler_params=_params(("arbitrary",)),
        name=f"inproj{l}",
    )(x, mod, norm_g, w_small)


def _rope_tables(n_tokens, head_dim, n_rep):
    rows = n_tokens // GRID_W
    row = np.repeat(np.arange(rows), GRID_W).astype(np.float64)
    col = np.tile(np.arange(GRID_W), rows).astype(np.float64)
    half = head_dim // 2
    q4 = head_dim // 4
    freqs = ROPE_THETA ** (-np.arange(0, half, 2, dtype=np.float64) / half)
    ang_r = row[:, None] * freqs
    ang_c = col[:, None] * freqs
    ang = np.concatenate([ang_r, ang_r, ang_c, ang_c], axis=-1)
    cos, sin = np.cos(ang), np.sin(ang)
    first = (np.arange(head_dim) % (2 * q4)) < q4
    s_dn = np.where(first, -sin, 0.0)
    s_up = np.where(first, 0.0, sin)
    return tuple(jnp.asarray(np.tile(t, (1, n_rep)), dtype=F32) for t in (cos, s_dn, s_up))


def _rope(x, cos, s_dn, s_up, q4):
    w = x.shape[-1]
    return x * cos + pltpu.roll(x, w - q4, 1) * s_dn + pltpu.roll(x, q4, 1) * s_up


def _softmax_pv(q, ks, vs, scale):
    ss = [_dot_nt(q, k) * scale for k in ks]
    m = ss[0].max(axis=-1, keepdims=True)
    for s in ss[1:]:
        m = jnp.maximum(m, s.max(axis=-1, keepdims=True))
    ps = [jnp.exp(s - m) for s in ss]
    den = ps[0].sum(axis=-1, keepdims=True)
    for p in ps[1:]:
        den = den + p.sum(axis=-1, keepdims=True)
    o = _dot(ps[0].astype(BF), vs[0])
    for p, v in zip(ps[1:], vs[1:]):
        o = o + _dot(p.astype(BF), v)
    return o / den


def _attend_heads(aq, cq, ka, va, kc, vc, lam, gsub, lam_init, y_ref):
    aqb, cqb = aq.astype(BF), cq.astype(BF)
    rep = A_HEADS // A_KV_HEADS
    for h in range(A_HEADS):
        g = h // rep
        sl = slice(g * A_HEAD_DIM, (g + 1) * A_HEAD_DIM)
        o = _softmax_pv(aqb[:, h * A_HEAD_DIM:(h + 1) * A_HEAD_DIM],
                        [k[:, sl] for k in ka], [v[:, sl] for v in va], A_HEAD_DIM ** -0.5)
        y_ref[:, h * A_HEAD_DIM:(h + 1) * A_HEAD_DIM] = o
    vd = 2 * C_HEAD_DIM
    for h in range(C_HEADS):
        vsl = slice(h * vd, (h + 1) * vd)
        os_ = []
        for j in range(2):
            sl = slice((2 * h + j) * C_HEAD_DIM, (2 * h + j + 1) * C_HEAD_DIM)
            os_.append(_softmax_pv(cqb[:, sl], [k[:, sl] for k in kc], [v[:, vsl] for v in vc], C_HEAD_DIM ** -0.5))
        d = os_[0] - lam * os_[1]
        y = _rms(d, gsub) * (1.0 - lam_init)
        y_ref[:, BRANCH_W + h * vd:BRANCH_W + (h + 1) * vd] = y


def _lambda(cl):
    s1 = jnp.sum(cl[0:1] * cl[1:2], axis=-1, keepdims=True)
    s2 = jnp.sum(cl[2:3] * cl[3:4], axis=-1, keepdims=True)
    return jnp.exp(s1) - jnp.exp(s2)


def _attn_prompt_kernel(z_ref, gaq, gak, gcq, gck, gsub, cl_ref, y_ref, oak, oav, ock, ocv, *, lam_init):
    z = z_ref[...]
    ak = _head_rms(z[:, 256:384], A_HEAD_DIM, gak[...])
    av = z[:, 384:512]
    ck = _head_rms(z[:, 768:1024], C_HEAD_DIM, gck[...])
    cv = z[:, 1024:1280]
    oak[...] = ak
    oav[...] = av
    ock[...] = ck
    ocv[...] = cv
    aq = _head_rms(z[:, 0:256], A_HEAD_DIM, gaq[...])
    cq = _head_rms(z[:, 512:768], C_HEAD_DIM, gcq[...])
    lam = _lambda(cl_ref[...]) + lam_init
    _attend_heads(aq, cq, [ak.astype(BF)], [av.astype(BF)], [ck.astype(BF)], [cv.astype(BF)],
                  lam, gsub[...], lam_init, y_ref)


def _attn_sample_kernel(z_ref, gaq, gak, gcq, gck, gsub, cl_ref, cak, cav, cck, ccv,
                        cosa, sda, sua, cosc, sdc, suc, y_ref, ka_s, va_s, kc_s, vc_s, *, lam_init):
    qi = pl.program_id(1)
    qa4, qc4 = A_HEAD_DIM // 4, C_HEAD_DIM // 4

    @pl.when(qi == 0)
    def _():
        ak = _head_rms(z_ref[:, 256:384], A_HEAD_DIM, gak[...])
        ka_s[...] = _rope(ak, cosa[:, :128], sda[:, :128], sua[:, :128], qa4).astype(BF)
        va_s[...] = z_ref[:, 384:512].astype(BF)
        ck = _head_rms(z_ref[:, 768:1024], C_HEAD_DIM, gck[...])
        kc_s[...] = _rope(ck, cosc[...], sdc[...], suc[...], qc4).astype(BF)
        vc_s[...] = z_ref[:, 1024:1280].astype(BF)

    rows = pl.ds(pl.multiple_of(qi * TM, TM), TM)
    aq = _head_rms(z_ref[rows, 0:256], A_HEAD_DIM, gaq[...])
    aq = _rope(aq, cosa[rows, :], sda[rows, :], sua[rows, :], qa4)
    cq = _head_rms(z_ref[rows, 512:768], C_HEAD_DIM, gcq[...])
    cq = _rope(cq, cosc[rows, :], sdc[rows, :], suc[rows, :], qc4)
    lam = _lambda(cl_ref[...]) + lam_init
    _attend_heads(aq, cq,
                  [cak[...].astype(BF), ka_s[...]], [cav[...].astype(BF), va_s[...]],
                  [cck[...].astype(BF), kc_s[...]], [ccv[...].astype(BF), vc_s[...]],
                  lam, gsub[...], lam_init, y_ref)


def _gain_specs(l, nd):
    zeros = (0,) * (nd - 1)
    widths = (256, 128, 256, 256, 64)
    return [pl.BlockSpec((None, 1, w), lambda *a: (l, 0, 0)) for w in widths] + \
           [pl.BlockSpec((None, 4, C_HEAD_DIM), lambda *a: (l, 0, 0))]


def _attn_prompt(l, zac, gains, c_lambda, lam_init):
    out_w = (2 * BRANCH_W, 128, 128, 256, 256)
    return pl.pallas_call(
        functools.partial(_attn_prompt_kernel, lam_init=lam_init),
        grid=(BATCH,),
        in_specs=[pl.BlockSpec((SEQ, AC_W), lambda b: (b, 0))] + _gain_specs(l, 1),
        out_specs=[pl.BlockSpec((SEQ, w), lambda b: (b, 0)) for w in out_w],
        out_shape=[jax.ShapeDtypeStruct((N_PROMPT, w), F32) for w in out_w],
        compiler_params=_params(("arbitrary",)),
        name=f"attn_prompt{l}",
    )(zac, *gains, c_lambda)


def _attn_sample(l, zac, gains, c_lambda, caches, tables, lam_init):
    first_blk = N_PROMPT // DEC_SEQ
    cache_specs = [pl.BlockSpec((None, None, PAST_LEN, w), lambda b, q: (b, l, 0, 0)) for w in (128, 128, 256, 256)]
    table_specs = [pl.BlockSpec((DEC_SEQ, 256), lambda b, q: (0, 0)) for _ in range(6)]
    return pl.pallas_call(
        functools.partial(_attn_sample_kernel, lam_init=lam_init),
        grid=(DEC_BATCH, DEC_SEQ // TM),
        in_specs=[pl.BlockSpec((DEC_SEQ, AC_W), lambda b, q: (first_blk + b, 0))] + _gain_specs(l, 2)
                 + cache_specs + table_specs,
        out_specs=pl.BlockSpec((TM, 2 * BRANCH_W), lambda b, q: (b * (DEC_SEQ // TM) + q, 0)),
        out_shape=jax.ShapeDtypeStruct((N_SAMPLE, 2 * BRANCH_W), F32),
        scratch_shapes=[pltpu.VMEM((DEC_SEQ, 128), BF), pltpu.VMEM((DEC_SEQ, 128), BF),
                        pltpu.VMEM((DEC_SEQ, 256), BF), pltpu.VMEM((DEC_SEQ, 256), BF)],
        compiler_params=_params(("arbitrary", "arbitrary")),
        name=f"attn_sample{l}",
    )(zac, *gains, c_lambda, *caches, *tables)


def _level_matrix(tt, rev):
    et = lax.broadcasted_iota(jnp.int32, (tt, tt), 0)
    es = lax.broadcasted_iota(jnp.int32, (tt, tt), 1)
    x = et ^ es
    lv = jnp.zeros((tt, tt), jnp.int32)
    for j in range(1, len(LEVELS)):
        lv = lv + (x >= (1 << j)).astype(jnp.int32)
    after = (et < es) if rev else (et > es)
    return jnp.where(after, lv, jnp.where(et == es, -1, -2))


def _gla_tile(q, k, v, la, lvl, st_ref, d, rev, nh, kd, vd, use_state, o_ref, rows, col0, accumulate):
    tt, w = q.shape
    e = lax.broadcasted_iota(jnp.int32, (tt, w), 0)
    if rev:
        e = (tt - 1) - e

    def prv(y, s):
        return pltpu.roll(y, (tt - s) if rev else s, 0)

    def nxt(y, s):
        return pltpu.roll(y, s if rev else tt - s, 0)

    def seg_prefix(x, b):
        y, s = x, 1
        while s < b:
            y = y + jnp.where((e & (b - 1)) >= s, prv(y, s), 0.0)
            s *= 2
        return y

    def seg_suffix(x, b):
        z, s = x, 1
        while s < b:
            z = z + jnp.where((e & (b - 1)) + s < b, nxt(z, s), 0.0)
            s *= 2
        return z

    la_n = nxt(la, 1)
    qs, ks = [q.astype(BF)], [k.astype(BF)]
    for b in LEVELS:
        h_b = seg_suffix(la_n, b)
        if b > 1:
            g_b = seg_prefix(jnp.where((e & (b - 1)) == 0, 0.0, la), b)
            qs.append((q * jnp.exp(g_b)).astype(BF))
        else:
            qs.append(qs[0])
        ks.append((k * jnp.exp(h_b)).astype(BF))
    cum = seg_prefix(la, tt)
    rem = seg_suffix(jnp.where(e == tt - 1, 0.0, la_n), tt)
    k_out = (k * jnp.exp(rem)).astype(BF)
    vb = v.astype(BF)
    if use_state:
        q_in = (q * jnp.exp(cum)).astype(BF)
        d_last = jnp.exp(cum[0:1] if rev else cum[tt - 1:tt])

    for h in range(nh):
        ksl = slice(h * kd, (h + 1) * kd)
        vsl = slice(h * vd, (h + 1) * vd)
        sc = jnp.where(lvl == -1, _dot_nt(qs[0][:, ksl], ks[0][:, ksl]), 0.0)
        for j in range(len(LEVELS)):
            sc = jnp.where(lvl == j, _dot_nt(qs[j + 1][:, ksl], ks[j + 1][:, ksl]), sc)
        o = _dot(sc.astype(BF), vb[:, vsl])
        kv = _dot_tn(vb[:, vsl], k_out[:, ksl])
        if use_state:
            st = st_ref[d, h]
            o = o + _dot_nt(q_in[:, ksl], st.astype(BF))
            st_ref[d, h] = st * d_last[:, ksl] + kv
        else:
            st_ref[d, h] = kv
        osl = slice(col0 + h * vd, col0 + (h + 1) * vd)
        if accumulate:
            o_ref[rows, osl] += o
        else:
            o_ref[rows, osl] = o


def _gla_kernel(*refs, layer, n_tiles, has_state):
    if has_state:
        (z_ref, lbl_ref, aw_ref, ab_ref, bng, dng, sh_in, sd_in, y_ref, o_scr, sth, std) = refs
        sth[...] = sh_in[...]
        std[...] = sd_in[...]
    else:
        (z_ref, lbl_ref, aw_ref, ab_ref, bng, dng, y_ref, sh_out, sd_out, o_scr, sth, std) = refs

    for d in range(2):
        rev = d == 1
        logits = [lbl_ref[d, i:i + 1, :] for i in range(DEPTH)]
        mx = functools.reduce(jnp.maximum, logits)
        ex = [jnp.exp(t - mx) for t in logits]
        den = functools.reduce(lambda a, b: a + b, ex)
        ps = [t / den for t in ex]
        lb = functools.reduce(lambda a, b: a + b, ps[:layer + 1]) - ps[0]
        log_lb = jnp.log(jnp.maximum(lb, LB_FLOOR))
        log_1m = jnp.log1p(-lb)
        lvl = _level_matrix(TM, rev)

        def tile(i, carry, d=d, rev=rev, lb=lb, log_lb=log_lb, log_1m=log_1m, lvl=lvl):
            j = (n_tiles - 1 - i) if rev else i
            rows = pl.ds(pl.multiple_of(j * TM, TM), TM)
            bq = z_ref[rows, 0:256]
            zf = z_ref[rows, 768:1024] if rev else z_ref[rows, 512:768]
            b2 = log_1m + _log_sigmoid(zf)
            la = jnp.maximum(log_lb, b2) + jnp.log1p(jnp.exp(-jnp.abs(log_lb - b2)))
            kb = (1.0 - lb) * jax.nn.sigmoid(-zf)
            _gla_tile(_silu(bq), kb, z_ref[rows, 256:512], la, lvl, sth, d, rev,
                      B_HEADS, B_KEY_DIM, B_VAL_DIM, has_state, o_scr, rows, 0, rev)
            pre = _dot(z_ref[rows, 2048:2176].astype(BF), aw_ref[d]) + ab_ref[d]
            la_d = _log_sigmoid(pre) / D_GATE_TAU
            _gla_tile(z_ref[rows, 1280:1408] * (D_KEY_DIM ** -0.5), z_ref[rows, 1408:1536], z_ref[rows, 1536:1792],
                      la_d, lvl, std, d, rev, D_HEADS, D_KEY_DIM, D_VAL_DIM, has_state, o_scr, rows, 256, rev)
            return carry

        lax.fori_loop(0, n_tiles, tile, 0)

    def finish(i, carry):
        rows = pl.ds(pl.multiple_of(i * TM, TM), TM)
        y_ref[rows, 0:256] = _head_rms(o_scr[rows, 0:256], B_VAL_DIM, bng[...]) * _silu(z_ref[rows, 1024:1280])
        y_ref[rows, 256:512] = _head_rms(o_scr[rows, 256:512], D_VAL_DIM, dng[...]) * _silu(z_ref[rows, 1792:2048])
        return carry

    lax.fori_loop(0, n_tiles, finish, 0)
    if not has_state:
        sh_out[...] = sth[...]
        sd_out[...] = std[...]


def _gla_common_specs(l, nd):
    return [
        pl.BlockSpec((2, DEPTH, 256), lambda *a: (0, 0, 0)),
        pl.BlockSpec((None, 2, 128, 128), lambda *a: (l, 0, 0, 0)),
        pl.BlockSpec((None, 2, 1, 128), lambda *a: (l, 0, 0, 0)),
        pl.BlockSpec((None, 1, 256), lambda *a: (l, 0, 0)),
        pl.BlockSpec((None, 1, 256), lambda *a: (l, 0, 0)),
    ]


_STATE_SCRATCH = [pltpu.VMEM((2, B_HEADS, B_VAL_DIM, B_KEY_DIM), F32), pltpu.VMEM((2, D_HEADS, D_VAL_DIM, D_KEY_DIM), F32)]


def _gla_prompt(l, zbd, small):
    return pl.pallas_call(
        functools.partial(_gla_kernel, layer=l, n_tiles=1, has_state=False),
        grid=(BATCH,),
        in_specs=[pl.BlockSpec((SEQ, BD_PAD), lambda b: (b, 0))] + _gla_common_specs(l, 1),
        out_specs=[
            pl.BlockSpec((SEQ, 2 * BRANCH_W), lambda b: (b, 0)),
            pl.BlockSpec((None, 2, B_HEADS, B_VAL_DIM, B_KEY_DIM), lambda b: (b, 0, 0, 0, 0)),
            pl.BlockSpec((None, 2, D_HEADS, D_VAL_DIM, D_KEY_DIM), lambda b: (b, 0, 0, 0, 0)),
        ],
        out_shape=[
            jax.ShapeDtypeStruct((N_PROMPT, 2 * BRANCH_W), F32),
            jax.ShapeDtypeStruct((BATCH, 2, B_HEADS, B_VAL_DIM, B_KEY_DIM), F32),
            jax.ShapeDtypeStruct((BATCH, 2, D_HEADS, D_VAL_DIM, D_KEY_DIM), F32),
        ],
        scratch_shapes=[pltpu.VMEM((SEQ, 2 * BRANCH_W), F32)] + _STATE_SCRATCH,
        compiler_params=_params(("arbitrary",)),
        name=f"gla_prompt{l}",
    )(zbd, *small)


def _gla_sample(l, zbd, small, st_h, st_d):
    first_blk = N_PROMPT // DEC_SEQ
    return pl.pallas_call(
        functools.partial(_gla_kernel, layer=l, n_tiles=DEC_SEQ // TM, has_state=True),
        grid=(DEC_BATCH,),
        in_specs=[pl.BlockSpec((DEC_SEQ, BD_PAD), lambda b: (first_blk + b, 0))] + _gla_common_specs(l, 1) + [
            pl.BlockSpec((None, None, 2, B_HEADS, B_VAL_DIM, B_KEY_DIM), lambda b: (b, l, 0, 0, 0, 0)),
            pl.BlockSpec((None, None, 2, D_HEADS, D_VAL_DIM, D_KEY_DIM), lambda b: (b, l, 0, 0, 0, 0)),
        ],
        out_specs=pl.BlockSpec((DEC_SEQ, 2 * BRANCH_W), lambda b: (b, 0)),
        out_shape=jax.ShapeDtypeStruct((N_SAMPLE, 2 * BRANCH_W), F32),
        scratch_shapes=[pltpu.VMEM((DEC_SEQ, 2 * BRANCH_W), F32)] + _STATE_SCRATCH,
        compiler_params=_params(("arbitrary",)),
        name=f"gla_sample{l}",
    )(zbd, *small, st_h, st_d)


def _mix_kernel(x_ref, h_ref, yac_ref, ybd_ref, mod_ref, wm_ref, wb_ref, wo_ref, o_ref):
    hb = h_ref[...]
    branches = (yac_ref[:, :BRANCH_W], ybd_ref[:, :BRANCH_W], yac_ref[:, BRANCH_W:], ybd_ref[:, BRANCH_W:])
    mixed = None
    for n, y in enumerate(branches):
        logits = _dot(hb, wm_ref[:, n * D_MODEL:(n + 1) * D_MODEL])
        term = jax.nn.sigmoid(logits) * _dot(y.astype(BF), wb_ref[n])
        mixed = term if mixed is None else mixed + term
    o_ref[...] = x_ref[...] + mod_ref[2:3, :] * _dot(mixed.astype(BF), wo_ref[...])


def _mix(l, x, h, yac, ybd, mod, w_merge, w_branch, w_out):
    tok = lambda w: pl.BlockSpec((TM, w), lambda i: (i, 0))
    return pl.pallas_call(
        _mix_kernel,
        grid=(N_TILES,),
        in_specs=[
            tok(D_MODEL), tok(D_MODEL), tok(2 * BRANCH_W), tok(2 * BRANCH_W),
            pl.BlockSpec((None, None, N_MOD, D_MODEL), lambda i: (l, _group_of_tile(i), 0, 0)),
            pl.BlockSpec((None, D_MODEL, N_BRANCH * D_MODEL), lambda i: (l, 0, 0)),
            pl.BlockSpec((None, N_BRANCH, BRANCH_W, D_MODEL), lambda i: (l, 0, 0, 0)),
            pl.BlockSpec((None, D_MODEL, D_MODEL), lambda i: (l, 0, 0)),
        ],
        out_specs=tok(D_MODEL),
        out_shape=jax.ShapeDtypeStruct((N_TOK, D_MODEL), F32),
        compiler_params=_params(("arbitrary",)),
        name=f"mix{l}",
    )(x, h, yac, ybd, mod, w_merge, w_branch, w_out)


def _ffn_kernel(xp_ref, x_ref, xn_ref, mod_ref, g_ref, wup_ref, cw_ref, cb_ref, wdn_ref, o_ref, hext):
    i = pl.program_id(0)
    pos = jnp.maximum(i - PROMPT_TILES, 0) % SAMPLE_TILES_PER_SEQ
    seq_first = (i < PROMPT_TILES) | (pos == 0)
    seq_last = (i < PROMPT_TILES) | (pos == SAMPLE_TILES_PER_SEQ - 1)
    m = mod_ref[...]
    g = g_ref[...]

    def pre(x):
        return _rms(x, g) * (1.0 + m[4:5]) + m[3:4]

    hext[0:HALO, :] = jnp.where(seq_first, 0.0, pre(xp_ref[...]))
    hext[HALO:HALO + TM, :] = pre(x_ref[...])
    hext[HALO + TM:, :] = jnp.where(seq_last, 0.0, pre(xn_ref[...]))
    hb = hext[...].astype(BF)
    ext = TM + 2 * HALO
    acc = None
    for c in range(D_FF // FF_CHUNK):
        halves = []
        for off in (0, D_FF):
            cols = slice(off + c * FF_CHUNK, off + (c + 1) * FF_CHUNK)
            u = _dot(hb, wup_ref[:, cols])
            u_prev = pltpu.roll(u, 1, 0)[HALO:HALO + TM]
            u_next = pltpu.roll(u, ext - 1, 0)[HALO:HALO + TM]
            halves.append(u_prev * cw_ref[0:1, cols] + u[HALO:HALO + TM] * cw_ref[1:2, cols]
                          + u_next * cw_ref[2:3, cols] + cb_ref[:, cols])
        act = (_silu(halves[1]) * halves[0]).astype(BF)
        t = _dot(act, wdn_ref[c * FF_CHUNK:(c + 1) * FF_CHUNK, :])
        acc = t if acc is None else acc + t
    o_ref[...] = x_ref[...] + m[5:6] * acc


def _ffn(l, x, mod, norm_g, w_up, conv_w, conv_b, w_down):
    per = TM // HALO
    last_blk = N_TOK // HALO - 1
    return pl.pallas_call(
        _ffn_kernel,
        grid=(N_TILES,),
        in_specs=[
            pl.BlockSpec((HALO, D_MODEL), lambda i: (jnp.maximum(i * per - 1, 0), 0)),
            pl.BlockSpec((TM, D_MODEL), lambda i: (i, 0)),
            pl.BlockSpec((HALO, D_MODEL), lambda i: (jnp.minimum((i + 1) * per, last_blk), 0)),
            pl.BlockSpec((None, None, N_MOD, D_MODEL), lambda i: (l, _group_of_tile(i), 0, 0)),
            pl.BlockSpec((None, 1, D_MODEL), lambda i: (l, 0, 0)),
            pl.BlockSpec((None, D_MODEL, 2 * D_FF), lambda i: (l, 0, 0)),
            pl.BlockSpec((None, CONV_WIDTH, 2 * D_FF), lambda i: (l, 0, 0)),
            pl.BlockSpec((None, 1, 2 * D_FF), lambda i: (l, 0, 0)),
            pl.BlockSpec((None, D_FF, D_MODEL), lambda i: (l, 0, 0)),
        ],
        out_specs=pl.BlockSpec((TM, D_MODEL), lambda i: (i, 0)),
        out_shape=jax.ShapeDtypeStruct((N_TOK, D_MODEL), F32),
        scratch_shapes=[pltpu.VMEM((TM + 2 * HALO, D_MODEL), F32)],
        compiler_params=_params(("arbitrary",)),
        name=f"ffn{l}",
    )(x, x, x, mod, norm_g, w_up, conv_w, conv_b, w_down)


def kernel(x_prompt, x_sample, cache_a_k, cache_a_v, cache_c_k, cache_c_v, state_hgrn, state_gla, c, c_ctx, w_ada, b_ada, norm1_g, norm2_g, w_in, a_qn_g, a_kn_g, c_qn_g, c_kn_g, c_lambda, c_subln_g, b_lb_logits, b_norm_g, d_alpha_w, d_alpha_b, d_norm_g, w_branch, w_out, w_up, conv_w, conv_b, w_down):
    x = jnp.concatenate([x_prompt.reshape(N_PROMPT, D_MODEL), x_sample.reshape(N_SAMPLE, D_MODEL)], axis=0)

    cond8 = jnp.concatenate([c_ctx[None, :], c, jnp.zeros((8 - N_GROUPS, D_MODEL), F32)], axis=0)
    mod = _modulation(cond8, w_ada, b_ada)[:, :N_GROUPS].reshape(DEPTH, N_GROUPS, N_MOD, D_MODEL)

    w_small = jnp.concatenate(
        [w_in[:, :, :MERGE_OFF], jnp.zeros((DEPTH, D_MODEL, BD_PAD - BD_W), F32)], axis=-1).astype(BF)
    w_merge = w_in[:, :, MERGE_OFF:].astype(BF)
    w_branch_b, w_out_b, w_up_b, w_down_b = (t.astype(BF) for t in (w_branch, w_out, w_up, w_down))

    tile4 = lambda g, n: jnp.tile(g, (1, n)).reshape(DEPTH, 1, -1)
    gains = (tile4(a_qn_g, A_HEADS), tile4(a_kn_g, A_KV_HEADS), tile4(c_qn_g, 2 * C_HEADS), tile4(c_kn_g, 2 * C_HEADS),
             c_subln_g.reshape(DEPTH, 1, 2 * C_HEAD_DIM))
    caches = (cache_a_k.reshape(DEC_BATCH, DEPTH, PAST_LEN, 128), cache_a_v.reshape(DEC_BATCH, DEPTH, PAST_LEN, 128),
              cache_c_k.reshape(DEC_BATCH, DEPTH, PAST_LEN, 256), cache_c_v.reshape(DEC_BATCH, DEPTH, PAST_LEN, 256))
    tables = _rope_tables(DEC_SEQ, A_HEAD_DIM, A_HEADS) + _rope_tables(DEC_SEQ, C_HEAD_DIM, 2 * C_HEADS)

    aw = jnp.zeros((DEPTH, 2, 128, 128), F32)
    aw = aw.at[:, 0, 0:D_GATE_RANK].set(d_alpha_w[:, 0]).at[:, 1, D_GATE_RANK:2 * D_GATE_RANK].set(d_alpha_w[:, 1])
    gla_small = (b_lb_logits, aw.astype(BF), d_alpha_b.reshape(DEPTH, 2, 1, 128),
                 tile4(b_norm_g, B_HEADS), tile4(d_norm_g, D_HEADS))
    st_h = jnp.swapaxes(state_hgrn, -1, -2)
    st_d = jnp.swapaxes(state_gla, -1, -2)

    n1 = norm1_g.reshape(DEPTH, 1, D_MODEL)
    n2 = norm2_g.reshape(DEPTH, 1, D_MODEL)
    cb = conv_b.reshape(DEPTH, 1, 2 * D_FF)

    collected = [[] for _ in range(6)]
    for l in range(DEPTH):
        lam_init = 0.8 - 0.6 * math.exp(-0.3 * l)
        zac, zbd, h = _inproj(l, x, mod, n1, w_small)
        y_p, ak, av, ck, cv = _attn_prompt(l, zac, gains, c_lambda, lam_init)
        y_s = _attn_sample(l, zac, gains, c_lambda, caches, tables, lam_init)
        yac = jnp.concatenate([y_p, y_s], axis=0)
        g_p, sh, sd = _gla_prompt(l, zbd, gla_small)
        g_s = _gla_sample(l, zbd, gla_small, st_h, st_d)
        ybd = jnp.concatenate([g_p, g_s], axis=0)
        x = _mix(l, x, h, yac, ybd, mod, w_merge, w_branch_b, w_out_b)
        x = _ffn(l, x, mod, n2, w_up_b, conv_w, cb, w_down_b)
        for acc, arr in zip(collected, (
                ak.reshape(BATCH, SEQ, A_KV_HEADS, A_HEAD_DIM), av.reshape(BATCH, SEQ, A_KV_HEADS, A_HEAD_DIM),
                ck.reshape(BATCH, SEQ, C_HEADS, 2, C_HEAD_DIM), cv.reshape(BATCH, SEQ, C_HEADS, 2 * C_HEAD_DIM),
                jnp.swapaxes(sh, -1, -2), jnp.swapaxes(sd, -1, -2))):
            acc.append(arr)

    y_prompt = x[:N_PROMPT].reshape(BATCH, SEQ, D_MODEL)
    y_sample = x[N_PROMPT:].reshape(DEC_BATCH, DEC_SEQ, D_MODEL)
    return (y_prompt, y_sample) + tuple(jnp.stack(acc, axis=1) for acc in collected)
```

```python
import functools
import math

import numpy as np
import jax
import jax.numpy as jnp
from jax import lax
from jax.experimental import pallas as pl
from jax.experimental.pallas import tpu as pltpu

F32 = jnp.float32
BF = jnp.bfloat16

D_MODEL = 1024
BATCH = 16
SEQ = 256
DEPTH = 2
DEC_BATCH = 2
DEC_SEQ = 1024
PAST_LEN = 512
GRID_W = 64
ROPE_THETA = 10000.0
EPS = 1e-6
LB_FLOOR = 1e-30
N_MOD = 6
N_BRANCH = 4
BRANCH_W = 256
A_HEADS, A_KV_HEADS, A_HEAD_DIM = 4, 2, 64
C_HEADS, C_HEAD_DIM = 4, 32
B_HEADS, B_KEY_DIM, B_VAL_DIM = 4, 64, 64
D_HEADS, D_KEY_DIM, D_VAL_DIM = 4, 32, 64
D_GATE_RANK = 16
D_GATE_TAU = 16.0
D_FF = 2816
CONV_WIDTH = 3

N_PROMPT = BATCH * SEQ
N_SAMPLE = DEC_BATCH * DEC_SEQ
N_TOK = N_PROMPT + N_SAMPLE
TM = 256
N_TILES = N_TOK // TM
PROMPT_TILES = N_PROMPT // TM
SAMPLE_TILES_PER_SEQ = DEC_SEQ // TM
N_GROUPS = 1 + DEC_BATCH

AC_W = 1280
BD_W = 2080
BD_PAD = 2176
SMALL_W = AC_W + BD_PAD
MERGE_OFF = AC_W + BD_W
HALO = 8
FF_CHUNK = 256
NPRO = 8
W_ROWS = D_MODEL // NPRO
LOG2E = 1.4426950408889634
VMEM_LIMIT = 56 * 1024 * 1024
LEVELS = (1, 2, 4, 8, 16, 32, 64, 128)


def _dot(a, b):
    return jnp.dot(a, b, preferred_element_type=F32)


def _dot_nt(a, b):
    return lax.dot_general(a, b, (((1,), (1,)), ((), ())), preferred_element_type=F32)


def _dot_tn(a, b):
    return lax.dot_general(a, b, (((0,), (0,)), ((), ())), preferred_element_type=F32)


def _silu(x):
    return x * jax.nn.sigmoid(x)


def _log_sigmoid(x):
    return jnp.minimum(x, 0.0) - jnp.log1p(jnp.exp(-jnp.abs(x)))


def _rms(x, g):
    return x * lax.rsqrt(jnp.mean(x * x, axis=-1, keepdims=True) + EPS) * g


def _head_rms(x, head_dim, g):
    w = x.shape[-1]
    sh = int(math.log2(head_dim))
    r = lax.shift_right_logical(lax.broadcasted_iota(jnp.int32, (w, w), 0), sh)
    c = lax.shift_right_logical(lax.broadcasted_iota(jnp.int32, (w, w), 1), sh)
    bd = jnp.where(r == c, 1.0 / head_dim, 0.0).astype(BF)
    x2 = x * x
    hi = x2.astype(BF)
    lo = (x2 - hi.astype(F32)).astype(BF)
    ms = _dot(hi, bd) + _dot(lo, bd)
    return x * lax.rsqrt(ms + EPS) * g


def _group_of_tile(i):
    return jnp.where(i < PROMPT_TILES, 0, 1 + jnp.maximum(i - PROMPT_TILES, 0) // SAMPLE_TILES_PER_SEQ)


def _params(sem):
    return pltpu.CompilerParams(dimension_semantics=sem, vmem_limit_bytes=VMEM_LIMIT)


def _mod_kernel(cond_ref, w_ref, b_ref, o_ref):
    s = _silu(cond_ref[...])
    o_ref[...] = _dot(s.astype(BF), w_ref[...].astype(BF)) + b_ref[...]


def _modulation(cond8, w_ada, b_ada):
    nb = 1536
    return pl.pallas_call(
        _mod_kernel,
        grid=(DEPTH, N_MOD * D_MODEL // nb),
        in_specs=[
            pl.BlockSpec((8, D_MODEL), lambda l, j: (0, 0)),
            pl.BlockSpec((None, D_MODEL, nb), lambda l, j: (l, 0, j)),
            pl.BlockSpec((None, 1, nb), lambda l, j: (l, 0, j)),
        ],
        out_specs=pl.BlockSpec((None, 8, nb), lambda l, j: (l, 0, j)),
        out_shape=jax.ShapeDtypeStruct((DEPTH, 8, N_MOD * D_MODEL), F32),
        compiler_params=_params(("arbitrary", "arbitrary")),
        name="modulation",
    )(cond8, w_ada, b_ada.reshape(DEPTH, 1, N_MOD * D_MODEL))


def _tile_of_step(i):
    return jnp.maximum(i - NPRO, 0)


def _chunk_of_step(i):
    return jnp.minimum(i, NPRO - 1)


def _tok_specs(width, split):
    if not split:
        return [pl.BlockSpec((TM, width), lambda i: (_tile_of_step(i), 0))]
    return [pl.BlockSpec((TM, width), lambda i: (jnp.minimum(_tile_of_step(i), PROMPT_TILES - 1), 0)),
            pl.BlockSpec((TM, width), lambda i: (jnp.maximum(_tile_of_step(i) - PROMPT_TILES, 0), 0))]


def _tok_load(t, refs):
    if len(refs) == 1:
        return refs[0][...]
    return jnp.where(t < PROMPT_TILES, refs[0][...], refs[1][...])


def _mod_spec(l):
    return pl.BlockSpec((None, None, N_MOD, D_MODEL), lambda i: (l, _group_of_tile(_tile_of_step(i)), 0, 0))


def _stage_rows(i, n):
    return pl.ds(pl.multiple_of(i * n, n), n)


def _inproj_kernel(*refs, n_x):
    x_refs = refs[:n_x]
    mod_ref, g_ref, w_ref, zac_ref, zbd_ref, h_ref, w_s = refs[n_x:]
    i = pl.program_id(0)

    @pl.when(i < NPRO)
    def _():
        w_s[_stage_rows(i, W_ROWS), :] = w_ref[...].astype(BF)

    @pl.when(i >= NPRO)
    def _():
        m = mod_ref[...]
        h = _rms(_tok_load(i - NPRO, x_refs), g_ref[...]) * (1.0 + m[1:2]) + m[0:1]
        hb = h.astype(BF)
        h_ref[...] = hb
        zac_ref[...] = _dot(hb, w_s[:, :AC_W])
        zbd_ref[...] = _dot(hb, w_s[:, AC_W:])


def _inproj(l, xs, mod, norm_g, w_in):
    split = len(xs) == 2
    return pl.pallas_call(
        functools.partial(_inproj_kernel, n_x=len(xs)),
        grid=(NPRO + N_TILES,),
        in_specs=_tok_specs(D_MODEL, split) + [
            _mod_spec(l),
            pl.BlockSpec((None, 1, D_MODEL), lambda i: (l, 0, 0)),
            pl.BlockSpec((None, W_ROWS, SMALL_W), lambda i: (l, _chunk_of_step(i), 0)),
        ],
        out_specs=[
            pl.BlockSpec((TM, AC_W), lambda i: (_tile_of_step(i), 0)),
            pl.BlockSpec((TM, BD_PAD), lambda i: (_tile_of_step(i), 0)),
            pl.BlockSpec((TM, D_MODEL), lambda i: (_tile_of_step(i), 0)),
        ],
        out_shape=[
            jax.ShapeDtypeStruct((N_TOK, AC_W), F32),
            jax.ShapeDtypeStruct((N_TOK, BD_PAD), F32),
            jax.ShapeDtypeStruct((N_TOK, D_MODEL), BF),
        ],
        scratch_shapes=[pltpu.VMEM((D_MODEL, SMALL_W), BF)],
        compiler_params=_params(("arbitrary",)),
        name=f"inproj{l}",
    )(*xs, mod, norm_g, w_in)


def _rope_tables(n_tokens, head_dim, n_rep):
    rows = n_tokens // GRID_W
    row = np.repeat(np.arange(rows), GRID_W).astype(np.float64)
    col = np.tile(np.arange(GRID_W), rows).astype(np.float64)
    half = head_dim // 2
    q4 = head_dim // 4
    freqs = ROPE_THETA ** (-np.arange(0, half, 2, dtype=np.float64) / half)
    ang_r = row[:, None] * freqs
    ang_c = col[:, None] * freqs
    ang = np.concatenate([ang_r, ang_r, ang_c, ang_c], axis=-1)
    cos, sin = np.cos(ang), np.sin(ang)
    first = (np.arange(head_dim) % (2 * q4)) < q4
    s_dn = np.where(first, -sin, 0.0)
    s_up = np.where(first, 0.0, sin)
    return tuple(jnp.asarray(np.tile(t, (1, n_rep)), dtype=F32) for t in (cos, s_dn, s_up))


def _rope(x, cos, s_dn, s_up, q4):
    w = x.shape[-1]
    return x * cos + pltpu.roll(x, w - q4, 1) * s_dn + pltpu.roll(x, q4, 1) * s_up


def _softmax_pv_group(maps):
    scores = [[_dot_nt(q, k) for k in ks] for q, ks, _, _ in maps]
    probs = []
    for ss in scores:
        m = ss[0].max(axis=-1, keepdims=True)
        for s in ss[1:]:
            m = jnp.maximum(m, s.max(axis=-1, keepdims=True))
        probs.append([jnp.exp2(s - m).astype(BF) for s in ss])
    outs = []
    for ps, (_, _, vexts, half) in zip(probs, maps):
        o = None
        for p, v in zip(ps, vexts):
            t = _dot(p, v)
            o = t if o is None else o + t
        outs.append(o[:, half * 64:(half + 1) * 64] / o[:, (1 - half) * 64:(1 - half) * 64 + 1])
    return outs


def _with_ones(v, half):
    lane_half = lax.shift_right_logical(lax.broadcasted_iota(jnp.int32, v.shape, 1), 6)
    return jnp.where(lane_half == half, v, jnp.ones_like(v))


def _attend_heads(aq, cq, ka, va, kc, vc, lam, gsub, lam_init, y_ref, group):
    aqb = (aq * (A_HEAD_DIM ** -0.5 * LOG2E)).astype(BF)
    cqb = (cq * (C_HEAD_DIM ** -0.5 * LOG2E)).astype(BF)
    rep = A_HEADS // A_KV_HEADS
    maps = []
    for h in range(A_HEADS):
        g = h // rep
        sl = slice(g * A_HEAD_DIM, (g + 1) * A_HEAD_DIM)
        maps.append((aqb[:, h * A_HEAD_DIM:(h + 1) * A_HEAD_DIM], [k[:, sl] for k in ka],
                     [_with_ones(v, g) for v in va], g))
    for h in range(C_HEADS):
        vh = [_with_ones(v[:, (h // 2) * 128:(h // 2 + 1) * 128], h % 2) for v in vc]
        for j in range(2):
            sl = slice((2 * h + j) * C_HEAD_DIM, (2 * h + j + 1) * C_HEAD_DIM)
            maps.append((cqb[:, sl], [k[:, sl] for k in kc], vh, h % 2))
    outs = []
    for i in range(0, len(maps), group):
        outs += _softmax_pv_group(maps[i:i + group])
    for h in range(A_HEADS):
        y_ref[:, h * A_HEAD_DIM:(h + 1) * A_HEAD_DIM] = outs[h]
    vd = 2 * C_HEAD_DIM
    for h in range(C_HEADS):
        d = outs[A_HEADS + 2 * h] - lam * outs[A_HEADS + 2 * h + 1]
        y_ref[:, BRANCH_W + h * vd:BRANCH_W + (h + 1) * vd] = _rms(d, gsub) * (1.0 - lam_init)


def _lambda(cl):
    s1 = jnp.sum(cl[0:1] * cl[1:2], axis=-1, keepdims=True)
    s2 = jnp.sum(cl[2:3] * cl[3:4], axis=-1, keepdims=True)
    return jnp.exp(s1) - jnp.exp(s2)


def _attn_prompt_kernel(z_ref, gaq, gak, gcq, gck, gsub, cl_ref, y_ref, oak, oav, ock, ocv, *, lam_init):
    z = z_ref[...]
    ak = _head_rms(z[:, 256:384], A_HEAD_DIM, gak[...])
    av = z[:, 384:512]
    ck = _head_rms(z[:, 768:1024], C_HEAD_DIM, gck[...])
    cv = z[:, 1024:1280]
    oak[...] = ak
    oav[...] = av
    ock[...] = ck
    ocv[...] = cv
    aq = _head_rms(z[:, 0:256], A_HEAD_DIM, gaq[...])
    cq = _head_rms(z[:, 512:768], C_HEAD_DIM, gcq[...])
    lam = _lambda(cl_ref[...]) + lam_init
    _attend_heads(aq, cq, [ak.astype(BF)], [av.astype(BF)], [ck.astype(BF)], [cv.astype(BF)],
                  lam, gsub[...], lam_init, y_ref, group=A_HEADS + 2 * C_HEADS)


def _attn_sample_kernel(z_ref, gaq, gak, gcq, gck, gsub, cl_ref, cak, cav, cck, ccv,
                        cosa, sda, sua, cosc, sdc, suc, y_ref, ka_s, va_s, kc_s, vc_s, *, lam_init):
    qi = pl.program_id(1)
    qa4, qc4 = A_HEAD_DIM // 4, C_HEAD_DIM // 4

    @pl.when(qi == 0)
    def _():
        ak = _head_rms(z_ref[:, 256:384], A_HEAD_DIM, gak[...])
        ka_s[...] = _rope(ak, cosa[:, :128], sda[:, :128], sua[:, :128], qa4).astype(BF)
        va_s[...] = z_ref[:, 384:512].astype(BF)
        ck = _head_rms(z_ref[:, 768:1024], C_HEAD_DIM, gck[...])
        kc_s[...] = _rope(ck, cosc[...], sdc[...], suc[...], qc4).astype(BF)
        vc_s[...] = z_ref[:, 1024:1280].astype(BF)

    rows = pl.ds(pl.multiple_of(qi * TM, TM), TM)
    aq = _head_rms(z_ref[rows, 0:256], A_HEAD_DIM, gaq[...])
    aq = _rope(aq, cosa[rows, :], sda[rows, :], sua[rows, :], qa4)
    cq = _head_rms(z_ref[rows, 512:768], C_HEAD_DIM, gcq[...])
    cq = _rope(cq, cosc[rows, :], sdc[rows, :], suc[rows, :], qc4)
    lam = _lambda(cl_ref[...]) + lam_init
    _attend_heads(aq, cq,
                  [cak[...].astype(BF), ka_s[...]], [cav[...].astype(BF), va_s[...]],
                  [cck[...].astype(BF), kc_s[...]], [ccv[...].astype(BF), vc_s[...]],
                  lam, gsub[...], lam_init, y_ref, group=4)


def _gain_specs(l, nd):
    zeros = (0,) * (nd - 1)
    widths = (256, 128, 256, 256, 64)
    return [pl.BlockSpec((None, 1, w), lambda *a: (l, 0, 0)) for w in widths] + \
           [pl.BlockSpec((None, 4, C_HEAD_DIM), lambda *a: (l, 0, 0))]


def _attn_prompt(l, zac, gains, c_lambda, lam_init):
    out_w = (2 * BRANCH_W, 128, 128, 256, 256)
    return pl.pallas_call(
        functools.partial(_attn_prompt_kernel, lam_init=lam_init),
        grid=(BATCH,),
        in_specs=[pl.BlockSpec((SEQ, AC_W), lambda b: (b, 0))] + _gain_specs(l, 1),
        out_specs=[pl.BlockSpec((SEQ, w), lambda b: (b, 0)) for w in out_w],
        out_shape=[jax.ShapeDtypeStruct((N_PROMPT, w), F32) for w in out_w],
        compiler_params=_params(("arbitrary",)),
        name=f"attn_prompt{l}",
    )(zac, *gains, c_lambda)


def _attn_sample(l, zac, gains, c_lambda, caches, tables, lam_init):
    first_blk = N_PROMPT // DEC_SEQ
    cache_specs = [pl.BlockSpec((None, None, PAST_LEN, w), lambda b, q: (b, l, 0, 0)) for w in (128, 128, 256, 256)]
    table_specs = [pl.BlockSpec((DEC_SEQ, 256), lambda b, q: (0, 0)) for _ in range(6)]
    return pl.pallas_call(
        functools.partial(_attn_sample_kernel, lam_init=lam_init),
        grid=(DEC_BATCH, DEC_SEQ // TM),
        in_specs=[pl.BlockSpec((DEC_SEQ, AC_W), lambda b, q: (first_blk + b, 0))] + _gain_specs(l, 2)
                 + cache_specs + table_specs,
        out_specs=pl.BlockSpec((TM, 2 * BRANCH_W), lambda b, q: (b * (DEC_SEQ // TM) + q, 0)),
        out_shape=jax.ShapeDtypeStruct((N_SAMPLE, 2 * BRANCH_W), F32),
        scratch_shapes=[pltpu.VMEM((DEC_SEQ, 128), BF), pltpu.VMEM((DEC_SEQ, 128), BF),
                        pltpu.VMEM((DEC_SEQ, 256), BF), pltpu.VMEM((DEC_SEQ, 256), BF)],
        compiler_params=_params(("arbitrary", "arbitrary")),
        name=f"attn_sample{l}",
    )(zac, *gains, c_lambda, *caches, *tables)


def _level_matrix(tt, rev):
    et = lax.broadcasted_iota(jnp.int32, (tt, tt), 0)
    es = lax.broadcasted_iota(jnp.int32, (tt, tt), 1)
    x = et ^ es
    lv = jnp.zeros((tt, tt), jnp.int32)
    for j in range(1, len(LEVELS)):
        lv = lv + (x >= (1 << j)).astype(jnp.int32)
    after = (et < es) if rev else (et > es)
    return jnp.where(after, lv, jnp.where(et == es, -1, -2))


def _gla_tile(q, k, v, la, lvl, st_ref, d, rev, nh, kd, vd, use_state, o_ref, rows, col0, accumulate):
    tt, w = q.shape
    e = lax.broadcasted_iota(jnp.int32, (tt, w), 0)
    if rev:
        e = (tt - 1) - e

    def prv(y, s):
        return pltpu.roll(y, (tt - s) if rev else s, 0)

    def nxt(y, s):
        return pltpu.roll(y, s if rev else tt - s, 0)

    def seg_prefix(x, b):
        y, s = x, 1
        while s < b:
            y = y + jnp.where((e & (b - 1)) >= s, prv(y, s), 0.0)
            s *= 2
        return y

    def seg_suffix(x, b):
        z, s = x, 1
        while s < b:
            z = z + jnp.where((e & (b - 1)) + s < b, nxt(z, s), 0.0)
            s *= 2
        return z

    la_n = nxt(la, 1)
    qs, ks = [q.astype(BF)], [k.astype(BF)]
    for b in LEVELS:
        h_b = seg_suffix(la_n, b)
        if b > 1:
            g_b = seg_prefix(jnp.where((e & (b - 1)) == 0, 0.0, la), b)
            qs.append((q * jnp.exp(g_b)).astype(BF))
        else:
            qs.append(qs[0])
        ks.append((k * jnp.exp(h_b)).astype(BF))
    cum = seg_prefix(la, tt)
    rem = seg_suffix(jnp.where(e == tt - 1, 0.0, la_n), tt)
    k_out = (k * jnp.exp(rem)).astype(BF)
    vb = v.astype(BF)
    if use_state:
        q_in = (q * jnp.exp(cum)).astype(BF)
        d_last = jnp.exp(cum[0:1] if rev else cum[tt - 1:tt])

    for h in range(nh):
        ksl = slice(h * kd, (h + 1) * kd)
        vsl = slice(h * vd, (h + 1) * vd)
        sc = jnp.where(lvl == -1, _dot_nt(qs[0][:, ksl], ks[0][:, ksl]), 0.0)
        for j in range(len(LEVELS)):
            sc = jnp.where(lvl == j, _dot_nt(qs[j + 1][:, ksl], ks[j + 1][:, ksl]), sc)
        o = _dot(sc.astype(BF), vb[:, vsl])
        kv = _dot_tn(vb[:, vsl], k_out[:, ksl])
        if use_state:
            st = st_ref[d, h]
            o = o + _dot_nt(q_in[:, ksl], st.astype(BF))
            st_ref[d, h] = st * d_last[:, ksl] + kv
        else:
            st_ref[d, h] = kv
        osl = slice(col0 + h * vd, col0 + (h + 1) * vd)
        if accumulate:
            o_ref[rows, osl] += o
        else:
            o_ref[rows, osl] = o


def _gla_kernel(*refs, layer, n_tiles, has_state):
    if has_state:
        (z_ref, lbl_ref, aw_ref, ab_ref, bng, dng, sh_in, sd_in, y_ref, o_scr, sth, std) = refs
        sth[...] = sh_in[...]
        std[...] = sd_in[...]
    else:
        (z_ref, lbl_ref, aw_ref, ab_ref, bng, dng, y_ref, sh_out, sd_out, o_scr, sth, std) = refs

    for d in range(2):
        rev = d == 1
        logits = [lbl_ref[d, i:i + 1, :] for i in range(DEPTH)]
        mx = functools.reduce(jnp.maximum, logits)
        ex = [jnp.exp(t - mx) for t in logits]
        den = functools.reduce(lambda a, b: a + b, ex)
        ps = [t / den for t in ex]
        lb = functools.reduce(lambda a, b: a + b, ps[:layer + 1]) - ps[0]
        log_lb = jnp.log(jnp.maximum(lb, LB_FLOOR))
        log_1m = jnp.log1p(-lb)
        lvl = _level_matrix(TM, rev)

        def tile(i, carry, d=d, rev=rev, lb=lb, log_lb=log_lb, log_1m=log_1m, lvl=lvl):
            j = (n_tiles - 1 - i) if rev else i
            rows = pl.ds(pl.multiple_of(j * TM, TM), TM)
            bq = z_ref[rows, 0:256]
            zf = z_ref[rows, 768:1024] if rev else z_ref[rows, 512:768]
            b2 = log_1m + _log_sigmoid(zf)
            la = jnp.maximum(log_lb, b2) + jnp.log1p(jnp.exp(-jnp.abs(log_lb - b2)))
            kb = (1.0 - lb) * jax.nn.sigmoid(-zf)
            _gla_tile(_silu(bq), kb, z_ref[rows, 256:512], la, lvl, sth, d, rev,
                      B_HEADS, B_KEY_DIM, B_VAL_DIM, has_state, o_scr, rows, 0, rev)
            pre = _dot(z_ref[rows, 2048:2176].astype(BF), aw_ref[d]) + ab_ref[d]
            la_d = _log_sigmoid(pre) / D_GATE_TAU
            _gla_tile(z_ref[rows, 1280:1408] * (D_KEY_DIM ** -0.5), z_ref[rows, 1408:1536], z_ref[rows, 1536:1792],
                      la_d, lvl, std, d, rev, D_HEADS, D_KEY_DIM, D_VAL_DIM, has_state, o_scr, rows, 256, rev)
            return carry

        lax.fori_loop(0, n_tiles, tile, 0)

    def finish(i, carry):
        rows = pl.ds(pl.multiple_of(i * TM, TM), TM)
        y_ref[rows, 0:256] = _head_rms(o_scr[rows, 0:256], B_VAL_DIM, bng[...]) * _silu(z_ref[rows, 1024:1280])
        y_ref[rows, 256:512] = _head_rms(o_scr[rows, 256:512], D_VAL_DIM, dng[...]) * _silu(z_ref[rows, 1792:2048])
        return carry

    lax.fori_loop(0, n_tiles, finish, 0)
    if not has_state:
        sh_out[...] = sth[...]
        sd_out[...] = std[...]


def _gla_common_specs(l, nd):
    return [
        pl.BlockSpec((2, DEPTH, 256), lambda *a: (0, 0, 0)),
        pl.BlockSpec((None, 2, 128, 128), lambda *a: (l, 0, 0, 0)),
        pl.BlockSpec((None, 2, 1, 128), lambda *a: (l, 0, 0, 0)),
        pl.BlockSpec((None, 1, 256), lambda *a: (l, 0, 0)),
        pl.BlockSpec((None, 1, 256), lambda *a: (l, 0, 0)),
    ]


_STATE_SCRATCH = [pltpu.VMEM((2, B_HEADS, B_VAL_DIM, B_KEY_DIM), F32), pltpu.VMEM((2, D_HEADS, D_VAL_DIM, D_KEY_DIM), F32)]


def _gla_prompt(l, zbd, small):
    return pl.pallas_call(
        functools.partial(_gla_kernel, layer=l, n_tiles=1, has_state=False),
        grid=(BATCH,),
        in_specs=[pl.BlockSpec((SEQ, BD_PAD), lambda b: (b, 0))] + _gla_common_specs(l, 1),
        out_specs=[
            pl.BlockSpec((SEQ, 2 * BRANCH_W), lambda b: (b, 0)),
            pl.BlockSpec((None, 2, B_HEADS, B_VAL_DIM, B_KEY_DIM), lambda b: (b, 0, 0, 0, 0)),
            pl.BlockSpec((None, 2, D_HEADS, D_VAL_DIM, D_KEY_DIM), lambda b: (b, 0, 0, 0, 0)),
        ],
        out_shape=[
            jax.ShapeDtypeStruct((N_PROMPT, 2 * BRANCH_W), F32),
            jax.ShapeDtypeStruct((BATCH, 2, B_HEADS, B_VAL_DIM, B_KEY_DIM), F32),
            jax.ShapeDtypeStruct((BATCH, 2, D_HEADS, D_VAL_DIM, D_KEY_DIM), F32),
        ],
        scratch_shapes=[pltpu.VMEM((SEQ, 2 * BRANCH_W), F32)] + _STATE_SCRATCH,
        compiler_params=_params(("arbitrary",)),
        name=f"gla_prompt{l}",
    )(zbd, *small)


def _gla_sample(l, zbd, small, st_h, st_d):
    first_blk = N_PROMPT // DEC_SEQ
    return pl.pallas_call(
        functools.partial(_gla_kernel, layer=l, n_tiles=DEC_SEQ // TM, has_state=True),
        grid=(DEC_BATCH,),
        in_specs=[pl.BlockSpec((DEC_SEQ, BD_PAD), lambda b: (first_blk + b, 0))] + _gla_common_specs(l, 1) + [
            pl.BlockSpec((None, None, 2, B_HEADS, B_VAL_DIM, B_KEY_DIM), lambda b: (b, l, 0, 0, 0, 0)),
            pl.BlockSpec((None, None, 2, D_HEADS, D_VAL_DIM, D_KEY_DIM), lambda b: (b, l, 0, 0, 0, 0)),
        ],
        out_specs=pl.BlockSpec((DEC_SEQ, 2 * BRANCH_W), lambda b: (b, 0)),
        out_shape=jax.ShapeDtypeStruct((N_SAMPLE, 2 * BRANCH_W), F32),
        scratch_shapes=[pltpu.VMEM((DEC_SEQ, 2 * BRANCH_W), F32)] + _STATE_SCRATCH,
        compiler_params=_params(("arbitrary",)),
        name=f"gla_sample{l}",
    )(zbd, *small, st_h, st_d)


def _mix_kernel(*refs, n_x):
    x_refs = refs[:n_x]
    (h_ref, yp_ref, ys_ref, gp_ref, gs_ref, mod_ref, win_ref, wb_ref, wo_ref, o_ref, wm_s, wb_s, wo_s) = refs[n_x:]
    i = pl.program_id(0)

    @pl.when(i < NPRO)
    def _():
        rows = _stage_rows(i, W_ROWS)
        wm_s[rows, :] = win_ref[...][:, MERGE_OFF:].astype(BF)
        wb_s[rows, :] = wb_ref[...].astype(BF)
        wo_s[rows, :] = wo_ref[...].astype(BF)

    @pl.when(i >= NPRO)
    def _():
        t = i - NPRO
        hb = h_ref[...]
        yac = _tok_load(t, (yp_ref, ys_ref))
        ybd = _tok_load(t, (gp_ref, gs_ref))
        branches = (yac[:, :BRANCH_W], ybd[:, :BRANCH_W], yac[:, BRANCH_W:], ybd[:, BRANCH_W:])
        mixed = None
        for n, y in enumerate(branches):
            logits = _dot(hb, wm_s[:, n * D_MODEL:(n + 1) * D_MODEL])
            term = jax.nn.sigmoid(logits) * _dot(y.astype(BF), wb_s[n * BRANCH_W:(n + 1) * BRANCH_W, :])
            mixed = term if mixed is None else mixed + term
        o_ref[...] = _tok_load(t, x_refs) + mod_ref[2:3, :] * _dot(mixed.astype(BF), wo_s[...])


def _mix(l, xs, h, y_p, y_s, g_p, g_s, mod, w_in, w_branch, w_out):
    chunk = lambda w: pl.BlockSpec((None, W_ROWS, w), lambda i: (l, _chunk_of_step(i), 0))
    return pl.pallas_call(
        functools.partial(_mix_kernel, n_x=len(xs)),
        grid=(NPRO + N_TILES,),
        in_specs=_tok_specs(D_MODEL, len(xs) == 2) + _tok_specs(D_MODEL, False)
                 + _tok_specs(2 * BRANCH_W, True) + _tok_specs(2 * BRANCH_W, True)
                 + [_mod_spec(l), chunk(w_in.shape[-1]), chunk(D_MODEL), chunk(D_MODEL)],
        out_specs=pl.BlockSpec((TM, D_MODEL), lambda i: (_tile_of_step(i), 0)),
        out_shape=jax.ShapeDtypeStruct((N_TOK, D_MODEL), F32),
        scratch_shapes=[pltpu.VMEM((D_MODEL, N_BRANCH * D_MODEL), BF), pltpu.VMEM((D_MODEL, D_MODEL), BF),
                        pltpu.VMEM((D_MODEL, D_MODEL), BF)],
        compiler_params=_params(("arbitrary",)),
        name=f"mix{l}",
    )(*xs, h, y_p, y_s, g_p, g_s, mod, w_in, w_branch.reshape(DEPTH, N_BRANCH * BRANCH_W, D_MODEL), w_out)


def _ffn_kernel(*refs, n_out):
    (xp_ref, x_ref, xn_ref, mod_ref, g_ref, wup_ref, cw_ref, cb_ref, wdn_ref) = refs[:9]
    o_refs = refs[9:9 + n_out]
    hext, u_s, wup_s, wdn_s = refs[9 + n_out:]
    i = pl.program_id(0)

    @pl.when(i < NPRO)
    def _():
        wup_s[_stage_rows(i, W_ROWS), :] = wup_ref[...].astype(BF)
        wdn_s[_stage_rows(i, D_FF // NPRO), :] = wdn_ref[...].astype(BF)

    @pl.when(i >= NPRO)
    def _():
        t = i - NPRO
        pos = jnp.maximum(t - PROMPT_TILES, 0) % SAMPLE_TILES_PER_SEQ
        seq_first = (t < PROMPT_TILES) | (pos == 0)
        seq_last = (t < PROMPT_TILES) | (pos == SAMPLE_TILES_PER_SEQ - 1)
        m = mod_ref[...]
        g = g_ref[...]

        def pre(x):
            return _rms(x, g) * (1.0 + m[4:5]) + m[3:4]

        hext[0:HALO, :] = jnp.where(seq_first, 0.0, pre(xp_ref[...]))
        hext[HALO:HALO + TM, :] = pre(x_ref[...])
        hext[HALO + TM:, :] = jnp.where(seq_last, 0.0, pre(xn_ref[...]))
        hb = hext[...].astype(BF)

        def up_conv_act(c):
            halves = []
            for k, off in enumerate((0, D_FF)):
                cols = slice(off + c * FF_CHUNK, off + (c + 1) * FF_CHUNK)
                u = u_s.at[(2 * c + k) % u_s.shape[0]]
                u[...] = _dot(hb, wup_s[:, cols])
                halves.append(u[HALO - 1:HALO - 1 + TM, :] * cw_ref[0:1, cols] + u[HALO:HALO + TM, :] * cw_ref[1:2, cols]
                              + u[HALO + 1:HALO + 1 + TM, :] * cw_ref[2:3, cols] + cb_ref[:, cols])
            return (_silu(halves[1]) * halves[0]).astype(BF)

        n_chunks = D_FF // FF_CHUNK
        act = up_conv_act(0)
        acc = None
        for c in range(n_chunks):
            nxt = up_conv_act(c + 1) if c + 1 < n_chunks else None
            part = _dot(act, wdn_s[c * FF_CHUNK:(c + 1) * FF_CHUNK, :])
            acc = part if acc is None else acc + part
            act = nxt
        res = x_ref[...] + m[5:6] * acc
        if n_out == 1:
            o_refs[0][...] = res
        else:
            @pl.when(t < PROMPT_TILES)
            def _():
                o_refs[0][...] = res

            @pl.when(t >= PROMPT_TILES)
            def _():
                o_refs[1][...] = res


def _ffn(l, x, mod, norm_g, w_up, conv_w, conv_b, w_down, split_out):
    per = TM // HALO
    last_blk = N_TOK // HALO - 1
    if split_out:
        out_specs = [pl.BlockSpec((TM, D_MODEL), lambda i: (jnp.minimum(_tile_of_step(i), PROMPT_TILES - 1), 0)),
                     pl.BlockSpec((TM, D_MODEL), lambda i: (jnp.maximum(_tile_of_step(i) - PROMPT_TILES, 0), 0))]
        out_shape = [jax.ShapeDtypeStruct((N_PROMPT, D_MODEL), F32), jax.ShapeDtypeStruct((N_SAMPLE, D_MODEL), F32)]
    else:
        out_specs = [pl.BlockSpec((TM, D_MODEL), lambda i: (_tile_of_step(i), 0))]
        out_shape = [jax.ShapeDtypeStruct((N_TOK, D_MODEL), F32)]
    return pl.pallas_call(
        functools.partial(_ffn_kernel, n_out=len(out_specs)),
        grid=(NPRO + N_TILES,),
        in_specs=[
            pl.BlockSpec((HALO, D_MODEL), lambda i: (jnp.maximum(_tile_of_step(i) * per - 1, 0), 0)),
            pl.BlockSpec((TM, D_MODEL), lambda i: (_tile_of_step(i), 0)),
            pl.BlockSpec((HALO, D_MODEL), lambda i: (jnp.minimum((_tile_of_step(i) + 1) * per, last_blk), 0)),
            _mod_spec(l),
            pl.BlockSpec((None, 1, D_MODEL), lambda i: (l, 0, 0)),
            pl.BlockSpec((None, W_ROWS, 2 * D_FF), lambda i: (l, _chunk_of_step(i), 0)),
            pl.BlockSpec((None, CONV_WIDTH, 2 * D_FF), lambda i: (l, 0, 0)),
            pl.BlockSpec((None, 1, 2 * D_FF), lambda i: (l, 0, 0)),
            pl.BlockSpec((None, D_FF // NPRO, D_MODEL), lambda i: (l, _chunk_of_step(i), 0)),
        ],
        out_specs=out_specs,
        out_shape=out_shape,
        scratch_shapes=[pltpu.VMEM((TM + 2 * HALO, D_MODEL), F32), pltpu.VMEM((4, TM + 2 * HALO, FF_CHUNK), F32),
                        pltpu.VMEM((D_MODEL, 2 * D_FF), BF), pltpu.VMEM((D_FF, D_MODEL), BF)],
        compiler_params=_params(("arbitrary",)),
        name=f"ffn{l}",
    )(x, x, x, mod, norm_g, w_up, conv_w, conv_b, w_down)


def kernel(x_prompt, x_sample, cache_a_k, cache_a_v, cache_c_k, cache_c_v, state_hgrn, state_gla, c, c_ctx, w_ada, b_ada, norm1_g, norm2_g, w_in, a_qn_g, a_kn_g, c_qn_g, c_kn_g, c_lambda, c_subln_g, b_lb_logits, b_norm_g, d_alpha_w, d_alpha_b, d_norm_g, w_branch, w_out, w_up, conv_w, conv_b, w_down):
    xs = (x_prompt.reshape(N_PROMPT, D_MODEL), x_sample.reshape(N_SAMPLE, D_MODEL))

    cond8 = jnp.concatenate([c_ctx[None, :], c, jnp.zeros((8 - N_GROUPS, D_MODEL), F32)], axis=0)
    mod = _modulation(cond8, w_ada, b_ada)[:, :N_GROUPS].reshape(DEPTH, N_GROUPS, N_MOD, D_MODEL)

    tile4 = lambda g, n: jnp.tile(g, (1, n)).reshape(DEPTH, 1, -1)
    gains = (tile4(a_qn_g, A_HEADS), tile4(a_kn_g, A_KV_HEADS), tile4(c_qn_g, 2 * C_HEADS), tile4(c_kn_g, 2 * C_HEADS),
             c_subln_g.reshape(DEPTH, 1, 2 * C_HEAD_DIM))
    caches = (cache_a_k.reshape(DEC_BATCH, DEPTH, PAST_LEN, 128), cache_a_v.reshape(DEC_BATCH, DEPTH, PAST_LEN, 128),
              cache_c_k.reshape(DEC_BATCH, DEPTH, PAST_LEN, 256), cache_c_v.reshape(DEC_BATCH, DEPTH, PAST_LEN, 256))
    tables = _rope_tables(DEC_SEQ, A_HEAD_DIM, A_HEADS) + _rope_tables(DEC_SEQ, C_HEAD_DIM, 2 * C_HEADS)

    aw = jnp.zeros((DEPTH, 2, 128, 128), F32)
    aw = aw.at[:, 0, 0:D_GATE_RANK].set(d_alpha_w[:, 0]).at[:, 1, D_GATE_RANK:2 * D_GATE_RANK].set(d_alpha_w[:, 1])
    gla_small = (b_lb_logits, aw.astype(BF), d_alpha_b.reshape(DEPTH, 2, 1, 128),
                 tile4(b_norm_g, B_HEADS), tile4(d_norm_g, D_HEADS))
    st_h = jnp.swapaxes(state_hgrn, -1, -2)
    st_d = jnp.swapaxes(state_gla, -1, -2)

    n1 = norm1_g.reshape(DEPTH, 1, D_MODEL)
    n2 = norm2_g.reshape(DEPTH, 1, D_MODEL)
    cb = conv_b.reshape(DEPTH, 1, 2 * D_FF)

    collected = [[] for _ in range(6)]
    for l in range(DEPTH):
        lam_init = 0.8 - 0.6 * math.exp(-0.3 * l)
        zac, zbd, h = _inproj(l, xs, mod, n1, w_in)
        y_p, ak, av, ck, cv = _attn_prompt(l, zac, gains, c_lambda, lam_init)
        y_s = _attn_sample(l, zac, gains, c_lambda, caches, tables, lam_init)
        g_p, sh, sd = _gla_prompt(l, zbd, gla_small)
        g_s = _gla_sample(l, zbd, gla_small, st_h, st_d)
        x1 = _mix(l, xs, h, y_p, y_s, g_p, g_s, mod, w_in, w_branch, w_out)
        xs = tuple(_ffn(l, x1, mod, n2, w_up, conv_w, cb, w_down, split_out=(l == DEPTH - 1)))
        for acc, arr in zip(collected, (
                ak.reshape(BATCH, SEQ, A_KV_HEADS, A_HEAD_DIM), av.reshape(BATCH, SEQ, A_KV_HEADS, A_HEAD_DIM),
                ck.reshape(BATCH, SEQ, C_HEADS, 2, C_HEAD_DIM), cv.reshape(BATCH, SEQ, C_HEADS, 2 * C_HEAD_DIM),
                jnp.swapaxes(sh, -1, -2), jnp.swapaxes(sd, -1, -2))):
            acc.append(arr)

    y_prompt = xs[0].reshape(BATCH, SEQ, D_MODEL)
    y_sample = xs[1].reshape(DEC_BATCH, DEC_SEQ, D_MODEL)
    return (y_prompt, y_sample) + tuple(jnp.stack(acc, axis=1) for acc in collected)
```

```python
import functools
import math

import numpy as np
import jax
import jax.numpy as jnp
from jax import lax
from jax.experimental import pallas as pl
from jax.experimental.pallas import tpu as pltpu

F32 = jnp.float32
BF = jnp.bfloat16

D_MODEL = 1024
BATCH = 16
SEQ = 256
DEPTH = 2
DEC_BATCH = 2
DEC_SEQ = 1024
PAST_LEN = 512
GRID_W = 64
ROPE_THETA = 10000.0
EPS = 1e-6
LB_FLOOR = 1e-30
N_MOD = 6
N_BRANCH = 4
BRANCH_W = 256
A_HEADS, A_KV_HEADS, A_HEAD_DIM = 4, 2, 64
C_HEADS, C_HEAD_DIM = 4, 32
B_HEADS, B_KEY_DIM, B_VAL_DIM = 4, 64, 64
D_HEADS, D_KEY_DIM, D_VAL_DIM = 4, 32, 64
D_GATE_RANK = 16
D_GATE_TAU = 16.0
D_FF = 2816
CONV_WIDTH = 3

N_PROMPT = BATCH * SEQ
N_SAMPLE = DEC_BATCH * DEC_SEQ
N_TOK = N_PROMPT + N_SAMPLE
TM = 256
N_TILES = N_TOK // TM
PROMPT_TILES = N_PROMPT // TM
SAMPLE_TILES_PER_SEQ = DEC_SEQ // TM
N_GROUPS = 1 + DEC_BATCH

AC_W = 1280
BD_W = 2080
BD_PAD = 2176
SMALL_W = AC_W + BD_PAD
MERGE_OFF = AC_W + BD_W
MERGE_ROWS = 560
HALO = 8
FF_CHUNK = 256
NPRO = 8
W_ROWS = D_MODEL // NPRO
LOG2E = 1.4426950408889634
VMEM_LIMIT = 56 * 1024 * 1024
TT = 128
N_LEVELS = 7


def _dot(a, b):
    return jnp.dot(a, b, preferred_element_type=F32)


def _dot_nt(a, b):
    return lax.dot_general(a, b, (((1,), (1,)), ((), ())), preferred_element_type=F32)


def _dot_tn(a, b):
    return lax.dot_general(a, b, (((0,), (0,)), ((), ())), preferred_element_type=F32)


def _silu(x):
    return x * jax.nn.sigmoid(x)


def _log_sigmoid(x):
    return jnp.minimum(x, 0.0) - jnp.log1p(jnp.exp(-jnp.abs(x)))


def _rms(x, g):
    return x * lax.rsqrt(jnp.mean(x * x, axis=-1, keepdims=True) + EPS) * g


def _head_rms(x, head_dim, g):
    w = x.shape[-1]
    sh = int(math.log2(head_dim))
    r = lax.shift_right_logical(lax.broadcasted_iota(jnp.int32, (w, w), 0), sh)
    c = lax.shift_right_logical(lax.broadcasted_iota(jnp.int32, (w, w), 1), sh)
    bd = jnp.where(r == c, 1.0 / head_dim, 0.0).astype(BF)
    x2 = x * x
    hi = x2.astype(BF)
    lo = (x2 - hi.astype(F32)).astype(BF)
    ms = _dot(hi, bd) + _dot(lo, bd)
    return x * lax.rsqrt(ms + EPS) * g


def _group_of_tile(i):
    return jnp.where(i < PROMPT_TILES, 0, 1 + jnp.maximum(i - PROMPT_TILES, 0) // SAMPLE_TILES_PER_SEQ)


def _params(sem):
    return pltpu.CompilerParams(dimension_semantics=sem, vmem_limit_bytes=VMEM_LIMIT)


def _mod_kernel(cond_ref, w_ref, b_ref, o_ref):
    s = _silu(cond_ref[...])
    o_ref[...] = _dot(s.astype(BF), w_ref[...].astype(BF)) + b_ref[...]


def _modulation(cond8, w_ada, b_ada):
    nb = 1536
    return pl.pallas_call(
        _mod_kernel,
        grid=(DEPTH, N_MOD * D_MODEL // nb),
        in_specs=[
            pl.BlockSpec((8, D_MODEL), lambda l, j: (0, 0)),
            pl.BlockSpec((None, D_MODEL, nb), lambda l, j: (l, 0, j)),
            pl.BlockSpec((None, 1, nb), lambda l, j: (l, 0, j)),
        ],
        out_specs=pl.BlockSpec((None, 8, nb), lambda l, j: (l, 0, j)),
        out_shape=jax.ShapeDtypeStruct((DEPTH, 8, N_MOD * D_MODEL), F32),
        compiler_params=_params(("arbitrary", "arbitrary")),
        name="modulation",
    )(cond8, w_ada, b_ada.reshape(DEPTH, 1, N_MOD * D_MODEL))


def _tile_of_step(i):
    return jnp.maximum(i - NPRO, 0)


def _chunk_of_step(i):
    return jnp.minimum(i, NPRO - 1)


def _tok_specs(width, split):
    if not split:
        return [pl.BlockSpec((TM, width), lambda i: (_tile_of_step(i), 0))]
    return [pl.BlockSpec((TM, width), lambda i: (jnp.minimum(_tile_of_step(i), PROMPT_TILES - 1), 0)),
            pl.BlockSpec((TM, width), lambda i: (jnp.maximum(_tile_of_step(i) - PROMPT_TILES, 0), 0))]


def _tok_load(t, refs):
    if len(refs) == 1:
        return refs[0][...]
    return jnp.where(t < PROMPT_TILES, refs[0][...], refs[1][...])


def _mod_spec(l):
    return pl.BlockSpec((None, None, N_MOD, D_MODEL), lambda i: (l, _group_of_tile(_tile_of_step(i)), 0, 0))


def _stage_rows(i, n):
    return pl.ds(pl.multiple_of(i * n, n), n)


def _inproj_kernel(*refs, n_x):
    x_refs = refs[:n_x]
    mod_ref, g_ref, w_ref, zac_ref, zbd_ref, h_ref, w_s = refs[n_x:]
    i = pl.program_id(0)

    @pl.when(i < NPRO)
    def _():
        w_s[_stage_rows(i, SMALL_W // NPRO), :] = w_ref[...].astype(BF)

    @pl.when(i >= NPRO)
    def _():
        m = mod_ref[...]
        h = _rms(_tok_load(i - NPRO, x_refs), g_ref[...]) * (1.0 + m[1:2]) + m[0:1]
        hb = h.astype(BF)
        h_ref[...] = hb
        zac_ref[...] = _dot_nt(hb, w_s[:AC_W, :])
        zbd_ref[...] = _dot_nt(hb, w_s[AC_W:, :])


def _inproj(l, xs, mod, norm_g, w_in_t):
    split = len(xs) == 2
    return pl.pallas_call(
        functools.partial(_inproj_kernel, n_x=len(xs)),
        grid=(NPRO + N_TILES,),
        in_specs=_tok_specs(D_MODEL, split) + [
            _mod_spec(l),
            pl.BlockSpec((None, 1, D_MODEL), lambda i: (l, 0, 0)),
            pl.BlockSpec((None, SMALL_W // NPRO, D_MODEL), lambda i: (l, _chunk_of_step(i), 0)),
        ],
        out_specs=[
            pl.BlockSpec((TM, AC_W), lambda i: (_tile_of_step(i), 0)),
            pl.BlockSpec((TM, BD_PAD), lambda i: (_tile_of_step(i), 0)),
            pl.BlockSpec((TM, D_MODEL), lambda i: (_tile_of_step(i), 0)),
        ],
        out_shape=[
            jax.ShapeDtypeStruct((N_TOK, AC_W), F32),
            jax.ShapeDtypeStruct((N_TOK, BD_PAD), F32),
            jax.ShapeDtypeStruct((N_TOK, D_MODEL), BF),
        ],
        scratch_shapes=[pltpu.VMEM((SMALL_W, D_MODEL), BF)],
        compiler_params=_params(("arbitrary",)),
        name=f"inproj{l}",
    )(*xs, mod, norm_g, w_in_t)


def _rope_tables(n_tokens, head_dim, n_rep):
    rows = n_tokens // GRID_W
    row = np.repeat(np.arange(rows), GRID_W).astype(np.float64)
    col = np.tile(np.arange(GRID_W), rows).astype(np.float64)
    half = head_dim // 2
    q4 = head_dim // 4
    freqs = ROPE_THETA ** (-np.arange(0, half, 2, dtype=np.float64) / half)
    ang_r = row[:, None] * freqs
    ang_c = col[:, None] * freqs
    ang = np.concatenate([ang_r, ang_r, ang_c, ang_c], axis=-1)
    cos, sin = np.cos(ang), np.sin(ang)
    first = (np.arange(head_dim) % (2 * q4)) < q4
    s_dn = np.where(first, -sin, 0.0)
    s_up = np.where(first, 0.0, sin)
    return tuple(jnp.asarray(np.tile(t, (1, n_rep)), dtype=F32) for t in (cos, s_dn, s_up))


def _rope(x, cos, s_dn, s_up, q4):
    w = x.shape[-1]
    return x * cos + pltpu.roll(x, w - q4, 1) * s_dn + pltpu.roll(x, q4, 1) * s_up


def _softmax_pv_group(maps):
    scores = [[_dot_nt(q, k) for k in ks] for q, ks, _, _ in maps]
    probs = []
    for ss in scores:
        m = ss[0].max(axis=-1, keepdims=True)
        for s in ss[1:]:
            m = jnp.maximum(m, s.max(axis=-1, keepdims=True))
        probs.append([jnp.exp2(s - m).astype(BF) for s in ss])
    outs = []
    for ps, (_, _, vexts, half) in zip(probs, maps):
        o = None
        for p, v in zip(ps, vexts):
            t = _dot(p, v)
            o = t if o is None else o + t
        outs.append(o[:, half * 64:(half + 1) * 64] / o[:, (1 - half) * 64:(1 - half) * 64 + 1])
    return outs


def _with_ones(v, half):
    lane_half = lax.shift_right_logical(lax.broadcasted_iota(jnp.int32, v.shape, 1), 6)
    return jnp.where(lane_half == half, v, jnp.ones_like(v))


def _attend_heads(aq, cq, ka, va, kc, vc, lam, gsub, lam_init, y_ref, group):
    aqb = (aq * (A_HEAD_DIM ** -0.5 * LOG2E)).astype(BF)
    cqb = (cq * (C_HEAD_DIM ** -0.5 * LOG2E)).astype(BF)
    rep = A_HEADS // A_KV_HEADS
    maps = []
    for h in range(A_HEADS):
        g = h // rep
        sl = slice(g * A_HEAD_DIM, (g + 1) * A_HEAD_DIM)
        maps.append((aqb[:, h * A_HEAD_DIM:(h + 1) * A_HEAD_DIM], [k[:, sl] for k in ka],
                     [_with_ones(v, g) for v in va], g))
    for h in range(C_HEADS):
        vh = [_with_ones(v[:, (h // 2) * 128:(h // 2 + 1) * 128], h % 2) for v in vc]
        for j in range(2):
            sl = slice((2 * h + j) * C_HEAD_DIM, (2 * h + j + 1) * C_HEAD_DIM)
            maps.append((cqb[:, sl], [k[:, sl] for k in kc], vh, h % 2))
    outs = []
    for i in range(0, len(maps), group):
        outs += _softmax_pv_group(maps[i:i + group])
    for h in range(A_HEADS):
        y_ref[:, h * A_HEAD_DIM:(h + 1) * A_HEAD_DIM] = outs[h]
    vd = 2 * C_HEAD_DIM
    for h in range(C_HEADS):
        d = outs[A_HEADS + 2 * h] - lam * outs[A_HEADS + 2 * h + 1]
        y_ref[:, BRANCH_W + h * vd:BRANCH_W + (h + 1) * vd] = _rms(d, gsub) * (1.0 - lam_init)


def _lambda(cl):
    s1 = jnp.sum(cl[0:1] * cl[1:2], axis=-1, keepdims=True)
    s2 = jnp.sum(cl[2:3] * cl[3:4], axis=-1, keepdims=True)
    return jnp.exp(s1) - jnp.exp(s2)


def _attn_prompt_kernel(z_ref, gaq, gak, gcq, gck, gsub, cl_ref, y_ref, oak, oav, ock, ocv, *, lam_init):
    z = z_ref[...]
    ak = _head_rms(z[:, 256:384], A_HEAD_DIM, gak[...])
    av = z[:, 384:512]
    ck = _head_rms(z[:, 768:1024], C_HEAD_DIM, gck[...])
    cv = z[:, 1024:1280]
    oak[...] = ak
    oav[...] = av
    ock[...] = ck
    ocv[...] = cv
    aq = _head_rms(z[:, 0:256], A_HEAD_DIM, gaq[...])
    cq = _head_rms(z[:, 512:768], C_HEAD_DIM, gcq[...])
    lam = _lambda(cl_ref[...]) + lam_init
    _attend_heads(aq, cq, [ak.astype(BF)], [av.astype(BF)], [ck.astype(BF)], [cv.astype(BF)],
                  lam, gsub[...], lam_init, y_ref, group=A_HEADS + 2 * C_HEADS)


def _attn_sample_kernel(z_ref, gaq, gak, gcq, gck, gsub, cl_ref, cak, cav, cck, ccv,
                        cosa, sda, sua, cosc, sdc, suc, y_ref, ka_s, va_s, kc_s, vc_s, *, lam_init):
    qi = pl.program_id(1)
    qa4, qc4 = A_HEAD_DIM // 4, C_HEAD_DIM // 4

    @pl.when(qi == 0)
    def _():
        ak = _head_rms(z_ref[:, 256:384], A_HEAD_DIM, gak[...])
        ka_s[...] = _rope(ak, cosa[:, :128], sda[:, :128], sua[:, :128], qa4).astype(BF)
        va_s[...] = z_ref[:, 384:512].astype(BF)
        ck = _head_rms(z_ref[:, 768:1024], C_HEAD_DIM, gck[...])
        kc_s[...] = _rope(ck, cosc[...], sdc[...], suc[...], qc4).astype(BF)
        vc_s[...] = z_ref[:, 1024:1280].astype(BF)

    rows = pl.ds(pl.multiple_of(qi * TM, TM), TM)
    aq = _head_rms(z_ref[rows, 0:256], A_HEAD_DIM, gaq[...])
    aq = _rope(aq, cosa[rows, :], sda[rows, :], sua[rows, :], qa4)
    cq = _head_rms(z_ref[rows, 512:768], C_HEAD_DIM, gcq[...])
    cq = _rope(cq, cosc[rows, :], sdc[rows, :], suc[rows, :], qc4)
    lam = _lambda(cl_ref[...]) + lam_init
    _attend_heads(aq, cq,
                  [cak[...].astype(BF), ka_s[...]], [cav[...].astype(BF), va_s[...]],
                  [cck[...].astype(BF), kc_s[...]], [ccv[...].astype(BF), vc_s[...]],
                  lam, gsub[...], lam_init, y_ref, group=4)


def _gain_specs(l, nd):
    zeros = (0,) * (nd - 1)
    widths = (256, 128, 256, 256, 64)
    return [pl.BlockSpec((None, 1, w), lambda *a: (l, 0, 0)) for w in widths] + \
           [pl.BlockSpec((None, 4, C_HEAD_DIM), lambda *a: (l, 0, 0))]


def _attn_prompt(l, zac, gains, c_lambda, lam_init):
    out_w = (2 * BRANCH_W, 128, 128, 256, 256)
    return pl.pallas_call(
        functools.partial(_attn_prompt_kernel, lam_init=lam_init),
        grid=(BATCH,),
        in_specs=[pl.BlockSpec((SEQ, AC_W), lambda b: (b, 0))] + _gain_specs(l, 1),
        out_specs=[pl.BlockSpec((SEQ, w), lambda b: (b, 0)) for w in out_w],
        out_shape=[jax.ShapeDtypeStruct((N_PROMPT, w), F32) for w in out_w],
        compiler_params=_params(("arbitrary",)),
        name=f"attn_prompt{l}",
    )(zac, *gains, c_lambda)


def _attn_sample(l, zac, gains, c_lambda, caches, tables, lam_init):
    first_blk = N_PROMPT // DEC_SEQ
    cache_specs = [pl.BlockSpec((None, None, PAST_LEN, w), lambda b, q: (b, l, 0, 0)) for w in (128, 128, 256, 256)]
    table_specs = [pl.BlockSpec((DEC_SEQ, 256), lambda b, q: (0, 0)) for _ in range(6)]
    return pl.pallas_call(
        functools.partial(_attn_sample_kernel, lam_init=lam_init),
        grid=(DEC_BATCH, DEC_SEQ // TM),
        in_specs=[pl.BlockSpec((DEC_SEQ, AC_W), lambda b, q: (first_blk + b, 0))] + _gain_specs(l, 2)
                 + cache_specs + table_specs,
        out_specs=pl.BlockSpec((TM, 2 * BRANCH_W), lambda b, q: (b * (DEC_SEQ // TM) + q, 0)),
        out_shape=jax.ShapeDtypeStruct((N_SAMPLE, 2 * BRANCH_W), F32),
        scratch_shapes=[pltpu.VMEM((DEC_SEQ, 128), BF), pltpu.VMEM((DEC_SEQ, 128), BF),
                        pltpu.VMEM((DEC_SEQ, 256), BF), pltpu.VMEM((DEC_SEQ, 256), BF)],
        compiler_params=_params(("arbitrary", "arbitrary")),
        name=f"attn_sample{l}",
    )(zac, *gains, c_lambda, *caches, *tables)


def _gla_constants():
    idx = np.arange(TT)
    scans, masks = [], []
    for rev in (False, True):
        eff = (TT - 1 - idx) if rev else idx
        et, eu = eff[:, None], eff[None, :]
        sc, mk = [], []
        for j in range(N_LEVELS):
            b = 1 << j
            start = et - et % b
            odd = (et // b) % 2 == 1
            sc.append(np.where(odd, (eu > start) & (eu <= et), (eu > et) & (eu <= start + b)))
            mk.append(((et // b) % 2 == 1) & (eu // b == et // b - 1))
        sc.append(eu <= et)
        sc.append(eu > et)
        mk.append(eu == et)
        scans.append(np.stack(sc))
        masks.append(np.stack(mk))
    return jnp.asarray(np.stack(scans), BF), jnp.asarray(np.stack(masks), BF)


def _gla_prepare(q, k, v, la2, scan_ref, d, rev, use_state):
    tt = q.shape[0]
    la_hi = la2.astype(BF)
    la_lo = (la2 - la_hi.astype(F32)).astype(BF)

    def factor(i):
        m = scan_ref[d, i]
        return jnp.exp2(_dot(m, la_hi) + _dot(m, la_lo))

    qs, ks = [], []
    for j in range(N_LEVELS):
        f = factor(j)
        qs.append((q * f).astype(BF))
        ks.append((k * f).astype(BF))
    qs.append(q.astype(BF))
    ks.append(k.astype(BF))
    k_out = (k * factor(N_LEVELS + 1)).astype(BF)
    vb = v.astype(BF)
    if use_state:
        f_cum = factor(N_LEVELS)
        q_in = (q * f_cum).astype(BF)
        d_last = f_cum[0:1] if rev else f_cum[tt - 1:tt]

    return dict(qs=qs, ks=ks, k_out=k_out, vb=vb, q_in=q_in if use_state else None,
                d_last=d_last if use_state else None)


def _lane_keep(x, lo, hi):
    lane = lax.broadcasted_iota(jnp.int32, x.shape, 1)
    return jnp.where((lane >= lo) & (lane < hi), x, jnp.zeros_like(x))


def _gla_scores(p, mask_ref, d, nh, kd):
    out = []
    for h in range(nh):
        c, lo = divmod(h * kd, 128)
        tile = slice(c * 128, (c + 1) * 128)
        sc = None
        for j in range(N_LEVELS + 1):
            kh = _lane_keep(p["ks"][j][:, tile], lo, lo + kd)
            t = _dot_nt(p["qs"][j][:, tile], kh).astype(BF) * mask_ref[d, j]
            sc = t if sc is None else sc + t
        out.append(sc)
    return out


def _gla_outputs(p, scs, blk_ref, st_ref, d, nh, kd, vd, use_state, o_ref, rows, col0, accumulate):
    per = 128 // vd
    for c in range(nh // per):
        vt = p["vb"][:, c * 128:(c + 1) * 128]
        kt = (c * per * kd) // 128
        ktile = slice(kt * 128, (kt + 1) * 128)
        o = None
        for i in range(per):
            t = _dot(scs[c * per + i], _lane_keep(vt, i * vd, (i + 1) * vd))
            o = t if o is None else o + t
        kv = _dot_tn(vt, p["k_out"][:, ktile]) * blk_ref[c]
        if use_state:
            st = st_ref[d, c]
            o = o + _dot_nt(p["q_in"][:, ktile], st.astype(BF))
            st_ref[d, c] = st * p["d_last"][:, ktile] + kv
        else:
            st_ref[d, c] = kv
        osl = slice(col0 + c * 128, col0 + (c + 1) * 128)
        if accumulate:
            o_ref[rows, osl] += o
        else:
            o_ref[rows, osl] = o


def _gla_kernel(*refs, layer, n_tiles, has_state):
    if has_state:
        (z_ref, lbl_ref, aw_ref, ab_ref, bng, dng, scan_ref, mask_ref, blkb_ref, blkd_ref, sh_in, sd_in,
         y_ref, o_scr, sth, std) = refs
        sth[...] = sh_in[...]
        std[...] = sd_in[...]
    else:
        (z_ref, lbl_ref, aw_ref, ab_ref, bng, dng, scan_ref, mask_ref, blkb_ref, blkd_ref,
         y_ref, sh_out, sd_out, o_scr, sth, std) = refs

    for d in range(2):
        rev = d == 1
        logits = [lbl_ref[d, i:i + 1, :] for i in range(DEPTH)]
        mx = functools.reduce(jnp.maximum, logits)
        ex = [jnp.exp(t - mx) for t in logits]
        den = functools.reduce(lambda a, b: a + b, ex)
        ps = [t / den for t in ex]
        lb = functools.reduce(lambda a, b: a + b, ps[:layer + 1]) - ps[0]
        log_lb = jnp.log(jnp.maximum(lb, LB_FLOOR))
        log_1m = jnp.log1p(-lb)

        def tile(i, use_state, d=d, rev=rev, lb=lb, log_lb=log_lb, log_1m=log_1m):
            j = (n_tiles - 1 - i) if rev else i
            rows = pl.ds(j * TT if isinstance(j, int) else pl.multiple_of(j * TT, TT), TT)
            bq = z_ref[rows, 0:256]
            zf = z_ref[rows, 768:1024] if rev else z_ref[rows, 512:768]
            b2 = log_1m + _log_sigmoid(zf)
            la = jnp.maximum(log_lb, b2) + jnp.log1p(jnp.exp(-jnp.abs(log_lb - b2)))
            kb = (1.0 - lb) * jax.nn.sigmoid(-zf)
            pre = _dot(z_ref[rows, 2048:2176].astype(BF), aw_ref[d]) + ab_ref[d]
            la_d = _log_sigmoid(pre) * (LOG2E / D_GATE_TAU)
            pb = _gla_prepare(_silu(bq), kb, z_ref[rows, 256:512], la * LOG2E, scan_ref, d, rev, use_state)
            pd = _gla_prepare(z_ref[rows, 1280:1408] * (D_KEY_DIM ** -0.5), z_ref[rows, 1408:1536],
                              z_ref[rows, 1536:1792], la_d, scan_ref, d, rev, use_state)
            sb = _gla_scores(pb, mask_ref, d, B_HEADS, B_KEY_DIM)
            sd = _gla_scores(pd, mask_ref, d, D_HEADS, D_KEY_DIM)
            _gla_outputs(pb, sb, blkb_ref, sth, d, B_HEADS, B_KEY_DIM, B_VAL_DIM, use_state, o_scr, rows, 0, rev)
            _gla_outputs(pd, sd, blkd_ref, std, d, D_HEADS, D_KEY_DIM, D_VAL_DIM, use_state, o_scr, rows, 256, rev)

        if has_state:
            def body(i, carry, tile=tile):
                tile(i, True)
                return carry

            lax.fori_loop(0, n_tiles, body, 0)
        else:
            for i in range(n_tiles):
                tile(i, i > 0)

    def finish(i, carry):
        rows = pl.ds(pl.multiple_of(i * TM, TM), TM)
        y_ref[rows, 0:256] = _head_rms(o_scr[rows, 0:256], B_VAL_DIM, bng[...]) * _silu(z_ref[rows, 1024:1280])
        y_ref[rows, 256:512] = _head_rms(o_scr[rows, 256:512], D_VAL_DIM, dng[...]) * _silu(z_ref[rows, 1792:2048])
        return carry

    lax.fori_loop(0, (n_tiles * TT) // TM, finish, 0)
    if not has_state:
        sh_out[...] = sth[...]
        sd_out[...] = std[...]


def _gla_common_specs(l, nd):
    return [
        pl.BlockSpec((2, DEPTH, 256), lambda *a: (0, 0, 0)),
        pl.BlockSpec((None, 2, 128, 128), lambda *a: (l, 0, 0, 0)),
        pl.BlockSpec((None, 2, 1, 128), lambda *a: (l, 0, 0, 0)),
        pl.BlockSpec((None, 1, 256), lambda *a: (l, 0, 0)),
        pl.BlockSpec((None, 1, 256), lambda *a: (l, 0, 0)),
        pl.BlockSpec((2, N_LEVELS + 2, TT, TT), lambda *a: (0, 0, 0, 0)),
        pl.BlockSpec((2, N_LEVELS + 1, TT, TT), lambda *a: (0, 0, 0, 0)),
        pl.BlockSpec((N_VTILES, 128, 128), lambda *a: (0, 0, 0)),
        pl.BlockSpec((N_VTILES, 128, 128), lambda *a: (0, 0, 0)),
    ]


N_VTILES = BRANCH_W // 128
_STATE_SCRATCH = [pltpu.VMEM((2, N_VTILES, 128, 128), F32), pltpu.VMEM((2, N_VTILES, 128, 128), F32)]


def _state_blocks(nh, kd, vd):
    per = 128 // vd
    return [(h // per, slice((h % per) * vd, (h % per + 1) * vd), slice((h * kd) % 128, (h * kd) % 128 + kd))
            for h in range(nh)]


def _state_pattern(nh, kd, vd):
    pat = np.zeros((N_VTILES, 128, 128), np.float32)
    for c, r, ln in _state_blocks(nh, kd, vd):
        pat[c, r, ln] = 1.0
    return jnp.asarray(pat)


def _pack_state(s, kd, vd):
    nh = s.shape[-3]
    st = jnp.swapaxes(s, -1, -2)
    out = jnp.zeros(s.shape[:-3] + (N_VTILES, 128, 128), F32)
    for h, (c, r, ln) in enumerate(_state_blocks(nh, kd, vd)):
        out = out.at[..., c, r, ln].set(st[..., h, :, :])
    return out


def _unpack_state(p, nh, kd, vd):
    st = jnp.stack([p[..., c, r, ln] for c, r, ln in _state_blocks(nh, kd, vd)], axis=-3)
    return jnp.swapaxes(st, -1, -2)


def _gla_prompt(l, zbd, small):
    packed = (BATCH, 2, N_VTILES, 128, 128)
    return pl.pallas_call(
        functools.partial(_gla_kernel, layer=l, n_tiles=SEQ // TT, has_state=False),
        grid=(BATCH,),
        in_specs=[pl.BlockSpec((SEQ, BD_PAD), lambda b: (b, 0))] + _gla_common_specs(l, 1),
        out_specs=[
            pl.BlockSpec((SEQ, 2 * BRANCH_W), lambda b: (b, 0)),
            pl.BlockSpec((None,) + packed[1:], lambda b: (b, 0, 0, 0, 0)),
            pl.BlockSpec((None,) + packed[1:], lambda b: (b, 0, 0, 0, 0)),
        ],
        out_shape=[
            jax.ShapeDtypeStruct((N_PROMPT, 2 * BRANCH_W), F32),
            jax.ShapeDtypeStruct(packed, F32),
            jax.ShapeDtypeStruct(packed, F32),
        ],
        scratch_shapes=[pltpu.VMEM((SEQ, 2 * BRANCH_W), F32)] + _STATE_SCRATCH,
        compiler_params=_params(("arbitrary",)),
        name=f"gla_prompt{l}",
    )(zbd, *small)


def _gla_sample(l, zbd, small, st_h, st_d):
    first_blk = N_PROMPT // DEC_SEQ
    packed = pl.BlockSpec((None, None, 2, N_VTILES, 128, 128), lambda b: (b, l, 0, 0, 0, 0))
    return pl.pallas_call(
        functools.partial(_gla_kernel, layer=l, n_tiles=DEC_SEQ // TT, has_state=True),
        grid=(DEC_BATCH,),
        in_specs=[pl.BlockSpec((DEC_SEQ, BD_PAD), lambda b: (first_blk + b, 0))] + _gla_common_specs(l, 1)
                 + [packed, packed],
        out_specs=pl.BlockSpec((DEC_SEQ, 2 * BRANCH_W), lambda b: (b, 0)),
        out_shape=jax.ShapeDtypeStruct((N_SAMPLE, 2 * BRANCH_W), F32),
        scratch_shapes=[pltpu.VMEM((DEC_SEQ, 2 * BRANCH_W), F32)] + _STATE_SCRATCH,
        compiler_params=_params(("arbitrary",)),
        name=f"gla_sample{l}",
    )(zbd, *small, st_h, st_d)


def _mix_kernel(*refs, n_x):
    x_refs = refs[:n_x]
    (h_ref, yp_ref, ys_ref, gp_ref, gs_ref, mod_ref, win_ref, wb_ref, wo_ref, o_ref, wm_s, wb_s, wo_s) = refs[n_x:]
    i = pl.program_id(0)

    @pl.when(i < NPRO)
    def _():
        rows = _stage_rows(i, W_ROWS)
        wm_s[_stage_rows(i, MERGE_ROWS), :] = win_ref[...].astype(BF)
        wb_s[rows, :] = wb_ref[...].astype(BF)
        wo_s[rows, :] = wo_ref[...].astype(BF)

    @pl.when(i >= NPRO)
    def _():
        t = i - NPRO
        hb = h_ref[...]
        yac = _tok_load(t, (yp_ref, ys_ref))
        ybd = _tok_load(t, (gp_ref, gs_ref))
        branches = (yac[:, :BRANCH_W], ybd[:, :BRANCH_W], yac[:, BRANCH_W:], ybd[:, BRANCH_W:])
        mixed = None
        for n, y in enumerate(branches):
            logits = _dot_nt(hb, wm_s[n * D_MODEL:(n + 1) * D_MODEL, :])
            term = jax.nn.sigmoid(logits) * _dot(y.astype(BF), wb_s[n * BRANCH_W:(n + 1) * BRANCH_W, :])
            mixed = term if mixed is None else mixed + term
        o_ref[...] = _tok_load(t, x_refs) + mod_ref[2:3, :] * _dot(mixed.astype(BF), wo_s[...])


def _mix(l, xs, h, y_p, y_s, g_p, g_s, mod, w_in_t, w_branch, w_out):
    chunk = pl.BlockSpec((None, W_ROWS, D_MODEL), lambda i: (l, _chunk_of_step(i), 0))
    return pl.pallas_call(
        functools.partial(_mix_kernel, n_x=len(xs)),
        grid=(NPRO + N_TILES,),
        in_specs=_tok_specs(D_MODEL, len(xs) == 2) + _tok_specs(D_MODEL, False)
                 + _tok_specs(2 * BRANCH_W, True) + _tok_specs(2 * BRANCH_W, True)
                 + [_mod_spec(l),
                    pl.BlockSpec((None, MERGE_ROWS, D_MODEL), lambda i: (l, MERGE_OFF // MERGE_ROWS + _chunk_of_step(i), 0)),
                    chunk, chunk],
        out_specs=pl.BlockSpec((TM, D_MODEL), lambda i: (_tile_of_step(i), 0)),
        out_shape=jax.ShapeDtypeStruct((N_TOK, D_MODEL), F32),
        scratch_shapes=[pltpu.VMEM((NPRO * MERGE_ROWS, D_MODEL), BF), pltpu.VMEM((D_MODEL, D_MODEL), BF),
                        pltpu.VMEM((D_MODEL, D_MODEL), BF)],
        compiler_params=_params(("arbitrary",)),
        name=f"mix{l}",
    )(*xs, h, y_p, y_s, g_p, g_s, mod, w_in_t, w_branch.reshape(DEPTH, N_BRANCH * BRANCH_W, D_MODEL), w_out)


def _ffn_kernel(*refs, n_out):
    (xp_ref, x_ref, xn_ref, mod_ref, g_ref, wup_ref, cw_ref, cb_ref, wdn_ref) = refs[:9]
    o_refs = refs[9:9 + n_out]
    hext, u_s, wup_s, wdn_s = refs[9 + n_out:]
    i = pl.program_id(0)

    @pl.when(i < NPRO)
    def _():
        wup_s[_stage_rows(i, W_ROWS), :] = wup_ref[...].astype(BF)
        wdn_s[_stage_rows(i, D_FF // NPRO), :] = wdn_ref[...].astype(BF)

    @pl.when(i >= NPRO)
    def _():
        t = i - NPRO
        pos = jnp.maximum(t - PROMPT_TILES, 0) % SAMPLE_TILES_PER_SEQ
        seq_first = (t < PROMPT_TILES) | (pos == 0)
        seq_last = (t < PROMPT_TILES) | (pos == SAMPLE_TILES_PER_SEQ - 1)
        m = mod_ref[...]
        g = g_ref[...]

        def pre(x):
            return _rms(x, g) * (1.0 + m[4:5]) + m[3:4]

        hext[0:HALO, :] = jnp.where(seq_first, 0.0, pre(xp_ref[...]))
        hext[HALO:HALO + TM, :] = pre(x_ref[...])
        hext[HALO + TM:, :] = jnp.where(seq_last, 0.0, pre(xn_ref[...]))
        hb = hext[...].astype(BF)

        def up_conv_act(c):
            halves = []
            for k, off in enumerate((0, D_FF)):
                cols = slice(off + c * FF_CHUNK, off + (c + 1) * FF_CHUNK)
                u = u_s.at[(2 * c + k) % u_s.shape[0]]
                u[...] = _dot(hb, wup_s[:, cols])
                halves.append(u[HALO - 1:HALO - 1 + TM, :] * cw_ref[0:1, cols] + u[HALO:HALO + TM, :] * cw_ref[1:2, cols]
                              + u[HALO + 1:HALO + 1 + TM, :] * cw_ref[2:3, cols] + cb_ref[:, cols])
            return (_silu(halves[1]) * halves[0]).astype(BF)

        n_chunks = D_FF // FF_CHUNK
        act = up_conv_act(0)
        acc = None
        for c in range(n_chunks):
            nxt = up_conv_act(c + 1) if c + 1 < n_chunks else None
            part = _dot(act, wdn_s[c * FF_CHUNK:(c + 1) * FF_CHUNK, :])
            acc = part if acc is None else acc + part
            act = nxt
        res = x_ref[...] + m[5:6] * acc
        if n_out == 1:
            o_refs[0][...] = res
        else:
            @pl.when(t < PROMPT_TILES)
            def _():
                o_refs[0][...] = res

            @pl.when(t >= PROMPT_TILES)
            def _():
                o_refs[1][...] = res


def _ffn(l, x, mod, norm_g, w_up, conv_w, conv_b, w_down, split_out):
    per = TM // HALO
    last_blk = N_TOK // HALO - 1
    if split_out:
        out_specs = [pl.BlockSpec((TM, D_MODEL), lambda i: (jnp.minimum(_tile_of_step(i), PROMPT_TILES - 1), 0)),
                     pl.BlockSpec((TM, D_MODEL), lambda i: (jnp.maximum(_tile_of_step(i) - PROMPT_TILES, 0), 0))]
        out_shape = [jax.ShapeDtypeStruct((N_PROMPT, D_MODEL), F32), jax.ShapeDtypeStruct((N_SAMPLE, D_MODEL), F32)]
    else:
        out_specs = [pl.BlockSpec((TM, D_MODEL), lambda i: (_tile_of_step(i), 0))]
        out_shape = [jax.ShapeDtypeStruct((N_TOK, D_MODEL), F32)]
    return pl.pallas_call(
        functools.partial(_ffn_kernel, n_out=len(out_specs)),
        grid=(NPRO + N_TILES,),
        in_specs=[
            pl.BlockSpec((HALO, D_MODEL), lambda i: (jnp.maximum(_tile_of_step(i) * per - 1, 0), 0)),
            pl.BlockSpec((TM, D_MODEL), lambda i: (_tile_of_step(i), 0)),
            pl.BlockSpec((HALO, D_MODEL), lambda i: (jnp.minimum((_tile_of_step(i) + 1) * per, last_blk), 0)),
            _mod_spec(l),
            pl.BlockSpec((None, 1, D_MODEL), lambda i: (l, 0, 0)),
            pl.BlockSpec((None, W_ROWS, 2 * D_FF), lambda i: (l, _chunk_of_step(i), 0)),
            pl.BlockSpec((None, CONV_WIDTH, 2 * D_FF), lambda i: (l, 0, 0)),
            pl.BlockSpec((None, 1, 2 * D_FF), lambda i: (l, 0, 0)),
            pl.BlockSpec((None, D_FF // NPRO, D_MODEL), lambda i: (l, _chunk_of_step(i), 0)),
        ],
        out_specs=out_specs,
        out_shape=out_shape,
        scratch_shapes=[pltpu.VMEM((TM + 2 * HALO, D_MODEL), F32), pltpu.VMEM((4, TM + 2 * HALO, FF_CHUNK), F32),
                        pltpu.VMEM((D_MODEL, 2 * D_FF), BF), pltpu.VMEM((D_FF, D_MODEL), BF)],
        compiler_params=_params(("arbitrary",)),
        name=f"ffn{l}",
    )(x, x, x, mod, norm_g, w_up, conv_w, conv_b, w_down)


def kernel(x_prompt, x_sample, cache_a_k, cache_a_v, cache_c_k, cache_c_v, state_hgrn, state_gla, c, c_ctx, w_ada, b_ada, norm1_g, norm2_g, w_in, a_qn_g, a_kn_g, c_qn_g, c_kn_g, c_lambda, c_subln_g, b_lb_logits, b_norm_g, d_alpha_w, d_alpha_b, d_norm_g, w_branch, w_out, w_up, conv_w, conv_b, w_down):
    xs = (x_prompt.reshape(N_PROMPT, D_MODEL), x_sample.reshape(N_SAMPLE, D_MODEL))
    w_in_t = jnp.swapaxes(w_in, 1, 2)

    cond8 = jnp.concatenate([c_ctx[None, :], c, jnp.zeros((8 - N_GROUPS, D_MODEL), F32)], axis=0)
    mod = _modulation(cond8, w_ada, b_ada)[:, :N_GROUPS].reshape(DEPTH, N_GROUPS, N_MOD, D_MODEL)

    tile4 = lambda g, n: jnp.tile(g, (1, n)).reshape(DEPTH, 1, -1)
    gains = (tile4(a_qn_g, A_HEADS), tile4(a_kn_g, A_KV_HEADS), tile4(c_qn_g, 2 * C_HEADS), tile4(c_kn_g, 2 * C_HEADS),
             c_subln_g.reshape(DEPTH, 1, 2 * C_HEAD_DIM))
    caches = (cache_a_k.reshape(DEC_BATCH, DEPTH, PAST_LEN, 128), cache_a_v.reshape(DEC_BATCH, DEPTH, PAST_LEN, 128),
              cache_c_k.reshape(DEC_BATCH, DEPTH, PAST_LEN, 256), cache_c_v.reshape(DEC_BATCH, DEPTH, PAST_LEN, 256))
    tables = _rope_tables(DEC_SEQ, A_HEAD_DIM, A_HEADS) + _rope_tables(DEC_SEQ, C_HEAD_DIM, 2 * C_HEADS)

    aw = jnp.zeros((DEPTH, 2, 128, 128), F32)
    aw = aw.at[:, 0, 0:D_GATE_RANK].set(d_alpha_w[:, 0]).at[:, 1, D_GATE_RANK:2 * D_GATE_RANK].set(d_alpha_w[:, 1])
    gla_small = (b_lb_logits, aw.astype(BF), d_alpha_b.reshape(DEPTH, 2, 1, 128),
                 tile4(b_norm_g, B_HEADS), tile4(d_norm_g, D_HEADS)) + _gla_constants() + (
                     _state_pattern(B_HEADS, B_KEY_DIM, B_VAL_DIM), _state_pattern(D_HEADS, D_KEY_DIM, D_VAL_DIM))
    st_h = _pack_state(state_hgrn, B_KEY_DIM, B_VAL_DIM)
    st_d = _pack_state(state_gla, D_KEY_DIM, D_VAL_DIM)

    n1 = norm1_g.reshape(DEPTH, 1, D_MODEL)
    n2 = norm2_g.reshape(DEPTH, 1, D_MODEL)
    cb = conv_b.reshape(DEPTH, 1, 2 * D_FF)

    collected = [[] for _ in range(6)]
    for l in range(DEPTH):
        lam_init = 0.8 - 0.6 * math.exp(-0.3 * l)
        zac, zbd, h = _inproj(l, xs, mod, n1, w_in_t)
        y_p, ak, av, ck, cv = _attn_prompt(l, zac, gains, c_lambda, lam_init)
        y_s = _attn_sample(l, zac, gains, c_lambda, caches, tables, lam_init)
        g_p, sh, sd = _gla_prompt(l, zbd, gla_small)
        g_s = _gla_sample(l, zbd, gla_small, st_h, st_d)
        x1 = _mix(l, xs, h, y_p, y_s, g_p, g_s, mod, w_in_t, w_branch, w_out)
        xs = tuple(_ffn(l, x1, mod, n2, w_up, conv_w, cb, w_down, split_out=(l == DEPTH - 1)))
        for acc, arr in zip(collected, (
                ak.reshape(BATCH, SEQ, A_KV_HEADS, A_HEAD_DIM), av.reshape(BATCH, SEQ, A_KV_HEADS, A_HEAD_DIM),
                ck.reshape(BATCH, SEQ, C_HEADS, 2, C_HEAD_DIM), cv.reshape(BATCH, SEQ, C_HEADS, 2 * C_HEAD_DIM),
                _unpack_state(sh, B_HEADS, B_KEY_DIM, B_VAL_DIM), _unpack_state(sd, D_HEADS, D_KEY_DIM, D_VAL_DIM))):
            acc.append(arr)

    y_prompt = xs[0].reshape(BATCH, SEQ, D_MODEL)
    y_sample = xs[1].reshape(DEC_BATCH, DEC_SEQ, D_MODEL)
    return (y_prompt, y_sample) + tuple(jnp.stack(acc, axis=1) for acc in collected)
```

```python
import functools
import math

import numpy as np
import jax
import jax.numpy as jnp
from jax import lax
from jax.experimental import pallas as pl
from jax.experimental.pallas import tpu as pltpu

F32 = jnp.float32
BF = jnp.bfloat16

D_MODEL = 1024
BATCH = 16
SEQ = 256
DEPTH = 2
DEC_BATCH = 2
DEC_SEQ = 1024
PAST_LEN = 512
GRID_W = 64
ROPE_THETA = 10000.0
EPS = 1e-6
LB_FLOOR = 1e-30
N_MOD = 6
N_BRANCH = 4
BRANCH_W = 256
A_HEADS, A_KV_HEADS, A_HEAD_DIM = 4, 2, 64
C_HEADS, C_HEAD_DIM = 4, 32
B_HEADS, B_KEY_DIM, B_VAL_DIM = 4, 64, 64
D_HEADS, D_KEY_DIM, D_VAL_DIM = 4, 32, 64
D_GATE_RANK = 16
D_GATE_TAU = 16.0
D_FF = 2816
CONV_WIDTH = 3

N_PROMPT = BATCH * SEQ
N_SAMPLE = DEC_BATCH * DEC_SEQ
N_TOK = N_PROMPT + N_SAMPLE
TM = 256
N_TILES = N_TOK // TM
PROMPT_TILES = N_PROMPT // TM
SAMPLE_TILES_PER_SEQ = DEC_SEQ // TM
N_GROUPS = 1 + DEC_BATCH

AC_W = 1280
BD_W = 2080
BD_PAD = 2176
SMALL_W = AC_W + BD_PAD
MERGE_OFF = AC_W + BD_W
MERGE_ROWS = 560
HALO = 8
FF_CHUNK = 256
FF_LOOKAHEAD = 2
NPRO = 8
W_ROWS = D_MODEL // NPRO
LOG2E = 1.4426950408889634
VMEM_LIMIT = 56 * 1024 * 1024
TT = 128
N_LEVELS = 7


def _dot(a, b):
    return jnp.dot(a, b, preferred_element_type=F32)


def _dot_nt(a, b):
    return lax.dot_general(a, b, (((1,), (1,)), ((), ())), preferred_element_type=F32)


def _dot_tn(a, b):
    return lax.dot_general(a, b, (((0,), (0,)), ((), ())), preferred_element_type=F32)


def _silu(x):
    return x * jax.nn.sigmoid(x)


def _log_sigmoid(x):
    return jnp.minimum(x, 0.0) - jnp.log1p(jnp.exp(-jnp.abs(x)))


def _rms(x, g):
    return x * lax.rsqrt(jnp.mean(x * x, axis=-1, keepdims=True) + EPS) * g


def _head_rms(x, head_dim, g):
    w = x.shape[-1]
    sh = int(math.log2(head_dim))
    r = lax.shift_right_logical(lax.broadcasted_iota(jnp.int32, (w, w), 0), sh)
    c = lax.shift_right_logical(lax.broadcasted_iota(jnp.int32, (w, w), 1), sh)
    bd = jnp.where(r == c, 1.0 / head_dim, 0.0).astype(BF)
    x2 = x * x
    hi = x2.astype(BF)
    lo = (x2 - hi.astype(F32)).astype(BF)
    ms = _dot(hi, bd) + _dot(lo, bd)
    return x * lax.rsqrt(ms + EPS) * g


def _group_of_tile(i):
    return jnp.where(i < PROMPT_TILES, 0, 1 + jnp.maximum(i - PROMPT_TILES, 0) // SAMPLE_TILES_PER_SEQ)


def _params(sem):
    return pltpu.CompilerParams(dimension_semantics=sem, vmem_limit_bytes=VMEM_LIMIT)


def _mod_kernel(cond_ref, w_ref, b_ref, o_ref):
    s = _silu(cond_ref[...])
    o_ref[...] = _dot(s.astype(BF), w_ref[...].astype(BF)) + b_ref[...]


def _modulation(cond8, w_ada, b_ada):
    nb = 1536
    return pl.pallas_call(
        _mod_kernel,
        grid=(DEPTH, N_MOD * D_MODEL // nb),
        in_specs=[
            pl.BlockSpec((8, D_MODEL), lambda l, j: (0, 0)),
            pl.BlockSpec((None, D_MODEL, nb), lambda l, j: (l, 0, j)),
            pl.BlockSpec((None, 1, nb), lambda l, j: (l, 0, j)),
        ],
        out_specs=pl.BlockSpec((None, 8, nb), lambda l, j: (l, 0, j)),
        out_shape=jax.ShapeDtypeStruct((DEPTH, 8, N_MOD * D_MODEL), F32),
        compiler_params=_params(("arbitrary", "arbitrary")),
        name="modulation",
    )(cond8, w_ada, b_ada.reshape(DEPTH, 1, N_MOD * D_MODEL))


def _tile_of_step(i):
    return jnp.maximum(i - NPRO, 0)


def _chunk_of_step(i):
    return jnp.minimum(i, NPRO - 1)


def _tok_specs(width, split):
    if not split:
        return [pl.BlockSpec((TM, width), lambda i: (_tile_of_step(i), 0))]
    return [pl.BlockSpec((TM, width), lambda i: (jnp.minimum(_tile_of_step(i), PROMPT_TILES - 1), 0)),
            pl.BlockSpec((TM, width), lambda i: (jnp.maximum(_tile_of_step(i) - PROMPT_TILES, 0), 0))]


def _tok_load(t, refs):
    if len(refs) == 1:
        return refs[0][...]
    return jnp.where(t < PROMPT_TILES, refs[0][...], refs[1][...])


def _mod_spec(l):
    return pl.BlockSpec((None, None, N_MOD, D_MODEL), lambda i: (l, _group_of_tile(_tile_of_step(i)), 0, 0))


def _stage_rows(i, n):
    return pl.ds(pl.multiple_of(i * n, n), n)


def _inproj_kernel(*refs, n_x):
    x_refs = refs[:n_x]
    mod_ref, g_ref, w_ref, zac_ref, zbd_ref, h_ref, w_s = refs[n_x:]
    i = pl.program_id(0)

    @pl.when(i < NPRO)
    def _():
        w_s[_stage_rows(i, SMALL_W // NPRO), :] = w_ref[...].astype(BF)

    @pl.when(i >= NPRO)
    def _():
        m = mod_ref[...]
        h = _rms(_tok_load(i - NPRO, x_refs), g_ref[...]) * (1.0 + m[1:2]) + m[0:1]
        hb = h.astype(BF)
        h_ref[...] = hb
        zac_ref[...] = _dot_nt(hb, w_s[:AC_W, :])
        zbd_ref[...] = _dot_nt(hb, w_s[AC_W:, :])


def _inproj(l, xs, mod, norm_g, w_in_t):
    split = len(xs) == 2
    return pl.pallas_call(
        functools.partial(_inproj_kernel, n_x=len(xs)),
        grid=(NPRO + N_TILES,),
        in_specs=_tok_specs(D_MODEL, split) + [
            _mod_spec(l),
            pl.BlockSpec((None, 1, D_MODEL), lambda i: (l, 0, 0)),
            pl.BlockSpec((None, SMALL_W // NPRO, D_MODEL), lambda i: (l, _chunk_of_step(i), 0)),
        ],
        out_specs=[
            pl.BlockSpec((TM, AC_W), lambda i: (_tile_of_step(i), 0)),
            pl.BlockSpec((TM, BD_PAD), lambda i: (_tile_of_step(i), 0)),
            pl.BlockSpec((TM, D_MODEL), lambda i: (_tile_of_step(i), 0)),
        ],
        out_shape=[
            jax.ShapeDtypeStruct((N_TOK, AC_W), F32),
            jax.ShapeDtypeStruct((N_TOK, BD_PAD), F32),
            jax.ShapeDtypeStruct((N_TOK, D_MODEL), BF),
        ],
        scratch_shapes=[pltpu.VMEM((SMALL_W, D_MODEL), BF)],
        compiler_params=_params(("arbitrary",)),
        name=f"inproj{l}",
    )(*xs, mod, norm_g, w_in_t)


def _rope_tables(n_tokens, head_dim, n_rep):
    rows = n_tokens // GRID_W
    row = np.repeat(np.arange(rows), GRID_W).astype(np.float64)
    col = np.tile(np.arange(GRID_W), rows).astype(np.float64)
    half = head_dim // 2
    q4 = head_dim // 4
    freqs = ROPE_THETA ** (-np.arange(0, half, 2, dtype=np.float64) / half)
    ang_r = row[:, None] * freqs
    ang_c = col[:, None] * freqs
    ang = np.concatenate([ang_r, ang_r, ang_c, ang_c], axis=-1)
    cos, sin = np.cos(ang), np.sin(ang)
    first = (np.arange(head_dim) % (2 * q4)) < q4
    s_dn = np.where(first, -sin, 0.0)
    s_up = np.where(first, 0.0, sin)
    return tuple(jnp.asarray(np.tile(t, (1, n_rep)), dtype=F32) for t in (cos, s_dn, s_up))


def _rope(x, cos, s_dn, s_up, q4):
    w = x.shape[-1]
    return x * cos + pltpu.roll(x, w - q4, 1) * s_dn + pltpu.roll(x, q4, 1) * s_up


def _softmax_pv_group(maps):
    scores = [[_dot(q, k) if t else _dot_nt(q, k) for k, t in zip(ks, fm)] for q, ks, _, _, fm in maps]
    probs = []
    for ss in scores:
        m = ss[0].max(axis=-1, keepdims=True)
        for s in ss[1:]:
            m = jnp.maximum(m, s.max(axis=-1, keepdims=True))
        probs.append([jnp.exp2(s - m).astype(BF) for s in ss])
    outs = []
    for ps, (_, _, vexts, half, fm) in zip(probs, maps):
        o = None
        for p, v, t in zip(ps, vexts, fm):
            part = _dot_nt(p, v) if t else _dot(p, v)
            o = part if o is None else o + part
        outs.append(o[:, half * 64:(half + 1) * 64] / o[:, (1 - half) * 64:(1 - half) * 64 + 1])
    return outs


def _with_ones(v, half, feature_major):
    idx = lax.broadcasted_iota(jnp.int32, v.shape, 0 if feature_major else 1)
    return jnp.where(lax.shift_right_logical(idx, 6) == half, v, jnp.ones_like(v))


def _attend_heads(aq, cq, ka, va, kc, vc, fm, lam, gsub, lam_init, y_ref, group):
    aqb = (aq * (A_HEAD_DIM ** -0.5 * LOG2E)).astype(BF)
    cqb = (cq * (C_HEAD_DIM ** -0.5 * LOG2E)).astype(BF)
    rep = A_HEADS // A_KV_HEADS

    def feat(x, t, sl):
        return x[sl, :] if t else x[:, sl]

    maps = []
    for h in range(A_HEADS):
        g = h // rep
        sl = slice(g * A_HEAD_DIM, (g + 1) * A_HEAD_DIM)
        maps.append((aqb[:, h * A_HEAD_DIM:(h + 1) * A_HEAD_DIM], [feat(k, t, sl) for k, t in zip(ka, fm)],
                     [_with_ones(v, g, t) for v, t in zip(va, fm)], g, fm))
    for h in range(C_HEADS):
        slab = slice((h // 2) * 128, (h // 2 + 1) * 128)
        vh = [_with_ones(feat(v, t, slab), h % 2, t) for v, t in zip(vc, fm)]
        for j in range(2):
            sl = slice((2 * h + j) * C_HEAD_DIM, (2 * h + j + 1) * C_HEAD_DIM)
            maps.append((cqb[:, sl], [feat(k, t, sl) for k, t in zip(kc, fm)], vh, h % 2, fm))
    outs = []
    for i in range(0, len(maps), group):
        outs += _softmax_pv_group(maps[i:i + group])
    for h in range(A_HEADS):
        y_ref[:, h * A_HEAD_DIM:(h + 1) * A_HEAD_DIM] = outs[h]
    vd = 2 * C_HEAD_DIM
    for h in range(C_HEADS):
        d = outs[A_HEADS + 2 * h] - lam * outs[A_HEADS + 2 * h + 1]
        y_ref[:, BRANCH_W + h * vd:BRANCH_W + (h + 1) * vd] = _rms(d, gsub) * (1.0 - lam_init)


def _lambda(cl):
    s1 = jnp.sum(cl[0:1] * cl[1:2], axis=-1, keepdims=True)
    s2 = jnp.sum(cl[2:3] * cl[3:4], axis=-1, keepdims=True)
    return jnp.exp(s1) - jnp.exp(s2)


def _attn_prompt_kernel(z_ref, gaq, gak, gcq, gck, gsub, cl_ref, y_ref, oak, oav, ock, ocv, *, lam_init):
    z = z_ref[...]
    ak = _head_rms(z[:, 256:384], A_HEAD_DIM, gak[...])
    av = z[:, 384:512]
    ck = _head_rms(z[:, 768:1024], C_HEAD_DIM, gck[...])
    cv = z[:, 1024:1280]
    oak[...] = ak.T
    oav[...] = av.T
    ock[...] = ck.T
    ocv[...] = cv.T
    aq = _head_rms(z[:, 0:256], A_HEAD_DIM, gaq[...])
    cq = _head_rms(z[:, 512:768], C_HEAD_DIM, gcq[...])
    lam = _lambda(cl_ref[...]) + lam_init
    _attend_heads(aq, cq, [ak.astype(BF)], [av.astype(BF)], [ck.astype(BF)], [cv.astype(BF)], [False],
                  lam, gsub[...], lam_init, y_ref, group=A_HEADS + 2 * C_HEADS)


def _attn_sample_kernel(z_ref, gaq, gak, gcq, gck, gsub, cl_ref, cak, cav, cck, ccv,
                        cosa, sda, sua, cosc, sdc, suc, y_ref, ka_s, va_s, kc_s, vc_s, *, lam_init):
    qi = pl.program_id(1)
    qa4, qc4 = A_HEAD_DIM // 4, C_HEAD_DIM // 4

    @pl.when(qi == 0)
    def _():
        ak = _head_rms(z_ref[:, 256:384], A_HEAD_DIM, gak[...])
        ka_s[...] = _rope(ak, cosa[:, :128], sda[:, :128], sua[:, :128], qa4).astype(BF)
        va_s[...] = z_ref[:, 384:512].astype(BF)
        ck = _head_rms(z_ref[:, 768:1024], C_HEAD_DIM, gck[...])
        kc_s[...] = _rope(ck, cosc[...], sdc[...], suc[...], qc4).astype(BF)
        vc_s[...] = z_ref[:, 1024:1280].astype(BF)

    rows = pl.ds(pl.multiple_of(qi * TM, TM), TM)
    aq = _head_rms(z_ref[rows, 0:256], A_HEAD_DIM, gaq[...])
    aq = _rope(aq, cosa[rows, :], sda[rows, :], sua[rows, :], qa4)
    cq = _head_rms(z_ref[rows, 512:768], C_HEAD_DIM, gcq[...])
    cq = _rope(cq, cosc[rows, :], sdc[rows, :], suc[rows, :], qc4)
    lam = _lambda(cl_ref[...]) + lam_init
    _attend_heads(aq, cq,
                  [cak[...].astype(BF), ka_s[...]], [cav[...].astype(BF), va_s[...]],
                  [cck[...].astype(BF), kc_s[...]], [ccv[...].astype(BF), vc_s[...]], [True, False],
                  lam, gsub[...], lam_init, y_ref, group=4)


def _gain_specs(l, nd):
    zeros = (0,) * (nd - 1)
    widths = (256, 128, 256, 256, 64)
    return [pl.BlockSpec((None, 1, w), lambda *a: (l, 0, 0)) for w in widths] + \
           [pl.BlockSpec((None, 4, C_HEAD_DIM), lambda *a: (l, 0, 0))]


def _attn_prompt(l, zac, gains, c_lambda, lam_init):
    cache_w = (128, 128, 256, 256)
    return pl.pallas_call(
        functools.partial(_attn_prompt_kernel, lam_init=lam_init),
        grid=(BATCH,),
        in_specs=[pl.BlockSpec((SEQ, AC_W), lambda b: (b, 0))] + _gain_specs(l, 1),
        out_specs=[pl.BlockSpec((SEQ, 2 * BRANCH_W), lambda b: (b, 0))]
                  + [pl.BlockSpec((None, w, SEQ), lambda b: (b, 0, 0)) for w in cache_w],
        out_shape=[jax.ShapeDtypeStruct((N_PROMPT, 2 * BRANCH_W), F32)]
                  + [jax.ShapeDtypeStruct((BATCH, w, SEQ), F32) for w in cache_w],
        compiler_params=_params(("arbitrary",)),
        name=f"attn_prompt{l}",
    )(zac, *gains, c_lambda)


def _attn_sample(l, zac, gains, c_lambda, caches, tables, lam_init):
    first_blk = N_PROMPT // DEC_SEQ
    cache_specs = [pl.BlockSpec((None, None, w, PAST_LEN), lambda b, q: (b, l, 0, 0)) for w in (128, 128, 256, 256)]
    table_specs = [pl.BlockSpec((DEC_SEQ, 256), lambda b, q: (0, 0)) for _ in range(6)]
    return pl.pallas_call(
        functools.partial(_attn_sample_kernel, lam_init=lam_init),
        grid=(DEC_BATCH, DEC_SEQ // TM),
        in_specs=[pl.BlockSpec((DEC_SEQ, AC_W), lambda b, q: (first_blk + b, 0))] + _gain_specs(l, 2)
                 + cache_specs + table_specs,
        out_specs=pl.BlockSpec((TM, 2 * BRANCH_W), lambda b, q: (b * (DEC_SEQ // TM) + q, 0)),
        out_shape=jax.ShapeDtypeStruct((N_SAMPLE, 2 * BRANCH_W), F32),
        scratch_shapes=[pltpu.VMEM((DEC_SEQ, 128), BF), pltpu.VMEM((DEC_SEQ, 128), BF),
                        pltpu.VMEM((DEC_SEQ, 256), BF), pltpu.VMEM((DEC_SEQ, 256), BF)],
        compiler_params=_params(("arbitrary", "arbitrary")),
        name=f"attn_sample{l}",
    )(zac, *gains, c_lambda, *caches, *tables)


N_VTILES = BRANCH_W // 128


def _state_blocks(nh, kd, vd):
    per = 128 // vd
    return [(h // per, slice((h * kd) % 128, (h * kd) % 128 + kd), slice((h % per) * vd, (h % per + 1) * vd))
            for h in range(nh)]


_B_BLOCKS = _state_blocks(B_HEADS, B_KEY_DIM, B_VAL_DIM)
_D_BLOCKS = _state_blocks(D_HEADS, D_KEY_DIM, D_VAL_DIM)


def _gla_constants():
    idx = np.arange(TT)
    scans, scans_t, masks = [], [], []
    for rev in (False, True):
        eff = (TT - 1 - idx) if rev else idx
        et, eu = eff[:, None], eff[None, :]
        sc, mk = [], []
        for j in range(N_LEVELS):
            b = 1 << j
            start = et - et % b
            odd = (et // b) % 2 == 1
            sc.append(np.where(odd, (eu > start) & (eu <= et), (eu > et) & (eu <= start + b)))
            mk.append(((et // b) % 2 == 1) & (eu // b == et // b - 1))
        sc.append(eu <= et)
        sc.append(eu > et)
        mk.append(eu == et)
        scans.append(np.concatenate([np.stack(sc)] * 2, axis=-1))
        scans_t.append(np.swapaxes(scans[-1][:N_LEVELS], -1, -2))
        masks.append(np.stack(mk))
    return tuple(jnp.asarray(np.stack(t), BF) for t in (scans, scans_t, masks))


def _gla_prepare(q, k, v, la2, scan_ref, scan_t_ref, d, rev, use_state):
    tt = q.shape[0]
    la_hi = la2.astype(BF)
    la_lo = (la2 - la_hi.astype(F32)).astype(BF)
    la_split = jnp.concatenate([la_hi, la_lo], axis=0)

    def factor(i):
        return jnp.exp2(_dot(scan_ref[d, i], la_split))

    qs, ks = [], []
    for j in range(N_LEVELS):
        f = factor(j)
        qs.append((q * f).astype(BF))
        ks.append((k * f).astype(BF).T)
    qs.append(q.astype(BF))
    ks.append(k.astype(BF).T)
    k_out = (k * factor(N_LEVELS + 1)).astype(BF)
    vb = v.astype(BF)
    q_in = d_tile = None
    if use_state:
        q_in = (q * factor(N_LEVELS)).astype(BF)
        ones = jnp.ones((2 * tt, 128), BF)
        d_tile = [jnp.exp2(_dot_tn(la_split[:, i * 128:(i + 1) * 128], ones)) for i in range(q.shape[1] // 128)]

    return dict(qs=qs, ks=ks, k_out=k_out, vb=vb, q_in=q_in, d_tile=d_tile)


def _lane_keep(x, lo, hi):
    lane = lax.broadcasted_iota(jnp.int32, x.shape, 1)
    return jnp.where((lane >= lo) & (lane < hi), x, jnp.zeros_like(x))


def _gla_scores(p, mask_ref, d, nh, kd):
    out = []
    for h in range(nh):
        c, lo = divmod(h * kd, 128)
        sc = None
        for j in range(N_LEVELS + 1):
            kt = p["ks"][j]
            own = kt[c * 128 + lo:c * 128 + lo + kd, :]
            pieces = [jnp.zeros((lo, kt.shape[1]), BF)] if lo else []
            pieces.append(own)
            if lo + kd < 128:
                pieces.append(jnp.zeros((128 - lo - kd, kt.shape[1]), BF))
            kh = jnp.concatenate(pieces, axis=0) if len(pieces) > 1 else own
            t = _dot(p["qs"][j][:, c * 128:(c + 1) * 128], kh).astype(BF) * mask_ref[d, j]
            sc = t if sc is None else sc + t
        out.append(sc)
    return out


def _gla_outputs(p, scs, blk_ref, st_ref, d, nh, kd, vd, use_state, o_ref, rows, col0, accumulate):
    per = 128 // vd
    for c in range(nh // per):
        vt = p["vb"][:, c * 128:(c + 1) * 128]
        kt = (c * per * kd) // 128
        ktile = slice(kt * 128, (kt + 1) * 128)
        o = None
        for i in range(per):
            t = _dot(scs[c * per + i], _lane_keep(vt, i * vd, (i + 1) * vd))
            o = t if o is None else o + t
        kv = _dot_tn(p["k_out"][:, ktile], vt) * blk_ref[c]
        if use_state:
            st = st_ref[d, c]
            o = o + _dot(p["q_in"][:, ktile], st.astype(BF))
            st_ref[d, c] = st * p["d_tile"][kt] + kv
        else:
            st_ref[d, c] = kv
        osl = slice(col0 + c * 128, col0 + (c + 1) * 128)
        if accumulate:
            o_ref[rows, osl] += o
        else:
            o_ref[rows, osl] = o


def _gla_kernel(*refs, layer, n_tiles, has_state):
    if has_state:
        (z_ref, lbl_ref, aw_ref, ab_ref, bng, dng, scan_ref, scan_t_ref, mask_ref, blkb_ref, blkd_ref, sh_in, sd_in,
         y_ref, o_scr, sth, std) = refs
        for packed, raw, blocks in ((sth, sh_in, _B_BLOCKS), (std, sd_in, _D_BLOCKS)):
            packed[...] = jnp.zeros(packed.shape, F32)
            for dd in range(2):
                for h, (c, r, ln) in enumerate(blocks):
                    packed[dd, c, r, ln] = raw[dd, h]
    else:
        (z_ref, lbl_ref, aw_ref, ab_ref, bng, dng, scan_ref, scan_t_ref, mask_ref, blkb_ref, blkd_ref,
         y_ref, sh_out, sd_out, o_scr, sth, std) = refs

    gates = []
    for d in range(2):
        logits = [lbl_ref[d, i:i + 1, :] for i in range(DEPTH)]
        mx = functools.reduce(jnp.maximum, logits)
        ex = [jnp.exp(t - mx) for t in logits]
        den = functools.reduce(lambda a, b: a + b, ex)
        ps = [t / den for t in ex]
        lb = functools.reduce(lambda a, b: a + b, ps[:layer + 1]) - ps[0]
        gates.append((lb, jnp.log(jnp.maximum(lb, LB_FLOOR)), jnp.log1p(-lb)))

    o_scr[...] = jnp.zeros(o_scr.shape, F32)

    def tile(i, use_state):
        preps = []
        for d in range(2):
            rev = d == 1
            lb, log_lb, log_1m = gates[d]
            j = (n_tiles - 1 - i) if rev else i
            rows = pl.ds(j * TT if isinstance(j, int) else pl.multiple_of(j * TT, TT), TT)
            bq = z_ref[rows, 0:256]
            zf = z_ref[rows, 768:1024] if rev else z_ref[rows, 512:768]
            b2 = log_1m + _log_sigmoid(zf)
            la = jnp.maximum(log_lb, b2) + jnp.log1p(jnp.exp(-jnp.abs(log_lb - b2)))
            kb = (1.0 - lb) * jax.nn.sigmoid(-zf)
            pre = _dot(z_ref[rows, 2048:2176].astype(BF), aw_ref[d]) + ab_ref[d]
            la_d = _log_sigmoid(pre) * (LOG2E / D_GATE_TAU)
            pb = _gla_prepare(_silu(bq), kb, z_ref[rows, 256:512], la * LOG2E, scan_ref, scan_t_ref, d, rev, use_state)
            pd = _gla_prepare(z_ref[rows, 1280:1408] * (D_KEY_DIM ** -0.5), z_ref[rows, 1408:1536],
                              z_ref[rows, 1536:1792], la_d, scan_ref, scan_t_ref, d, rev, use_state)
            preps.append((d, rows, pb, pd))
        scores = [(_gla_scores(pb, mask_ref, d, B_HEADS, B_KEY_DIM), _gla_scores(pd, mask_ref, d, D_HEADS, D_KEY_DIM))
                  for d, _, pb, pd in preps]
        for (d, rows, pb, pd), (sb, sd) in zip(preps, scores):
            _gla_outputs(pb, sb, blkb_ref, sth, d, B_HEADS, B_KEY_DIM, B_VAL_DIM, use_state, o_scr, rows, 0, True)
            _gla_outputs(pd, sd, blkd_ref, std, d, D_HEADS, D_KEY_DIM, D_VAL_DIM, use_state, o_scr, rows, 256, True)

    if has_state:
        def body(i, carry):
            tile(i, True)
            return carry

        lax.fori_loop(0, n_tiles, body, 0)
    else:
        for i in range(n_tiles):
            tile(i, i > 0)

    def finish(i, carry):
        rows = pl.ds(pl.multiple_of(i * TM, TM), TM)
        y_ref[rows, 0:256] = _head_rms(o_scr[rows, 0:256], B_VAL_DIM, bng[...]) * _silu(z_ref[rows, 1024:1280])
        y_ref[rows, 256:512] = _head_rms(o_scr[rows, 256:512], D_VAL_DIM, dng[...]) * _silu(z_ref[rows, 1792:2048])
        return carry

    lax.fori_loop(0, (n_tiles * TT) // TM, finish, 0)
    if not has_state:
        for packed, raw, blocks in ((sth, sh_out, _B_BLOCKS), (std, sd_out, _D_BLOCKS)):
            for dd in range(2):
                for h, (c, r, ln) in enumerate(blocks):
                    raw[dd, h] = packed[dd, c, r, ln]


def _gla_common_specs(l, nd):
    return [
        pl.BlockSpec((2, DEPTH, 256), lambda *a: (0, 0, 0)),
        pl.BlockSpec((None, 2, 128, 128), lambda *a: (l, 0, 0, 0)),
        pl.BlockSpec((None, 2, 1, 128), lambda *a: (l, 0, 0, 0)),
        pl.BlockSpec((None, 1, 256), lambda *a: (l, 0, 0)),
        pl.BlockSpec((None, 1, 256), lambda *a: (l, 0, 0)),
        pl.BlockSpec((2, N_LEVELS + 2, TT, 2 * TT), lambda *a: (0, 0, 0, 0)),
        pl.BlockSpec((2, N_LEVELS, 2 * TT, TT), lambda *a: (0, 0, 0, 0)),
        pl.BlockSpec((2, N_LEVELS + 1, TT, TT), lambda *a: (0, 0, 0, 0)),
        pl.BlockSpec((N_VTILES, 128, 128), lambda *a: (0, 0, 0)),
        pl.BlockSpec((N_VTILES, 128, 128), lambda *a: (0, 0, 0)),
    ]


_STATE_SCRATCH = [pltpu.VMEM((2, N_VTILES, 128, 128), F32), pltpu.VMEM((2, N_VTILES, 128, 128), F32)]


def _state_pattern(blocks):
    pat = np.zeros((N_VTILES, 128, 128), np.float32)
    for c, r, ln in blocks:
        pat[c, r, ln] = 1.0
    return jnp.asarray(pat)


def _gla_prompt(l, zbd, small):
    raw_b = (BATCH, 2, B_HEADS, B_KEY_DIM, B_VAL_DIM)
    raw_d = (BATCH, 2, D_HEADS, D_KEY_DIM, D_VAL_DIM)
    return pl.pallas_call(
        functools.partial(_gla_kernel, layer=l, n_tiles=SEQ // TT, has_state=False),
        grid=(BATCH,),
        in_specs=[pl.BlockSpec((SEQ, BD_PAD), lambda b: (b, 0))] + _gla_common_specs(l, 1),
        out_specs=[
            pl.BlockSpec((SEQ, 2 * BRANCH_W), lambda b: (b, 0)),
            pl.BlockSpec((None,) + raw_b[1:], lambda b: (b, 0, 0, 0, 0)),
            pl.BlockSpec((None,) + raw_d[1:], lambda b: (b, 0, 0, 0, 0)),
        ],
        out_shape=[
            jax.ShapeDtypeStruct((N_PROMPT, 2 * BRANCH_W), F32),
            jax.ShapeDtypeStruct(raw_b, F32),
            jax.ShapeDtypeStruct(raw_d, F32),
        ],
        scratch_shapes=[pltpu.VMEM((SEQ, 2 * BRANCH_W), F32)] + _STATE_SCRATCH,
        compiler_params=_params(("arbitrary",)),
        name=f"gla_prompt{l}",
    )(zbd, *small)


def _gla_sample(l, zbd, small, st_h, st_d):
    first_blk = N_PROMPT // DEC_SEQ
    return pl.pallas_call(
        functools.partial(_gla_kernel, layer=l, n_tiles=DEC_SEQ // TT, has_state=True),
        grid=(DEC_BATCH,),
        in_specs=[pl.BlockSpec((DEC_SEQ, BD_PAD), lambda b: (first_blk + b, 0))] + _gla_common_specs(l, 1) + [
            pl.BlockSpec((None, None, 2, B_HEADS, B_KEY_DIM, B_VAL_DIM), lambda b: (b, l, 0, 0, 0, 0)),
            pl.BlockSpec((None, None, 2, D_HEADS, D_KEY_DIM, D_VAL_DIM), lambda b: (b, l, 0, 0, 0, 0)),
        ],
        out_specs=pl.BlockSpec((DEC_SEQ, 2 * BRANCH_W), lambda b: (b, 0)),
        out_shape=jax.ShapeDtypeStruct((N_SAMPLE, 2 * BRANCH_W), F32),
        scratch_shapes=[pltpu.VMEM((DEC_SEQ, 2 * BRANCH_W), F32)] + _STATE_SCRATCH,
        compiler_params=_params(("arbitrary",)),
        name=f"gla_sample{l}",
    )(zbd, *small, st_h, st_d)


def _mix_kernel(*refs, n_x):
    x_refs = refs[:n_x]
    (h_ref, yp_ref, ys_ref, gp_ref, gs_ref, mod_ref, win_ref, wb_ref, wo_ref, o_ref, wm_s, wb_s, wo_s) = refs[n_x:]
    i = pl.program_id(0)

    @pl.when(i < NPRO)
    def _():
        rows = _stage_rows(i, W_ROWS)
        wm_s[_stage_rows(i, MERGE_ROWS), :] = win_ref[...].astype(BF)
        wb_s[rows, :] = wb_ref[...].astype(BF)
        wo_s[rows, :] = wo_ref[...].astype(BF)

    @pl.when(i >= NPRO)
    def _():
        t = i - NPRO
        hb = h_ref[...]
        yac = _tok_load(t, (yp_ref, ys_ref))
        ybd = _tok_load(t, (gp_ref, gs_ref))
        branches = (yac[:, :BRANCH_W], ybd[:, :BRANCH_W], yac[:, BRANCH_W:], ybd[:, BRANCH_W:])
        mixed = None
        for n, y in enumerate(branches):
            logits = _dot_nt(hb, wm_s[n * D_MODEL:(n + 1) * D_MODEL, :])
            term = jax.nn.sigmoid(logits) * _dot(y.astype(BF), wb_s[n * BRANCH_W:(n + 1) * BRANCH_W, :])
            mixed = term if mixed is None else mixed + term
        o_ref[...] = _tok_load(t, x_refs) + mod_ref[2:3, :] * _dot(mixed.astype(BF), wo_s[...])


def _mix(l, xs, h, y_p, y_s, g_p, g_s, mod, w_in_t, w_branch, w_out):
    chunk = pl.BlockSpec((None, W_ROWS, D_MODEL), lambda i: (l, _chunk_of_step(i), 0))
    return pl.pallas_call(
        functools.partial(_mix_kernel, n_x=len(xs)),
        grid=(NPRO + N_TILES,),
        in_specs=_tok_specs(D_MODEL, len(xs) == 2) + _tok_specs(D_MODEL, False)
                 + _tok_specs(2 * BRANCH_W, True) + _tok_specs(2 * BRANCH_W, True)
                 + [_mod_spec(l),
                    pl.BlockSpec((None, MERGE_ROWS, D_MODEL), lambda i: (l, MERGE_OFF // MERGE_ROWS + _chunk_of_step(i), 0)),
                    chunk, chunk],
        out_specs=pl.BlockSpec((TM, D_MODEL), lambda i: (_tile_of_step(i), 0)),
        out_shape=jax.ShapeDtypeStruct((N_TOK, D_MODEL), F32),
        scratch_shapes=[pltpu.VMEM((NPRO * MERGE_ROWS, D_MODEL), BF), pltpu.VMEM((D_MODEL, D_MODEL), BF),
                        pltpu.VMEM((D_MODEL, D_MODEL), BF)],
        compiler_params=_params(("arbitrary",)),
        name=f"mix{l}",
    )(*xs, h, y_p, y_s, g_p, g_s, mod, w_in_t, w_branch.reshape(DEPTH, N_BRANCH * BRANCH_W, D_MODEL), w_out)


def _ffn_kernel(*refs, n_out):
    (xp_ref, x_ref, xn_ref, mod_ref, g_ref, wup_ref, cw_ref, cb_ref, wdn_ref) = refs[:9]
    o_refs = refs[9:9 + n_out]
    hext, u_s, wup_s, wdn_s = refs[9 + n_out:]
    i = pl.program_id(0)

    @pl.when(i < NPRO)
    def _():
        wup_s[_stage_rows(i, W_ROWS), :] = wup_ref[...].astype(BF)
        wdn_s[_stage_rows(i, D_FF // NPRO), :] = wdn_ref[...].astype(BF)

    @pl.when(i >= NPRO)
    def _():
        t = i - NPRO
        pos = jnp.maximum(t - PROMPT_TILES, 0) % SAMPLE_TILES_PER_SEQ
        seq_first = (t < PROMPT_TILES) | (pos == 0)
        seq_last = (t < PROMPT_TILES) | (pos == SAMPLE_TILES_PER_SEQ - 1)
        m = mod_ref[...]
        g = g_ref[...]

        def pre(x):
            return _rms(x, g) * (1.0 + m[4:5]) + m[3:4]

        hext[0:HALO, :] = jnp.where(seq_first, 0.0, pre(xp_ref[...]))
        hext[HALO:HALO + TM, :] = pre(x_ref[...])
        hext[HALO + TM:, :] = jnp.where(seq_last, 0.0, pre(xn_ref[...]))
        hb = hext[...].astype(BF)

        def up_conv_act(c):
            halves = []
            for k, off in enumerate((0, D_FF)):
                cols = slice(off + c * FF_CHUNK, off + (c + 1) * FF_CHUNK)
                u = u_s.at[(2 * c + k) % u_s.shape[0]]
                u[...] = _dot(hb, wup_s[:, cols])
                halves.append(u[HALO - 1:HALO - 1 + TM, :] * cw_ref[0:1, cols] + u[HALO:HALO + TM, :] * cw_ref[1:2, cols]
                              + u[HALO + 1:HALO + 1 + TM, :] * cw_ref[2:3, cols] + cb_ref[:, cols])
            return (_silu(halves[1]) * halves[0]).astype(BF)

        n_chunks = D_FF // FF_CHUNK
        acts = [up_conv_act(c) for c in range(FF_LOOKAHEAD)]
        acc = None
        for c in range(n_chunks):
            if c + FF_LOOKAHEAD < n_chunks:
                acts.append(up_conv_act(c + FF_LOOKAHEAD))
            part = _dot(acts.pop(0), wdn_s[c * FF_CHUNK:(c + 1) * FF_CHUNK, :])
            acc = part if acc is None else acc + part
        res = x_ref[...] + m[5:6] * acc
        if n_out == 1:
            o_refs[0][...] = res
        else:
            @pl.when(t < PROMPT_TILES)
            def _():
                o_refs[0][...] = res

            @pl.when(t >= PROMPT_TILES)
            def _():
                o_refs[1][...] = res


def _ffn(l, x, mod, norm_g, w_up, conv_w, conv_b, w_down, split_out):
    per = TM // HALO
    last_blk = N_TOK // HALO - 1
    if split_out:
        out_specs = [pl.BlockSpec((TM, D_MODEL), lambda i: (jnp.minimum(_tile_of_step(i), PROMPT_TILES - 1), 0)),
                     pl.BlockSpec((TM, D_MODEL), lambda i: (jnp.maximum(_tile_of_step(i) - PROMPT_TILES, 0), 0))]
        out_shape = [jax.ShapeDtypeStruct((N_PROMPT, D_MODEL), F32), jax.ShapeDtypeStruct((N_SAMPLE, D_MODEL), F32)]
    else:
        out_specs = [pl.BlockSpec((TM, D_MODEL), lambda i: (_tile_of_step(i), 0))]
        out_shape = [jax.ShapeDtypeStruct((N_TOK, D_MODEL), F32)]
    return pl.pallas_call(
        functools.partial(_ffn_kernel, n_out=len(out_specs)),
        grid=(NPRO + N_TILES,),
        in_specs=[
            pl.BlockSpec((HALO, D_MODEL), lambda i: (jnp.maximum(_tile_of_step(i) * per - 1, 0), 0)),
            pl.BlockSpec((TM, D_MODEL), lambda i: (_tile_of_step(i), 0)),
            pl.BlockSpec((HALO, D_MODEL), lambda i: (jnp.minimum((_tile_of_step(i) + 1) * per, last_blk), 0)),
            _mod_spec(l),
            pl.BlockSpec((None, 1, D_MODEL), lambda i: (l, 0, 0)),
            pl.BlockSpec((None, W_ROWS, 2 * D_FF), lambda i: (l, _chunk_of_step(i), 0)),
            pl.BlockSpec((None, CONV_WIDTH, 2 * D_FF), lambda i: (l, 0, 0)),
            pl.BlockSpec((None, 1, 2 * D_FF), lambda i: (l, 0, 0)),
            pl.BlockSpec((None, D_FF // NPRO, D_MODEL), lambda i: (l, _chunk_of_step(i), 0)),
        ],
        out_specs=out_specs,
        out_shape=out_shape,
        scratch_shapes=[pltpu.VMEM((TM + 2 * HALO, D_MODEL), F32), pltpu.VMEM((2 * (FF_LOOKAHEAD + 1), TM + 2 * HALO, FF_CHUNK), F32),
                        pltpu.VMEM((D_MODEL, 2 * D_FF), BF), pltpu.VMEM((D_FF, D_MODEL), BF)],
        compiler_params=_params(("arbitrary",)),
        name=f"ffn{l}",
    )(x, x, x, mod, norm_g, w_up, conv_w, conv_b, w_down)


def kernel(x_prompt, x_sample, cache_a_k, cache_a_v, cache_c_k, cache_c_v, state_hgrn, state_gla, c, c_ctx, w_ada, b_ada, norm1_g, norm2_g, w_in, a_qn_g, a_kn_g, c_qn_g, c_kn_g, c_lambda, c_subln_g, b_lb_logits, b_norm_g, d_alpha_w, d_alpha_b, d_norm_g, w_branch, w_out, w_up, conv_w, conv_b, w_down):
    xs = (x_prompt.reshape(N_PROMPT, D_MODEL), x_sample.reshape(N_SAMPLE, D_MODEL))
    w_in_t = jnp.swapaxes(w_in, 1, 2)

    cond8 = jnp.concatenate([c_ctx[None, :], c, jnp.zeros((8 - N_GROUPS, D_MODEL), F32)], axis=0)
    mod = _modulation(cond8, w_ada, b_ada)[:, :N_GROUPS].reshape(DEPTH, N_GROUPS, N_MOD, D_MODEL)

    tile4 = lambda g, n: jnp.tile(g, (1, n)).reshape(DEPTH, 1, -1)
    gains = (tile4(a_qn_g, A_HEADS), tile4(a_kn_g, A_KV_HEADS), tile4(c_qn_g, 2 * C_HEADS), tile4(c_kn_g, 2 * C_HEADS),
             c_subln_g.reshape(DEPTH, 1, 2 * C_HEAD_DIM))
    fmaj = lambda t: jnp.moveaxis(t, 2, -1).reshape(DEC_BATCH, DEPTH, -1, PAST_LEN)
    caches = tuple(fmaj(t) for t in (cache_a_k, cache_a_v, cache_c_k, cache_c_v))
    tables = _rope_tables(DEC_SEQ, A_HEAD_DIM, A_HEADS) + _rope_tables(DEC_SEQ, C_HEAD_DIM, 2 * C_HEADS)

    aw = jnp.zeros((DEPTH, 2, 128, 128), F32)
    aw = aw.at[:, 0, 0:D_GATE_RANK].set(d_alpha_w[:, 0]).at[:, 1, D_GATE_RANK:2 * D_GATE_RANK].set(d_alpha_w[:, 1])
    gla_small = (b_lb_logits, aw.astype(BF), d_alpha_b.reshape(DEPTH, 2, 1, 128),
                 tile4(b_norm_g, B_HEADS), tile4(d_norm_g, D_HEADS)) + _gla_constants() + (
                     _state_pattern(_B_BLOCKS), _state_pattern(_D_BLOCKS))

    n1 = norm1_g.reshape(DEPTH, 1, D_MODEL)
    n2 = norm2_g.reshape(DEPTH, 1, D_MODEL)
    cb = conv_b.reshape(DEPTH, 1, 2 * D_FF)

    collected = [[] for _ in range(6)]
    for l in range(DEPTH):
        lam_init = 0.8 - 0.6 * math.exp(-0.3 * l)
        zac, zbd, h = _inproj(l, xs, mod, n1, w_in_t)
        y_p, ak, av, ck, cv = _attn_prompt(l, zac, gains, c_lambda, lam_init)
        y_s = _attn_sample(l, zac, gains, c_lambda, caches, tables, lam_init)
        g_p, sh, sd = _gla_prompt(l, zbd, gla_small)
        g_s = _gla_sample(l, zbd, gla_small, state_hgrn, state_gla)
        x1 = _mix(l, xs, h, y_p, y_s, g_p, g_s, mod, w_in_t, w_branch, w_out)
        xs = tuple(_ffn(l, x1, mod, n2, w_up, conv_w, cb, w_down, split_out=(l == DEPTH - 1)))
        for acc, arr in zip(collected, (ak, av, ck, cv, sh, sd)):
            acc.append(arr)

    y_prompt = xs[0].reshape(BATCH, SEQ, D_MODEL)
    y_sample = xs[1].reshape(DEC_BATCH, DEC_SEQ, D_MODEL)
    stacked = [jnp.stack(acc, axis=1) for acc in collected]
    feat_shapes = ((A_KV_HEADS, A_HEAD_DIM), (A_KV_HEADS, A_HEAD_DIM), (C_HEADS, 2, C_HEAD_DIM), (C_HEADS, 2 * C_HEAD_DIM))
    ctx = [jnp.moveaxis(t.reshape((BATCH, DEPTH) + fs + (SEQ,)), -1, 2) for t, fs in zip(stacked[:4], feat_shapes)]
    return (y_prompt, y_sample) + tuple(ctx) + tuple(stacked[4:])
```

```python
import functools
import math

import numpy as np
import jax
import jax.numpy as jnp
from jax import lax
from jax.experimental import pallas as pl
from jax.experimental.pallas import tpu as pltpu

F32 = jnp.float32
BF = jnp.bfloat16

D_MODEL = 1024
BATCH = 16
SEQ = 256
DEPTH = 2
DEC_BATCH = 2
DEC_SEQ = 1024
PAST_LEN = 512
GRID_W = 64
ROPE_THETA = 10000.0
EPS = 1e-6
LB_FLOOR = 1e-30
N_MOD = 6
N_BRANCH = 4
BRANCH_W = 256
A_HEADS, A_KV_HEADS, A_HEAD_DIM = 4, 2, 64
C_HEADS, C_HEAD_DIM = 4, 32
B_HEADS, B_KEY_DIM, B_VAL_DIM = 4, 64, 64
D_HEADS, D_KEY_DIM, D_VAL_DIM = 4, 32, 64
D_GATE_RANK = 16
D_GATE_TAU = 16.0
D_FF = 2816
CONV_WIDTH = 3

N_PROMPT = BATCH * SEQ
N_SAMPLE = DEC_BATCH * DEC_SEQ
N_TOK = N_PROMPT + N_SAMPLE
TM = 256
TD = 2 * SEQ
N_TILES = N_TOK // TD
PROMPT_TILES = N_PROMPT // TD
SAMPLE_TILES_PER_SEQ = DEC_SEQ // TD
N_GROUPS = 1 + DEC_BATCH

AC_W = 1280
BD_W = 2080
BD_PAD = 2176
SMALL_W = AC_W + BD_PAD
MERGE_OFF = AC_W + BD_W
MERGE_ROWS = 560
HALO = 8
FF_A0 = HALO
FF_B0 = FF_A0 + SEQ + HALO
FF_ROWS = 2 * SEQ + 2 * HALO
FF_EXT = FF_A0 + FF_ROWS + HALO
FF_CHUNK = 256
FF_LOOKAHEAD = 2
NPRO = 8
W_ROWS = D_MODEL // NPRO
LOG2E = 1.4426950408889634
VMEM_LIMIT = 56 * 1024 * 1024
TT = 128
N_LEVELS = 7


def _dot(a, b):
    return jnp.dot(a, b, preferred_element_type=F32)


def _dot_nt(a, b):
    return lax.dot_general(a, b, (((1,), (1,)), ((), ())), preferred_element_type=F32)


def _dot_tn(a, b):
    return lax.dot_general(a, b, (((0,), (0,)), ((), ())), preferred_element_type=F32)


def _silu(x):
    return x * jax.nn.sigmoid(x)


def _log_sigmoid(x):
    return jnp.minimum(x, 0.0) - jnp.log1p(jnp.exp(-jnp.abs(x)))


def _rms(x, g):
    return x * lax.rsqrt(jnp.mean(x * x, axis=-1, keepdims=True) + EPS) * g


def _head_rms(x, head_dim, g):
    w = x.shape[-1]
    sh = int(math.log2(head_dim))
    r = lax.shift_right_logical(lax.broadcasted_iota(jnp.int32, (w, w), 0), sh)
    c = lax.shift_right_logical(lax.broadcasted_iota(jnp.int32, (w, w), 1), sh)
    bd = jnp.where(r == c, 1.0 / head_dim, 0.0).astype(BF)
    x2 = x * x
    hi = x2.astype(BF)
    lo = (x2 - hi.astype(F32)).astype(BF)
    ms = _dot(hi, bd) + _dot(lo, bd)
    return x * lax.rsqrt(ms + EPS) * g


def _group_of_tile(i):
    return jnp.where(i < PROMPT_TILES, 0, 1 + jnp.maximum(i - PROMPT_TILES, 0) // SAMPLE_TILES_PER_SEQ)


def _params(sem):
    return pltpu.CompilerParams(dimension_semantics=sem, vmem_limit_bytes=VMEM_LIMIT)


def _mod_kernel(cond_ref, w_ref, b_ref, o_ref):
    s = _silu(cond_ref[...])
    o_ref[...] = _dot(s.astype(BF), w_ref[...].astype(BF)) + b_ref[...]


def _modulation(cond8, w_ada, b_ada):
    nb = 1536
    return pl.pallas_call(
        _mod_kernel,
        grid=(DEPTH, N_MOD * D_MODEL // nb),
        in_specs=[
            pl.BlockSpec((8, D_MODEL), lambda l, j: (0, 0)),
            pl.BlockSpec((None, D_MODEL, nb), lambda l, j: (l, 0, j)),
            pl.BlockSpec((None, 1, nb), lambda l, j: (l, 0, j)),
        ],
        out_specs=pl.BlockSpec((None, 8, nb), lambda l, j: (l, 0, j)),
        out_shape=jax.ShapeDtypeStruct((DEPTH, 8, N_MOD * D_MODEL), F32),
        compiler_params=_params(("arbitrary", "arbitrary")),
        name="modulation",
    )(cond8, w_ada, b_ada.reshape(DEPTH, 1, N_MOD * D_MODEL))


def _tile_of_step(i):
    return jnp.maximum(i - NPRO, 0)


def _chunk_of_step(i):
    return jnp.minimum(i, NPRO - 1)


def _tok_specs(width, split):
    if not split:
        return [pl.BlockSpec((TD, width), lambda i: (_tile_of_step(i), 0))]
    return [pl.BlockSpec((TD, width), lambda i: (jnp.minimum(_tile_of_step(i), PROMPT_TILES - 1), 0)),
            pl.BlockSpec((TD, width), lambda i: (jnp.maximum(_tile_of_step(i) - PROMPT_TILES, 0), 0))]


def _tok_load(t, refs):
    if len(refs) == 1:
        return refs[0][...]
    return jnp.where(t < PROMPT_TILES, refs[0][...], refs[1][...])


def _mod_spec(l):
    return pl.BlockSpec((None, None, N_MOD, D_MODEL), lambda i: (l, _group_of_tile(_tile_of_step(i)), 0, 0))


def _stage_rows(i, n):
    return pl.ds(pl.multiple_of(i * n, n), n)


def _inproj_kernel(*refs, n_x):
    x_refs = refs[:n_x]
    mod_ref, g_ref, w_ref, zac_ref, zbd_ref, h_ref, w_s = refs[n_x:]
    i = pl.program_id(0)

    @pl.when(i < NPRO)
    def _():
        w_s[_stage_rows(i, SMALL_W // NPRO), :] = w_ref[...].astype(BF)

    @pl.when(i >= NPRO)
    def _():
        m = mod_ref[...]
        h = _rms(_tok_load(i - NPRO, x_refs), g_ref[...]) * (1.0 + m[1:2]) + m[0:1]
        hb = h.astype(BF)
        h_ref[...] = hb
        zac_ref[...] = _dot_nt(hb, w_s[:AC_W, :])
        zbd_ref[...] = _dot_nt(hb, w_s[AC_W:, :])


def _inproj(l, xs, mod, norm_g, w_in_t):
    split = len(xs) == 2
    return pl.pallas_call(
        functools.partial(_inproj_kernel, n_x=len(xs)),
        grid=(NPRO + N_TILES,),
        in_specs=_tok_specs(D_MODEL, split) + [
            _mod_spec(l),
            pl.BlockSpec((None, 1, D_MODEL), lambda i: (l, 0, 0)),
            pl.BlockSpec((None, SMALL_W // NPRO, D_MODEL), lambda i: (l, _chunk_of_step(i), 0)),
        ],
        out_specs=[
            pl.BlockSpec((TD, AC_W), lambda i: (_tile_of_step(i), 0)),
            pl.BlockSpec((TD, BD_PAD), lambda i: (_tile_of_step(i), 0)),
            pl.BlockSpec((TD, D_MODEL), lambda i: (_tile_of_step(i), 0)),
        ],
        out_shape=[
            jax.ShapeDtypeStruct((N_TOK, AC_W), F32),
            jax.ShapeDtypeStruct((N_TOK, BD_PAD), F32),
            jax.ShapeDtypeStruct((N_TOK, D_MODEL), BF),
        ],
        scratch_shapes=[pltpu.VMEM((SMALL_W, D_MODEL), BF)],
        compiler_params=_params(("arbitrary",)),
        name=f"inproj{l}",
    )(*xs, mod, norm_g, w_in_t)


def _rope_tables(n_tokens, head_dim, n_rep):
    rows = n_tokens // GRID_W
    row = np.repeat(np.arange(rows), GRID_W).astype(np.float64)
    col = np.tile(np.arange(GRID_W), rows).astype(np.float64)
    half = head_dim // 2
    q4 = head_dim // 4
    freqs = ROPE_THETA ** (-np.arange(0, half, 2, dtype=np.float64) / half)
    ang_r = row[:, None] * freqs
    ang_c = col[:, None] * freqs
    ang = np.concatenate([ang_r, ang_r, ang_c, ang_c], axis=-1)
    cos, sin = np.cos(ang), np.sin(ang)
    first = (np.arange(head_dim) % (2 * q4)) < q4
    s_dn = np.where(first, -sin, 0.0)
    s_up = np.where(first, 0.0, sin)
    return tuple(jnp.asarray(np.tile(t, (1, n_rep)), dtype=F32) for t in (cos, s_dn, s_up))


def _rope(x, cos, s_dn, s_up, q4):
    w = x.shape[-1]
    return x * cos + pltpu.roll(x, w - q4, 1) * s_dn + pltpu.roll(x, q4, 1) * s_up


def _softmax_pv_group(maps):
    scores = [[_dot(q, k) if t else _dot_nt(q, k) for k, t in zip(ks, fm)] for q, ks, _, _, fm in maps]
    probs = []
    for ss in scores:
        m = ss[0].max(axis=-1, keepdims=True)
        for s in ss[1:]:
            m = jnp.maximum(m, s.max(axis=-1, keepdims=True))
        probs.append([jnp.exp2(s - m).astype(BF) for s in ss])
    outs = []
    for ps, (_, _, vexts, half, fm) in zip(probs, maps):
        o = None
        for p, v, t in zip(ps, vexts, fm):
            part = _dot_nt(p, v) if t else _dot(p, v)
            o = part if o is None else o + part
        outs.append(o[:, half * 64:(half + 1) * 64] / o[:, (1 - half) * 64:(1 - half) * 64 + 1])
    return outs


def _with_ones(v, half, feature_major):
    idx = lax.broadcasted_iota(jnp.int32, v.shape, 0 if feature_major else 1)
    return jnp.where(lax.shift_right_logical(idx, 6) == half, v, jnp.ones_like(v))


def _attend_heads(aq, cq, ka, va, kc, vc, fm, lam, gsub, lam_init, y_ref, group):
    aqb = (aq * (A_HEAD_DIM ** -0.5 * LOG2E)).astype(BF)
    cqb = (cq * (C_HEAD_DIM ** -0.5 * LOG2E)).astype(BF)
    rep = A_HEADS // A_KV_HEADS

    def feat(x, t, sl):
        return x[sl, :] if t else x[:, sl]

    maps = []
    for h in range(A_HEADS):
        g = h // rep
        sl = slice(g * A_HEAD_DIM, (g + 1) * A_HEAD_DIM)
        maps.append((aqb[:, h * A_HEAD_DIM:(h + 1) * A_HEAD_DIM], [feat(k, t, sl) for k, t in zip(ka, fm)],
                     [_with_ones(v, g, t) for v, t in zip(va, fm)], g, fm))
    for h in range(C_HEADS):
        slab = slice((h // 2) * 128, (h // 2 + 1) * 128)
        vh = [_with_ones(feat(v, t, slab), h % 2, t) for v, t in zip(vc, fm)]
        for j in range(2):
            sl = slice((2 * h + j) * C_HEAD_DIM, (2 * h + j + 1) * C_HEAD_DIM)
            maps.append((cqb[:, sl], [feat(k, t, sl) for k, t in zip(kc, fm)], vh, h % 2, fm))
    outs = []
    for i in range(0, len(maps), group):
        outs += _softmax_pv_group(maps[i:i + group])
    for h in range(A_HEADS):
        y_ref[:, h * A_HEAD_DIM:(h + 1) * A_HEAD_DIM] = outs[h]
    vd = 2 * C_HEAD_DIM
    for h in range(C_HEADS):
        d = outs[A_HEADS + 2 * h] - lam * outs[A_HEADS + 2 * h + 1]
        y_ref[:, BRANCH_W + h * vd:BRANCH_W + (h + 1) * vd] = _rms(d, gsub) * (1.0 - lam_init)


def _lambda(cl):
    s1 = jnp.sum(cl[0:1] * cl[1:2], axis=-1, keepdims=True)
    s2 = jnp.sum(cl[2:3] * cl[3:4], axis=-1, keepdims=True)
    return jnp.exp(s1) - jnp.exp(s2)


def _attn_prompt_kernel(z_ref, gaq, gak, gcq, gck, gsub, cl_ref, y_ref, oak, oav, ock, ocv, *, lam_init):
    z = z_ref[...]
    ak = _head_rms(z[:, 256:384], A_HEAD_DIM, gak[...])
    av = z[:, 384:512]
    ck = _head_rms(z[:, 768:1024], C_HEAD_DIM, gck[...])
    cv = z[:, 1024:1280]
    oak[...] = ak.T
    oav[...] = av.T
    ock[...] = ck.T
    ocv[...] = cv.T
    aq = _head_rms(z[:, 0:256], A_HEAD_DIM, gaq[...])
    cq = _head_rms(z[:, 512:768], C_HEAD_DIM, gcq[...])
    lam = _lambda(cl_ref[...]) + lam_init
    _attend_heads(aq, cq, [ak.astype(BF)], [av.astype(BF)], [ck.astype(BF)], [cv.astype(BF)], [False],
                  lam, gsub[...], lam_init, y_ref, group=A_HEADS + 2 * C_HEADS)


def _attn_sample_kernel(z_ref, gaq, gak, gcq, gck, gsub, cl_ref, cak, cav, cck, ccv,
                        cosa, sda, sua, cosc, sdc, suc, y_ref, ka_s, va_s, kc_s, vc_s, *, lam_init):
    qi = pl.program_id(1)
    qa4, qc4 = A_HEAD_DIM // 4, C_HEAD_DIM // 4

    @pl.when(qi == 0)
    def _():
        ak = _head_rms(z_ref[:, 256:384], A_HEAD_DIM, gak[...])
        ka_s[...] = _rope(ak, cosa[:, :128], sda[:, :128], sua[:, :128], qa4).astype(BF)
        va_s[...] = z_ref[:, 384:512].astype(BF)
        ck = _head_rms(z_ref[:, 768:1024], C_HEAD_DIM, gck[...])
        kc_s[...] = _rope(ck, cosc[...], sdc[...], suc[...], qc4).astype(BF)
        vc_s[...] = z_ref[:, 1024:1280].astype(BF)

    rows = pl.ds(pl.multiple_of(qi * TM, TM), TM)
    aq = _head_rms(z_ref[rows, 0:256], A_HEAD_DIM, gaq[...])
    aq = _rope(aq, cosa[rows, :], sda[rows, :], sua[rows, :], qa4)
    cq = _head_rms(z_ref[rows, 512:768], C_HEAD_DIM, gcq[...])
    cq = _rope(cq, cosc[rows, :], sdc[rows, :], suc[rows, :], qc4)
    lam = _lambda(cl_ref[...]) + lam_init
    _attend_heads(aq, cq,
                  [cak[...].astype(BF), ka_s[...]], [cav[...].astype(BF), va_s[...]],
                  [cck[...].astype(BF), kc_s[...]], [ccv[...].astype(BF), vc_s[...]], [True, False],
                  lam, gsub[...], lam_init, y_ref, group=4)


def _gain_specs(l, nd):
    zeros = (0,) * (nd - 1)
    widths = (256, 128, 256, 256, 64)
    return [pl.BlockSpec((None, 1, w), lambda *a: (l, 0, 0)) for w in widths] + \
           [pl.BlockSpec((None, 4, C_HEAD_DIM), lambda *a: (l, 0, 0))]


def _attn_sample(l, zac, gains, c_lambda, caches, tables, lam_init):
    first_blk = N_PROMPT // DEC_SEQ
    cache_specs = [pl.BlockSpec((None, None, w, PAST_LEN), lambda b, q: (b, l, 0, 0)) for w in (128, 128, 256, 256)]
    table_specs = [pl.BlockSpec((DEC_SEQ, 256), lambda b, q: (0, 0)) for _ in range(6)]
    return pl.pallas_call(
        functools.partial(_attn_sample_kernel, lam_init=lam_init),
        grid=(DEC_BATCH, DEC_SEQ // TM),
        in_specs=[pl.BlockSpec((DEC_SEQ, AC_W), lambda b, q: (first_blk + b, 0))] + _gain_specs(l, 2)
                 + cache_specs + table_specs,
        out_specs=pl.BlockSpec((TM, 2 * BRANCH_W), lambda b, q: (b * (DEC_SEQ // TM) + q, 0)),
        out_shape=jax.ShapeDtypeStruct((N_SAMPLE, 2 * BRANCH_W), F32),
        scratch_shapes=[pltpu.VMEM((DEC_SEQ, 128), BF), pltpu.VMEM((DEC_SEQ, 128), BF),
                        pltpu.VMEM((DEC_SEQ, 256), BF), pltpu.VMEM((DEC_SEQ, 256), BF)],
        compiler_params=_params(("arbitrary", "arbitrary")),
        name=f"attn_sample{l}",
    )(zac, *gains, c_lambda, *caches, *tables)


N_VTILES = BRANCH_W // 128


def _state_blocks(nh, kd, vd):
    per = 128 // vd
    return [(h // per, slice((h * kd) % 128, (h * kd) % 128 + kd), slice((h % per) * vd, (h % per + 1) * vd))
            for h in range(nh)]


_B_BLOCKS = _state_blocks(B_HEADS, B_KEY_DIM, B_VAL_DIM)
_D_BLOCKS = _state_blocks(D_HEADS, D_KEY_DIM, D_VAL_DIM)


def _gla_constants():
    idx = np.arange(TT)
    scans, masks = [], []
    for rev in (False, True):
        eff = (TT - 1 - idx) if rev else idx
        et, eu = eff[:, None], eff[None, :]
        sc, mk = [], []
        for j in range(N_LEVELS):
            b = 1 << j
            start = et - et % b
            odd = (et // b) % 2 == 1
            sc.append(np.where(odd, (eu > start) & (eu <= et), (eu > et) & (eu <= start + b)))
            mk.append(((et // b) % 2 == 1) & (eu // b == et // b - 1))
        sc.append(eu <= et)
        sc.append(eu > et)
        mk.append(eu == et)
        scans.append(np.concatenate([np.stack(sc)] * 2, axis=-1))
        masks.append(np.stack(mk))
    return tuple(jnp.asarray(np.stack(t), BF) for t in (scans, masks))


def _gla_prepare(q, k, v, la2, scan_ref, d, rev, use_state):
    tt = q.shape[0]
    la_hi = la2.astype(BF)
    la_lo = (la2 - la_hi.astype(F32)).astype(BF)
    la_split = jnp.concatenate([la_hi, la_lo], axis=0)

    def factor(i):
        return jnp.exp2(_dot(scan_ref[d, i], la_split))

    qs, ks = [], []
    for j in range(N_LEVELS):
        f = factor(j)
        qs.append((q * f).astype(BF))
        ks.append((k * f).astype(BF).T)
    qs.append(q.astype(BF))
    ks.append(k.astype(BF).T)
    k_out = (k * factor(N_LEVELS + 1)).astype(BF)
    vb = v.astype(BF)
    q_in = d_tile = None
    if use_state:
        q_in = (q * factor(N_LEVELS)).astype(BF)
        ones = jnp.ones((2 * tt, 128), BF)
        d_tile = [jnp.exp2(_dot_tn(la_split[:, i * 128:(i + 1) * 128], ones)) for i in range(q.shape[1] // 128)]

    return dict(qs=qs, ks=ks, k_out=k_out, vb=vb, q_in=q_in, d_tile=d_tile)


def _lane_keep(x, lo, hi):
    lane = lax.broadcasted_iota(jnp.int32, x.shape, 1)
    return jnp.where((lane >= lo) & (lane < hi), x, jnp.zeros_like(x))


def _gla_scores(p, mask_ref, d, nh, kd):
    out = []
    for h in range(nh):
        c, lo = divmod(h * kd, 128)
        sc = None
        for j in range(N_LEVELS + 1):
            kt = p["ks"][j]
            own = kt[c * 128 + lo:c * 128 + lo + kd, :]
            pieces = [jnp.zeros((lo, kt.shape[1]), BF)] if lo else []
            pieces.append(own)
            if lo + kd < 128:
                pieces.append(jnp.zeros((128 - lo - kd, kt.shape[1]), BF))
            kh = jnp.concatenate(pieces, axis=0) if len(pieces) > 1 else own
            t = _dot(p["qs"][j][:, c * 128:(c + 1) * 128], kh).astype(BF) * mask_ref[d, j]
            sc = t if sc is None else sc + t
        out.append(sc)
    return out


def _gla_outputs(p, scs, blk_ref, st_ref, d, nh, kd, vd, use_state, o_ref, rows, col0, accumulate):
    per = 128 // vd
    for c in range(nh // per):
        vt = p["vb"][:, c * 128:(c + 1) * 128]
        kt = (c * per * kd) // 128
        ktile = slice(kt * 128, (kt + 1) * 128)
        o = None
        for i in range(per):
            t = _dot(scs[c * per + i], _lane_keep(vt, i * vd, (i + 1) * vd))
            o = t if o is None else o + t
        kv = _dot_tn(p["k_out"][:, ktile], vt) * blk_ref[c]
        if use_state:
            st = st_ref[d, c]
            o = o + _dot(p["q_in"][:, ktile], st.astype(BF))
            st_ref[d, c] = st * p["d_tile"][kt] + kv
        else:
            st_ref[d, c] = kv
        osl = slice(col0 + c * 128, col0 + (c + 1) * 128)
        if accumulate:
            o_ref[rows, osl] += o
        else:
            o_ref[rows, osl] = o


def _gla_kernel(*refs, layer, n_tiles, has_state):
    if has_state:
        (z_ref, lbl_ref, aw_ref, ab_ref, bng, dng, scan_ref, mask_ref, blkb_ref, blkd_ref, sh_in, sd_in,
         y_ref, o_scr, sth, std) = refs
        for packed, raw, blocks in ((sth, sh_in, _B_BLOCKS), (std, sd_in, _D_BLOCKS)):
            packed[...] = jnp.zeros(packed.shape, F32)
            for dd in range(2):
                for h, (c, r, ln) in enumerate(blocks):
                    packed[dd, c, r, ln] = raw[dd, h]
    else:
        (z_ref, lbl_ref, aw_ref, ab_ref, bng, dng, scan_ref, mask_ref, blkb_ref, blkd_ref,
         y_ref, sh_out, sd_out, o_scr, sth, std) = refs

    gates = []
    for d in range(2):
        logits = [lbl_ref[d, i:i + 1, :] for i in range(DEPTH)]
        mx = functools.reduce(jnp.maximum, logits)
        ex = [jnp.exp(t - mx) for t in logits]
        den = functools.reduce(lambda a, b: a + b, ex)
        ps = [t / den for t in ex]
        lb = functools.reduce(lambda a, b: a + b, ps[:layer + 1]) - ps[0]
        gates.append((lb, jnp.log(jnp.maximum(lb, LB_FLOOR)), jnp.log1p(-lb)))

    o_scr[...] = jnp.zeros(o_scr.shape, F32)

    def tile(i, use_state):
        preps = []
        for d in range(2):
            rev = d == 1
            lb, log_lb, log_1m = gates[d]
            j = (n_tiles - 1 - i) if rev else i
            rows = pl.ds(j * TT if isinstance(j, int) else pl.multiple_of(j * TT, TT), TT)
            bq = z_ref[rows, 0:256]
            zf = z_ref[rows, 768:1024] if rev else z_ref[rows, 512:768]
            b2 = log_1m + _log_sigmoid(zf)
            la = jnp.maximum(log_lb, b2) + jnp.log1p(jnp.exp(-jnp.abs(log_lb - b2)))
            kb = (1.0 - lb) * jax.nn.sigmoid(-zf)
            pre = _dot(z_ref[rows, 2048:2176].astype(BF), aw_ref[d]) + ab_ref[d]
            la_d = _log_sigmoid(pre) * (LOG2E / D_GATE_TAU)
            pb = _gla_prepare(_silu(bq), kb, z_ref[rows, 256:512], la * LOG2E, scan_ref, d, rev, use_state)
            pd = _gla_prepare(z_ref[rows, 1280:1408] * (D_KEY_DIM ** -0.5), z_ref[rows, 1408:1536],
                              z_ref[rows, 1536:1792], la_d, scan_ref, d, rev, use_state)
            preps.append((d, rows, pb, pd))
        scores = [(_gla_scores(pb, mask_ref, d, B_HEADS, B_KEY_DIM), _gla_scores(pd, mask_ref, d, D_HEADS, D_KEY_DIM))
                  for d, _, pb, pd in preps]
        for (d, rows, pb, pd), (sb, sd) in zip(preps, scores):
            _gla_outputs(pb, sb, blkb_ref, sth, d, B_HEADS, B_KEY_DIM, B_VAL_DIM, use_state, o_scr, rows, 0, True)
            _gla_outputs(pd, sd, blkd_ref, std, d, D_HEADS, D_KEY_DIM, D_VAL_DIM, use_state, o_scr, rows, 256, True)

    if has_state:
        def body(i, carry):
            tile(i, True)
            return carry

        lax.fori_loop(0, n_tiles, body, 0)
    else:
        for i in range(n_tiles):
            tile(i, i > 0)

    def finish(i, carry):
        rows = pl.ds(pl.multiple_of(i * TM, TM), TM)
        y_ref[rows, 0:256] = _head_rms(o_scr[rows, 0:256], B_VAL_DIM, bng[...]) * _silu(z_ref[rows, 1024:1280])
        y_ref[rows, 256:512] = _head_rms(o_scr[rows, 256:512], D_VAL_DIM, dng[...]) * _silu(z_ref[rows, 1792:2048])
        return carry

    lax.fori_loop(0, (n_tiles * TT) // TM, finish, 0)
    if not has_state:
        for packed, raw, blocks in ((sth, sh_out, _B_BLOCKS), (std, sd_out, _D_BLOCKS)):
            for dd in range(2):
                for h, (c, r, ln) in enumerate(blocks):
                    raw[dd, h] = packed[dd, c, r, ln]


def _gla_common_specs(l, nd):
    return [
        pl.BlockSpec((2, DEPTH, 256), lambda *a: (0, 0, 0)),
        pl.BlockSpec((None, 2, 128, 128), lambda *a: (l, 0, 0, 0)),
        pl.BlockSpec((None, 2, 1, 128), lambda *a: (l, 0, 0, 0)),
        pl.BlockSpec((None, 1, 256), lambda *a: (l, 0, 0)),
        pl.BlockSpec((None, 1, 256), lambda *a: (l, 0, 0)),
        pl.BlockSpec((2, N_LEVELS + 2, TT, 2 * TT), lambda *a: (0, 0, 0, 0)),
        pl.BlockSpec((2, N_LEVELS + 1, TT, TT), lambda *a: (0, 0, 0, 0)),
        pl.BlockSpec((N_VTILES, 128, 128), lambda *a: (0, 0, 0)),
        pl.BlockSpec((N_VTILES, 128, 128), lambda *a: (0, 0, 0)),
    ]


_STATE_SCRATCH = [pltpu.VMEM((2, N_VTILES, 128, 128), F32), pltpu.VMEM((2, N_VTILES, 128, 128), F32)]


def _state_pattern(blocks):
    pat = np.zeros((N_VTILES, 128, 128), np.float32)
    for c, r, ln in blocks:
        pat[c, r, ln] = 1.0
    return jnp.asarray(pat)


N_ATTN_IN = 7
N_ATTN_OUT = 5


def _ctx_mixers_kernel(*refs, layer, lam_init):
    n_in = N_ATTN_IN + 1 + len(_gla_common_specs(0, 1))
    a_in, g_in, outs = refs[:N_ATTN_IN], refs[N_ATTN_IN:n_in], refs[n_in:]
    _attn_prompt_kernel(*a_in, *outs[:N_ATTN_OUT], lam_init=lam_init)
    _gla_kernel(*g_in, *outs[N_ATTN_OUT:], layer=layer, n_tiles=SEQ // TT, has_state=False)


def _ctx_mixers(l, zac, zbd, gains, c_lambda, small, lam_init):
    cache_w = (128, 128, 256, 256)
    raw_b = (BATCH, 2, B_HEADS, B_KEY_DIM, B_VAL_DIM)
    raw_d = (BATCH, 2, D_HEADS, D_KEY_DIM, D_VAL_DIM)
    seq = lambda w: pl.BlockSpec((SEQ, w), lambda b: (b, 0))
    return pl.pallas_call(
        functools.partial(_ctx_mixers_kernel, layer=l, lam_init=lam_init),
        grid=(BATCH,),
        in_specs=[seq(AC_W)] + _gain_specs(l, 1) + [seq(BD_PAD)] + _gla_common_specs(l, 1),
        out_specs=[seq(2 * BRANCH_W)] + [pl.BlockSpec((None, w, SEQ), lambda b: (b, 0, 0)) for w in cache_w] + [
            seq(2 * BRANCH_W),
            pl.BlockSpec((None,) + raw_b[1:], lambda b: (b, 0, 0, 0, 0)),
            pl.BlockSpec((None,) + raw_d[1:], lambda b: (b, 0, 0, 0, 0)),
        ],
        out_shape=[jax.ShapeDtypeStruct((N_PROMPT, 2 * BRANCH_W), F32)]
                  + [jax.ShapeDtypeStruct((BATCH, w, SEQ), F32) for w in cache_w] + [
            jax.ShapeDtypeStruct((N_PROMPT, 2 * BRANCH_W), F32),
            jax.ShapeDtypeStruct(raw_b, F32),
            jax.ShapeDtypeStruct(raw_d, F32),
        ],
        scratch_shapes=[pltpu.VMEM((SEQ, 2 * BRANCH_W), F32)] + _STATE_SCRATCH,
        compiler_params=_params(("arbitrary",)),
        name=f"ctx_mixers{l}",
    )(zac, *gains, c_lambda, zbd, *small)


def _gla_sample(l, zbd, small, st_h, st_d):
    first_blk = N_PROMPT // DEC_SEQ
    return pl.pallas_call(
        functools.partial(_gla_kernel, layer=l, n_tiles=DEC_SEQ // TT, has_state=True),
        grid=(DEC_BATCH,),
        in_specs=[pl.BlockSpec((DEC_SEQ, BD_PAD), lambda b: (first_blk + b, 0))] + _gla_common_specs(l, 1) + [
            pl.BlockSpec((None, None, 2, B_HEADS, B_KEY_DIM, B_VAL_DIM), lambda b: (b, l, 0, 0, 0, 0)),
            pl.BlockSpec((None, None, 2, D_HEADS, D_KEY_DIM, D_VAL_DIM), lambda b: (b, l, 0, 0, 0, 0)),
        ],
        out_specs=pl.BlockSpec((DEC_SEQ, 2 * BRANCH_W), lambda b: (b, 0)),
        out_shape=jax.ShapeDtypeStruct((N_SAMPLE, 2 * BRANCH_W), F32),
        scratch_shapes=[pltpu.VMEM((DEC_SEQ, 2 * BRANCH_W), F32)] + _STATE_SCRATCH,
        compiler_params=_params(("arbitrary",)),
        name=f"gla_sample{l}",
    )(zbd, *small, st_h, st_d)


def _mix_kernel(*refs, n_x):
    x_refs = refs[:n_x]
    (h_ref, yp_ref, ys_ref, gp_ref, gs_ref, mod_ref, win_ref, wb_ref, wo_ref, o_ref, wm_s, wb_s, wo_s) = refs[n_x:]
    i = pl.program_id(0)

    @pl.when(i < NPRO)
    def _():
        rows = _stage_rows(i, W_ROWS)
        wm_s[_stage_rows(i, MERGE_ROWS), :] = win_ref[...].astype(BF)
        wb_s[rows, :] = wb_ref[...].astype(BF)
        wo_s[rows, :] = wo_ref[...].astype(BF)

    @pl.when(i >= NPRO)
    def _():
        t = i - NPRO
        hb = h_ref[...]
        yac = _tok_load(t, (yp_ref, ys_ref))
        ybd = _tok_load(t, (gp_ref, gs_ref))
        branches = (yac[:, :BRANCH_W], ybd[:, :BRANCH_W], yac[:, BRANCH_W:], ybd[:, BRANCH_W:])
        mixed = None
        for n, y in enumerate(branches):
            logits = _dot_nt(hb, wm_s[n * D_MODEL:(n + 1) * D_MODEL, :])
            term = jax.nn.sigmoid(logits) * _dot(y.astype(BF), wb_s[n * BRANCH_W:(n + 1) * BRANCH_W, :])
            mixed = term if mixed is None else mixed + term
        o_ref[...] = _tok_load(t, x_refs) + mod_ref[2:3, :] * _dot(mixed.astype(BF), wo_s[...])


def _mix(l, xs, h, y_p, y_s, g_p, g_s, mod, w_in_t, w_branch, w_out):
    chunk = pl.BlockSpec((None, W_ROWS, D_MODEL), lambda i: (l, _chunk_of_step(i), 0))
    return pl.pallas_call(
        functools.partial(_mix_kernel, n_x=len(xs)),
        grid=(NPRO + N_TILES,),
        in_specs=_tok_specs(D_MODEL, len(xs) == 2) + _tok_specs(D_MODEL, False)
                 + _tok_specs(2 * BRANCH_W, True) + _tok_specs(2 * BRANCH_W, True)
                 + [_mod_spec(l),
                    pl.BlockSpec((None, MERGE_ROWS, D_MODEL), lambda i: (l, MERGE_OFF // MERGE_ROWS + _chunk_of_step(i), 0)),
                    chunk, chunk],
        out_specs=pl.BlockSpec((TD, D_MODEL), lambda i: (_tile_of_step(i), 0)),
        out_shape=jax.ShapeDtypeStruct((N_TOK, D_MODEL), F32),
        scratch_shapes=[pltpu.VMEM((NPRO * MERGE_ROWS, D_MODEL), BF), pltpu.VMEM((D_MODEL, D_MODEL), BF),
                        pltpu.VMEM((D_MODEL, D_MODEL), BF)],
        compiler_params=_params(("arbitrary",)),
        name=f"mix{l}",
    )(*xs, h, y_p, y_s, g_p, g_s, mod, w_in_t, w_branch.reshape(DEPTH, N_BRANCH * BRANCH_W, D_MODEL), w_out)


def _ffn_kernel(*refs, n_out):
    (xp_ref, x_ref, xn_ref, mod_ref, g_ref, wup_ref, cw_ref, cb_ref, wdn_ref) = refs[:9]
    o_refs = refs[9:9 + n_out]
    hext, u_s, wup_s, wdn_s = refs[9 + n_out:]
    i = pl.program_id(0)

    @pl.when(i < NPRO)
    def _():
        wup_s[_stage_rows(i, W_ROWS), :] = wup_ref[...].astype(BF)
        wdn_s[_stage_rows(i, D_FF // NPRO), :] = wdn_ref[...].astype(BF)

    @pl.when(i >= NPRO)
    def _():
        t = i - NPRO
        ctx = t < PROMPT_TILES
        pos = jnp.maximum(t - PROMPT_TILES, 0) % SAMPLE_TILES_PER_SEQ
        seq_first = ctx | (pos == 0)
        seq_last = ctx | (pos == SAMPLE_TILES_PER_SEQ - 1)
        m = mod_ref[...]
        g = g_ref[...]

        def pre(x):
            return _rms(x, g) * (1.0 + m[4:5]) + m[3:4]

        h_a = pre(x_ref[0:SEQ, :])
        h_b = pre(x_ref[SEQ:TD, :])
        sub = lax.broadcasted_iota(jnp.int32, (HALO, D_MODEL), 0)
        gap = jnp.where(sub == HALO - 1, h_a[SEQ - HALO:, :], jnp.where(sub == 0, h_b[:HALO, :], 0.0))
        hext[0:HALO, :] = jnp.where(seq_first, 0.0, pre(xp_ref[...]))
        hext[FF_A0:FF_A0 + SEQ, :] = h_a
        hext[FF_A0 + SEQ:FF_B0, :] = jnp.where(ctx, 0.0, gap)
        hext[FF_B0:FF_B0 + SEQ, :] = h_b
        hext[FF_B0 + SEQ:FF_B0 + SEQ + HALO, :] = jnp.where(seq_last, 0.0, pre(xn_ref[...]))
        hext[FF_B0 + SEQ + HALO:, :] = jnp.zeros((HALO, D_MODEL), F32)
        hb = hext[...].astype(BF)

        def up_conv_act(c):
            halves = []
            for k, off in enumerate((0, D_FF)):
                cols = slice(off + c * FF_CHUNK, off + (c + 1) * FF_CHUNK)
                u = u_s.at[(2 * c + k) % u_s.shape[0]]
                u[...] = _dot(hb, wup_s[:, cols])
                taps = [u[FF_A0 - 1 + j:FF_A0 - 1 + j + FF_ROWS, :] * cw_ref[j:j + 1, cols] for j in range(CONV_WIDTH)]
                halves.append(taps[0] + taps[1] + taps[2] + cb_ref[:, cols])
            return (_silu(halves[1]) * halves[0]).astype(BF)

        n_chunks = D_FF // FF_CHUNK
        acts = [up_conv_act(c) for c in range(FF_LOOKAHEAD)]
        acc = None
        for c in range(n_chunks):
            if c + FF_LOOKAHEAD < n_chunks:
                acts.append(up_conv_act(c + FF_LOOKAHEAD))
            part = _dot(acts.pop(0), wdn_s[c * FF_CHUNK:(c + 1) * FF_CHUNK, :])
            acc = part if acc is None else acc + part
        gate = m[5:6]

        def store(o_ref):
            o_ref[0:SEQ, :] = x_ref[0:SEQ, :] + gate * acc[0:SEQ]
            o_ref[SEQ:TD, :] = x_ref[SEQ:TD, :] + gate * acc[FF_B0 - FF_A0:FF_B0 - FF_A0 + SEQ]

        if n_out == 1:
            store(o_refs[0])
        else:
            @pl.when(ctx)
            def _():
                store(o_refs[0])

            @pl.when(jnp.logical_not(ctx))
            def _():
                store(o_refs[1])


def _ffn(l, x, mod, norm_g, w_up, conv_w, conv_b, w_down, split_out):
    per = TD // HALO
    last_blk = N_TOK // HALO - 1
    if split_out:
        out_specs = [pl.BlockSpec((TD, D_MODEL), lambda i: (jnp.minimum(_tile_of_step(i), PROMPT_TILES - 1), 0)),
                     pl.BlockSpec((TD, D_MODEL), lambda i: (jnp.maximum(_tile_of_step(i) - PROMPT_TILES, 0), 0))]
        out_shape = [jax.ShapeDtypeStruct((N_PROMPT, D_MODEL), F32), jax.ShapeDtypeStruct((N_SAMPLE, D_MODEL), F32)]
    else:
        out_specs = [pl.BlockSpec((TD, D_MODEL), lambda i: (_tile_of_step(i), 0))]
        out_shape = [jax.ShapeDtypeStruct((N_TOK, D_MODEL), F32)]
    return pl.pallas_call(
        functools.partial(_ffn_kernel, n_out=len(out_specs)),
        grid=(NPRO + N_TILES,),
        in_specs=[
            pl.BlockSpec((HALO, D_MODEL), lambda i: (jnp.maximum(_tile_of_step(i) * per - 1, 0), 0)),
            pl.BlockSpec((TD, D_MODEL), lambda i: (_tile_of_step(i), 0)),
            pl.BlockSpec((HALO, D_MODEL), lambda i: (jnp.minimum((_tile_of_step(i) + 1) * per, last_blk), 0)),
            _mod_spec(l),
            pl.BlockSpec((None, 1, D_MODEL), lambda i: (l, 0, 0)),
            pl.BlockSpec((None, W_ROWS, 2 * D_FF), lambda i: (l, _chunk_of_step(i), 0)),
            pl.BlockSpec((None, CONV_WIDTH, 2 * D_FF), lambda i: (l, 0, 0)),
            pl.BlockSpec((None, 1, 2 * D_FF), lambda i: (l, 0, 0)),
            pl.BlockSpec((None, D_FF // NPRO, D_MODEL), lambda i: (l, _chunk_of_step(i), 0)),
        ],
        out_specs=out_specs,
        out_shape=out_shape,
        scratch_shapes=[pltpu.VMEM((FF_EXT, D_MODEL), F32), pltpu.VMEM((2 * (FF_LOOKAHEAD + 1), FF_EXT, FF_CHUNK), F32),
                        pltpu.VMEM((D_MODEL, 2 * D_FF), BF), pltpu.VMEM((D_FF, D_MODEL), BF)],
        compiler_params=_params(("arbitrary",)),
        name=f"ffn{l}",
    )(x, x, x, mod, norm_g, w_up, conv_w, conv_b, w_down)


def kernel(x_prompt, x_sample, cache_a_k, cache_a_v, cache_c_k, cache_c_v, state_hgrn, state_gla, c, c_ctx, w_ada, b_ada, norm1_g, norm2_g, w_in, a_qn_g, a_kn_g, c_qn_g, c_kn_g, c_lambda, c_subln_g, b_lb_logits, b_norm_g, d_alpha_w, d_alpha_b, d_norm_g, w_branch, w_out, w_up, conv_w, conv_b, w_down):
    xs = (x_prompt.reshape(N_PROMPT, D_MODEL), x_sample.reshape(N_SAMPLE, D_MODEL))
    w_in_t = jnp.swapaxes(w_in, 1, 2)

    cond8 = jnp.concatenate([c_ctx[None, :], c, jnp.zeros((8 - N_GROUPS, D_MODEL), F32)], axis=0)
    mod = _modulation(cond8, w_ada, b_ada)[:, :N_GROUPS].reshape(DEPTH, N_GROUPS, N_MOD, D_MODEL)

    tile4 = lambda g, n: jnp.tile(g, (1, n)).reshape(DEPTH, 1, -1)
    gains = (tile4(a_qn_g, A_HEADS), tile4(a_kn_g, A_KV_HEADS), tile4(c_qn_g, 2 * C_HEADS), tile4(c_kn_g, 2 * C_HEADS),
             c_subln_g.reshape(DEPTH, 1, 2 * C_HEAD_DIM))
    fmaj = lambda t: jnp.moveaxis(t, 2, -1).reshape(DEC_BATCH, DEPTH, -1, PAST_LEN)
    caches = tuple(fmaj(t) for t in (cache_a_k, cache_a_v, cache_c_k, cache_c_v))
    tables = _rope_tables(DEC_SEQ, A_HEAD_DIM, A_HEADS) + _rope_tables(DEC_SEQ, C_HEAD_DIM, 2 * C_HEADS)

    aw = jnp.zeros((DEPTH, 2, 128, 128), F32)
    aw = aw.at[:, 0, 0:D_GATE_RANK].set(d_alpha_w[:, 0]).at[:, 1, D_GATE_RANK:2 * D_GATE_RANK].set(d_alpha_w[:, 1])
    gla_small = (b_lb_logits, aw.astype(BF), d_alpha_b.reshape(DEPTH, 2, 1, 128),
                 tile4(b_norm_g, B_HEADS), tile4(d_norm_g, D_HEADS)) + _gla_constants() + (
                     _state_pattern(_B_BLOCKS), _state_pattern(_D_BLOCKS))

    n1 = norm1_g.reshape(DEPTH, 1, D_MODEL)
    n2 = norm2_g.reshape(DEPTH, 1, D_MODEL)
    cb = conv_b.reshape(DEPTH, 1, 2 * D_FF)

    collected = [[] for _ in range(6)]
    for l in range(DEPTH):
        lam_init = 0.8 - 0.6 * math.exp(-0.3 * l)
        zac, zbd, h = _inproj(l, xs, mod, n1, w_in_t)
        y_p, ak, av, ck, cv, g_p, sh, sd = _ctx_mixers(l, zac, zbd, gains, c_lambda, gla_small, lam_init)
        y_s = _attn_sample(l, zac, gains, c_lambda, caches, tables, lam_init)
        g_s = _gla_sample(l, zbd, gla_small, state_hgrn, state_gla)
        x1 = _mix(l, xs, h, y_p, y_s, g_p, g_s, mod, w_in_t, w_branch, w_out)
        xs = tuple(_ffn(l, x1, mod, n2, w_up, conv_w, cb, w_down, split_out=(l == DEPTH - 1)))
        for acc, arr in zip(collected, (ak, av, ck, cv, sh, sd)):
            acc.append(arr)

    y_prompt = xs[0].reshape(BATCH, SEQ, D_MODEL)
    y_sample = xs[1].reshape(DEC_BATCH, DEC_SEQ, D_MODEL)
    stacked = [jnp.stack(acc, axis=1) for acc in collected]
    feat_shapes = ((A_KV_HEADS, A_HEAD_DIM), (A_KV_HEADS, A_HEAD_DIM), (C_HEADS, 2, C_HEAD_DIM), (C_HEADS, 2 * C_HEAD_DIM))
    ctx = [jnp.moveaxis(t.reshape((BATCH, DEPTH) + fs + (SEQ,)), -1, 2) for t, fs in zip(stacked[:4], feat_shapes)]
    return (y_prompt, y_sample) + tuple(ctx) + tuple(stacked[4:])
```

```python
import functools
import math

import numpy as np
import jax
import jax.numpy as jnp
from jax import lax
from jax.experimental import pallas as pl
from jax.experimental.pallas import tpu as pltpu

F32 = jnp.float32
BF = jnp.bfloat16

D_MODEL = 1024
BATCH = 16
SEQ = 256
DEPTH = 2
DEC_BATCH = 2
DEC_SEQ = 1024
PAST_LEN = 512
GRID_W = 64
ROPE_THETA = 10000.0
EPS = 1e-6
LB_FLOOR = 1e-30
N_MOD = 6
N_BRANCH = 4
BRANCH_W = 256
A_HEADS, A_KV_HEADS, A_HEAD_DIM = 4, 2, 64
C_HEADS, C_HEAD_DIM = 4, 32
B_HEADS, B_KEY_DIM, B_VAL_DIM = 4, 64, 64
D_HEADS, D_KEY_DIM, D_VAL_DIM = 4, 32, 64
D_GATE_RANK = 16
D_GATE_TAU = 16.0
D_FF = 2816
CONV_WIDTH = 3

N_PROMPT = BATCH * SEQ
N_SAMPLE = DEC_BATCH * DEC_SEQ
N_TOK = N_PROMPT + N_SAMPLE
TM = 256
TD = 2 * SEQ
N_TILES = N_TOK // TD
PROMPT_TILES = N_PROMPT // TD
SAMPLE_TILES_PER_SEQ = DEC_SEQ // TD
N_GROUPS = 1 + DEC_BATCH

AC_W = 1280
BD_W = 2080
BD_PAD = 2176
SMALL_W = AC_W + BD_PAD
MERGE_OFF = AC_W + BD_W
MERGE_ROWS = 560
HALO = 8
FF_A0 = HALO
FF_B0 = FF_A0 + SEQ + HALO
FF_ROWS = 2 * SEQ + 2 * HALO
FF_EXT = FF_A0 + FF_ROWS + HALO
FF_CHUNK = 256
FF_U_SLOTS = 8
FF_LOOKAHEAD = 11
NPRO = 8
W_ROWS = D_MODEL // NPRO
LOG2E = 1.4426950408889634
VMEM_LIMIT = 56 * 1024 * 1024
TT = 128
N_LEVELS = 7
SCORE_ROWS = 16


def _dot(a, b):
    return jnp.dot(a, b, preferred_element_type=F32)


def _dot_nt(a, b):
    return lax.dot_general(a, b, (((1,), (1,)), ((), ())), preferred_element_type=F32)


def _dot_tn(a, b):
    return lax.dot_general(a, b, (((0,), (0,)), ((), ())), preferred_element_type=F32)


def _silu(x):
    return x * jax.nn.sigmoid(x)


def _log_sigmoid(x):
    return jnp.minimum(x, 0.0) - jnp.log1p(jnp.exp(-jnp.abs(x)))


def _rms(x, g):
    return x * lax.rsqrt(jnp.mean(x * x, axis=-1, keepdims=True) + EPS) * g


def _head_rms(x, head_dim, g):
    w = x.shape[-1]
    sh = int(math.log2(head_dim))
    r = lax.shift_right_logical(lax.broadcasted_iota(jnp.int32, (w, w), 0), sh)
    c = lax.shift_right_logical(lax.broadcasted_iota(jnp.int32, (w, w), 1), sh)
    bd = jnp.where(r == c, 1.0 / head_dim, 0.0).astype(BF)
    x2 = x * x
    hi = x2.astype(BF)
    lo = (x2 - hi.astype(F32)).astype(BF)
    ms = _dot(hi, bd) + _dot(lo, bd)
    return x * lax.rsqrt(ms + EPS) * g


def _group_of_tile(i):
    return jnp.where(i < PROMPT_TILES, 0, 1 + jnp.maximum(i - PROMPT_TILES, 0) // SAMPLE_TILES_PER_SEQ)


def _params(sem):
    return pltpu.CompilerParams(dimension_semantics=sem, vmem_limit_bytes=VMEM_LIMIT)


def _mod_kernel(cond_ref, w_ref, b_ref, o_ref):
    s = _silu(cond_ref[...])
    o_ref[...] = _dot(s.astype(BF), w_ref[...].astype(BF)) + b_ref[...]


def _modulation(cond8, w_ada, b_ada):
    nb = 1536
    return pl.pallas_call(
        _mod_kernel,
        grid=(DEPTH, N_MOD * D_MODEL // nb),
        in_specs=[
            pl.BlockSpec((8, D_MODEL), lambda l, j: (0, 0)),
            pl.BlockSpec((None, D_MODEL, nb), lambda l, j: (l, 0, j)),
            pl.BlockSpec((None, 1, nb), lambda l, j: (l, 0, j)),
        ],
        out_specs=pl.BlockSpec((None, 8, nb), lambda l, j: (l, 0, j)),
        out_shape=jax.ShapeDtypeStruct((DEPTH, 8, N_MOD * D_MODEL), F32),
        compiler_params=_params(("arbitrary", "arbitrary")),
        name="modulation",
    )(cond8, w_ada, b_ada.reshape(DEPTH, 1, N_MOD * D_MODEL))


def _tile_of_step(i):
    return jnp.maximum(i - NPRO, 0)


def _chunk_of_step(i):
    return jnp.minimum(i, NPRO - 1)


def _tok_specs(width, split):
    if not split:
        return [pl.BlockSpec((TD, width), lambda i: (_tile_of_step(i), 0))]
    return [pl.BlockSpec((TD, width), lambda i: (jnp.minimum(_tile_of_step(i), PROMPT_TILES - 1), 0)),
            pl.BlockSpec((TD, width), lambda i: (jnp.maximum(_tile_of_step(i) - PROMPT_TILES, 0), 0))]


def _tok_load(t, refs):
    if len(refs) == 1:
        return refs[0][...]
    return jnp.where(t < PROMPT_TILES, refs[0][...], refs[1][...])


def _mod_spec(l):
    return pl.BlockSpec((None, None, N_MOD, D_MODEL), lambda i: (l, _group_of_tile(_tile_of_step(i)), 0, 0))


def _stage_rows(i, n):
    return pl.ds(pl.multiple_of(i * n, n), n)


def _inproj_kernel(*refs, n_x):
    x_refs = refs[:n_x]
    mod_ref, g_ref, w_ref, zac_ref, zbd_ref, h_ref, w_s = refs[n_x:]
    i = pl.program_id(0)

    @pl.when(i < NPRO)
    def _():
        w_s[_stage_rows(i, SMALL_W // NPRO), :] = w_ref[...].astype(BF)

    @pl.when(i >= NPRO)
    def _():
        m = mod_ref[...]
        h = _rms(_tok_load(i - NPRO, x_refs), g_ref[...]) * (1.0 + m[1:2]) + m[0:1]
        hb = h.astype(BF)
        h_ref[...] = hb
        zac_ref[...] = _dot_nt(hb, w_s[:AC_W, :])
        zbd_ref[...] = _dot_nt(hb, w_s[AC_W:, :])


def _inproj(l, xs, mod, norm_g, w_in_t):
    split = len(xs) == 2
    return pl.pallas_call(
        functools.partial(_inproj_kernel, n_x=len(xs)),
        grid=(NPRO + N_TILES,),
        in_specs=_tok_specs(D_MODEL, split) + [
            _mod_spec(l),
            pl.BlockSpec((None, 1, D_MODEL), lambda i: (l, 0, 0)),
            pl.BlockSpec((None, SMALL_W // NPRO, D_MODEL), lambda i: (l, _chunk_of_step(i), 0)),
        ],
        out_specs=[
            pl.BlockSpec((TD, AC_W), lambda i: (_tile_of_step(i), 0)),
            pl.BlockSpec((TD, BD_PAD), lambda i: (_tile_of_step(i), 0)),
            pl.BlockSpec((TD, D_MODEL), lambda i: (_tile_of_step(i), 0)),
        ],
        out_shape=[
            jax.ShapeDtypeStruct((N_TOK, AC_W), F32),
            jax.ShapeDtypeStruct((N_TOK, BD_PAD), F32),
            jax.ShapeDtypeStruct((N_TOK, D_MODEL), BF),
        ],
        scratch_shapes=[pltpu.VMEM((SMALL_W, D_MODEL), BF)],
        compiler_params=_params(("arbitrary",)),
        name=f"inproj{l}",
    )(*xs, mod, norm_g, w_in_t)


def _rope_tables(n_tokens, head_dim, n_rep):
    rows = n_tokens // GRID_W
    row = np.repeat(np.arange(rows), GRID_W).astype(np.float64)
    col = np.tile(np.arange(GRID_W), rows).astype(np.float64)
    half = head_dim // 2
    q4 = head_dim // 4
    freqs = ROPE_THETA ** (-np.arange(0, half, 2, dtype=np.float64) / half)
    ang_r = row[:, None] * freqs
    ang_c = col[:, None] * freqs
    ang = np.concatenate([ang_r, ang_r, ang_c, ang_c], axis=-1)
    cos, sin = np.cos(ang), np.sin(ang)
    first = (np.arange(head_dim) % (2 * q4)) < q4
    s_dn = np.where(first, -sin, 0.0)
    s_up = np.where(first, 0.0, sin)
    return tuple(jnp.asarray(np.tile(t, (1, n_rep)), dtype=F32) for t in (cos, s_dn, s_up))


def _rope(x, cos, s_dn, s_up, q4):
    w = x.shape[-1]
    return x * cos + pltpu.roll(x, w - q4, 1) * s_dn + pltpu.roll(x, q4, 1) * s_up


def _softmax_pv_group(maps):
    scores = [[_dot(q, k) if t else _dot_nt(q, k) for k, t in zip(ks, fm)] for q, ks, _, _, fm in maps]
    probs = []
    for ss in scores:
        m = ss[0].max(axis=-1, keepdims=True)
        for s in ss[1:]:
            m = jnp.maximum(m, s.max(axis=-1, keepdims=True))
        probs.append([jnp.exp2(s - m).astype(BF) for s in ss])
    outs = []
    for ps, (_, _, vexts, half, fm) in zip(probs, maps):
        o = None
        for p, v, t in zip(ps, vexts, fm):
            part = _dot_nt(p, v) if t else _dot(p, v)
            o = part if o is None else o + part
        outs.append(o[:, half * 64:(half + 1) * 64] / o[:, (1 - half) * 64:(1 - half) * 64 + 1])
    return outs


def _with_ones(v, half, feature_major):
    idx = lax.broadcasted_iota(jnp.int32, v.shape, 0 if feature_major else 1)
    return jnp.where(lax.shift_right_logical(idx, 6) == half, v, jnp.ones_like(v))


def _attend_heads(aq, cq, ka, va, kc, vc, fm, lam, gsub, lam_init, y_ref, group):
    aqb = (aq * (A_HEAD_DIM ** -0.5 * LOG2E)).astype(BF)
    cqb = (cq * (C_HEAD_DIM ** -0.5 * LOG2E)).astype(BF)
    rep = A_HEADS // A_KV_HEADS

    def feat(x, t, sl):
        return x[sl, :] if t else x[:, sl]

    maps = []
    for h in range(A_HEADS):
        g = h // rep
        sl = slice(g * A_HEAD_DIM, (g + 1) * A_HEAD_DIM)
        maps.append((aqb[:, h * A_HEAD_DIM:(h + 1) * A_HEAD_DIM], [feat(k, t, sl) for k, t in zip(ka, fm)],
                     [_with_ones(v, g, t) for v, t in zip(va, fm)], g, fm))
    for h in range(C_HEADS):
        slab = slice((h // 2) * 128, (h // 2 + 1) * 128)
        vh = [_with_ones(feat(v, t, slab), h % 2, t) for v, t in zip(vc, fm)]
        for j in range(2):
            sl = slice((2 * h + j) * C_HEAD_DIM, (2 * h + j + 1) * C_HEAD_DIM)
            maps.append((cqb[:, sl], [feat(k, t, sl) for k, t in zip(kc, fm)], vh, h % 2, fm))
    outs = []
    for i in range(0, len(maps), group):
        outs += _softmax_pv_group(maps[i:i + group])
    for h in range(A_HEADS):
        y_ref[:, h * A_HEAD_DIM:(h + 1) * A_HEAD_DIM] = outs[h]
    vd = 2 * C_HEAD_DIM
    for h in range(C_HEADS):
        d = outs[A_HEADS + 2 * h] - lam * outs[A_HEADS + 2 * h + 1]
        y_ref[:, BRANCH_W + h * vd:BRANCH_W + (h + 1) * vd] = _rms(d, gsub) * (1.0 - lam_init)


def _lambda(cl):
    s1 = jnp.sum(cl[0:1] * cl[1:2], axis=-1, keepdims=True)
    s2 = jnp.sum(cl[2:3] * cl[3:4], axis=-1, keepdims=True)
    return jnp.exp(s1) - jnp.exp(s2)


def _attn_prompt_kernel(z_ref, gaq, gak, gcq, gck, gsub, cl_ref, y_ref, oak, oav, ock, ocv, *, lam_init):
    z = z_ref[...]
    ak = _head_rms(z[:, 256:384], A_HEAD_DIM, gak[...])
    av = z[:, 384:512]
    ck = _head_rms(z[:, 768:1024], C_HEAD_DIM, gck[...])
    cv = z[:, 1024:1280]
    oak[...] = ak.T
    oav[...] = av.T
    ock[...] = ck.T
    ocv[...] = cv.T
    aq = _head_rms(z[:, 0:256], A_HEAD_DIM, gaq[...])
    cq = _head_rms(z[:, 512:768], C_HEAD_DIM, gcq[...])
    lam = _lambda(cl_ref[...]) + lam_init
    _attend_heads(aq, cq, [ak.astype(BF)], [av.astype(BF)], [ck.astype(BF)], [cv.astype(BF)], [False],
                  lam, gsub[...], lam_init, y_ref, group=A_HEADS + 2 * C_HEADS)


def _attn_sample_kernel(z_ref, gaq, gak, gcq, gck, gsub, cl_ref, cak, cav, cck, ccv,
                        cosa, sda, sua, cosc, sdc, suc, y_ref, ka_s, va_s, kc_s, vc_s, *, lam_init):
    qi = pl.program_id(1)
    qa4, qc4 = A_HEAD_DIM // 4, C_HEAD_DIM // 4

    @pl.when(qi == 0)
    def _():
        ak = _head_rms(z_ref[:, 256:384], A_HEAD_DIM, gak[...])
        ka_s[...] = _rope(ak, cosa[:, :128], sda[:, :128], sua[:, :128], qa4).astype(BF)
        va_s[...] = z_ref[:, 384:512].astype(BF)
        ck = _head_rms(z_ref[:, 768:1024], C_HEAD_DIM, gck[...])
        kc_s[...] = _rope(ck, cosc[...], sdc[...], suc[...], qc4).astype(BF)
        vc_s[...] = z_ref[:, 1024:1280].astype(BF)

    rows = pl.ds(pl.multiple_of(qi * TM, TM), TM)
    aq = _head_rms(z_ref[rows, 0:256], A_HEAD_DIM, gaq[...])
    aq = _rope(aq, cosa[rows, :], sda[rows, :], sua[rows, :], qa4)
    cq = _head_rms(z_ref[rows, 512:768], C_HEAD_DIM, gcq[...])
    cq = _rope(cq, cosc[rows, :], sdc[rows, :], suc[rows, :], qc4)
    lam = _lambda(cl_ref[...]) + lam_init
    _attend_heads(aq, cq,
                  [cak[...].astype(BF), ka_s[...]], [cav[...].astype(BF), va_s[...]],
                  [cck[...].astype(BF), kc_s[...]], [ccv[...].astype(BF), vc_s[...]], [True, False],
                  lam, gsub[...], lam_init, y_ref, group=4)


def _gain_specs(l, nd):
    zeros = (0,) * (nd - 1)
    widths = (256, 128, 256, 256, 64)
    return [pl.BlockSpec((None, 1, w), lambda *a: (l, 0, 0)) for w in widths] + \
           [pl.BlockSpec((None, 4, C_HEAD_DIM), lambda *a: (l, 0, 0))]


def _attn_sample(l, zac, gains, c_lambda, caches, tables, lam_init):
    first_blk = N_PROMPT // DEC_SEQ
    cache_specs = [pl.BlockSpec((None, None, w, PAST_LEN), lambda b, q: (b, l, 0, 0)) for w in (128, 128, 256, 256)]
    table_specs = [pl.BlockSpec((DEC_SEQ, 256), lambda b, q: (0, 0)) for _ in range(6)]
    return pl.pallas_call(
        functools.partial(_attn_sample_kernel, lam_init=lam_init),
        grid=(DEC_BATCH, DEC_SEQ // TM),
        in_specs=[pl.BlockSpec((DEC_SEQ, AC_W), lambda b, q: (first_blk + b, 0))] + _gain_specs(l, 2)
                 + cache_specs + table_specs,
        out_specs=pl.BlockSpec((TM, 2 * BRANCH_W), lambda b, q: (b * (DEC_SEQ // TM) + q, 0)),
        out_shape=jax.ShapeDtypeStruct((N_SAMPLE, 2 * BRANCH_W), F32),
        scratch_shapes=[pltpu.VMEM((DEC_SEQ, 128), BF), pltpu.VMEM((DEC_SEQ, 128), BF),
                        pltpu.VMEM((DEC_SEQ, 256), BF), pltpu.VMEM((DEC_SEQ, 256), BF)],
        compiler_params=_params(("arbitrary", "arbitrary")),
        name=f"attn_sample{l}",
    )(zac, *gains, c_lambda, *caches, *tables)


N_VTILES = BRANCH_W // 128


def _state_blocks(nh, kd, vd):
    per = 128 // vd
    return [(h // per, slice((h * kd) % 128, (h * kd) % 128 + kd), slice((h % per) * vd, (h % per + 1) * vd))
            for h in range(nh)]


_B_BLOCKS = _state_blocks(B_HEADS, B_KEY_DIM, B_VAL_DIM)
_D_BLOCKS = _state_blocks(D_HEADS, D_KEY_DIM, D_VAL_DIM)


def _gla_constants():
    idx = np.arange(TT)
    scans, masks = [], []
    for rev in (False, True):
        eff = (TT - 1 - idx) if rev else idx
        et, eu = eff[:, None], eff[None, :]
        sc, mk = [], []
        for j in range(N_LEVELS):
            b = 1 << j
            start = et - et % b
            odd = (et // b) % 2 == 1
            sc.append(np.where(odd, (eu > start) & (eu <= et), (eu > et) & (eu <= start + b)))
            mk.append(((et // b) % 2 == 1) & (eu // b == et // b - 1))
        sc.append(eu <= et)
        sc.append(eu > et)
        mk.append(eu == et)
        scans.append(np.concatenate([np.stack(sc)] * 2, axis=-1))
        masks.append(np.stack(mk))
    return tuple(jnp.asarray(np.stack(t), BF) for t in (scans, masks))


def _gla_prepare(q, k, v, la2, scan_ref, d, rev, use_state):
    tt = q.shape[0]
    la_hi = la2.astype(BF)
    la_lo = (la2 - la_hi.astype(F32)).astype(BF)
    la_split = jnp.concatenate([la_hi, la_lo], axis=0)

    def factor(i):
        return jnp.exp2(_dot(scan_ref[d, i], la_split))

    qs, ks = [], []
    for j in range(N_LEVELS):
        f = factor(j)
        qs.append((q * f).astype(BF))
        ks.append((k * f).astype(BF).T)
    qs.append(q.astype(BF))
    ks.append(k.astype(BF).T)
    k_out = (k * factor(N_LEVELS + 1)).astype(BF)
    vb = v.astype(BF)
    q_in = d_tile = None
    if use_state:
        q_in = (q * factor(N_LEVELS)).astype(BF)
        ones = jnp.ones((2 * tt, 128), BF)
        d_tile = [jnp.exp2(_dot_tn(la_split[:, i * 128:(i + 1) * 128], ones)) for i in range(q.shape[1] // 128)]

    return dict(qs=qs, ks=ks, k_out=k_out, vb=vb, q_in=q_in, d_tile=d_tile)


def _lane_keep(x, lo, hi):
    lane = lax.broadcasted_iota(jnp.int32, x.shape, 1)
    return jnp.where((lane >= lo) & (lane < hi), x, jnp.zeros_like(x))


def _level_rows(j, rev):
    b = 1 << j
    if j == N_LEVELS or b < SCORE_ROWS:
        return None
    return [(i * b, (i + 1) * b) for i in range(TT // b) if (i % 2 == 1) != rev]


def _gla_scores(p, mask_ref, d, rev, nh, kd):
    out = []
    for h in range(nh):
        c, lo = divmod(h * kd, 128)
        blocks = [None] * (TT // SCORE_ROWS)

        def add(r0, val):
            for i in range(val.shape[0] // SCORE_ROWS):
                piece = val[i * SCORE_ROWS:(i + 1) * SCORE_ROWS]
                k = r0 // SCORE_ROWS + i
                blocks[k] = piece if blocks[k] is None else blocks[k] + piece

        for j in range(N_LEVELS + 1):
            kt = p["ks"][j]
            own = kt[c * 128 + lo:c * 128 + lo + kd, :]
            pieces = [jnp.zeros((lo, kt.shape[1]), BF)] if lo else []
            pieces.append(own)
            if lo + kd < 128:
                pieces.append(jnp.zeros((128 - lo - kd, kt.shape[1]), BF))
            kh = jnp.concatenate(pieces, axis=0) if len(pieces) > 1 else own
            qj = p["qs"][j][:, c * 128:(c + 1) * 128]
            ranges = _level_rows(j, rev)
            if ranges is None:
                add(0, _dot(qj, kh).astype(BF) * mask_ref[d, j])
            else:
                q_rows = jnp.concatenate([qj[a:b] for a, b in ranges], axis=0) if len(ranges) > 1 else qj[ranges[0][0]:ranges[0][1]]
                m_rows = [mask_ref[d, j, a:b, :] for a, b in ranges]
                t = _dot(q_rows, kh).astype(BF)
                off = 0
                for (a, b), m in zip(ranges, m_rows):
                    add(a, t[off:off + b - a] * m)
                    off += b - a
        zero = jnp.zeros((SCORE_ROWS, TT), BF)
        out.append(jnp.concatenate([z if z is not None else zero for z in blocks], axis=0))
    return out


def _gla_outputs(p, scs, blk_ref, st_ref, d, nh, kd, vd, use_state, o_ref, rows, col0, accumulate):
    per = 128 // vd
    for c in range(nh // per):
        vt = p["vb"][:, c * 128:(c + 1) * 128]
        kt = (c * per * kd) // 128
        ktile = slice(kt * 128, (kt + 1) * 128)
        o = None
        for i in range(per):
            t = _dot(scs[c * per + i], _lane_keep(vt, i * vd, (i + 1) * vd))
            o = t if o is None else o + t
        kv = _dot_tn(p["k_out"][:, ktile], vt) * blk_ref[c]
        if use_state:
            st = st_ref[d, c]
            o = o + _dot(p["q_in"][:, ktile], st.astype(BF))
            st_ref[d, c] = st * p["d_tile"][kt] + kv
        else:
            st_ref[d, c] = kv
        osl = slice(col0 + c * 128, col0 + (c + 1) * 128)
        if accumulate:
            o_ref[rows, osl] += o
        else:
            o_ref[rows, osl] = o


def _gla_kernel(*refs, layer, n_tiles, has_state):
    if has_state:
        (z_ref, lbl_ref, aw_ref, ab_ref, bng, dng, scan_ref, mask_ref, blkb_ref, blkd_ref, sh_in, sd_in,
         y_ref, o_scr, sth, std) = refs
        for packed, raw, blocks in ((sth, sh_in, _B_BLOCKS), (std, sd_in, _D_BLOCKS)):
            packed[...] = jnp.zeros(packed.shape, F32)
            for dd in range(2):
                for h, (c, r, ln) in enumerate(blocks):
                    packed[dd, c, r, ln] = raw[dd, h]
    else:
        (z_ref, lbl_ref, aw_ref, ab_ref, bng, dng, scan_ref, mask_ref, blkb_ref, blkd_ref,
         y_ref, sh_out, sd_out, o_scr, sth, std) = refs

    gates = []
    for d in range(2):
        logits = [lbl_ref[d, i:i + 1, :] for i in range(DEPTH)]
        mx = functools.reduce(jnp.maximum, logits)
        ex = [jnp.exp(t - mx) for t in logits]
        den = functools.reduce(lambda a, b: a + b, ex)
        ps = [t / den for t in ex]
        lb = functools.reduce(lambda a, b: a + b, ps[:layer + 1]) - ps[0]
        gates.append((lb, jnp.log(jnp.maximum(lb, LB_FLOOR)), jnp.log1p(-lb)))

    o_scr[...] = jnp.zeros(o_scr.shape, F32)

    def tile(i, use_state):
        preps = []
        for d in range(2):
            rev = d == 1
            lb, log_lb, log_1m = gates[d]
            j = (n_tiles - 1 - i) if rev else i
            rows = pl.ds(j * TT if isinstance(j, int) else pl.multiple_of(j * TT, TT), TT)
            bq = z_ref[rows, 0:256]
            zf = z_ref[rows, 768:1024] if rev else z_ref[rows, 512:768]
            b2 = log_1m + _log_sigmoid(zf)
            la = jnp.maximum(log_lb, b2) + jnp.log1p(jnp.exp(-jnp.abs(log_lb - b2)))
            kb = (1.0 - lb) * jax.nn.sigmoid(-zf)
            pre = _dot(z_ref[rows, 2048:2176].astype(BF), aw_ref[d]) + ab_ref[d]
            la_d = _log_sigmoid(pre) * (LOG2E / D_GATE_TAU)
            pb = _gla_prepare(_silu(bq), kb, z_ref[rows, 256:512], la * LOG2E, scan_ref, d, rev, use_state)
            pd = _gla_prepare(z_ref[rows, 1280:1408] * (D_KEY_DIM ** -0.5), z_ref[rows, 1408:1536],
                              z_ref[rows, 1536:1792], la_d, scan_ref, d, rev, use_state)
            preps.append((d, rows, pb, pd))
        scores = [(_gla_scores(pb, mask_ref, d, d == 1, B_HEADS, B_KEY_DIM),
                   _gla_scores(pd, mask_ref, d, d == 1, D_HEADS, D_KEY_DIM)) for d, _, pb, pd in preps]
        for (d, rows, pb, pd), (sb, sd) in zip(preps, scores):
            _gla_outputs(pb, sb, blkb_ref, sth, d, B_HEADS, B_KEY_DIM, B_VAL_DIM, use_state, o_scr, rows, 0, True)
            _gla_outputs(pd, sd, blkd_ref, std, d, D_HEADS, D_KEY_DIM, D_VAL_DIM, use_state, o_scr, rows, 256, True)

    if has_state:
        def body(i, carry):
            tile(i, True)
            return carry

        lax.fori_loop(0, n_tiles, body, 0)
    else:
        for i in range(n_tiles):
            tile(i, i > 0)

    def finish(i, carry):
        rows = pl.ds(pl.multiple_of(i * TM, TM), TM)
        y_ref[rows, 0:256] = _head_rms(o_scr[rows, 0:256], B_VAL_DIM, bng[...]) * _silu(z_ref[rows, 1024:1280])
        y_ref[rows, 256:512] = _head_rms(o_scr[rows, 256:512], D_VAL_DIM, dng[...]) * _silu(z_ref[rows, 1792:2048])
        return carry

    lax.fori_loop(0, (n_tiles * TT) // TM, finish, 0)
    if not has_state:
        for packed, raw, blocks in ((sth, sh_out, _B_BLOCKS), (std, sd_out, _D_BLOCKS)):
            for dd in range(2):
                for h, (c, r, ln) in enumerate(blocks):
                    raw[dd, h] = packed[dd, c, r, ln]


def _gla_common_specs(l, nd):
    return [
        pl.BlockSpec((2, DEPTH, 256), lambda *a: (0, 0, 0)),
        pl.BlockSpec((None, 2, 128, 128), lambda *a: (l, 0, 0, 0)),
        pl.BlockSpec((None, 2, 1, 128), lambda *a: (l, 0, 0, 0)),
        pl.BlockSpec((None, 1, 256), lambda *a: (l, 0, 0)),
        pl.BlockSpec((None, 1, 256), lambda *a: (l, 0, 0)),
        pl.BlockSpec((2, N_LEVELS + 2, TT, 2 * TT), lambda *a: (0, 0, 0, 0)),
        pl.BlockSpec((2, N_LEVELS + 1, TT, TT), lambda *a: (0, 0, 0, 0)),
        pl.BlockSpec((N_VTILES, 128, 128), lambda *a: (0, 0, 0)),
        pl.BlockSpec((N_VTILES, 128, 128), lambda *a: (0, 0, 0)),
    ]


_STATE_SCRATCH = [pltpu.VMEM((2, N_VTILES, 128, 128), F32), pltpu.VMEM((2, N_VTILES, 128, 128), F32)]


def _state_pattern(blocks):
    pat = np.zeros((N_VTILES, 128, 128), np.float32)
    for c, r, ln in blocks:
        pat[c, r, ln] = 1.0
    return jnp.asarray(pat)


N_ATTN_IN = 7
N_ATTN_OUT = 5


def _ctx_mixers_kernel(*refs, layer, lam_init):
    n_in = N_ATTN_IN + 1 + len(_gla_common_specs(0, 1))
    a_in, g_in, outs = refs[:N_ATTN_IN], refs[N_ATTN_IN:n_in], refs[n_in:]
    _attn_prompt_kernel(*a_in, *outs[:N_ATTN_OUT], lam_init=lam_init)
    _gla_kernel(*g_in, *outs[N_ATTN_OUT:], layer=layer, n_tiles=SEQ // TT, has_state=False)


def _ctx_mixers(l, zac, zbd, gains, c_lambda, small, lam_init):
    cache_w = (128, 128, 256, 256)
    raw_b = (BATCH, 2, B_HEADS, B_KEY_DIM, B_VAL_DIM)
    raw_d = (BATCH, 2, D_HEADS, D_KEY_DIM, D_VAL_DIM)
    seq = lambda w: pl.BlockSpec((SEQ, w), lambda b: (b, 0))
    return pl.pallas_call(
        functools.partial(_ctx_mixers_kernel, layer=l, lam_init=lam_init),
        grid=(BATCH,),
        in_specs=[seq(AC_W)] + _gain_specs(l, 1) + [seq(BD_PAD)] + _gla_common_specs(l, 1),
        out_specs=[seq(2 * BRANCH_W)] + [pl.BlockSpec((None, w, SEQ), lambda b: (b, 0, 0)) for w in cache_w] + [
            seq(2 * BRANCH_W),
            pl.BlockSpec((None,) + raw_b[1:], lambda b: (b, 0, 0, 0, 0)),
            pl.BlockSpec((None,) + raw_d[1:], lambda b: (b, 0, 0, 0, 0)),
        ],
        out_shape=[jax.ShapeDtypeStruct((N_PROMPT, 2 * BRANCH_W), F32)]
                  + [jax.ShapeDtypeStruct((BATCH, w, SEQ), F32) for w in cache_w] + [
            jax.ShapeDtypeStruct((N_PROMPT, 2 * BRANCH_W), F32),
            jax.ShapeDtypeStruct(raw_b, F32),
            jax.ShapeDtypeStruct(raw_d, F32),
        ],
        scratch_shapes=[pltpu.VMEM((SEQ, 2 * BRANCH_W), F32)] + _STATE_SCRATCH,
        compiler_params=_params(("arbitrary",)),
        name=f"ctx_mixers{l}",
    )(zac, *gains, c_lambda, zbd, *small)


def _gla_sample(l, zbd, small, st_h, st_d):
    first_blk = N_PROMPT // DEC_SEQ
    return pl.pallas_call(
        functools.partial(_gla_kernel, layer=l, n_tiles=DEC_SEQ // TT, has_state=True),
        grid=(DEC_BATCH,),
        in_specs=[pl.BlockSpec((DEC_SEQ, BD_PAD), lambda b: (first_blk + b, 0))] + _gla_common_specs(l, 1) + [
            pl.BlockSpec((None, None, 2, B_HEADS, B_KEY_DIM, B_VAL_DIM), lambda b: (b, l, 0, 0, 0, 0)),
            pl.BlockSpec((None, None, 2, D_HEADS, D_KEY_DIM, D_VAL_DIM), lambda b: (b, l, 0, 0, 0, 0)),
        ],
        out_specs=pl.BlockSpec((DEC_SEQ, 2 * BRANCH_W), lambda b: (b, 0)),
        out_shape=jax.ShapeDtypeStruct((N_SAMPLE, 2 * BRANCH_W), F32),
        scratch_shapes=[pltpu.VMEM((DEC_SEQ, 2 * BRANCH_W), F32)] + _STATE_SCRATCH,
        compiler_params=_params(("arbitrary",)),
        name=f"gla_sample{l}",
    )(zbd, *small, st_h, st_d)


def _mix_kernel(*refs, n_x):
    x_refs = refs[:n_x]
    (h_ref, yp_ref, ys_ref, gp_ref, gs_ref, mod_ref, win_ref, wb_ref, wo_ref, o_ref, wm_s, wb_s, wo_s) = refs[n_x:]
    i = pl.program_id(0)

    @pl.when(i < NPRO)
    def _():
        rows = _stage_rows(i, W_ROWS)
        wm_s[_stage_rows(i, MERGE_ROWS), :] = win_ref[...].astype(BF)
        wb_s[rows, :] = wb_ref[...].astype(BF)
        wo_s[rows, :] = wo_ref[...].astype(BF)

    @pl.when(i >= NPRO)
    def _():
        t = i - NPRO
        hb = h_ref[...]
        yac = _tok_load(t, (yp_ref, ys_ref))
        ybd = _tok_load(t, (gp_ref, gs_ref))
        branches = (yac[:, :BRANCH_W], ybd[:, :BRANCH_W], yac[:, BRANCH_W:], ybd[:, BRANCH_W:])
        mixed = None
        for n, y in enumerate(branches):
            logits = _dot_nt(hb, wm_s[n * D_MODEL:(n + 1) * D_MODEL, :])
            term = jax.nn.sigmoid(logits) * _dot(y.astype(BF), wb_s[n * BRANCH_W:(n + 1) * BRANCH_W, :])
            mixed = term if mixed is None else mixed + term
        o_ref[...] = _tok_load(t, x_refs) + mod_ref[2:3, :] * _dot(mixed.astype(BF), wo_s[...])


def _mix(l, xs, h, y_p, y_s, g_p, g_s, mod, w_in_t, w_branch, w_out):
    chunk = pl.BlockSpec((None, W_ROWS, D_MODEL), lambda i: (l, _chunk_of_step(i), 0))
    return pl.pallas_call(
        functools.partial(_mix_kernel, n_x=len(xs)),
        grid=(NPRO + N_TILES,),
        in_specs=_tok_specs(D_MODEL, len(xs) == 2) + _tok_specs(D_MODEL, False)
                 + _tok_specs(2 * BRANCH_W, True) + _tok_specs(2 * BRANCH_W, True)
                 + [_mod_spec(l),
                    pl.BlockSpec((None, MERGE_ROWS, D_MODEL), lambda i: (l, MERGE_OFF // MERGE_ROWS + _chunk_of_step(i), 0)),
                    chunk, chunk],
        out_specs=pl.BlockSpec((TD, D_MODEL), lambda i: (_tile_of_step(i), 0)),
        out_shape=jax.ShapeDtypeStruct((N_TOK, D_MODEL), F32),
        scratch_shapes=[pltpu.VMEM((NPRO * MERGE_ROWS, D_MODEL), BF), pltpu.VMEM((D_MODEL, D_MODEL), BF),
                        pltpu.VMEM((D_MODEL, D_MODEL), BF)],
        compiler_params=_params(("arbitrary",)),
        name=f"mix{l}",
    )(*xs, h, y_p, y_s, g_p, g_s, mod, w_in_t, w_branch.reshape(DEPTH, N_BRANCH * BRANCH_W, D_MODEL), w_out)


def _ffn_kernel(*refs, n_out):
    (xp_ref, x_ref, xn_ref, mod_ref, g_ref, wup_ref, cw_ref, cb_ref, wdn_ref) = refs[:9]
    o_refs = refs[9:9 + n_out]
    hext, u_s, wup_s, wdn_s = refs[9 + n_out:]
    i = pl.program_id(0)

    @pl.when(i < NPRO)
    def _():
        wup_s[_stage_rows(i, W_ROWS), :] = wup_ref[...].astype(BF)
        wdn_s[_stage_rows(i, D_FF // NPRO), :] = wdn_ref[...].astype(BF)

    @pl.when(i >= NPRO)
    def _():
        t = i - NPRO
        ctx = t < PROMPT_TILES
        pos = jnp.maximum(t - PROMPT_TILES, 0) % SAMPLE_TILES_PER_SEQ
        seq_first = ctx | (pos == 0)
        seq_last = ctx | (pos == SAMPLE_TILES_PER_SEQ - 1)
        m = mod_ref[...]
        g = g_ref[...]

        def pre(x):
            return _rms(x, g) * (1.0 + m[4:5]) + m[3:4]

        h_a = pre(x_ref[0:SEQ, :])
        h_b = pre(x_ref[SEQ:TD, :])
        sub = lax.broadcasted_iota(jnp.int32, (HALO, D_MODEL), 0)
        gap = jnp.where(sub == HALO - 1, h_a[SEQ - HALO:, :], jnp.where(sub == 0, h_b[:HALO, :], 0.0))
        hext[0:HALO, :] = jnp.where(seq_first, 0.0, pre(xp_ref[...]))
        hext[FF_A0:FF_A0 + SEQ, :] = h_a
        hext[FF_A0 + SEQ:FF_B0, :] = jnp.where(ctx, 0.0, gap)
        hext[FF_B0:FF_B0 + SEQ, :] = h_b
        hext[FF_B0 + SEQ:FF_B0 + SEQ + HALO, :] = jnp.where(seq_last, 0.0, pre(xn_ref[...]))
        hext[FF_B0 + SEQ + HALO:, :] = jnp.zeros((HALO, D_MODEL), F32)
        hb = hext[...].astype(BF)

        def up_conv_act(c):
            halves = []
            for k, off in enumerate((0, D_FF)):
                cols = slice(off + c * FF_CHUNK, off + (c + 1) * FF_CHUNK)
                u = u_s.at[(2 * c + k) % u_s.shape[0]]
                u[...] = _dot(hb, wup_s[:, cols])
                taps = [u[FF_A0 - 1 + j:FF_A0 - 1 + j + FF_ROWS, :] * cw_ref[j:j + 1, cols] for j in range(CONV_WIDTH)]
                halves.append(taps[0] + taps[1] + taps[2] + cb_ref[:, cols])
            return (_silu(halves[1]) * halves[0]).astype(BF)

        n_chunks = D_FF // FF_CHUNK
        acts = [up_conv_act(c) for c in range(min(FF_LOOKAHEAD, n_chunks))]
        acc = None
        for c in range(n_chunks):
            if c + FF_LOOKAHEAD < n_chunks:
                acts.append(up_conv_act(c + FF_LOOKAHEAD))
            part = _dot(acts.pop(0), wdn_s[c * FF_CHUNK:(c + 1) * FF_CHUNK, :])
            acc = part if acc is None else acc + part
        gate = m[5:6]

        def store(o_ref):
            o_ref[0:SEQ, :] = x_ref[0:SEQ, :] + gate * acc[0:SEQ]
            o_ref[SEQ:TD, :] = x_ref[SEQ:TD, :] + gate * acc[FF_B0 - FF_A0:FF_B0 - FF_A0 + SEQ]

        if n_out == 1:
            store(o_refs[0])
        else:
            @pl.when(ctx)
            def _():
                store(o_refs[0])

            @pl.when(jnp.logical_not(ctx))
            def _():
                store(o_refs[1])


def _ffn(l, x, mod, norm_g, w_up, conv_w, conv_b, w_down, split_out):
    per = TD // HALO
    last_blk = N_TOK // HALO - 1
    if split_out:
        out_specs = [pl.BlockSpec((TD, D_MODEL), lambda i: (jnp.minimum(_tile_of_step(i), PROMPT_TILES - 1), 0)),
                     pl.BlockSpec((TD, D_MODEL), lambda i: (jnp.maximum(_tile_of_step(i) - PROMPT_TILES, 0), 0))]
        out_shape = [jax.ShapeDtypeStruct((N_PROMPT, D_MODEL), F32), jax.ShapeDtypeStruct((N_SAMPLE, D_MODEL), F32)]
    else:
        out_specs = [pl.BlockSpec((TD, D_MODEL), lambda i: (_tile_of_step(i), 0))]
        out_shape = [jax.ShapeDtypeStruct((N_TOK, D_MODEL), F32)]
    return pl.pallas_call(
        functools.partial(_ffn_kernel, n_out=len(out_specs)),
        grid=(NPRO + N_TILES,),
        in_specs=[
            pl.BlockSpec((HALO, D_MODEL), lambda i: (jnp.maximum(_tile_of_step(i) * per - 1, 0), 0)),
            pl.BlockSpec((TD, D_MODEL), lambda i: (_tile_of_step(i), 0)),
            pl.BlockSpec((HALO, D_MODEL), lambda i: (jnp.minimum((_tile_of_step(i) + 1) * per, last_blk), 0)),
            _mod_spec(l),
            pl.BlockSpec((None, 1, D_MODEL), lambda i: (l, 0, 0)),
            pl.BlockSpec((None, W_ROWS, 2 * D_FF), lambda i: (l, _chunk_of_step(i), 0)),
            pl.BlockSpec((None, CONV_WIDTH, 2 * D_FF), lambda i: (l, 0, 0)),
            pl.BlockSpec((None, 1, 2 * D_FF), lambda i: (l, 0, 0)),
            pl.BlockSpec((None, D_FF // NPRO, D_MODEL), lambda i: (l, _chunk_of_step(i), 0)),
        ],
        out_specs=out_specs,
        out_shape=out_shape,
        scratch_shapes=[pltpu.VMEM((FF_EXT, D_MODEL), F32), pltpu.VMEM((FF_U_SLOTS, FF_EXT, FF_CHUNK), F32),
                        pltpu.VMEM((D_MODEL, 2 * D_FF), BF), pltpu.VMEM((D_FF, D_MODEL), BF)],
        compiler_params=_params(("arbitrary",)),
        name=f"ffn{l}",
    )(x, x, x, mod, norm_g, w_up, conv_w, conv_b, w_down)


def kernel(x_prompt, x_sample, cache_a_k, cache_a_v, cache_c_k, cache_c_v, state_hgrn, state_gla, c, c_ctx, w_ada, b_ada, norm1_g, norm2_g, w_in, a_qn_g, a_kn_g, c_qn_g, c_kn_g, c_lambda, c_subln_g, b_lb_logits, b_norm_g, d_alpha_w, d_alpha_b, d_norm_g, w_branch, w_out, w_up, conv_w, conv_b, w_down):
    xs = (x_prompt.reshape(N_PROMPT, D_MODEL), x_sample.reshape(N_SAMPLE, D_MODEL))
    w_in_t = jnp.swapaxes(w_in, 1, 2)

    cond8 = jnp.concatenate([c_ctx[None, :], c, jnp.zeros((8 - N_GROUPS, D_MODEL), F32)], axis=0)
    mod = _modulation(cond8, w_ada, b_ada)[:, :N_GROUPS].reshape(DEPTH, N_GROUPS, N_MOD, D_MODEL)

    tile4 = lambda g, n: jnp.tile(g, (1, n)).reshape(DEPTH, 1, -1)
    gains = (tile4(a_qn_g, A_HEADS), tile4(a_kn_g, A_KV_HEADS), tile4(c_qn_g, 2 * C_HEADS), tile4(c_kn_g, 2 * C_HEADS),
             c_subln_g.reshape(DEPTH, 1, 2 * C_HEAD_DIM))
    fmaj = lambda t: jnp.moveaxis(t, 2, -1).reshape(DEC_BATCH, DEPTH, -1, PAST_LEN)
    caches = tuple(fmaj(t) for t in (cache_a_k, cache_a_v, cache_c_k, cache_c_v))
    tables = _rope_tables(DEC_SEQ, A_HEAD_DIM, A_HEADS) + _rope_tables(DEC_SEQ, C_HEAD_DIM, 2 * C_HEADS)

    aw = jnp.zeros((DEPTH, 2, 128, 128), F32)
    aw = aw.at[:, 0, 0:D_GATE_RANK].set(d_alpha_w[:, 0]).at[:, 1, D_GATE_RANK:2 * D_GATE_RANK].set(d_alpha_w[:, 1])
    gla_small = (b_lb_logits, aw.astype(BF), d_alpha_b.reshape(DEPTH, 2, 1, 128),
                 tile4(b_norm_g, B_HEADS), tile4(d_norm_g, D_HEADS)) + _gla_constants() + (
                     _state_pattern(_B_BLOCKS), _state_pattern(_D_BLOCKS))

    n1 = norm1_g.reshape(DEPTH, 1, D_MODEL)
    n2 = norm2_g.reshape(DEPTH, 1, D_MODEL)
    cb = conv_b.reshape(DEPTH, 1, 2 * D_FF)

    collected = [[] for _ in range(6)]
    for l in range(DEPTH):
        lam_init = 0.8 - 0.6 * math.exp(-0.3 * l)
        zac, zbd, h = _inproj(l, xs, mod, n1, w_in_t)
        y_p, ak, av, ck, cv, g_p, sh, sd = _ctx_mixers(l, zac, zbd, gains, c_lambda, gla_small, lam_init)
        y_s = _attn_sample(l, zac, gains, c_lambda, caches, tables, lam_init)
        g_s = _gla_sample(l, zbd, gla_small, state_hgrn, state_gla)
        x1 = _mix(l, xs, h, y_p, y_s, g_p, g_s, mod, w_in_t, w_branch, w_out)
        xs = tuple(_ffn(l, x1, mod, n2, w_up, conv_w, cb, w_down, split_out=(l == DEPTH - 1)))
        for acc, arr in zip(collected, (ak, av, ck, cv, sh, sd)):
            acc.append(arr)

    y_prompt = xs[0].reshape(BATCH, SEQ, D_MODEL)
    y_sample = xs[1].reshape(DEC_BATCH, DEC_SEQ, D_MODEL)
    stacked = [jnp.stack(acc, axis=1) for acc in collected]
    feat_shapes = ((A_KV_HEADS, A_HEAD_DIM), (A_KV_HEADS, A_HEAD_DIM), (C_HEADS, 2, C_HEAD_DIM), (C_HEADS, 2 * C_HEAD_DIM))
    ctx = [jnp.moveaxis(t.reshape((BATCH, DEPTH) + fs + (SEQ,)), -1, 2) for t, fs in zip(stacked[:4], feat_shapes)]
    return (y_prompt, y_sample) + tuple(ctx) + tuple(stacked[4:])
```

```python
import functools
import math

import numpy as np
import jax
import jax.numpy as jnp
from jax import lax
from jax.experimental import pallas as pl
from jax.experimental.pallas import tpu as pltpu

F32 = jnp.float32
BF = jnp.bfloat16

D_MODEL = 1024
BATCH = 16
SEQ = 256
DEPTH = 2
DEC_BATCH = 2
DEC_SEQ = 1024
PAST_LEN = 512
GRID_W = 64
ROPE_THETA = 10000.0
EPS = 1e-6
LB_FLOOR = 1e-30
N_MOD = 6
N_BRANCH = 4
BRANCH_W = 256
A_HEADS, A_KV_HEADS, A_HEAD_DIM = 4, 2, 64
C_HEADS, C_HEAD_DIM = 4, 32
B_HEADS, B_KEY_DIM, B_VAL_DIM = 4, 64, 64
D_HEADS, D_KEY_DIM, D_VAL_DIM = 4, 32, 64
D_GATE_RANK = 16
D_GATE_TAU = 16.0
D_FF = 2816
CONV_WIDTH = 3

N_PROMPT = BATCH * SEQ
N_SAMPLE = DEC_BATCH * DEC_SEQ
N_TOK = N_PROMPT + N_SAMPLE
TM = 256
TD = 2 * SEQ
N_TILES = N_TOK // TD
PROMPT_TILES = N_PROMPT // TD
SAMPLE_TILES_PER_SEQ = DEC_SEQ // TD
N_GROUPS = 1 + DEC_BATCH

AC_W = 1280
BD_W = 2080
BD_PAD = 2176
SMALL_W = AC_W + BD_PAD
MERGE_OFF = AC_W + BD_W
MERGE_ROWS = 560
HALO = 8
FF_A0 = HALO
FF_B0 = FF_A0 + SEQ + HALO
FF_ROWS = 2 * SEQ + 2 * HALO
FF_EXT = FF_A0 + FF_ROWS + HALO
FF_CHUNK = 256
FF_U_SLOTS = 8
FF_LOOKAHEAD = 11
NPRO = 8
W_ROWS = D_MODEL // NPRO
LOG2E = 1.4426950408889634
VMEM_LIMIT = 56 * 1024 * 1024
TT = 128
N_LEVELS = 7
SCORE_ROWS = 16


def _dot(a, b):
    return jnp.dot(a, b, preferred_element_type=F32)


def _dot_nt(a, b):
    return lax.dot_general(a, b, (((1,), (1,)), ((), ())), preferred_element_type=F32)


def _dot_tn(a, b):
    return lax.dot_general(a, b, (((0,), (0,)), ((), ())), preferred_element_type=F32)


def _silu(x):
    return x * jax.nn.sigmoid(x)


def _log_sigmoid(x):
    return jnp.minimum(x, 0.0) - jnp.log1p(jnp.exp(-jnp.abs(x)))


def _rms(x, g):
    return x * lax.rsqrt(jnp.mean(x * x, axis=-1, keepdims=True) + EPS) * g


def _head_rms(x, head_dim, g):
    w = x.shape[-1]
    sh = int(math.log2(head_dim))
    r = lax.shift_right_logical(lax.broadcasted_iota(jnp.int32, (w, w), 0), sh)
    c = lax.shift_right_logical(lax.broadcasted_iota(jnp.int32, (w, w), 1), sh)
    bd = jnp.where(r == c, 1.0 / head_dim, 0.0).astype(BF)
    x2 = x * x
    hi = x2.astype(BF)
    lo = (x2 - hi.astype(F32)).astype(BF)
    ms = _dot(hi, bd) + _dot(lo, bd)
    return x * lax.rsqrt(ms + EPS) * g


def _group_of_tile(i):
    return jnp.where(i < PROMPT_TILES, 0, 1 + jnp.maximum(i - PROMPT_TILES, 0) // SAMPLE_TILES_PER_SEQ)


def _params(sem):
    return pltpu.CompilerParams(dimension_semantics=sem, vmem_limit_bytes=VMEM_LIMIT)


def _mod_kernel(cond_ref, w_ref, b_ref, o_ref):
    s = _silu(cond_ref[...])
    o_ref[...] = _dot(s.astype(BF), w_ref[...].astype(BF)) + b_ref[...]


def _modulation(cond8, w_ada, b_ada):
    nb = 1536
    return pl.pallas_call(
        _mod_kernel,
        grid=(DEPTH, N_MOD * D_MODEL // nb),
        in_specs=[
            pl.BlockSpec((8, D_MODEL), lambda l, j: (0, 0)),
            pl.BlockSpec((None, D_MODEL, nb), lambda l, j: (l, 0, j)),
            pl.BlockSpec((None, 1, nb), lambda l, j: (l, 0, j)),
        ],
        out_specs=pl.BlockSpec((None, 8, nb), lambda l, j: (l, 0, j)),
        out_shape=jax.ShapeDtypeStruct((DEPTH, 8, N_MOD * D_MODEL), F32),
        compiler_params=_params(("arbitrary", "arbitrary")),
        name="modulation",
    )(cond8, w_ada, b_ada.reshape(DEPTH, 1, N_MOD * D_MODEL))


def _tile_of_step(i):
    return jnp.maximum(i - NPRO, 0)


def _chunk_of_step(i):
    return jnp.minimum(i, NPRO - 1)


def _tok_specs(width, split):
    if not split:
        return [pl.BlockSpec((TD, width), lambda i: (_tile_of_step(i), 0))]
    return [pl.BlockSpec((TD, width), lambda i: (jnp.minimum(_tile_of_step(i), PROMPT_TILES - 1), 0)),
            pl.BlockSpec((TD, width), lambda i: (jnp.maximum(_tile_of_step(i) - PROMPT_TILES, 0), 0))]


def _tok_load(t, refs):
    if len(refs) == 1:
        return refs[0][...]
    return jnp.where(t < PROMPT_TILES, refs[0][...], refs[1][...])


def _mod_spec(l):
    return pl.BlockSpec((None, None, N_MOD, D_MODEL), lambda i: (l, _group_of_tile(_tile_of_step(i)), 0, 0))


def _stage_rows(i, n):
    return pl.ds(pl.multiple_of(i * n, n), n)


def _inproj_kernel(*refs, n_x):
    x_refs = refs[:n_x]
    mod_ref, g_ref, w_ref, zac_ref, zbd_ref, h_ref, w_s = refs[n_x:]
    i = pl.program_id(0)

    @pl.when(i < NPRO)
    def _():
        w_s[_stage_rows(i, SMALL_W // NPRO), :] = w_ref[...].astype(BF)

    @pl.when(i >= NPRO)
    def _():
        m = mod_ref[...]
        h = _rms(_tok_load(i - NPRO, x_refs), g_ref[...]) * (1.0 + m[1:2]) + m[0:1]
        hb = h.astype(BF)
        h_ref[...] = hb
        zac_ref[...] = _dot_nt(hb, w_s[:AC_W, :])
        zbd_ref[...] = _dot_nt(hb, w_s[AC_W:, :])


def _inproj(l, xs, mod, norm_g, w_in_t):
    split = len(xs) == 2
    return pl.pallas_call(
        functools.partial(_inproj_kernel, n_x=len(xs)),
        grid=(NPRO + N_TILES,),
        in_specs=_tok_specs(D_MODEL, split) + [
            _mod_spec(l),
            pl.BlockSpec((None, 1, D_MODEL), lambda i: (l, 0, 0)),
            pl.BlockSpec((None, SMALL_W // NPRO, D_MODEL), lambda i: (l, _chunk_of_step(i), 0)),
        ],
        out_specs=[
            pl.BlockSpec((TD, AC_W), lambda i: (_tile_of_step(i), 0)),
            pl.BlockSpec((TD, BD_PAD), lambda i: (_tile_of_step(i), 0)),
            pl.BlockSpec((TD, D_MODEL), lambda i: (_tile_of_step(i), 0)),
        ],
        out_shape=[
            jax.ShapeDtypeStruct((N_TOK, AC_W), F32),
            jax.ShapeDtypeStruct((N_TOK, BD_PAD), F32),
            jax.ShapeDtypeStruct((N_TOK, D_MODEL), BF),
        ],
        scratch_shapes=[pltpu.VMEM((SMALL_W, D_MODEL), BF)],
        compiler_params=_params(("arbitrary",)),
        name=f"inproj{l}",
    )(*xs, mod, norm_g, w_in_t)


def _rope_tables(n_tokens, head_dim, n_rep):
    rows = n_tokens // GRID_W
    row = np.repeat(np.arange(rows), GRID_W).astype(np.float64)
    col = np.tile(np.arange(GRID_W), rows).astype(np.float64)
    half = head_dim // 2
    q4 = head_dim // 4
    freqs = ROPE_THETA ** (-np.arange(0, half, 2, dtype=np.float64) / half)
    ang_r = row[:, None] * freqs
    ang_c = col[:, None] * freqs
    ang = np.concatenate([ang_r, ang_r, ang_c, ang_c], axis=-1)
    cos, sin = np.cos(ang), np.sin(ang)
    first = (np.arange(head_dim) % (2 * q4)) < q4
    s_dn = np.where(first, -sin, 0.0)
    s_up = np.where(first, 0.0, sin)
    return tuple(jnp.asarray(np.tile(t, (1, n_rep)), dtype=F32) for t in (cos, s_dn, s_up))


def _rope(x, cos, s_dn, s_up, q4):
    w = x.shape[-1]
    return x * cos + pltpu.roll(x, w - q4, 1) * s_dn + pltpu.roll(x, q4, 1) * s_up


def _softmax_pv_group(maps, outs):
    scores = [[_dot(q, k) if t else _dot_nt(q, k) for k, t in zip(ks, fm)] for q, ks, _, _, fm in maps]
    yield
    probs = []
    for ss in scores:
        m = ss[0].max(axis=-1, keepdims=True)
        for s in ss[1:]:
            m = jnp.maximum(m, s.max(axis=-1, keepdims=True))
        probs.append([jnp.exp2(s - m).astype(BF) for s in ss])
    yield
    for ps, (_, _, vexts, half, fm) in zip(probs, maps):
        o = None
        for p, v, t in zip(ps, vexts, fm):
            part = _dot_nt(p, v) if t else _dot(p, v)
            o = part if o is None else o + part
        outs.append(o[:, half * 64:(half + 1) * 64] / o[:, (1 - half) * 64:(1 - half) * 64 + 1])
    yield


def _with_ones(v, half, feature_major):
    idx = lax.broadcasted_iota(jnp.int32, v.shape, 0 if feature_major else 1)
    return jnp.where(lax.shift_right_logical(idx, 6) == half, v, jnp.ones_like(v))


def _attend_heads(aq, cq, ka, va, kc, vc, fm, lam, gsub, lam_init, y_ref, group):
    aqb = (aq * (A_HEAD_DIM ** -0.5 * LOG2E)).astype(BF)
    cqb = (cq * (C_HEAD_DIM ** -0.5 * LOG2E)).astype(BF)
    rep = A_HEADS // A_KV_HEADS

    def feat(x, t, sl):
        return x[sl, :] if t else x[:, sl]

    maps = []
    for h in range(A_HEADS):
        g = h // rep
        sl = slice(g * A_HEAD_DIM, (g + 1) * A_HEAD_DIM)
        maps.append((aqb[:, h * A_HEAD_DIM:(h + 1) * A_HEAD_DIM], [feat(k, t, sl) for k, t in zip(ka, fm)],
                     [_with_ones(v, g, t) for v, t in zip(va, fm)], g, fm))
    for h in range(C_HEADS):
        slab = slice((h // 2) * 128, (h // 2 + 1) * 128)
        vh = [_with_ones(feat(v, t, slab), h % 2, t) for v, t in zip(vc, fm)]
        for j in range(2):
            sl = slice((2 * h + j) * C_HEAD_DIM, (2 * h + j + 1) * C_HEAD_DIM)
            maps.append((cqb[:, sl], [feat(k, t, sl) for k, t in zip(kc, fm)], vh, h % 2, fm))
    outs = []
    for i in range(0, len(maps), group):
        yield from _softmax_pv_group(maps[i:i + group], outs)
    for h in range(A_HEADS):
        y_ref[:, h * A_HEAD_DIM:(h + 1) * A_HEAD_DIM] = outs[h]
    vd = 2 * C_HEAD_DIM
    for h in range(C_HEADS):
        d = outs[A_HEADS + 2 * h] - lam * outs[A_HEADS + 2 * h + 1]
        y_ref[:, BRANCH_W + h * vd:BRANCH_W + (h + 1) * vd] = _rms(d, gsub) * (1.0 - lam_init)


def _lambda(cl):
    s1 = jnp.sum(cl[0:1] * cl[1:2], axis=-1, keepdims=True)
    s2 = jnp.sum(cl[2:3] * cl[3:4], axis=-1, keepdims=True)
    return jnp.exp(s1) - jnp.exp(s2)


def _attn_prompt_kernel(z_ref, gaq, gak, gcq, gck, gsub, cl_ref, y_ref, oak, oav, ock, ocv, *, lam_init):
    z = z_ref[...]
    ak = _head_rms(z[:, 256:384], A_HEAD_DIM, gak[...])
    av = z[:, 384:512]
    ck = _head_rms(z[:, 768:1024], C_HEAD_DIM, gck[...])
    cv = z[:, 1024:1280]
    oak[...] = ak.T
    oav[...] = av.T
    ock[...] = ck.T
    ocv[...] = cv.T
    aq = _head_rms(z[:, 0:256], A_HEAD_DIM, gaq[...])
    cq = _head_rms(z[:, 512:768], C_HEAD_DIM, gcq[...])
    lam = _lambda(cl_ref[...]) + lam_init
    yield
    yield from _attend_heads(aq, cq, [ak.astype(BF)], [av.astype(BF)], [ck.astype(BF)], [cv.astype(BF)], [False],
                             lam, gsub[...], lam_init, y_ref, group=A_HEADS + 2 * C_HEADS)


def _attn_sample_kernel(z_ref, gaq, gak, gcq, gck, gsub, cl_ref, cak, cav, cck, ccv,
                        cosa, sda, sua, cosc, sdc, suc, y_ref, ka_s, va_s, kc_s, vc_s, *, lam_init):
    qi = pl.program_id(1)
    qa4, qc4 = A_HEAD_DIM // 4, C_HEAD_DIM // 4

    @pl.when(qi == 0)
    def _():
        ak = _head_rms(z_ref[:, 256:384], A_HEAD_DIM, gak[...])
        ka_s[...] = _rope(ak, cosa[:, :128], sda[:, :128], sua[:, :128], qa4).astype(BF)
        va_s[...] = z_ref[:, 384:512].astype(BF)
        ck = _head_rms(z_ref[:, 768:1024], C_HEAD_DIM, gck[...])
        kc_s[...] = _rope(ck, cosc[...], sdc[...], suc[...], qc4).astype(BF)
        vc_s[...] = z_ref[:, 1024:1280].astype(BF)

    rows = pl.ds(pl.multiple_of(qi * TM, TM), TM)
    aq = _head_rms(z_ref[rows, 0:256], A_HEAD_DIM, gaq[...])
    aq = _rope(aq, cosa[rows, :], sda[rows, :], sua[rows, :], qa4)
    cq = _head_rms(z_ref[rows, 512:768], C_HEAD_DIM, gcq[...])
    cq = _rope(cq, cosc[rows, :], sdc[rows, :], suc[rows, :], qc4)
    lam = _lambda(cl_ref[...]) + lam_init
    yield from _attend_heads(aq, cq,
                             [cak[...].astype(BF), ka_s[...]], [cav[...].astype(BF), va_s[...]],
                             [cck[...].astype(BF), kc_s[...]], [ccv[...].astype(BF), vc_s[...]], [True, False],
                             lam, gsub[...], lam_init, y_ref, group=4)


def _gain_specs(l, nd):
    zeros = (0,) * (nd - 1)
    widths = (256, 128, 256, 256, 64)
    return [pl.BlockSpec((None, 1, w), lambda *a: (l, 0, 0)) for w in widths] + \
           [pl.BlockSpec((None, 4, C_HEAD_DIM), lambda *a: (l, 0, 0))]


N_VTILES = BRANCH_W // 128


def _state_blocks(nh, kd, vd):
    per = 128 // vd
    return [(h // per, slice((h * kd) % 128, (h * kd) % 128 + kd), slice((h % per) * vd, (h % per + 1) * vd))
            for h in range(nh)]


_B_BLOCKS = _state_blocks(B_HEADS, B_KEY_DIM, B_VAL_DIM)
_D_BLOCKS = _state_blocks(D_HEADS, D_KEY_DIM, D_VAL_DIM)


def _gla_constants():
    idx = np.arange(TT)
    scans, masks = [], []
    for rev in (False, True):
        eff = (TT - 1 - idx) if rev else idx
        et, eu = eff[:, None], eff[None, :]
        sc, mk = [], []
        for j in range(N_LEVELS):
            b = 1 << j
            start = et - et % b
            odd = (et // b) % 2 == 1
            sc.append(np.where(odd, (eu > start) & (eu <= et), (eu > et) & (eu <= start + b)))
            mk.append(((et // b) % 2 == 1) & (eu // b == et // b - 1))
        sc.append(eu <= et)
        sc.append(eu > et)
        mk.append(eu == et)
        scans.append(np.concatenate([np.stack(sc)] * 2, axis=-1))
        masks.append(np.stack(mk))
    return tuple(jnp.asarray(np.stack(t), BF) for t in (scans, masks))


def _gla_prepare(q, k, v, la2, scan_ref, d, rev, use_state):
    tt = q.shape[0]
    la_hi = la2.astype(BF)
    la_lo = (la2 - la_hi.astype(F32)).astype(BF)
    la_split = jnp.concatenate([la_hi, la_lo], axis=0)

    def factor(i):
        return jnp.exp2(_dot(scan_ref[d, i], la_split))

    qs, ks = [], []
    for j in range(N_LEVELS):
        f = factor(j)
        qs.append((q * f).astype(BF))
        ks.append((k * f).astype(BF).T)
    qs.append(q.astype(BF))
    ks.append(k.astype(BF).T)
    k_out = (k * factor(N_LEVELS + 1)).astype(BF)
    vb = v.astype(BF)
    q_in = d_tile = None
    if use_state:
        q_in = (q * factor(N_LEVELS)).astype(BF)
        ones = jnp.ones((2 * tt, 128), BF)
        d_tile = [jnp.exp2(_dot_tn(la_split[:, i * 128:(i + 1) * 128], ones)) for i in range(q.shape[1] // 128)]

    return dict(qs=qs, ks=ks, k_out=k_out, vb=vb, q_in=q_in, d_tile=d_tile)


def _lane_keep(x, lo, hi):
    lane = lax.broadcasted_iota(jnp.int32, x.shape, 1)
    return jnp.where((lane >= lo) & (lane < hi), x, jnp.zeros_like(x))


def _level_rows(j, rev):
    b = 1 << j
    if j == N_LEVELS or b < SCORE_ROWS:
        return None
    return [(i * b, (i + 1) * b) for i in range(TT // b) if (i % 2 == 1) != rev]


def _gla_scores(p, mask_ref, d, rev, nh, kd):
    out = []
    for h in range(nh):
        c, lo = divmod(h * kd, 128)
        blocks = [None] * (TT // SCORE_ROWS)

        def add(r0, val):
            for i in range(val.shape[0] // SCORE_ROWS):
                piece = val[i * SCORE_ROWS:(i + 1) * SCORE_ROWS]
                k = r0 // SCORE_ROWS + i
                blocks[k] = piece if blocks[k] is None else blocks[k] + piece

        for j in range(N_LEVELS + 1):
            kt = p["ks"][j]
            own = kt[c * 128 + lo:c * 128 + lo + kd, :]
            pieces = [jnp.zeros((lo, kt.shape[1]), BF)] if lo else []
            pieces.append(own)
            if lo + kd < 128:
                pieces.append(jnp.zeros((128 - lo - kd, kt.shape[1]), BF))
            kh = jnp.concatenate(pieces, axis=0) if len(pieces) > 1 else own
            qj = p["qs"][j][:, c * 128:(c + 1) * 128]
            ranges = _level_rows(j, rev)
            if ranges is None:
                add(0, _dot(qj, kh).astype(BF) * mask_ref[d, j])
            else:
                q_rows = jnp.concatenate([qj[a:b] for a, b in ranges], axis=0) if len(ranges) > 1 else qj[ranges[0][0]:ranges[0][1]]
                m_rows = [mask_ref[d, j, a:b, :] for a, b in ranges]
                t = _dot(q_rows, kh).astype(BF)
                off = 0
                for (a, b), m in zip(ranges, m_rows):
                    add(a, t[off:off + b - a] * m)
                    off += b - a
        zero = jnp.zeros((SCORE_ROWS, TT), BF)
        out.append(jnp.concatenate([z if z is not None else zero for z in blocks], axis=0))
    return out


def _gla_outputs(p, scs, blk_ref, st_ref, d, nh, kd, vd, use_state, o_ref, rows, col0, accumulate):
    per = 128 // vd
    for c in range(nh // per):
        vt = p["vb"][:, c * 128:(c + 1) * 128]
        kt = (c * per * kd) // 128
        ktile = slice(kt * 128, (kt + 1) * 128)
        o = None
        for i in range(per):
            t = _dot(scs[c * per + i], _lane_keep(vt, i * vd, (i + 1) * vd))
            o = t if o is None else o + t
        kv = _dot_tn(p["k_out"][:, ktile], vt) * blk_ref[c]
        if use_state:
            st = st_ref[d, c]
            o = o + _dot(p["q_in"][:, ktile], st.astype(BF))
            st_ref[d, c] = st * p["d_tile"][kt] + kv
        else:
            st_ref[d, c] = kv
        osl = slice(col0 + c * 128, col0 + (c + 1) * 128)
        if accumulate:
            o_ref[rows, osl] += o
        else:
            o_ref[rows, osl] = o


def _gla_kernel(*refs, layer, n_tiles, has_state, pump=None):
    pump = pump or (lambda: None)
    if has_state:
        (z_ref, lbl_ref, aw_ref, ab_ref, bng, dng, scan_ref, mask_ref, blkb_ref, blkd_ref, sh_in, sd_in,
         y_ref, o_scr, sth, std) = refs
        for packed, raw, blocks in ((sth, sh_in, _B_BLOCKS), (std, sd_in, _D_BLOCKS)):
            packed[...] = jnp.zeros(packed.shape, F32)
            for dd in range(2):
                for h, (c, r, ln) in enumerate(blocks):
                    packed[dd, c, r, ln] = raw[dd, h]
    else:
        (z_ref, lbl_ref, aw_ref, ab_ref, bng, dng, scan_ref, mask_ref, blkb_ref, blkd_ref,
         y_ref, sh_out, sd_out, o_scr, sth, std) = refs

    gates = []
    for d in range(2):
        logits = [lbl_ref[d, i:i + 1, :] for i in range(DEPTH)]
        mx = functools.reduce(jnp.maximum, logits)
        ex = [jnp.exp(t - mx) for t in logits]
        den = functools.reduce(lambda a, b: a + b, ex)
        ps = [t / den for t in ex]
        lb = functools.reduce(lambda a, b: a + b, ps[:layer + 1]) - ps[0]
        gates.append((lb, jnp.log(jnp.maximum(lb, LB_FLOOR)), jnp.log1p(-lb)))

    o_scr[...] = jnp.zeros(o_scr.shape, F32)

    def tile(i, use_state):
        preps = []
        for d in range(2):
            rev = d == 1
            lb, log_lb, log_1m = gates[d]
            j = (n_tiles - 1 - i) if rev else i
            rows = pl.ds(j * TT if isinstance(j, int) else pl.multiple_of(j * TT, TT), TT)
            bq = z_ref[rows, 0:256]
            zf = z_ref[rows, 768:1024] if rev else z_ref[rows, 512:768]
            b2 = log_1m + _log_sigmoid(zf)
            la = jnp.maximum(log_lb, b2) + jnp.log1p(jnp.exp(-jnp.abs(log_lb - b2)))
            kb = (1.0 - lb) * jax.nn.sigmoid(-zf)
            pre = _dot(z_ref[rows, 2048:2176].astype(BF), aw_ref[d]) + ab_ref[d]
            la_d = _log_sigmoid(pre) * (LOG2E / D_GATE_TAU)
            pb = _gla_prepare(_silu(bq), kb, z_ref[rows, 256:512], la * LOG2E, scan_ref, d, rev, use_state)
            pd = _gla_prepare(z_ref[rows, 1280:1408] * (D_KEY_DIM ** -0.5), z_ref[rows, 1408:1536],
                              z_ref[rows, 1536:1792], la_d, scan_ref, d, rev, use_state)
            preps.append((d, rows, pb, pd))
        pump()
        scores = [(_gla_scores(pb, mask_ref, d, d == 1, B_HEADS, B_KEY_DIM),
                   _gla_scores(pd, mask_ref, d, d == 1, D_HEADS, D_KEY_DIM)) for d, _, pb, pd in preps]
        pump()
        for (d, rows, pb, pd), (sb, sd) in zip(preps, scores):
            _gla_outputs(pb, sb, blkb_ref, sth, d, B_HEADS, B_KEY_DIM, B_VAL_DIM, use_state, o_scr, rows, 0, True)
            _gla_outputs(pd, sd, blkd_ref, std, d, D_HEADS, D_KEY_DIM, D_VAL_DIM, use_state, o_scr, rows, 256, True)
        pump()

    if has_state:
        def body(i, carry):
            tile(i, True)
            return carry

        lax.fori_loop(0, n_tiles, body, 0)
    else:
        for i in range(n_tiles):
            tile(i, i > 0)

    def finish(i, carry):
        rows = pl.ds(pl.multiple_of(i * TM, TM), TM)
        y_ref[rows, 0:256] = _head_rms(o_scr[rows, 0:256], B_VAL_DIM, bng[...]) * _silu(z_ref[rows, 1024:1280])
        y_ref[rows, 256:512] = _head_rms(o_scr[rows, 256:512], D_VAL_DIM, dng[...]) * _silu(z_ref[rows, 1792:2048])
        return carry

    lax.fori_loop(0, (n_tiles * TT) // TM, finish, 0)
    if not has_state:
        for packed, raw, blocks in ((sth, sh_out, _B_BLOCKS), (std, sd_out, _D_BLOCKS)):
            for dd in range(2):
                for h, (c, r, ln) in enumerate(blocks):
                    raw[dd, h] = packed[dd, c, r, ln]


def _gla_common_specs(l, nd):
    return [
        pl.BlockSpec((2, DEPTH, 256), lambda *a: (0, 0, 0)),
        pl.BlockSpec((None, 2, 128, 128), lambda *a: (l, 0, 0, 0)),
        pl.BlockSpec((None, 2, 1, 128), lambda *a: (l, 0, 0, 0)),
        pl.BlockSpec((None, 1, 256), lambda *a: (l, 0, 0)),
        pl.BlockSpec((None, 1, 256), lambda *a: (l, 0, 0)),
        pl.BlockSpec((2, N_LEVELS + 2, TT, 2 * TT), lambda *a: (0, 0, 0, 0)),
        pl.BlockSpec((2, N_LEVELS + 1, TT, TT), lambda *a: (0, 0, 0, 0)),
        pl.BlockSpec((N_VTILES, 128, 128), lambda *a: (0, 0, 0)),
        pl.BlockSpec((N_VTILES, 128, 128), lambda *a: (0, 0, 0)),
    ]


_STATE_SCRATCH = [pltpu.VMEM((2, N_VTILES, 128, 128), F32), pltpu.VMEM((2, N_VTILES, 128, 128), F32)]


def _state_pattern(blocks):
    pat = np.zeros((N_VTILES, 128, 128), np.float32)
    for c, r, ln in blocks:
        pat[c, r, ln] = 1.0
    return jnp.asarray(pat)


N_ATTN_IN = 7
N_ATTN_OUT = 5


def _ctx_mixers_kernel(*refs, layer, lam_init, n_carried):
    n_in = N_ATTN_IN + 1 + len(_gla_common_specs(0, 1))
    a_in, g_in, outs = refs[:N_ATTN_IN], refs[N_ATTN_IN:n_in], refs[n_in + n_carried:]
    y_ac, caches, y_bd, states = outs[0], outs[1:N_ATTN_OUT], outs[N_ATTN_OUT], outs[N_ATTN_OUT + 1:]
    attn = _attn_prompt_kernel(*a_in, y_ac, *caches, lam_init=lam_init)
    _gla_kernel(*g_in, y_bd, *states, layer=layer, n_tiles=SEQ // TT, has_state=False,
                pump=lambda: next(attn, None))
    for _ in attn:
        pass


def _ctx_mixers(l, zac, zbd, gains, c_lambda, small, lam_init, carried):
    stacked = [(BATCH, DEPTH, w, SEQ) for w in (128, 128, 256, 256)] + [
        (BATCH, DEPTH, 2, B_HEADS, B_KEY_DIM, B_VAL_DIM), (BATCH, DEPTH, 2, D_HEADS, D_KEY_DIM, D_VAL_DIM)]
    seq = lambda w: pl.BlockSpec((SEQ, w), lambda b: (b, 0))
    layer_block = lambda s: pl.BlockSpec((None, None) + s[2:], lambda b: (b, l) + (0,) * (len(s) - 2))
    in_specs = [seq(AC_W)] + _gain_specs(l, 1) + [seq(BD_PAD)] + _gla_common_specs(l, 1)
    n_carried = 0 if carried is None else len(carried)
    aliases = {}
    if carried is not None:
        out_idx = [1, 2, 3, 4, 6, 7]
        aliases = {len(in_specs) + k: out_idx[k] for k in range(n_carried)}
        in_specs = in_specs + [pl.BlockSpec(memory_space=pl.ANY)] * n_carried
    out = pl.pallas_call(
        functools.partial(_ctx_mixers_kernel, layer=l, lam_init=lam_init, n_carried=n_carried),
        grid=(BATCH,),
        in_specs=in_specs,
        out_specs=[seq(2 * BRANCH_W)] + [layer_block(s) for s in stacked[:4]] + [seq(2 * BRANCH_W)]
                  + [layer_block(s) for s in stacked[4:]],
        out_shape=[jax.ShapeDtypeStruct((N_PROMPT, 2 * BRANCH_W), F32)]
                  + [jax.ShapeDtypeStruct(s, F32) for s in stacked[:4]]
                  + [jax.ShapeDtypeStruct((N_PROMPT, 2 * BRANCH_W), F32)]
                  + [jax.ShapeDtypeStruct(s, F32) for s in stacked[4:]],
        scratch_shapes=[pltpu.VMEM((SEQ, 2 * BRANCH_W), F32)] + _STATE_SCRATCH,
        input_output_aliases=aliases,
        compiler_params=_params(("arbitrary",)),
        name=f"ctx_mixers{l}",
    )(zac, *gains, c_lambda, zbd, *small, *(carried or ()))
    return out[0], out[5], out[1:5] + out[6:]


def _attn_sample_body(*refs, lam_init):
    for _ in _attn_sample_kernel(*refs, lam_init=lam_init):
        pass


def _attn_sample(l, zac, gains, c_lambda, caches, tables, lam_init):
    first_blk = N_PROMPT // DEC_SEQ
    cache_specs = [pl.BlockSpec((None, None, w, PAST_LEN), lambda b, q: (b, l, 0, 0)) for w in (128, 128, 256, 256)]
    table_specs = [pl.BlockSpec((DEC_SEQ, 256), lambda b, q: (0, 0)) for _ in range(6)]
    return pl.pallas_call(
        functools.partial(_attn_sample_body, lam_init=lam_init),
        grid=(DEC_BATCH, DEC_SEQ // TM),
        in_specs=[pl.BlockSpec((DEC_SEQ, AC_W), lambda b, q: (first_blk + b, 0))] + _gain_specs(l, 2)
                 + cache_specs + table_specs,
        out_specs=pl.BlockSpec((TM, 2 * BRANCH_W), lambda b, q: (b * (DEC_SEQ // TM) + q, 0)),
        out_shape=jax.ShapeDtypeStruct((N_SAMPLE, 2 * BRANCH_W), F32),
        scratch_shapes=[pltpu.VMEM((DEC_SEQ, 128), BF), pltpu.VMEM((DEC_SEQ, 128), BF),
                        pltpu.VMEM((DEC_SEQ, 256), BF), pltpu.VMEM((DEC_SEQ, 256), BF)],
        compiler_params=_params(("arbitrary", "arbitrary")),
        name=f"attn_sample{l}",
    )(zac, *gains, c_lambda, *caches, *tables)


def _gla_sample(l, zbd, small, st_h, st_d):
    first_blk = N_PROMPT // DEC_SEQ
    return pl.pallas_call(
        functools.partial(_gla_kernel, layer=l, n_tiles=DEC_SEQ // TT, has_state=True),
        grid=(DEC_BATCH,),
        in_specs=[pl.BlockSpec((DEC_SEQ, BD_PAD), lambda b: (first_blk + b, 0))] + _gla_common_specs(l, 1) + [
            pl.BlockSpec((None, None, 2, B_HEADS, B_KEY_DIM, B_VAL_DIM), lambda b: (b, l, 0, 0, 0, 0)),
            pl.BlockSpec((None, None, 2, D_HEADS, D_KEY_DIM, D_VAL_DIM), lambda b: (b, l, 0, 0, 0, 0)),
        ],
        out_specs=pl.BlockSpec((DEC_SEQ, 2 * BRANCH_W), lambda b: (b, 0)),
        out_shape=jax.ShapeDtypeStruct((N_SAMPLE, 2 * BRANCH_W), F32),
        scratch_shapes=[pltpu.VMEM((DEC_SEQ, 2 * BRANCH_W), F32)] + _STATE_SCRATCH,
        compiler_params=_params(("arbitrary",)),
        name=f"gla_sample{l}",
    )(zbd, *small, st_h, st_d)


def _mix_kernel(*refs, n_x):
    x_refs = refs[:n_x]
    (h_ref, yp_ref, ys_ref, gp_ref, gs_ref, mod_ref, win_ref, wb_ref, wo_ref, o_ref, wm_s, wb_s, wo_s) = refs[n_x:]
    i = pl.program_id(0)

    @pl.when(i < NPRO)
    def _():
        rows = _stage_rows(i, W_ROWS)
        wm_s[_stage_rows(i, MERGE_ROWS), :] = win_ref[...].astype(BF)
        wb_s[rows, :] = wb_ref[...].astype(BF)
        wo_s[rows, :] = wo_ref[...].astype(BF)

    @pl.when(i >= NPRO)
    def _():
        t = i - NPRO
        hb = h_ref[...]
        yac = _tok_load(t, (yp_ref, ys_ref))
        ybd = _tok_load(t, (gp_ref, gs_ref))
        branches = (yac[:, :BRANCH_W], ybd[:, :BRANCH_W], yac[:, BRANCH_W:], ybd[:, BRANCH_W:])
        mixed = None
        for n, y in enumerate(branches):
            logits = _dot_nt(hb, wm_s[n * D_MODEL:(n + 1) * D_MODEL, :])
            term = jax.nn.sigmoid(logits) * _dot(y.astype(BF), wb_s[n * BRANCH_W:(n + 1) * BRANCH_W, :])
            mixed = term if mixed is None else mixed + term
        o_ref[...] = _tok_load(t, x_refs) + mod_ref[2:3, :] * _dot(mixed.astype(BF), wo_s[...])


def _mix(l, xs, h, y_p, y_s, g_p, g_s, mod, w_in_t, w_branch, w_out):
    chunk = pl.BlockSpec((None, W_ROWS, D_MODEL), lambda i: (l, _chunk_of_step(i), 0))
    return pl.pallas_call(
        functools.partial(_mix_kernel, n_x=len(xs)),
        grid=(NPRO + N_TILES,),
        in_specs=_tok_specs(D_MODEL, len(xs) == 2) + _tok_specs(D_MODEL, False)
                 + _tok_specs(2 * BRANCH_W, True) + _tok_specs(2 * BRANCH_W, True)
                 + [_mod_spec(l),
                    pl.BlockSpec((None, MERGE_ROWS, D_MODEL), lambda i: (l, MERGE_OFF // MERGE_ROWS + _chunk_of_step(i), 0)),
                    chunk, chunk],
        out_specs=pl.BlockSpec((TD, D_MODEL), lambda i: (_tile_of_step(i), 0)),
        out_shape=jax.ShapeDtypeStruct((N_TOK, D_MODEL), F32),
        scratch_shapes=[pltpu.VMEM((NPRO * MERGE_ROWS, D_MODEL), BF), pltpu.VMEM((D_MODEL, D_MODEL), BF),
                        pltpu.VMEM((D_MODEL, D_MODEL), BF)],
        compiler_params=_params(("arbitrary",)),
        name=f"mix{l}",
    )(*xs, h, y_p, y_s, g_p, g_s, mod, w_in_t, w_branch.reshape(DEPTH, N_BRANCH * BRANCH_W, D_MODEL), w_out)


def _ffn_kernel(*refs, n_out):
    (xp_ref, x_ref, xn_ref, mod_ref, g_ref, wup_ref, cw_ref, cb_ref, wdn_ref) = refs[:9]
    o_refs = refs[9:9 + n_out]
    hext, u_s, wup_s, wdn_s = refs[9 + n_out:]
    i = pl.program_id(0)

    @pl.when(i < NPRO)
    def _():
        wup_s[_stage_rows(i, W_ROWS), :] = wup_ref[...].astype(BF)
        wdn_s[_stage_rows(i, D_FF // NPRO), :] = wdn_ref[...].astype(BF)

    @pl.when(i >= NPRO)
    def _():
        t = i - NPRO
        ctx = t < PROMPT_TILES
        pos = jnp.maximum(t - PROMPT_TILES, 0) % SAMPLE_TILES_PER_SEQ
        seq_first = ctx | (pos == 0)
        seq_last = ctx | (pos == SAMPLE_TILES_PER_SEQ - 1)
        m = mod_ref[...]
        g = g_ref[...]

        def pre(x):
            return _rms(x, g) * (1.0 + m[4:5]) + m[3:4]

        h_a = pre(x_ref[0:SEQ, :])
        h_b = pre(x_ref[SEQ:TD, :])
        sub = lax.broadcasted_iota(jnp.int32, (HALO, D_MODEL), 0)
        gap = jnp.where(sub == HALO - 1, h_a[SEQ - HALO:, :], jnp.where(sub == 0, h_b[:HALO, :], 0.0))
        hext[0:HALO, :] = jnp.where(seq_first, 0.0, pre(xp_ref[...]))
        hext[FF_A0:FF_A0 + SEQ, :] = h_a
        hext[FF_A0 + SEQ:FF_B0, :] = jnp.where(ctx, 0.0, gap)
        hext[FF_B0:FF_B0 + SEQ, :] = h_b
        hext[FF_B0 + SEQ:FF_B0 + SEQ + HALO, :] = jnp.where(seq_last, 0.0, pre(xn_ref[...]))
        hext[FF_B0 + SEQ + HALO:, :] = jnp.zeros((HALO, D_MODEL), F32)
        hb = hext[...].astype(BF)

        def up_conv_act(c):
            halves = []
            for k, off in enumerate((0, D_FF)):
                cols = slice(off + c * FF_CHUNK, off + (c + 1) * FF_CHUNK)
                u = u_s.at[(2 * c + k) % u_s.shape[0]]
                u[...] = _dot(hb, wup_s[:, cols])
                taps = [u[FF_A0 - 1 + j:FF_A0 - 1 + j + FF_ROWS, :] * cw_ref[j:j + 1, cols] for j in range(CONV_WIDTH)]
                halves.append(taps[0] + taps[1] + taps[2] + cb_ref[:, cols])
            return (_silu(halves[1]) * halves[0]).astype(BF)

        n_chunks = D_FF // FF_CHUNK
        acts = [up_conv_act(c) for c in range(min(FF_LOOKAHEAD, n_chunks))]
        acc = None
        for c in range(n_chunks):
            if c + FF_LOOKAHEAD < n_chunks:
                acts.append(up_conv_act(c + FF_LOOKAHEAD))
            part = _dot(acts.pop(0), wdn_s[c * FF_CHUNK:(c + 1) * FF_CHUNK, :])
            acc = part if acc is None else acc + part
        gate = m[5:6]

        def store(o_ref):
            o_ref[0:SEQ, :] = x_ref[0:SEQ, :] + gate * acc[0:SEQ]
            o_ref[SEQ:TD, :] = x_ref[SEQ:TD, :] + gate * acc[FF_B0 - FF_A0:FF_B0 - FF_A0 + SEQ]

        if n_out == 1:
            store(o_refs[0])
        else:
            @pl.when(ctx)
            def _():
                store(o_refs[0])

            @pl.when(jnp.logical_not(ctx))
            def _():
                store(o_refs[1])


def _ffn(l, x, mod, norm_g, w_up, conv_w, conv_b, w_down, split_out):
    per = TD // HALO
    last_blk = N_TOK // HALO - 1
    if split_out:
        out_specs = [pl.BlockSpec((TD, D_MODEL), lambda i: (jnp.minimum(_tile_of_step(i), PROMPT_TILES - 1), 0)),
                     pl.BlockSpec((TD, D_MODEL), lambda i: (jnp.maximum(_tile_of_step(i) - PROMPT_TILES, 0), 0))]
        out_shape = [jax.ShapeDtypeStruct((N_PROMPT, D_MODEL), F32), jax.ShapeDtypeStruct((N_SAMPLE, D_MODEL), F32)]
    else:
        out_specs = [pl.BlockSpec((TD, D_MODEL), lambda i: (_tile_of_step(i), 0))]
        out_shape = [jax.ShapeDtypeStruct((N_TOK, D_MODEL), F32)]
    return pl.pallas_call(
        functools.partial(_ffn_kernel, n_out=len(out_specs)),
        grid=(NPRO + N_TILES,),
        in_specs=[
            pl.BlockSpec((HALO, D_MODEL), lambda i: (jnp.maximum(_tile_of_step(i) * per - 1, 0), 0)),
            pl.BlockSpec((TD, D_MODEL), lambda i: (_tile_of_step(i), 0)),
            pl.BlockSpec((HALO, D_MODEL), lambda i: (jnp.minimum((_tile_of_step(i) + 1) * per, last_blk), 0)),
            _mod_spec(l),
            pl.BlockSpec((None, 1, D_MODEL), lambda i: (l, 0, 0)),
            pl.BlockSpec((None, W_ROWS, 2 * D_FF), lambda i: (l, _chunk_of_step(i), 0)),
            pl.BlockSpec((None, CONV_WIDTH, 2 * D_FF), lambda i: (l, 0, 0)),
            pl.BlockSpec((None, 1, 2 * D_FF), lambda i: (l, 0, 0)),
            pl.BlockSpec((None, D_FF // NPRO, D_MODEL), lambda i: (l, _chunk_of_step(i), 0)),
        ],
        out_specs=out_specs,
        out_shape=out_shape,
        scratch_shapes=[pltpu.VMEM((FF_EXT, D_MODEL), F32), pltpu.VMEM((FF_U_SLOTS, FF_EXT, FF_CHUNK), F32),
                        pltpu.VMEM((D_MODEL, 2 * D_FF), BF), pltpu.VMEM((D_FF, D_MODEL), BF)],
        compiler_params=_params(("arbitrary",)),
        name=f"ffn{l}",
    )(x, x, x, mod, norm_g, w_up, conv_w, conv_b, w_down)


def kernel(x_prompt, x_sample, cache_a_k, cache_a_v, cache_c_k, cache_c_v, state_hgrn, state_gla, c, c_ctx, w_ada, b_ada, norm1_g, norm2_g, w_in, a_qn_g, a_kn_g, c_qn_g, c_kn_g, c_lambda, c_subln_g, b_lb_logits, b_norm_g, d_alpha_w, d_alpha_b, d_norm_g, w_branch, w_out, w_up, conv_w, conv_b, w_down):
    xs = (x_prompt.reshape(N_PROMPT, D_MODEL), x_sample.reshape(N_SAMPLE, D_MODEL))
    w_in_t = jnp.swapaxes(w_in, 1, 2)

    cond8 = jnp.concatenate([c_ctx[None, :], c, jnp.zeros((8 - N_GROUPS, D_MODEL), F32)], axis=0)
    mod = _modulation(cond8, w_ada, b_ada)[:, :N_GROUPS].reshape(DEPTH, N_GROUPS, N_MOD, D_MODEL)

    tile4 = lambda g, n: jnp.tile(g, (1, n)).reshape(DEPTH, 1, -1)
    gains = (tile4(a_qn_g, A_HEADS), tile4(a_kn_g, A_KV_HEADS), tile4(c_qn_g, 2 * C_HEADS), tile4(c_kn_g, 2 * C_HEADS),
             c_subln_g.reshape(DEPTH, 1, 2 * C_HEAD_DIM))
    fmaj = lambda t: jnp.moveaxis(t, 2, -1).reshape(DEC_BATCH, DEPTH, -1, PAST_LEN)
    caches = tuple(fmaj(t) for t in (cache_a_k, cache_a_v, cache_c_k, cache_c_v))
    tables = _rope_tables(DEC_SEQ, A_HEAD_DIM, A_HEADS) + _rope_tables(DEC_SEQ, C_HEAD_DIM, 2 * C_HEADS)

    aw = jnp.zeros((DEPTH, 2, 128, 128), F32)
    aw = aw.at[:, 0, 0:D_GATE_RANK].set(d_alpha_w[:, 0]).at[:, 1, D_GATE_RANK:2 * D_GATE_RANK].set(d_alpha_w[:, 1])
    gla_small = (b_lb_logits, aw.astype(BF), d_alpha_b.reshape(DEPTH, 2, 1, 128),
                 tile4(b_norm_g, B_HEADS), tile4(d_norm_g, D_HEADS)) + _gla_constants() + (
                     _state_pattern(_B_BLOCKS), _state_pattern(_D_BLOCKS))

    n1 = norm1_g.reshape(DEPTH, 1, D_MODEL)
    n2 = norm2_g.reshape(DEPTH, 1, D_MODEL)
    cb = conv_b.reshape(DEPTH, 1, 2 * D_FF)

    carried = None
    for l in range(DEPTH):
        lam_init = 0.8 - 0.6 * math.exp(-0.3 * l)
        zac, zbd, h = _inproj(l, xs, mod, n1, w_in_t)
        y_p, g_p, carried = _ctx_mixers(l, zac, zbd, gains, c_lambda, gla_small, lam_init, carried)
        y_s = _attn_sample(l, zac, gains, c_lambda, caches, tables, lam_init)
        g_s = _gla_sample(l, zbd, gla_small, state_hgrn, state_gla)
        x1 = _mix(l, xs, h, y_p, y_s, g_p, g_s, mod, w_in_t, w_branch, w_out)
        xs = tuple(_ffn(l, x1, mod, n2, w_up, conv_w, cb, w_down, split_out=(l == DEPTH - 1)))

    y_prompt = xs[0].reshape(BATCH, SEQ, D_MODEL)
    y_sample = xs[1].reshape(DEC_BATCH, DEC_SEQ, D_MODEL)
    stacked = list(carried)
    feat_shapes = ((A_KV_HEADS, A_HEAD_DIM), (A_KV_HEADS, A_HEAD_DIM), (C_HEADS, 2, C_HEAD_DIM), (C_HEADS, 2 * C_HEAD_DIM))
    ctx = [jnp.moveaxis(t.reshape((BATCH, DEPTH) + fs + (SEQ,)), -1, 2) for t, fs in zip(stacked[:4], feat_shapes)]
    return (y_prompt, y_sample) + tuple(ctx) + tuple(stacked[4:])
```

```python
import functools
import math

import numpy as np
import jax
import jax.numpy as jnp
from jax import lax
from jax.experimental import pallas as pl
from jax.experimental.pallas import tpu as pltpu

F32 = jnp.float32
BF = jnp.bfloat16

D_MODEL = 1024
BATCH = 16
SEQ = 256
DEPTH = 2
DEC_BATCH = 2
DEC_SEQ = 1024
PAST_LEN = 512
GRID_W = 64
ROPE_THETA = 10000.0
EPS = 1e-6
LB_FLOOR = 1e-30
N_MOD = 6
N_BRANCH = 4
BRANCH_W = 256
A_HEADS, A_KV_HEADS, A_HEAD_DIM = 4, 2, 64
C_HEADS, C_HEAD_DIM = 4, 32
B_HEADS, B_KEY_DIM, B_VAL_DIM = 4, 64, 64
D_HEADS, D_KEY_DIM, D_VAL_DIM = 4, 32, 64
D_GATE_RANK = 16
D_GATE_TAU = 16.0
D_FF = 2816
CONV_WIDTH = 3

N_PROMPT = BATCH * SEQ
N_SAMPLE = DEC_BATCH * DEC_SEQ
N_TOK = N_PROMPT + N_SAMPLE
TM = 256
TQ = 512
TD = 2 * SEQ
N_TILES = N_TOK // TD
PROMPT_TILES = N_PROMPT // TD
SAMPLE_TILES_PER_SEQ = DEC_SEQ // TD
N_GROUPS = 1 + DEC_BATCH

AC_W = 1280
BD_W = 2080
BD_PAD = 2176
SMALL_W = AC_W + BD_PAD
MERGE_OFF = AC_W + BD_W
MERGE_ROWS = 560
HALO = 8
FF_A0 = HALO
FF_B0 = FF_A0 + SEQ + HALO
FF_ROWS = 2 * SEQ + 2 * HALO
FF_EXT = FF_A0 + FF_ROWS + HALO
FF_CHUNK = 256
FF_U_SLOTS = 8
FF_LOOKAHEAD = 11
NPRO = 8
W_ROWS = D_MODEL // NPRO
LOG2E = 1.4426950408889634
VMEM_LIMIT = 56 * 1024 * 1024
TT = 128
N_LEVELS = 7
SCORE_ROWS = 16


def _dot(a, b):
    return jnp.dot(a, b, preferred_element_type=F32)


def _dot_nt(a, b):
    return lax.dot_general(a, b, (((1,), (1,)), ((), ())), preferred_element_type=F32)


def _dot_tn(a, b):
    return lax.dot_general(a, b, (((0,), (0,)), ((), ())), preferred_element_type=F32)


def _silu(x):
    return x * jax.nn.sigmoid(x)


def _log_sigmoid(x):
    return jnp.minimum(x, 0.0) - jnp.log1p(jnp.exp(-jnp.abs(x)))


def _rms(x, g):
    return x * lax.rsqrt(jnp.mean(x * x, axis=-1, keepdims=True) + EPS) * g


def _head_rms(x, head_dim, g):
    w = x.shape[-1]
    sh = int(math.log2(head_dim))
    r = lax.shift_right_logical(lax.broadcasted_iota(jnp.int32, (w, w), 0), sh)
    c = lax.shift_right_logical(lax.broadcasted_iota(jnp.int32, (w, w), 1), sh)
    bd = jnp.where(r == c, 1.0 / head_dim, 0.0).astype(BF)
    x2 = x * x
    hi = x2.astype(BF)
    lo = (x2 - hi.astype(F32)).astype(BF)
    ms = _dot(hi, bd) + _dot(lo, bd)
    return x * lax.rsqrt(ms + EPS) * g


def _group_of_tile(i):
    return jnp.where(i < PROMPT_TILES, 0, 1 + jnp.maximum(i - PROMPT_TILES, 0) // SAMPLE_TILES_PER_SEQ)


def _params(sem):
    return pltpu.CompilerParams(dimension_semantics=sem, vmem_limit_bytes=VMEM_LIMIT)


def _mod_kernel(cond_ref, w_ref, b_ref, o_ref):
    s = _silu(cond_ref[...])
    o_ref[...] = _dot(s.astype(BF), w_ref[...].astype(BF)) + b_ref[...]


def _modulation(cond8, w_ada, b_ada):
    nb = 1536
    return pl.pallas_call(
        _mod_kernel,
        grid=(DEPTH, N_MOD * D_MODEL // nb),
        in_specs=[
            pl.BlockSpec((8, D_MODEL), lambda l, j: (0, 0)),
            pl.BlockSpec((None, D_MODEL, nb), lambda l, j: (l, 0, j)),
            pl.BlockSpec((None, 1, nb), lambda l, j: (l, 0, j)),
        ],
        out_specs=pl.BlockSpec((None, 8, nb), lambda l, j: (l, 0, j)),
        out_shape=jax.ShapeDtypeStruct((DEPTH, 8, N_MOD * D_MODEL), F32),
        compiler_params=_params(("arbitrary", "arbitrary")),
        name="modulation",
    )(cond8, w_ada, b_ada.reshape(DEPTH, 1, N_MOD * D_MODEL))


def _tile_of_step(i):
    return jnp.maximum(i - NPRO, 0)


def _chunk_of_step(i):
    return jnp.minimum(i, NPRO - 1)


def _tok_specs(width, split):
    if not split:
        return [pl.BlockSpec((TD, width), lambda i: (_tile_of_step(i), 0))]
    return [pl.BlockSpec((TD, width), lambda i: (jnp.minimum(_tile_of_step(i), PROMPT_TILES - 1), 0)),
            pl.BlockSpec((TD, width), lambda i: (jnp.maximum(_tile_of_step(i) - PROMPT_TILES, 0), 0))]


def _tok_load(t, refs):
    if len(refs) == 1:
        return refs[0][...]
    return jnp.where(t < PROMPT_TILES, refs[0][...], refs[1][...])


def _mod_spec(l):
    return pl.BlockSpec((None, None, N_MOD, D_MODEL), lambda i: (l, _group_of_tile(_tile_of_step(i)), 0, 0))


def _stage_rows(i, n):
    return pl.ds(pl.multiple_of(i * n, n), n)


def _inproj_kernel(*refs, n_x):
    x_refs = refs[:n_x]
    mod_ref, g_ref, w_ref, zac_ref, zbd_ref, h_ref, w_s = refs[n_x:]
    i = pl.program_id(0)

    @pl.when(i < NPRO)
    def _():
        w_s[_stage_rows(i, SMALL_W // NPRO), :] = w_ref[...].astype(BF)

    @pl.when(i >= NPRO)
    def _():
        m = mod_ref[...]
        h = _rms(_tok_load(i - NPRO, x_refs), g_ref[...]) * (1.0 + m[1:2]) + m[0:1]
        hb = h.astype(BF)
        h_ref[...] = hb
        zac_ref[...] = _dot_nt(hb, w_s[:AC_W, :])
        zbd_ref[...] = _dot_nt(hb, w_s[AC_W:, :])


def _inproj(l, xs, mod, norm_g, w_in_t):
    split = len(xs) == 2
    return pl.pallas_call(
        functools.partial(_inproj_kernel, n_x=len(xs)),
        grid=(NPRO + N_TILES,),
        in_specs=_tok_specs(D_MODEL, split) + [
            _mod_spec(l),
            pl.BlockSpec((None, 1, D_MODEL), lambda i: (l, 0, 0)),
            pl.BlockSpec((None, SMALL_W // NPRO, D_MODEL), lambda i: (l, _chunk_of_step(i), 0)),
        ],
        out_specs=[
            pl.BlockSpec((TD, AC_W), lambda i: (_tile_of_step(i), 0)),
            pl.BlockSpec((TD, BD_PAD), lambda i: (_tile_of_step(i), 0)),
            pl.BlockSpec((TD, D_MODEL), lambda i: (_tile_of_step(i), 0)),
        ],
        out_shape=[
            jax.ShapeDtypeStruct((N_TOK, AC_W), F32),
            jax.ShapeDtypeStruct((N_TOK, BD_PAD), F32),
            jax.ShapeDtypeStruct((N_TOK, D_MODEL), BF),
        ],
        scratch_shapes=[pltpu.VMEM((SMALL_W, D_MODEL), BF)],
        compiler_params=_params(("arbitrary",)),
        name=f"inproj{l}",
    )(*xs, mod, norm_g, w_in_t)


def _rope_tables(n_tokens, head_dim, n_rep):
    rows = n_tokens // GRID_W
    row = np.repeat(np.arange(rows), GRID_W).astype(np.float64)
    col = np.tile(np.arange(GRID_W), rows).astype(np.float64)
    half = head_dim // 2
    q4 = head_dim // 4
    freqs = ROPE_THETA ** (-np.arange(0, half, 2, dtype=np.float64) / half)
    ang_r = row[:, None] * freqs
    ang_c = col[:, None] * freqs
    ang = np.concatenate([ang_r, ang_r, ang_c, ang_c], axis=-1)
    cos, sin = np.cos(ang), np.sin(ang)
    first = (np.arange(head_dim) % (2 * q4)) < q4
    s_dn = np.where(first, -sin, 0.0)
    s_up = np.where(first, 0.0, sin)
    return tuple(jnp.asarray(np.tile(t, (1, n_rep)), dtype=F32) for t in (cos, s_dn, s_up))


def _rope(x, cos, s_dn, s_up, q4):
    w = x.shape[-1]
    return x * cos + pltpu.roll(x, w - q4, 1) * s_dn + pltpu.roll(x, q4, 1) * s_up


def _softmax_pv_group(maps, outs):
    scores = [[_dot(q, k) if t else _dot_nt(q, k) for k, t in zip(ks, fm)] for q, ks, _, _, fm in maps]
    yield
    probs = []
    for ss in scores:
        m = ss[0].max(axis=-1, keepdims=True)
        for s in ss[1:]:
            m = jnp.maximum(m, s.max(axis=-1, keepdims=True))
        probs.append([jnp.exp2(s - m).astype(BF) for s in ss])
    yield
    for ps, (_, _, vexts, half, fm) in zip(probs, maps):
        o = None
        for p, v, t in zip(ps, vexts, fm):
            part = _dot_nt(p, v) if t else _dot(p, v)
            o = part if o is None else o + part
        outs.append(o[:, half * 64:(half + 1) * 64] / o[:, (1 - half) * 64:(1 - half) * 64 + 1])
    yield


def _with_ones(v, half, feature_major):
    idx = lax.broadcasted_iota(jnp.int32, v.shape, 0 if feature_major else 1)
    return jnp.where(lax.shift_right_logical(idx, 6) == half, v, jnp.ones_like(v))


def _attend_heads(aq, cq, ka, va, kc, vc, fm, lam, gsub, lam_init, y_ref, group):
    aqb = (aq * (A_HEAD_DIM ** -0.5 * LOG2E)).astype(BF)
    cqb = (cq * (C_HEAD_DIM ** -0.5 * LOG2E)).astype(BF)
    rep = A_HEADS // A_KV_HEADS

    def feat(x, t, sl):
        return x[sl, :] if t else x[:, sl]

    maps = []
    for h in range(A_HEADS):
        g = h // rep
        sl = slice(g * A_HEAD_DIM, (g + 1) * A_HEAD_DIM)
        maps.append((aqb[:, h * A_HEAD_DIM:(h + 1) * A_HEAD_DIM], [feat(k, t, sl) for k, t in zip(ka, fm)],
                     [_with_ones(v, g, t) for v, t in zip(va, fm)], g, fm))
    for h in range(C_HEADS):
        slab = slice((h // 2) * 128, (h // 2 + 1) * 128)
        vh = [_with_ones(feat(v, t, slab), h % 2, t) for v, t in zip(vc, fm)]
        for j in range(2):
            sl = slice((2 * h + j) * C_HEAD_DIM, (2 * h + j + 1) * C_HEAD_DIM)
            maps.append((cqb[:, sl], [feat(k, t, sl) for k, t in zip(kc, fm)], vh, h % 2, fm))
    outs = []
    for i in range(0, len(maps), group):
        yield from _softmax_pv_group(maps[i:i + group], outs)
    for h in range(A_HEADS):
        y_ref[:, h * A_HEAD_DIM:(h + 1) * A_HEAD_DIM] = outs[h]
    vd = 2 * C_HEAD_DIM
    for h in range(C_HEADS):
        d = outs[A_HEADS + 2 * h] - lam * outs[A_HEADS + 2 * h + 1]
        y_ref[:, BRANCH_W + h * vd:BRANCH_W + (h + 1) * vd] = _rms(d, gsub) * (1.0 - lam_init)


def _lambda(cl):
    s1 = jnp.sum(cl[0:1] * cl[1:2], axis=-1, keepdims=True)
    s2 = jnp.sum(cl[2:3] * cl[3:4], axis=-1, keepdims=True)
    return jnp.exp(s1) - jnp.exp(s2)


def _attn_prompt_kernel(z_ref, gaq, gak, gcq, gck, gsub, cl_ref, y_ref, oak, oav, ock, ocv, *, lam_init):
    z = z_ref[...]
    ak = _head_rms(z[:, 256:384], A_HEAD_DIM, gak[...])
    av = z[:, 384:512]
    ck = _head_rms(z[:, 768:1024], C_HEAD_DIM, gck[...])
    cv = z[:, 1024:1280]
    oak[...] = ak.T
    oav[...] = av.T
    ock[...] = ck.T
    ocv[...] = cv.T
    aq = _head_rms(z[:, 0:256], A_HEAD_DIM, gaq[...])
    cq = _head_rms(z[:, 512:768], C_HEAD_DIM, gcq[...])
    lam = _lambda(cl_ref[...]) + lam_init
    yield
    yield from _attend_heads(aq, cq, [ak.astype(BF)], [av.astype(BF)], [ck.astype(BF)], [cv.astype(BF)], [False],
                             lam, gsub[...], lam_init, y_ref, group=A_HEADS + 2 * C_HEADS)


def _attn_sample_kernel(z_ref, gaq, gak, gcq, gck, gsub, cl_ref, cak, cav, cck, ccv,
                        cosa, sda, sua, cosc, sdc, suc, y_ref, ka_s, va_s, kc_s, vc_s, *, lam_init):
    qi = pl.program_id(1)
    qa4, qc4 = A_HEAD_DIM // 4, C_HEAD_DIM // 4

    @pl.when(qi == 0)
    def _():
        ak = _head_rms(z_ref[:, 256:384], A_HEAD_DIM, gak[...])
        ka_s[...] = _rope(ak, cosa[:, :128], sda[:, :128], sua[:, :128], qa4).astype(BF)
        va_s[...] = z_ref[:, 384:512].astype(BF)
        ck = _head_rms(z_ref[:, 768:1024], C_HEAD_DIM, gck[...])
        kc_s[...] = _rope(ck, cosc[...], sdc[...], suc[...], qc4).astype(BF)
        vc_s[...] = z_ref[:, 1024:1280].astype(BF)

    rows = pl.ds(pl.multiple_of(qi * TQ, TQ), TQ)
    aq = _head_rms(z_ref[rows, 0:256], A_HEAD_DIM, gaq[...])
    aq = _rope(aq, cosa[rows, :], sda[rows, :], sua[rows, :], qa4)
    cq = _head_rms(z_ref[rows, 512:768], C_HEAD_DIM, gcq[...])
    cq = _rope(cq, cosc[rows, :], sdc[rows, :], suc[rows, :], qc4)
    lam = _lambda(cl_ref[...]) + lam_init
    yield from _attend_heads(aq, cq,
                             [cak[...].astype(BF), ka_s[...]], [cav[...].astype(BF), va_s[...]],
                             [cck[...].astype(BF), kc_s[...]], [ccv[...].astype(BF), vc_s[...]], [True, False],
                             lam, gsub[...], lam_init, y_ref, group=4)


def _gain_specs(l, nd):
    zeros = (0,) * (nd - 1)
    widths = (256, 128, 256, 256, 64)
    return [pl.BlockSpec((None, 1, w), lambda *a: (l, 0, 0)) for w in widths] + \
           [pl.BlockSpec((None, 4, C_HEAD_DIM), lambda *a: (l, 0, 0))]


N_VTILES = BRANCH_W // 128


def _state_blocks(nh, kd, vd):
    per = 128 // vd
    return [(h // per, slice((h * kd) % 128, (h * kd) % 128 + kd), slice((h % per) * vd, (h % per + 1) * vd))
            for h in range(nh)]


_B_BLOCKS = _state_blocks(B_HEADS, B_KEY_DIM, B_VAL_DIM)
_D_BLOCKS = _state_blocks(D_HEADS, D_KEY_DIM, D_VAL_DIM)


def _gla_constants():
    idx = np.arange(TT)
    scans, masks = [], []
    for rev in (False, True):
        eff = (TT - 1 - idx) if rev else idx
        et, eu = eff[:, None], eff[None, :]
        sc, mk = [], []
        for j in range(N_LEVELS):
            b = 1 << j
            start = et - et % b
            odd = (et // b) % 2 == 1
            sc.append(np.where(odd, (eu > start) & (eu <= et), (eu > et) & (eu <= start + b)))
            mk.append(((et // b) % 2 == 1) & (eu // b == et // b - 1))
        sc.append(eu <= et)
        sc.append(eu > et)
        mk.append(eu == et)
        scans.append(np.concatenate([np.concatenate(sc, axis=0)] * 2, axis=-1))
        masks.append(np.stack(mk))
    return tuple(jnp.asarray(np.stack(t), BF) for t in (scans, masks))


def _gla_prepare(q, k, v, la2, scan_ref, d, rev, use_state):
    tt = q.shape[0]
    la_hi = la2.astype(BF)
    la_lo = (la2 - la_hi.astype(F32)).astype(BF)
    la_split = jnp.concatenate([la_hi, la_lo], axis=0)

    sums = _dot(scan_ref[d], la_split)

    def factor(i):
        return jnp.exp2(sums[i * tt:(i + 1) * tt])

    qs, ks = [], []
    for j in range(N_LEVELS):
        f = factor(j)
        qs.append((q * f).astype(BF))
        ks.append((k * f).astype(BF).T)
    qs.append(q.astype(BF))
    ks.append(k.astype(BF).T)
    k_out = (k * factor(N_LEVELS + 1)).astype(BF)
    vb = v.astype(BF)
    q_in = d_tile = None
    if use_state:
        q_in = (q * factor(N_LEVELS)).astype(BF)
        ones = jnp.ones((2 * tt, 128), BF)
        d_tile = [jnp.exp2(_dot_tn(la_split[:, i * 128:(i + 1) * 128], ones)) for i in range(q.shape[1] // 128)]

    return dict(qs=qs, ks=ks, k_out=k_out, vb=vb, q_in=q_in, d_tile=d_tile)


def _lane_keep(x, lo, hi):
    lane = lax.broadcasted_iota(jnp.int32, x.shape, 1)
    return jnp.where((lane >= lo) & (lane < hi), x, jnp.zeros_like(x))


def _level_rows(j, rev):
    b = 1 << j
    if j == N_LEVELS or b < SCORE_ROWS:
        return None
    return [(i * b, (i + 1) * b) for i in range(TT // b) if (i % 2 == 1) != rev]


def _gla_scores(p, mask_ref, d, rev, nh, kd):
    out = []
    for h in range(nh):
        c, lo = divmod(h * kd, 128)
        blocks = [None] * (TT // SCORE_ROWS)

        def add(r0, val):
            for i in range(val.shape[0] // SCORE_ROWS):
                piece = val[i * SCORE_ROWS:(i + 1) * SCORE_ROWS]
                k = r0 // SCORE_ROWS + i
                blocks[k] = piece if blocks[k] is None else blocks[k] + piece

        for j in range(N_LEVELS + 1):
            kt = p["ks"][j]
            own = kt[c * 128 + lo:c * 128 + lo + kd, :]
            pieces = [jnp.zeros((lo, kt.shape[1]), BF)] if lo else []
            pieces.append(own)
            if lo + kd < 128:
                pieces.append(jnp.zeros((128 - lo - kd, kt.shape[1]), BF))
            kh = jnp.concatenate(pieces, axis=0) if len(pieces) > 1 else own
            qj = p["qs"][j][:, c * 128:(c + 1) * 128]
            ranges = _level_rows(j, rev)
            if ranges is None:
                add(0, _dot(qj, kh).astype(BF) * mask_ref[d, j])
            else:
                q_rows = jnp.concatenate([qj[a:b] for a, b in ranges], axis=0) if len(ranges) > 1 else qj[ranges[0][0]:ranges[0][1]]
                m_rows = [mask_ref[d, j, a:b, :] for a, b in ranges]
                t = _dot(q_rows, kh).astype(BF)
                off = 0
                for (a, b), m in zip(ranges, m_rows):
                    add(a, t[off:off + b - a] * m)
                    off += b - a
        zero = jnp.zeros((SCORE_ROWS, TT), BF)
        out.append(jnp.concatenate([z if z is not None else zero for z in blocks], axis=0))
    return out


def _gla_outputs(p, scs, blk_ref, st_ref, d, nh, kd, vd, use_state, o_ref, rows, col0, accumulate):
    per = 128 // vd
    for c in range(nh // per):
        vt = p["vb"][:, c * 128:(c + 1) * 128]
        kt = (c * per * kd) // 128
        ktile = slice(kt * 128, (kt + 1) * 128)
        o = None
        for i in range(per):
            t = _dot(scs[c * per + i], _lane_keep(vt, i * vd, (i + 1) * vd))
            o = t if o is None else o + t
        kv = _dot_tn(p["k_out"][:, ktile], vt) * blk_ref[c]
        if use_state:
            st = st_ref[d, c]
            o = o + _dot(p["q_in"][:, ktile], st.astype(BF))
            st_ref[d, c] = st * p["d_tile"][kt] + kv
        else:
            st_ref[d, c] = kv
        osl = slice(col0 + c * 128, col0 + (c + 1) * 128)
        if accumulate:
            o_ref[rows, osl] += o
        else:
            o_ref[rows, osl] = o


def _gla_kernel(*refs, layer, n_tiles, has_state, pump=None):
    pump = pump or (lambda: None)
    if has_state:
        (z_ref, lbl_ref, aw_ref, ab_ref, bng, dng, scan_ref, mask_ref, blkb_ref, blkd_ref, sh_in, sd_in,
         y_ref, o_scr, sth, std) = refs
        for packed, raw, blocks in ((sth, sh_in, _B_BLOCKS), (std, sd_in, _D_BLOCKS)):
            packed[...] = jnp.zeros(packed.shape, F32)
            for dd in range(2):
                for h, (c, r, ln) in enumerate(blocks):
                    packed[dd, c, r, ln] = raw[dd, h]
    else:
        (z_ref, lbl_ref, aw_ref, ab_ref, bng, dng, scan_ref, mask_ref, blkb_ref, blkd_ref,
         y_ref, sh_out, sd_out, o_scr, sth, std) = refs

    gates = []
    for d in range(2):
        logits = [lbl_ref[d, i:i + 1, :] for i in range(DEPTH)]
        mx = functools.reduce(jnp.maximum, logits)
        ex = [jnp.exp(t - mx) for t in logits]
        den = functools.reduce(lambda a, b: a + b, ex)
        ps = [t / den for t in ex]
        lb = functools.reduce(lambda a, b: a + b, ps[:layer + 1]) - ps[0]
        gates.append((lb, jnp.log(jnp.maximum(lb, LB_FLOOR)), jnp.log1p(-lb)))

    o_scr[...] = jnp.zeros(o_scr.shape, F32)

    def tile(i, use_state):
        preps = []
        for d in range(2):
            rev = d == 1
            lb, log_lb, log_1m = gates[d]
            j = (n_tiles - 1 - i) if rev else i
            rows = pl.ds(j * TT if isinstance(j, int) else pl.multiple_of(j * TT, TT), TT)
            bq = z_ref[rows, 0:256]
            zf = z_ref[rows, 768:1024] if rev else z_ref[rows, 512:768]
            b2 = log_1m + _log_sigmoid(zf)
            la = jnp.maximum(log_lb, b2) + jnp.log1p(jnp.exp(-jnp.abs(log_lb - b2)))
            kb = (1.0 - lb) * jax.nn.sigmoid(-zf)
            pre = _dot(z_ref[rows, 2048:2176].astype(BF), aw_ref[d]) + ab_ref[d]
            la_d = _log_sigmoid(pre) * (LOG2E / D_GATE_TAU)
            pb = _gla_prepare(_silu(bq), kb, z_ref[rows, 256:512], la * LOG2E, scan_ref, d, rev, use_state)
            pd = _gla_prepare(z_ref[rows, 1280:1408] * (D_KEY_DIM ** -0.5), z_ref[rows, 1408:1536],
                              z_ref[rows, 1536:1792], la_d, scan_ref, d, rev, use_state)
            preps.append((d, rows, pb, pd))
        pump()
        scores = [(_gla_scores(pb, mask_ref, d, d == 1, B_HEADS, B_KEY_DIM),
                   _gla_scores(pd, mask_ref, d, d == 1, D_HEADS, D_KEY_DIM)) for d, _, pb, pd in preps]
        pump()
        for (d, rows, pb, pd), (sb, sd) in zip(preps, scores):
            _gla_outputs(pb, sb, blkb_ref, sth, d, B_HEADS, B_KEY_DIM, B_VAL_DIM, use_state, o_scr, rows, 0, True)
            _gla_outputs(pd, sd, blkd_ref, std, d, D_HEADS, D_KEY_DIM, D_VAL_DIM, use_state, o_scr, rows, 256, True)
        pump()

    if has_state:
        def body(i, carry):
            tile(i, True)
            return carry

        lax.fori_loop(0, n_tiles, body, 0)
    else:
        for i in range(n_tiles):
            tile(i, i > 0)

    def finish(i, carry):
        rows = pl.ds(pl.multiple_of(i * TM, TM), TM)
        y_ref[rows, 0:256] = _head_rms(o_scr[rows, 0:256], B_VAL_DIM, bng[...]) * _silu(z_ref[rows, 1024:1280])
        y_ref[rows, 256:512] = _head_rms(o_scr[rows, 256:512], D_VAL_DIM, dng[...]) * _silu(z_ref[rows, 1792:2048])
        return carry

    lax.fori_loop(0, (n_tiles * TT) // TM, finish, 0)
    if not has_state:
        for packed, raw, blocks in ((sth, sh_out, _B_BLOCKS), (std, sd_out, _D_BLOCKS)):
            for dd in range(2):
                for h, (c, r, ln) in enumerate(blocks):
                    raw[dd, h] = packed[dd, c, r, ln]


def _gla_common_specs(l, nd):
    return [
        pl.BlockSpec((2, DEPTH, 256), lambda *a: (0, 0, 0)),
        pl.BlockSpec((None, 2, 128, 128), lambda *a: (l, 0, 0, 0)),
        pl.BlockSpec((None, 2, 1, 128), lambda *a: (l, 0, 0, 0)),
        pl.BlockSpec((None, 1, 256), lambda *a: (l, 0, 0)),
        pl.BlockSpec((None, 1, 256), lambda *a: (l, 0, 0)),
        pl.BlockSpec((2, (N_LEVELS + 2) * TT, 2 * TT), lambda *a: (0, 0, 0)),
        pl.BlockSpec((2, N_LEVELS + 1, TT, TT), lambda *a: (0, 0, 0, 0)),
        pl.BlockSpec((N_VTILES, 128, 128), lambda *a: (0, 0, 0)),
        pl.BlockSpec((N_VTILES, 128, 128), lambda *a: (0, 0, 0)),
    ]


_STATE_SCRATCH = [pltpu.VMEM((2, N_VTILES, 128, 128), F32), pltpu.VMEM((2, N_VTILES, 128, 128), F32)]


def _state_pattern(blocks):
    pat = np.zeros((N_VTILES, 128, 128), np.float32)
    for c, r, ln in blocks:
        pat[c, r, ln] = 1.0
    return jnp.asarray(pat)


N_ATTN_IN = 7
N_ATTN_OUT = 5


def _ctx_mixers_kernel(*refs, layer, lam_init, n_carried):
    n_in = N_ATTN_IN + 1 + len(_gla_common_specs(0, 1))
    a_in, g_in, outs = refs[:N_ATTN_IN], refs[N_ATTN_IN:n_in], refs[n_in + n_carried:]
    y_ac, caches, y_bd, states = outs[0], outs[1:N_ATTN_OUT], outs[N_ATTN_OUT], outs[N_ATTN_OUT + 1:]
    attn = _attn_prompt_kernel(*a_in, y_ac, *caches, lam_init=lam_init)
    _gla_kernel(*g_in, y_bd, *states, layer=layer, n_tiles=SEQ // TT, has_state=False,
                pump=lambda: next(attn, None))
    for _ in attn:
        pass


def _ctx_mixers(l, zac, zbd, gains, c_lambda, small, lam_init, carried):
    stacked = [(BATCH, DEPTH, w, SEQ) for w in (128, 128, 256, 256)] + [
        (BATCH, DEPTH, 2, B_HEADS, B_KEY_DIM, B_VAL_DIM), (BATCH, DEPTH, 2, D_HEADS, D_KEY_DIM, D_VAL_DIM)]
    seq = lambda w: pl.BlockSpec((SEQ, w), lambda b: (b, 0))
    layer_block = lambda s: pl.BlockSpec((None, None) + s[2:], lambda b: (b, l) + (0,) * (len(s) - 2))
    in_specs = [seq(AC_W)] + _gain_specs(l, 1) + [seq(BD_PAD)] + _gla_common_specs(l, 1)
    n_carried = 0 if carried is None else len(carried)
    aliases = {}
    if carried is not None:
        out_idx = [1, 2, 3, 4, 6, 7]
        aliases = {len(in_specs) + k: out_idx[k] for k in range(n_carried)}
        in_specs = in_specs + [pl.BlockSpec(memory_space=pl.ANY)] * n_carried
    out = pl.pallas_call(
        functools.partial(_ctx_mixers_kernel, layer=l, lam_init=lam_init, n_carried=n_carried),
        grid=(BATCH,),
        in_specs=in_specs,
        out_specs=[seq(2 * BRANCH_W)] + [layer_block(s) for s in stacked[:4]] + [seq(2 * BRANCH_W)]
                  + [layer_block(s) for s in stacked[4:]],
        out_shape=[jax.ShapeDtypeStruct((N_PROMPT, 2 * BRANCH_W), F32)]
                  + [jax.ShapeDtypeStruct(s, F32) for s in stacked[:4]]
                  + [jax.ShapeDtypeStruct((N_PROMPT, 2 * BRANCH_W), F32)]
                  + [jax.ShapeDtypeStruct(s, F32) for s in stacked[4:]],
        scratch_shapes=[pltpu.VMEM((SEQ, 2 * BRANCH_W), F32)] + _STATE_SCRATCH,
        input_output_aliases=aliases,
        compiler_params=_params(("arbitrary",)),
        name=f"ctx_mixers{l}",
    )(zac, *gains, c_lambda, zbd, *small, *(carried or ()))
    return out[0], out[5], out[1:5] + out[6:]


def _attn_sample_body(*refs, lam_init):
    for _ in _attn_sample_kernel(*refs, lam_init=lam_init):
        pass


def _attn_sample(l, zac, gains, c_lambda, caches, tables, lam_init):
    first_blk = N_PROMPT // DEC_SEQ
    cache_specs = [pl.BlockSpec((None, None, w, PAST_LEN), lambda b, q: (b, l, 0, 0)) for w in (128, 128, 256, 256)]
    table_specs = [pl.BlockSpec((DEC_SEQ, 256), lambda b, q: (0, 0)) for _ in range(6)]
    return pl.pallas_call(
        functools.partial(_attn_sample_body, lam_init=lam_init),
        grid=(DEC_BATCH, DEC_SEQ // TQ),
        in_specs=[pl.BlockSpec((DEC_SEQ, AC_W), lambda b, q: (first_blk + b, 0))] + _gain_specs(l, 2)
                 + cache_specs + table_specs,
        out_specs=pl.BlockSpec((TQ, 2 * BRANCH_W), lambda b, q: (b * (DEC_SEQ // TQ) + q, 0)),
        out_shape=jax.ShapeDtypeStruct((N_SAMPLE, 2 * BRANCH_W), F32),
        scratch_shapes=[pltpu.VMEM((DEC_SEQ, 128), BF), pltpu.VMEM((DEC_SEQ, 128), BF),
                        pltpu.VMEM((DEC_SEQ, 256), BF), pltpu.VMEM((DEC_SEQ, 256), BF)],
        compiler_params=_params(("arbitrary", "arbitrary")),
        name=f"attn_sample{l}",
    )(zac, *gains, c_lambda, *caches, *tables)


def _gla_sample(l, zbd, small, st_h, st_d):
    first_blk = N_PROMPT // DEC_SEQ
    return pl.pallas_call(
        functools.partial(_gla_kernel, layer=l, n_tiles=DEC_SEQ // TT, has_state=True),
        grid=(DEC_BATCH,),
        in_specs=[pl.BlockSpec((DEC_SEQ, BD_PAD), lambda b: (first_blk + b, 0))] + _gla_common_specs(l, 1) + [
            pl.BlockSpec((None, None, 2, B_HEADS, B_KEY_DIM, B_VAL_DIM), lambda b: (b, l, 0, 0, 0, 0)),
            pl.BlockSpec((None, None, 2, D_HEADS, D_KEY_DIM, D_VAL_DIM), lambda b: (b, l, 0, 0, 0, 0)),
        ],
        out_specs=pl.BlockSpec((DEC_SEQ, 2 * BRANCH_W), lambda b: (b, 0)),
        out_shape=jax.ShapeDtypeStruct((N_SAMPLE, 2 * BRANCH_W), F32),
        scratch_shapes=[pltpu.VMEM((DEC_SEQ, 2 * BRANCH_W), F32)] + _STATE_SCRATCH,
        compiler_params=_params(("arbitrary",)),
        name=f"gla_sample{l}",
    )(zbd, *small, st_h, st_d)


def _mix_kernel(*refs, n_x):
    x_refs = refs[:n_x]
    (h_ref, yp_ref, ys_ref, gp_ref, gs_ref, mod_ref, win_ref, wb_ref, wo_ref, o_ref, wm_s, wb_s, wo_s) = refs[n_x:]
    i = pl.program_id(0)

    @pl.when(i < NPRO)
    def _():
        rows = _stage_rows(i, W_ROWS)
        wm_s[_stage_rows(i, MERGE_ROWS), :] = win_ref[...].astype(BF)
        wb_s[rows, :] = wb_ref[...].astype(BF)
        wo_s[rows, :] = wo_ref[...].astype(BF)

    @pl.when(i >= NPRO)
    def _():
        t = i - NPRO
        hb = h_ref[...]
        yac = _tok_load(t, (yp_ref, ys_ref))
        ybd = _tok_load(t, (gp_ref, gs_ref))
        branches = (yac[:, :BRANCH_W], ybd[:, :BRANCH_W], yac[:, BRANCH_W:], ybd[:, BRANCH_W:])
        mixed = None
        for n, y in enumerate(branches):
            logits = _dot_nt(hb, wm_s[n * D_MODEL:(n + 1) * D_MODEL, :])
            term = jax.nn.sigmoid(logits) * _dot(y.astype(BF), wb_s[n * BRANCH_W:(n + 1) * BRANCH_W, :])
            mixed = term if mixed is None else mixed + term
        o_ref[...] = _tok_load(t, x_refs) + mod_ref[2:3, :] * _dot(mixed.astype(BF), wo_s[...])


def _mix(l, xs, h, y_p, y_s, g_p, g_s, mod, w_in_t, w_branch, w_out):
    chunk = pl.BlockSpec((None, W_ROWS, D_MODEL), lambda i: (l, _chunk_of_step(i), 0))
    return pl.pallas_call(
        functools.partial(_mix_kernel, n_x=len(xs)),
        grid=(NPRO + N_TILES,),
        in_specs=_tok_specs(D_MODEL, len(xs) == 2) + _tok_specs(D_MODEL, False)
                 + _tok_specs(2 * BRANCH_W, True) + _tok_specs(2 * BRANCH_W, True)
                 + [_mod_spec(l),
                    pl.BlockSpec((None, MERGE_ROWS, D_MODEL), lambda i: (l, MERGE_OFF // MERGE_ROWS + _chunk_of_step(i), 0)),
                    chunk, chunk],
        out_specs=pl.BlockSpec((TD, D_MODEL), lambda i: (_tile_of_step(i), 0)),
        out_shape=jax.ShapeDtypeStruct((N_TOK, D_MODEL), F32),
        scratch_shapes=[pltpu.VMEM((NPRO * MERGE_ROWS, D_MODEL), BF), pltpu.VMEM((D_MODEL, D_MODEL), BF),
                        pltpu.VMEM((D_MODEL, D_MODEL), BF)],
        compiler_params=_params(("arbitrary",)),
        name=f"mix{l}",
    )(*xs, h, y_p, y_s, g_p, g_s, mod, w_in_t, w_branch.reshape(DEPTH, N_BRANCH * BRANCH_W, D_MODEL), w_out)


def _ffn_kernel(*refs, n_out):
    (xp_ref, x_ref, xn_ref, mod_ref, g_ref, wup_ref, cw_ref, cb_ref, wdn_ref) = refs[:9]
    o_refs = refs[9:9 + n_out]
    hext, u_s, wup_s, wdn_s = refs[9 + n_out:]
    i = pl.program_id(0)

    @pl.when(i < NPRO)
    def _():
        wup_s[_stage_rows(i, W_ROWS), :] = wup_ref[...].astype(BF)
        wdn_s[_stage_rows(i, D_FF // NPRO), :] = wdn_ref[...].astype(BF)

    @pl.when(i >= NPRO)
    def _():
        t = i - NPRO
        ctx = t < PROMPT_TILES
        pos = jnp.maximum(t - PROMPT_TILES, 0) % SAMPLE_TILES_PER_SEQ
        seq_first = ctx | (pos == 0)
        seq_last = ctx | (pos == SAMPLE_TILES_PER_SEQ - 1)
        m = mod_ref[...]
        g = g_ref[...]

        def pre(x):
            return _rms(x, g) * (1.0 + m[4:5]) + m[3:4]

        h_a = pre(x_ref[0:SEQ, :])
        h_b = pre(x_ref[SEQ:TD, :])
        sub = lax.broadcasted_iota(jnp.int32, (HALO, D_MODEL), 0)
        gap = jnp.where(sub == HALO - 1, h_a[SEQ - HALO:, :], jnp.where(sub == 0, h_b[:HALO, :], 0.0))
        hext[0:HALO, :] = jnp.where(seq_first, 0.0, pre(xp_ref[...]))
        hext[FF_A0:FF_A0 + SEQ, :] = h_a
        hext[FF_A0 + SEQ:FF_B0, :] = jnp.where(ctx, 0.0, gap)
        hext[FF_B0:FF_B0 + SEQ, :] = h_b
        hext[FF_B0 + SEQ:FF_B0 + SEQ + HALO, :] = jnp.where(seq_last, 0.0, pre(xn_ref[...]))
        hext[FF_B0 + SEQ + HALO:, :] = jnp.zeros((HALO, D_MODEL), F32)
        hb = hext[...].astype(BF)

        def up_conv_act(c):
            halves = []
            for k, off in enumerate((0, D_FF)):
                cols = slice(off + c * FF_CHUNK, off + (c + 1) * FF_CHUNK)
                u = u_s.at[(2 * c + k) % u_s.shape[0]]
                u[...] = _dot(hb, wup_s[:, cols])
                taps = [u[FF_A0 - 1 + j:FF_A0 - 1 + j + FF_ROWS, :] * cw_ref[j:j + 1, cols] for j in range(CONV_WIDTH)]
                halves.append(taps[0] + taps[1] + taps[2] + cb_ref[:, cols])
            return (_silu(halves[1]) * halves[0]).astype(BF)

        n_chunks = D_FF // FF_CHUNK
        acts = [up_conv_act(c) for c in range(min(FF_LOOKAHEAD, n_chunks))]
        acc = None
        for c in range(n_chunks):
            if c + FF_LOOKAHEAD < n_chunks:
                acts.append(up_conv_act(c + FF_LOOKAHEAD))
            part = _dot(acts.pop(0), wdn_s[c * FF_CHUNK:(c + 1) * FF_CHUNK, :])
            acc = part if acc is None else acc + part
        gate = m[5:6]

        def store(o_ref):
            o_ref[0:SEQ, :] = x_ref[0:SEQ, :] + gate * acc[0:SEQ]
            o_ref[SEQ:TD, :] = x_ref[SEQ:TD, :] + gate * acc[FF_B0 - FF_A0:FF_B0 - FF_A0 + SEQ]

        if n_out == 1:
            store(o_refs[0])
        else:
            @pl.when(ctx)
            def _():
                store(o_refs[0])

            @pl.when(jnp.logical_not(ctx))
            def _():
                store(o_refs[1])


def _ffn(l, x, mod, norm_g, w_up, conv_w, conv_b, w_down, split_out):
    per = TD // HALO
    last_blk = N_TOK // HALO - 1
    if split_out:
        out_specs = [pl.BlockSpec((TD, D_MODEL), lambda i: (jnp.minimum(_tile_of_step(i), PROMPT_TILES - 1), 0)),
                     pl.BlockSpec((TD, D_MODEL), lambda i: (jnp.maximum(_tile_of_step(i) - PROMPT_TILES, 0), 0))]
        out_shape = [jax.ShapeDtypeStruct((N_PROMPT, D_MODEL), F32), jax.ShapeDtypeStruct((N_SAMPLE, D_MODEL), F32)]
    else:
        out_specs = [pl.BlockSpec((TD, D_MODEL), lambda i: (_tile_of_step(i), 0))]
        out_shape = [jax.ShapeDtypeStruct((N_TOK, D_MODEL), F32)]
    return pl.pallas_call(
        functools.partial(_ffn_kernel, n_out=len(out_specs)),
        grid=(NPRO + N_TILES,),
        in_specs=[
            pl.BlockSpec((HALO, D_MODEL), lambda i: (jnp.maximum(_tile_of_step(i) * per - 1, 0), 0)),
            pl.BlockSpec((TD, D_MODEL), lambda i: (_tile_of_step(i), 0)),
            pl.BlockSpec((HALO, D_MODEL), lambda i: (jnp.minimum((_tile_of_step(i) + 1) * per, last_blk), 0)),
            _mod_spec(l),
            pl.BlockSpec((None, 1, D_MODEL), lambda i: (l, 0, 0)),
            pl.BlockSpec((None, W_ROWS, 2 * D_FF), lambda i: (l, _chunk_of_step(i), 0)),
            pl.BlockSpec((None, CONV_WIDTH, 2 * D_FF), lambda i: (l, 0, 0)),
            pl.BlockSpec((None, 1, 2 * D_FF), lambda i: (l, 0, 0)),
            pl.BlockSpec((None, D_FF // NPRO, D_MODEL), lambda i: (l, _chunk_of_step(i), 0)),
        ],
        out_specs=out_specs,
        out_shape=out_shape,
        scratch_shapes=[pltpu.VMEM((FF_EXT, D_MODEL), F32), pltpu.VMEM((FF_U_SLOTS, FF_EXT, FF_CHUNK), F32),
                        pltpu.VMEM((D_MODEL, 2 * D_FF), BF), pltpu.VMEM((D_FF, D_MODEL), BF)],
        compiler_params=_params(("arbitrary",)),
        name=f"ffn{l}",
    )(x, x, x, mod, norm_g, w_up, conv_w, conv_b, w_down)


def kernel(x_prompt, x_sample, cache_a_k, cache_a_v, cache_c_k, cache_c_v, state_hgrn, state_gla, c, c_ctx, w_ada, b_ada, norm1_g, norm2_g, w_in, a_qn_g, a_kn_g, c_qn_g, c_kn_g, c_lambda, c_subln_g, b_lb_logits, b_norm_g, d_alpha_w, d_alpha_b, d_norm_g, w_branch, w_out, w_up, conv_w, conv_b, w_down):
    xs = (x_prompt.reshape(N_PROMPT, D_MODEL), x_sample.reshape(N_SAMPLE, D_MODEL))
    w_in_t = jnp.swapaxes(w_in, 1, 2)

    cond8 = jnp.concatenate([c_ctx[None, :], c, jnp.zeros((8 - N_GROUPS, D_MODEL), F32)], axis=0)
    mod = _modulation(cond8, w_ada, b_ada)[:, :N_GROUPS].reshape(DEPTH, N_GROUPS, N_MOD, D_MODEL)

    tile4 = lambda g, n: jnp.tile(g, (1, n)).reshape(DEPTH, 1, -1)
    gains = (tile4(a_qn_g, A_HEADS), tile4(a_kn_g, A_KV_HEADS), tile4(c_qn_g, 2 * C_HEADS), tile4(c_kn_g, 2 * C_HEADS),
             c_subln_g.reshape(DEPTH, 1, 2 * C_HEAD_DIM))
    fmaj = lambda t: jnp.moveaxis(t, 2, -1).reshape(DEC_BATCH, DEPTH, -1, PAST_LEN)
    caches = tuple(fmaj(t) for t in (cache_a_k, cache_a_v, cache_c_k, cache_c_v))
    tables = _rope_tables(DEC_SEQ, A_HEAD_DIM, A_HEADS) + _rope_tables(DEC_SEQ, C_HEAD_DIM, 2 * C_HEADS)

    aw = jnp.zeros((DEPTH, 2, 128, 128), F32)
    aw = aw.at[:, 0, 0:D_GATE_RANK].set(d_alpha_w[:, 0]).at[:, 1, D_GATE_RANK:2 * D_GATE_RANK].set(d_alpha_w[:, 1])
    gla_small = (b_lb_logits, aw.astype(BF), d_alpha_b.reshape(DEPTH, 2, 1, 128),
                 tile4(b_norm_g, B_HEADS), tile4(d_norm_g, D_HEADS)) + _gla_constants() + (
                     _state_pattern(_B_BLOCKS), _state_pattern(_D_BLOCKS))

    n1 = norm1_g.reshape(DEPTH, 1, D_MODEL)
    n2 = norm2_g.reshape(DEPTH, 1, D_MODEL)
    cb = conv_b.reshape(DEPTH, 1, 2 * D_FF)

    carried = None
    for l in range(DEPTH):
        lam_init = 0.8 - 0.6 * math.exp(-0.3 * l)
        zac, zbd, h = _inproj(l, xs, mod, n1, w_in_t)
        y_p, g_p, carried = _ctx_mixers(l, zac, zbd, gains, c_lambda, gla_small, lam_init, carried)
        y_s = _attn_sample(l, zac, gains, c_lambda, caches, tables, lam_init)
        g_s = _gla_sample(l, zbd, gla_small, state_hgrn, state_gla)
        x1 = _mix(l, xs, h, y_p, y_s, g_p, g_s, mod, w_in_t, w_branch, w_out)
        xs = tuple(_ffn(l, x1, mod, n2, w_up, conv_w, cb, w_down, split_out=(l == DEPTH - 1)))

    y_prompt = xs[0].reshape(BATCH, SEQ, D_MODEL)
    y_sample = xs[1].reshape(DEC_BATCH, DEC_SEQ, D_MODEL)
    stacked = list(carried)
    feat_shapes = ((A_KV_HEADS, A_HEAD_DIM), (A_KV_HEADS, A_HEAD_DIM), (C_HEADS, 2, C_HEAD_DIM), (C_HEADS, 2 * C_HEAD_DIM))
    ctx = [jnp.moveaxis(t.reshape((BATCH, DEPTH) + fs + (SEQ,)), -1, 2) for t, fs in zip(stacked[:4], feat_shapes)]
    return (y_prompt, y_sample) + tuple(ctx) + tuple(stacked[4:])
```

```python
import functools
import math

import numpy as np
import jax
import jax.numpy as jnp
from jax import lax
from jax.experimental import pallas as pl
from jax.experimental.pallas import tpu as pltpu

F32 = jnp.float32
BF = jnp.bfloat16

D_MODEL = 1024
BATCH = 16
SEQ = 256
DEPTH = 2
DEC_BATCH = 2
DEC_SEQ = 1024
PAST_LEN = 512
GRID_W = 64
ROPE_THETA = 10000.0
EPS = 1e-6
LB_FLOOR = 1e-30
N_MOD = 6
N_BRANCH = 4
BRANCH_W = 256
A_HEADS, A_KV_HEADS, A_HEAD_DIM = 4, 2, 64
C_HEADS, C_HEAD_DIM = 4, 32
B_HEADS, B_KEY_DIM, B_VAL_DIM = 4, 64, 64
D_HEADS, D_KEY_DIM, D_VAL_DIM = 4, 32, 64
D_GATE_RANK = 16
D_GATE_TAU = 16.0
D_FF = 2816
CONV_WIDTH = 3

N_PROMPT = BATCH * SEQ
N_SAMPLE = DEC_BATCH * DEC_SEQ
N_TOK = N_PROMPT + N_SAMPLE
TM = 256
TQ = 512
TD = 2 * SEQ
N_TILES = N_TOK // TD
PROMPT_TILES = N_PROMPT // TD
SAMPLE_TILES_PER_SEQ = DEC_SEQ // TD
N_GROUPS = 1 + DEC_BATCH

AC_W = 1280
BD_W = 2080
BD_PAD = 2176
SMALL_W = AC_W + BD_PAD
MERGE_OFF = AC_W + BD_W
MERGE_ROWS = 560
HALO = 8
FF_A0 = HALO
FF_B0 = FF_A0 + SEQ + HALO
FF_ROWS = 2 * SEQ + 2 * HALO
FF_EXT = FF_A0 + FF_ROWS + HALO
FF_CHUNK = 256
NPRO = 8
W_ROWS = D_MODEL // NPRO
LOG2E = 1.4426950408889634
VMEM_LIMIT = 56 * 1024 * 1024
TT = 128
N_LEVELS = 7
SCORE_ROWS = 16


def _dot(a, b):
    return jnp.dot(a, b, preferred_element_type=F32)


def _dot_nt(a, b):
    return lax.dot_general(a, b, (((1,), (1,)), ((), ())), preferred_element_type=F32)


def _dot_tn(a, b):
    return lax.dot_general(a, b, (((0,), (0,)), ((), ())), preferred_element_type=F32)


def _silu(x):
    return x * jax.nn.sigmoid(x)


def _log_sigmoid(x):
    return jnp.minimum(x, 0.0) - jnp.log1p(jnp.exp(-jnp.abs(x)))


def _rms(x, g):
    return x * lax.rsqrt(jnp.mean(x * x, axis=-1, keepdims=True) + EPS) * g


def _head_rms(x, head_dim, g):
    w = x.shape[-1]
    sh = int(math.log2(head_dim))
    r = lax.shift_right_logical(lax.broadcasted_iota(jnp.int32, (w, w), 0), sh)
    c = lax.shift_right_logical(lax.broadcasted_iota(jnp.int32, (w, w), 1), sh)
    bd = jnp.where(r == c, 1.0 / head_dim, 0.0).astype(BF)
    x2 = x * x
    hi = x2.astype(BF)
    lo = (x2 - hi.astype(F32)).astype(BF)
    ms = _dot(hi, bd) + _dot(lo, bd)
    return x * lax.rsqrt(ms + EPS) * g


def _group_of_tile(i):
    return jnp.where(i < PROMPT_TILES, 0, 1 + jnp.maximum(i - PROMPT_TILES, 0) // SAMPLE_TILES_PER_SEQ)


def _params(sem):
    return pltpu.CompilerParams(dimension_semantics=sem, vmem_limit_bytes=VMEM_LIMIT)


def _mod_kernel(cond_ref, w_ref, b_ref, o_ref):
    s = _silu(cond_ref[...])
    o_ref[...] = _dot(s.astype(BF), w_ref[...].astype(BF)) + b_ref[...]


def _modulation(cond8, w_ada, b_ada):
    nb = 1536
    return pl.pallas_call(
        _mod_kernel,
        grid=(DEPTH, N_MOD * D_MODEL // nb),
        in_specs=[
            pl.BlockSpec((8, D_MODEL), lambda l, j: (0, 0)),
            pl.BlockSpec((None, D_MODEL, nb), lambda l, j: (l, 0, j)),
            pl.BlockSpec((None, 1, nb), lambda l, j: (l, 0, j)),
        ],
        out_specs=pl.BlockSpec((None, 8, nb), lambda l, j: (l, 0, j)),
        out_shape=jax.ShapeDtypeStruct((DEPTH, 8, N_MOD * D_MODEL), F32),
        compiler_params=_params(("arbitrary", "arbitrary")),
        name="modulation",
    )(cond8, w_ada, b_ada.reshape(DEPTH, 1, N_MOD * D_MODEL))


def _tile_of_step(i):
    return jnp.maximum(i - NPRO, 0)


def _chunk_of_step(i):
    return jnp.minimum(i, NPRO - 1)


def _tok_specs(width, split):
    if not split:
        return [pl.BlockSpec((TD, width), lambda i: (_tile_of_step(i), 0))]
    return [pl.BlockSpec((TD, width), lambda i: (jnp.minimum(_tile_of_step(i), PROMPT_TILES - 1), 0)),
            pl.BlockSpec((TD, width), lambda i: (jnp.maximum(_tile_of_step(i) - PROMPT_TILES, 0), 0))]


def _tok_load(t, refs):
    if len(refs) == 1:
        return refs[0][...]
    return jnp.where(t < PROMPT_TILES, refs[0][...], refs[1][...])


def _mod_spec(l):
    return pl.BlockSpec((None, None, N_MOD, D_MODEL), lambda i: (l, _group_of_tile(_tile_of_step(i)), 0, 0))


def _stage_rows(i, n):
    return pl.ds(pl.multiple_of(i * n, n), n)


def _inproj_kernel(*refs, n_x):
    x_refs = refs[:n_x]
    mod_ref, g_ref, w_ref, zac_ref, zbd_ref, h_ref, w_s = refs[n_x:]
    i = pl.program_id(0)

    @pl.when(i < NPRO)
    def _():
        w_s[_stage_rows(i, SMALL_W // NPRO), :] = w_ref[...].astype(BF)

    @pl.when(i >= NPRO)
    def _():
        m = mod_ref[...]
        h = _rms(_tok_load(i - NPRO, x_refs), g_ref[...]) * (1.0 + m[1:2]) + m[0:1]
        hb = h.astype(BF)
        h_ref[...] = hb
        zac_ref[...] = _dot_nt(hb, w_s[:AC_W, :])
        zbd_ref[...] = _dot_nt(hb, w_s[AC_W:, :])


def _inproj(l, xs, mod, norm_g, w_in_t):
    split = len(xs) == 2
    return pl.pallas_call(
        functools.partial(_inproj_kernel, n_x=len(xs)),
        grid=(NPRO + N_TILES,),
        in_specs=_tok_specs(D_MODEL, split) + [
            _mod_spec(l),
            pl.BlockSpec((None, 1, D_MODEL), lambda i: (l, 0, 0)),
            pl.BlockSpec((None, SMALL_W // NPRO, D_MODEL), lambda i: (l, _chunk_of_step(i), 0)),
        ],
        out_specs=[
            pl.BlockSpec((TD, AC_W), lambda i: (_tile_of_step(i), 0)),
            pl.BlockSpec((TD, BD_PAD), lambda i: (_tile_of_step(i), 0)),
            pl.BlockSpec((TD, D_MODEL), lambda i: (_tile_of_step(i), 0)),
        ],
        out_shape=[
            jax.ShapeDtypeStruct((N_TOK, AC_W), F32),
            jax.ShapeDtypeStruct((N_TOK, BD_PAD), F32),
            jax.ShapeDtypeStruct((N_TOK, D_MODEL), BF),
        ],
        scratch_shapes=[pltpu.VMEM((SMALL_W, D_MODEL), BF)],
        compiler_params=_params(("arbitrary",)),
        name=f"inproj{l}",
    )(*xs, mod, norm_g, w_in_t)


def _rope_tables(n_tokens, head_dim, n_rep):
    rows = n_tokens // GRID_W
    row = np.repeat(np.arange(rows), GRID_W).astype(np.float64)
    col = np.tile(np.arange(GRID_W), rows).astype(np.float64)
    half = head_dim // 2
    q4 = head_dim // 4
    freqs = ROPE_THETA ** (-np.arange(0, half, 2, dtype=np.float64) / half)
    ang_r = row[:, None] * freqs
    ang_c = col[:, None] * freqs
    ang = np.concatenate([ang_r, ang_r, ang_c, ang_c], axis=-1)
    cos, sin = np.cos(ang), np.sin(ang)
    first = (np.arange(head_dim) % (2 * q4)) < q4
    s_dn = np.where(first, -sin, 0.0)
    s_up = np.where(first, 0.0, sin)
    return tuple(jnp.asarray(np.tile(t, (1, n_rep)), dtype=F32) for t in (cos, s_dn, s_up))


def _rope(x, cos, s_dn, s_up, q4):
    w = x.shape[-1]
    return x * cos + pltpu.roll(x, w - q4, 1) * s_dn + pltpu.roll(x, q4, 1) * s_up


def _softmax_pv_group(maps, outs):
    scores = [[_dot(q, k) if t else _dot_nt(q, k) for k, t in zip(ks, fm)] for q, ks, _, _, fm in maps]
    yield
    probs = []
    for ss in scores:
        m = ss[0].max(axis=-1, keepdims=True)
        for s in ss[1:]:
            m = jnp.maximum(m, s.max(axis=-1, keepdims=True))
        probs.append([jnp.exp2(s - m).astype(BF) for s in ss])
    yield
    for ps, (_, _, vexts, half, fm) in zip(probs, maps):
        o = None
        for p, v, t in zip(ps, vexts, fm):
            part = _dot_nt(p, v) if t else _dot(p, v)
            o = part if o is None else o + part
        outs.append(o[:, half * 64:(half + 1) * 64] / o[:, (1 - half) * 64:(1 - half) * 64 + 1])
    yield


def _with_ones(v, half, feature_major):
    idx = lax.broadcasted_iota(jnp.int32, v.shape, 0 if feature_major else 1)
    return jnp.where(lax.shift_right_logical(idx, 6) == half, v, jnp.ones_like(v))


def _attend_heads(aq, cq, ka, va, kc, vc, fm, lam, gsub, lam_init, y_ref, group):
    aqb = (aq * (A_HEAD_DIM ** -0.5 * LOG2E)).astype(BF)
    cqb = (cq * (C_HEAD_DIM ** -0.5 * LOG2E)).astype(BF)
    rep = A_HEADS // A_KV_HEADS

    def feat(x, t, sl):
        return x[sl, :] if t else x[:, sl]

    maps = []
    for h in range(A_HEADS):
        g = h // rep
        sl = slice(g * A_HEAD_DIM, (g + 1) * A_HEAD_DIM)
        maps.append((aqb[:, h * A_HEAD_DIM:(h + 1) * A_HEAD_DIM], [feat(k, t, sl) for k, t in zip(ka, fm)],
                     [_with_ones(v, g, t) for v, t in zip(va, fm)], g, fm))
    for h in range(C_HEADS):
        slab = slice((h // 2) * 128, (h // 2 + 1) * 128)
        vh = [_with_ones(feat(v, t, slab), h % 2, t) for v, t in zip(vc, fm)]
        for j in range(2):
            sl = slice((2 * h + j) * C_HEAD_DIM, (2 * h + j + 1) * C_HEAD_DIM)
            maps.append((cqb[:, sl], [feat(k, t, sl) for k, t in zip(kc, fm)], vh, h % 2, fm))
    outs = []
    for i in range(0, len(maps), group):
        yield from _softmax_pv_group(maps[i:i + group], outs)
    for h in range(A_HEADS):
        y_ref[:, h * A_HEAD_DIM:(h + 1) * A_HEAD_DIM] = outs[h]
    vd = 2 * C_HEAD_DIM
    for h in range(C_HEADS):
        d = outs[A_HEADS + 2 * h] - lam * outs[A_HEADS + 2 * h + 1]
        y_ref[:, BRANCH_W + h * vd:BRANCH_W + (h + 1) * vd] = _rms(d, gsub) * (1.0 - lam_init)


def _lambda(cl):
    s1 = jnp.sum(cl[0:1] * cl[1:2], axis=-1, keepdims=True)
    s2 = jnp.sum(cl[2:3] * cl[3:4], axis=-1, keepdims=True)
    return jnp.exp(s1) - jnp.exp(s2)


def _attn_prompt_kernel(z_ref, gaq, gak, gcq, gck, gsub, cl_ref, y_ref, oak, oav, ock, ocv, *, lam_init):
    z = z_ref[...]
    ak = _head_rms(z[:, 256:384], A_HEAD_DIM, gak[...])
    av = z[:, 384:512]
    ck = _head_rms(z[:, 768:1024], C_HEAD_DIM, gck[...])
    cv = z[:, 1024:1280]
    oak[...] = ak.T
    oav[...] = av.T
    ock[...] = ck.T
    ocv[...] = cv.T
    aq = _head_rms(z[:, 0:256], A_HEAD_DIM, gaq[...])
    cq = _head_rms(z[:, 512:768], C_HEAD_DIM, gcq[...])
    lam = _lambda(cl_ref[...]) + lam_init
    yield
    yield from _attend_heads(aq, cq, [ak.astype(BF)], [av.astype(BF)], [ck.astype(BF)], [cv.astype(BF)], [False],
                             lam, gsub[...], lam_init, y_ref, group=A_HEADS + 2 * C_HEADS)


def _attn_sample_kernel(z_ref, gaq, gak, gcq, gck, gsub, cl_ref, cak, cav, cck, ccv,
                        cosa, sda, sua, cosc, sdc, suc, y_ref, ka_s, va_s, kc_s, vc_s, *, lam_init):
    qi = pl.program_id(1)
    qa4, qc4 = A_HEAD_DIM // 4, C_HEAD_DIM // 4

    @pl.when(qi == 0)
    def _():
        ak = _head_rms(z_ref[:, 256:384], A_HEAD_DIM, gak[...])
        ka_s[...] = _rope(ak, cosa[:, :128], sda[:, :128], sua[:, :128], qa4).astype(BF)
        va_s[...] = z_ref[:, 384:512].astype(BF)
        ck = _head_rms(z_ref[:, 768:1024], C_HEAD_DIM, gck[...])
        kc_s[...] = _rope(ck, cosc[...], sdc[...], suc[...], qc4).astype(BF)
        vc_s[...] = z_ref[:, 1024:1280].astype(BF)

    rows = pl.ds(pl.multiple_of(qi * TQ, TQ), TQ)
    aq = _head_rms(z_ref[rows, 0:256], A_HEAD_DIM, gaq[...])
    aq = _rope(aq, cosa[rows, :], sda[rows, :], sua[rows, :], qa4)
    cq = _head_rms(z_ref[rows, 512:768], C_HEAD_DIM, gcq[...])
    cq = _rope(cq, cosc[rows, :], sdc[rows, :], suc[rows, :], qc4)
    lam = _lambda(cl_ref[...]) + lam_init
    yield from _attend_heads(aq, cq,
                             [cak[...].astype(BF), ka_s[...]], [cav[...].astype(BF), va_s[...]],
                             [cck[...].astype(BF), kc_s[...]], [ccv[...].astype(BF), vc_s[...]], [True, False],
                             lam, gsub[...], lam_init, y_ref, group=4)


def _gain_specs(l, nd):
    zeros = (0,) * (nd - 1)
    widths = (256, 128, 256, 256, 64)
    return [pl.BlockSpec((None, 1, w), lambda *a: (l, 0, 0)) for w in widths] + \
           [pl.BlockSpec((None, 4, C_HEAD_DIM), lambda *a: (l, 0, 0))]


N_VTILES = BRANCH_W // 128


def _state_blocks(nh, kd, vd):
    per = 128 // vd
    return [(h // per, slice((h * kd) % 128, (h * kd) % 128 + kd), slice((h % per) * vd, (h % per + 1) * vd))
            for h in range(nh)]


_B_BLOCKS = _state_blocks(B_HEADS, B_KEY_DIM, B_VAL_DIM)
_D_BLOCKS = _state_blocks(D_HEADS, D_KEY_DIM, D_VAL_DIM)


def _gla_constants():
    idx = np.arange(TT)
    scans, masks = [], []
    for rev in (False, True):
        eff = (TT - 1 - idx) if rev else idx
        et, eu = eff[:, None], eff[None, :]
        sc, mk = [], []
        for j in range(N_LEVELS):
            b = 1 << j
            start = et - et % b
            odd = (et // b) % 2 == 1
            sc.append(np.where(odd, (eu > start) & (eu <= et), (eu > et) & (eu <= start + b)))
            mk.append(((et // b) % 2 == 1) & (eu // b == et // b - 1))
        sc.append(eu <= et)
        sc.append(eu > et)
        mk.append(eu == et)
        scans.append(np.concatenate([np.concatenate(sc, axis=0)] * 2, axis=-1))
        masks.append(np.stack(mk))
    return tuple(jnp.asarray(np.stack(t), BF) for t in (scans, masks))


def _gla_prepare(q, k, v, la2, scan_ref, d, rev, use_state):
    tt = q.shape[0]
    la_hi = la2.astype(BF)
    la_lo = (la2 - la_hi.astype(F32)).astype(BF)
    la_split = jnp.concatenate([la_hi, la_lo], axis=0)

    sums = _dot(scan_ref[d], la_split)

    def factor(i):
        return jnp.exp2(sums[i * tt:(i + 1) * tt])

    qs, ks = [], []
    for j in range(N_LEVELS):
        f = factor(j)
        qs.append((q * f).astype(BF))
        ks.append((k * f).astype(BF).T)
    qs.append(q.astype(BF))
    ks.append(k.astype(BF).T)
    k_out = (k * factor(N_LEVELS + 1)).astype(BF)
    vb = v.astype(BF)
    q_in = d_tile = None
    if use_state:
        q_in = (q * factor(N_LEVELS)).astype(BF)
        ones = jnp.ones((2 * tt, 128), BF)
        d_tile = [jnp.exp2(_dot_tn(la_split[:, i * 128:(i + 1) * 128], ones)) for i in range(q.shape[1] // 128)]

    return dict(qs=qs, ks=ks, k_out=k_out, vb=vb, q_in=q_in, d_tile=d_tile)


def _lane_keep(x, lo, hi):
    lane = lax.broadcasted_iota(jnp.int32, x.shape, 1)
    return jnp.where((lane >= lo) & (lane < hi), x, jnp.zeros_like(x))


def _level_rows(j, rev):
    b = 1 << j
    if j == N_LEVELS or b < SCORE_ROWS:
        return None
    return [(i * b, (i + 1) * b) for i in range(TT // b) if (i % 2 == 1) != rev]


def _gla_scores(p, mask_ref, d, rev, nh, kd):
    out = []
    for h in range(nh):
        c, lo = divmod(h * kd, 128)
        blocks = [None] * (TT // SCORE_ROWS)

        def add(r0, val):
            for i in range(val.shape[0] // SCORE_ROWS):
                piece = val[i * SCORE_ROWS:(i + 1) * SCORE_ROWS]
                k = r0 // SCORE_ROWS + i
                blocks[k] = piece if blocks[k] is None else blocks[k] + piece

        for j in range(N_LEVELS + 1):
            kt = p["ks"][j]
            own = kt[c * 128 + lo:c * 128 + lo + kd, :]
            pieces = [jnp.zeros((lo, kt.shape[1]), BF)] if lo else []
            pieces.append(own)
            if lo + kd < 128:
                pieces.append(jnp.zeros((128 - lo - kd, kt.shape[1]), BF))
            kh = jnp.concatenate(pieces, axis=0) if len(pieces) > 1 else own
            qj = p["qs"][j][:, c * 128:(c + 1) * 128]
            ranges = _level_rows(j, rev)
            if ranges is None:
                add(0, _dot(qj, kh).astype(BF) * mask_ref[d, j])
            else:
                q_rows = jnp.concatenate([qj[a:b] for a, b in ranges], axis=0) if len(ranges) > 1 else qj[ranges[0][0]:ranges[0][1]]
                m_rows = [mask_ref[d, j, a:b, :] for a, b in ranges]
                t = _dot(q_rows, kh).astype(BF)
                off = 0
                for (a, b), m in zip(ranges, m_rows):
                    add(a, t[off:off + b - a] * m)
                    off += b - a
        zero = jnp.zeros((SCORE_ROWS, TT), BF)
        out.append(jnp.concatenate([z if z is not None else zero for z in blocks], axis=0))
    return out


def _gla_outputs(p, scs, blk_ref, st_ref, d, nh, kd, vd, use_state, o_ref, rows, col0, accumulate):
    per = 128 // vd
    for c in range(nh // per):
        vt = p["vb"][:, c * 128:(c + 1) * 128]
        kt = (c * per * kd) // 128
        ktile = slice(kt * 128, (kt + 1) * 128)
        o = None
        for i in range(per):
            t = _dot(scs[c * per + i], _lane_keep(vt, i * vd, (i + 1) * vd))
            o = t if o is None else o + t
        kv = _dot_tn(p["k_out"][:, ktile], vt) * blk_ref[c]
        if use_state:
            st = st_ref[d, c]
            o = o + _dot(p["q_in"][:, ktile], st.astype(BF))
            st_ref[d, c] = st * p["d_tile"][kt] + kv
        else:
            st_ref[d, c] = kv
        osl = slice(col0 + c * 128, col0 + (c + 1) * 128)
        if accumulate:
            o_ref[rows, osl] += o
        else:
            o_ref[rows, osl] = o


def _gla_kernel(*refs, layer, n_tiles, has_state, pump=None):
    pump = pump or (lambda: None)
    if has_state:
        (z_ref, lbl_ref, aw_ref, ab_ref, bng, dng, scan_ref, mask_ref, blkb_ref, blkd_ref, sh_in, sd_in,
         y_ref, o_scr, sth, std) = refs
        for packed, raw, blocks in ((sth, sh_in, _B_BLOCKS), (std, sd_in, _D_BLOCKS)):
            packed[...] = jnp.zeros(packed.shape, F32)
            for dd in range(2):
                for h, (c, r, ln) in enumerate(blocks):
                    packed[dd, c, r, ln] = raw[dd, h]
    else:
        (z_ref, lbl_ref, aw_ref, ab_ref, bng, dng, scan_ref, mask_ref, blkb_ref, blkd_ref,
         y_ref, sh_out, sd_out, o_scr, sth, std) = refs

    gates = []
    for d in range(2):
        logits = [lbl_ref[d, i:i + 1, :] for i in range(DEPTH)]
        mx = functools.reduce(jnp.maximum, logits)
        ex = [jnp.exp(t - mx) for t in logits]
        den = functools.reduce(lambda a, b: a + b, ex)
        ps = [t / den for t in ex]
        lb = functools.reduce(lambda a, b: a + b, ps[:layer + 1]) - ps[0]
        gates.append((lb, jnp.log(jnp.maximum(lb, LB_FLOOR)), jnp.log1p(-lb)))

    o_scr[...] = jnp.zeros(o_scr.shape, F32)

    def tile(i, use_state):
        preps = []
        for d in range(2):
            rev = d == 1
            lb, log_lb, log_1m = gates[d]
            j = (n_tiles - 1 - i) if rev else i
            rows = pl.ds(j * TT if isinstance(j, int) else pl.multiple_of(j * TT, TT), TT)
            bq = z_ref[rows, 0:256]
            zf = z_ref[rows, 768:1024] if rev else z_ref[rows, 512:768]
            b2 = log_1m + _log_sigmoid(zf)
            la = jnp.maximum(log_lb, b2) + jnp.log1p(jnp.exp(-jnp.abs(log_lb - b2)))
            kb = (1.0 - lb) * jax.nn.sigmoid(-zf)
            pre = _dot(z_ref[rows, 2048:2176].astype(BF), aw_ref[d]) + ab_ref[d]
            la_d = _log_sigmoid(pre) * (LOG2E / D_GATE_TAU)
            pb = _gla_prepare(_silu(bq), kb, z_ref[rows, 256:512], la * LOG2E, scan_ref, d, rev, use_state)
            pd = _gla_prepare(z_ref[rows, 1280:1408] * (D_KEY_DIM ** -0.5), z_ref[rows, 1408:1536],
                              z_ref[rows, 1536:1792], la_d, scan_ref, d, rev, use_state)
            preps.append((d, rows, pb, pd))
        pump()
        scores = [(_gla_scores(pb, mask_ref, d, d == 1, B_HEADS, B_KEY_DIM),
                   _gla_scores(pd, mask_ref, d, d == 1, D_HEADS, D_KEY_DIM)) for d, _, pb, pd in preps]
        pump()
        for (d, rows, pb, pd), (sb, sd) in zip(preps, scores):
            _gla_outputs(pb, sb, blkb_ref, sth, d, B_HEADS, B_KEY_DIM, B_VAL_DIM, use_state, o_scr, rows, 0, True)
            _gla_outputs(pd, sd, blkd_ref, std, d, D_HEADS, D_KEY_DIM, D_VAL_DIM, use_state, o_scr, rows, 256, True)
        pump()

    if has_state:
        def body(i, carry):
            tile(i, True)
            return carry

        lax.fori_loop(0, n_tiles, body, 0)
    else:
        for i in range(n_tiles):
            tile(i, i > 0)

    def finish(i, carry):
        rows = pl.ds(pl.multiple_of(i * TM, TM), TM)
        y_ref[rows, 0:256] = _head_rms(o_scr[rows, 0:256], B_VAL_DIM, bng[...]) * _silu(z_ref[rows, 1024:1280])
        y_ref[rows, 256:512] = _head_rms(o_scr[rows, 256:512], D_VAL_DIM, dng[...]) * _silu(z_ref[rows, 1792:2048])
        return carry

    lax.fori_loop(0, (n_tiles * TT) // TM, finish, 0)
    if not has_state:
        for packed, raw, blocks in ((sth, sh_out, _B_BLOCKS), (std, sd_out, _D_BLOCKS)):
            for dd in range(2):
                for h, (c, r, ln) in enumerate(blocks):
                    raw[dd, h] = packed[dd, c, r, ln]


def _gla_common_specs(l, nd):
    return [
        pl.BlockSpec((2, DEPTH, 256), lambda *a: (0, 0, 0)),
        pl.BlockSpec((None, 2, 128, 128), lambda *a: (l, 0, 0, 0)),
        pl.BlockSpec((None, 2, 1, 128), lambda *a: (l, 0, 0, 0)),
        pl.BlockSpec((None, 1, 256), lambda *a: (l, 0, 0)),
        pl.BlockSpec((None, 1, 256), lambda *a: (l, 0, 0)),
        pl.BlockSpec((2, (N_LEVELS + 2) * TT, 2 * TT), lambda *a: (0, 0, 0)),
        pl.BlockSpec((2, N_LEVELS + 1, TT, TT), lambda *a: (0, 0, 0, 0)),
        pl.BlockSpec((N_VTILES, 128, 128), lambda *a: (0, 0, 0)),
        pl.BlockSpec((N_VTILES, 128, 128), lambda *a: (0, 0, 0)),
    ]


_STATE_SCRATCH = [pltpu.VMEM((2, N_VTILES, 128, 128), F32), pltpu.VMEM((2, N_VTILES, 128, 128), F32)]


def _state_pattern(blocks):
    pat = np.zeros((N_VTILES, 128, 128), np.float32)
    for c, r, ln in blocks:
        pat[c, r, ln] = 1.0
    return jnp.asarray(pat)


N_ATTN_IN = 7
N_ATTN_OUT = 5


def _ctx_mixers_kernel(*refs, layer, lam_init, n_carried):
    n_in = N_ATTN_IN + 1 + len(_gla_common_specs(0, 1))
    a_in, g_in, outs = refs[:N_ATTN_IN], refs[N_ATTN_IN:n_in], refs[n_in + n_carried:]
    y_ac, caches, y_bd, states = outs[0], outs[1:N_ATTN_OUT], outs[N_ATTN_OUT], outs[N_ATTN_OUT + 1:]
    attn = _attn_prompt_kernel(*a_in, y_ac, *caches, lam_init=lam_init)
    _gla_kernel(*g_in, y_bd, *states, layer=layer, n_tiles=SEQ // TT, has_state=False,
                pump=lambda: next(attn, None))
    for _ in attn:
        pass


def _ctx_mixers(l, zac, zbd, gains, c_lambda, small, lam_init, carried):
    stacked = [(BATCH, DEPTH, w, SEQ) for w in (128, 128, 256, 256)] + [
        (BATCH, DEPTH, 2, B_HEADS, B_KEY_DIM, B_VAL_DIM), (BATCH, DEPTH, 2, D_HEADS, D_KEY_DIM, D_VAL_DIM)]
    seq = lambda w: pl.BlockSpec((SEQ, w), lambda b: (b, 0))
    layer_block = lambda s: pl.BlockSpec((None, None) + s[2:], lambda b: (b, l) + (0,) * (len(s) - 2))
    in_specs = [seq(AC_W)] + _gain_specs(l, 1) + [seq(BD_PAD)] + _gla_common_specs(l, 1)
    n_carried = 0 if carried is None else len(carried)
    aliases = {}
    if carried is not None:
        out_idx = [1, 2, 3, 4, 6, 7]
        aliases = {len(in_specs) + k: out_idx[k] for k in range(n_carried)}
        in_specs = in_specs + [pl.BlockSpec(memory_space=pl.ANY)] * n_carried
    out = pl.pallas_call(
        functools.partial(_ctx_mixers_kernel, layer=l, lam_init=lam_init, n_carried=n_carried),
        grid=(BATCH,),
        in_specs=in_specs,
        out_specs=[seq(2 * BRANCH_W)] + [layer_block(s) for s in stacked[:4]] + [seq(2 * BRANCH_W)]
                  + [layer_block(s) for s in stacked[4:]],
        out_shape=[jax.ShapeDtypeStruct((N_PROMPT, 2 * BRANCH_W), F32)]
                  + [jax.ShapeDtypeStruct(s, F32) for s in stacked[:4]]
                  + [jax.ShapeDtypeStruct((N_PROMPT, 2 * BRANCH_W), F32)]
                  + [jax.ShapeDtypeStruct(s, F32) for s in stacked[4:]],
        scratch_shapes=[pltpu.VMEM((SEQ, 2 * BRANCH_W), F32)] + _STATE_SCRATCH,
        input_output_aliases=aliases,
        compiler_params=_params(("arbitrary",)),
        name=f"ctx_mixers{l}",
    )(zac, *gains, c_lambda, zbd, *small, *(carried or ()))
    return out[0], out[5], out[1:5] + out[6:]


def _attn_sample_body(*refs, lam_init):
    for _ in _attn_sample_kernel(*refs, lam_init=lam_init):
        pass


def _attn_sample(l, zac, gains, c_lambda, caches, tables, lam_init):
    first_blk = N_PROMPT // DEC_SEQ
    cache_specs = [pl.BlockSpec((None, None, w, PAST_LEN), lambda b, q: (b, l, 0, 0)) for w in (128, 128, 256, 256)]
    table_specs = [pl.BlockSpec((DEC_SEQ, 256), lambda b, q: (0, 0)) for _ in range(6)]
    return pl.pallas_call(
        functools.partial(_attn_sample_body, lam_init=lam_init),
        grid=(DEC_BATCH, DEC_SEQ // TQ),
        in_specs=[pl.BlockSpec((DEC_SEQ, AC_W), lambda b, q: (first_blk + b, 0))] + _gain_specs(l, 2)
                 + cache_specs + table_specs,
        out_specs=pl.BlockSpec((TQ, 2 * BRANCH_W), lambda b, q: (b * (DEC_SEQ // TQ) + q, 0)),
        out_shape=jax.ShapeDtypeStruct((N_SAMPLE, 2 * BRANCH_W), F32),
        scratch_shapes=[pltpu.VMEM((DEC_SEQ, 128), BF), pltpu.VMEM((DEC_SEQ, 128), BF),
                        pltpu.VMEM((DEC_SEQ, 256), BF), pltpu.VMEM((DEC_SEQ, 256), BF)],
        compiler_params=_params(("arbitrary", "arbitrary")),
        name=f"attn_sample{l}",
    )(zac, *gains, c_lambda, *caches, *tables)


def _gla_sample(l, zbd, small, st_h, st_d):
    first_blk = N_PROMPT // DEC_SEQ
    return pl.pallas_call(
        functools.partial(_gla_kernel, layer=l, n_tiles=DEC_SEQ // TT, has_state=True),
        grid=(DEC_BATCH,),
        in_specs=[pl.BlockSpec((DEC_SEQ, BD_PAD), lambda b: (first_blk + b, 0))] + _gla_common_specs(l, 1) + [
            pl.BlockSpec((None, None, 2, B_HEADS, B_KEY_DIM, B_VAL_DIM), lambda b: (b, l, 0, 0, 0, 0)),
            pl.BlockSpec((None, None, 2, D_HEADS, D_KEY_DIM, D_VAL_DIM), lambda b: (b, l, 0, 0, 0, 0)),
        ],
        out_specs=pl.BlockSpec((DEC_SEQ, 2 * BRANCH_W), lambda b: (b, 0)),
        out_shape=jax.ShapeDtypeStruct((N_SAMPLE, 2 * BRANCH_W), F32),
        scratch_shapes=[pltpu.VMEM((DEC_SEQ, 2 * BRANCH_W), F32)] + _STATE_SCRATCH,
        compiler_params=_params(("arbitrary",)),
        name=f"gla_sample{l}",
    )(zbd, *small, st_h, st_d)


def _mix_kernel(*refs, n_x):
    x_refs = refs[:n_x]
    (h_ref, yp_ref, ys_ref, gp_ref, gs_ref, mod_ref, win_ref, wb_ref, wo_ref, o_ref, wm_s, wb_s, wo_s) = refs[n_x:]
    i = pl.program_id(0)

    @pl.when(i < NPRO)
    def _():
        rows = _stage_rows(i, W_ROWS)
        wm_s[_stage_rows(i, MERGE_ROWS), :] = win_ref[...].astype(BF)
        wb_s[rows, :] = wb_ref[...].astype(BF)
        wo_s[rows, :] = wo_ref[...].astype(BF)

    @pl.when(i >= NPRO)
    def _():
        t = i - NPRO
        hb = h_ref[...]
        yac = _tok_load(t, (yp_ref, ys_ref))
        ybd = _tok_load(t, (gp_ref, gs_ref))
        branches = (yac[:, :BRANCH_W], ybd[:, :BRANCH_W], yac[:, BRANCH_W:], ybd[:, BRANCH_W:])
        mixed = None
        for n, y in enumerate(branches):
            logits = _dot_nt(hb, wm_s[n * D_MODEL:(n + 1) * D_MODEL, :])
            term = jax.nn.sigmoid(logits) * _dot(y.astype(BF), wb_s[n * BRANCH_W:(n + 1) * BRANCH_W, :])
            mixed = term if mixed is None else mixed + term
        o_ref[...] = _tok_load(t, x_refs) + mod_ref[2:3, :] * _dot(mixed.astype(BF), wo_s[...])


def _mix(l, xs, h, y_p, y_s, g_p, g_s, mod, w_in_t, w_branch, w_out):
    chunk = pl.BlockSpec((None, W_ROWS, D_MODEL), lambda i: (l, _chunk_of_step(i), 0))
    return pl.pallas_call(
        functools.partial(_mix_kernel, n_x=len(xs)),
        grid=(NPRO + N_TILES,),
        in_specs=_tok_specs(D_MODEL, len(xs) == 2) + _tok_specs(D_MODEL, False)
                 + _tok_specs(2 * BRANCH_W, True) + _tok_specs(2 * BRANCH_W, True)
                 + [_mod_spec(l),
                    pl.BlockSpec((None, MERGE_ROWS, D_MODEL), lambda i: (l, MERGE_OFF // MERGE_ROWS + _chunk_of_step(i), 0)),
                    chunk, chunk],
        out_specs=pl.BlockSpec((TD, D_MODEL), lambda i: (_tile_of_step(i), 0)),
        out_shape=jax.ShapeDtypeStruct((N_TOK, D_MODEL), F32),
        scratch_shapes=[pltpu.VMEM((NPRO * MERGE_ROWS, D_MODEL), BF), pltpu.VMEM((D_MODEL, D_MODEL), BF),
                        pltpu.VMEM((D_MODEL, D_MODEL), BF)],
        compiler_params=_params(("arbitrary",)),
        name=f"mix{l}",
    )(*xs, h, y_p, y_s, g_p, g_s, mod, w_in_t, w_branch.reshape(DEPTH, N_BRANCH * BRANCH_W, D_MODEL), w_out)


def _ffn_kernel(*refs, n_out):
    (xp_ref, x_ref, xn_ref, mod_ref, g_ref, wup_ref, cw_ref, cb_ref, wdn_ref) = refs[:9]
    o_refs = refs[9:9 + n_out]
    hext, wup_s, wdn_s = refs[9 + n_out:12 + n_out]
    i = pl.program_id(0)

    @pl.when(i < NPRO)
    def _():
        wup_s[_stage_rows(i, W_ROWS), :] = wup_ref[...].astype(BF)
        wdn_s[_stage_rows(i, D_FF // NPRO), :] = wdn_ref[...].astype(BF)

    if n_out == 2:
        keep_s = refs[12 + n_out]

        @pl.when(i == 0)
        def _():
            keep_s[...] = jnp.zeros(keep_s.shape, F32)

    @pl.when(i >= NPRO)
    def _():
        t = i - NPRO
        ctx = t < PROMPT_TILES
        pos = jnp.maximum(t - PROMPT_TILES, 0) % SAMPLE_TILES_PER_SEQ
        seq_first = ctx | (pos == 0)
        seq_last = ctx | (pos == SAMPLE_TILES_PER_SEQ - 1)
        m = mod_ref[...]
        g = g_ref[...]

        def pre(x):
            return _rms(x, g) * (1.0 + m[4:5]) + m[3:4]

        h_a = pre(x_ref[0:SEQ, :])
        h_b = pre(x_ref[SEQ:TD, :])
        sub = lax.broadcasted_iota(jnp.int32, (HALO, D_MODEL), 0)
        gap = jnp.where(sub == HALO - 1, h_a[SEQ - HALO:, :], jnp.where(sub == 0, h_b[:HALO, :], 0.0))
        hext[0:HALO, :] = jnp.where(seq_first, 0.0, pre(xp_ref[...]))
        hext[FF_A0:FF_A0 + SEQ, :] = h_a
        hext[FF_A0 + SEQ:FF_B0, :] = jnp.where(ctx, 0.0, gap)
        hext[FF_B0:FF_B0 + SEQ, :] = h_b
        hext[FF_B0 + SEQ:FF_B0 + SEQ + HALO, :] = jnp.where(seq_last, 0.0, pre(xn_ref[...]))
        hext[FF_B0 + SEQ + HALO:, :] = jnp.zeros((HALO, D_MODEL), F32)
        hb = hext[...].astype(BF)

        n_grp = FF_EXT // 8
        first, last = FF_A0 // 8, (FF_A0 + FF_ROWS) // 8
        sub8 = lax.broadcasted_iota(jnp.int32, (1, 8, FF_CHUNK), 1)

        def up_conv_act(c):
            halves = []
            for off in (0, D_FF):
                cols = slice(off + c * FF_CHUNK, off + (c + 1) * FF_CHUNK)
                u = _dot(hb, wup_s[:, cols]).reshape(n_grp, 8, FF_CHUNK)
                down = pltpu.roll(u, 1, axis=1)
                up = pltpu.roll(u, 7, axis=1)
                prev = jnp.where(sub8 == 0, down[first - 1:last - 1], down[first:last])
                nxt = jnp.where(sub8 == 7, up[first + 1:last + 1], up[first:last])
                conv = (prev * cw_ref[0:1, cols] + u[first:last] * cw_ref[1:2, cols] + nxt * cw_ref[2:3, cols]
                        + cb_ref[:, cols])
                halves.append(conv.reshape(FF_ROWS, FF_CHUNK))
            return (_silu(halves[1]) * halves[0]).astype(BF)

        n_chunks = D_FF // FF_CHUNK
        acc = _dot(jnp.concatenate([up_conv_act(c) for c in range(n_chunks)], axis=1), wdn_s[...])
        gate = m[5:6]
        for rows, a0 in ((slice(0, SEQ), 0), (slice(SEQ, TD), FF_B0 - FF_A0)):
            res = x_ref[rows, :] + gate * acc[a0:a0 + SEQ]
            if n_out == 1:
                o_refs[0][rows, :] = res
            else:
                kept = jnp.where(ctx, res, keep_s[rows, :])
                keep_s[rows, :] = kept
                o_refs[0][rows, :] = kept
                o_refs[1][rows, :] = res


def _ffn(l, x, mod, norm_g, w_up, conv_w, conv_b, w_down, split_out):
    per = TD // HALO
    last_blk = N_TOK // HALO - 1
    if split_out:
        out_specs = [pl.BlockSpec((TD, D_MODEL), lambda i: (jnp.minimum(_tile_of_step(i), PROMPT_TILES - 1), 0)),
                     pl.BlockSpec((TD, D_MODEL), lambda i: (jnp.maximum(_tile_of_step(i) - PROMPT_TILES, 0), 0))]
        out_shape = [jax.ShapeDtypeStruct((N_PROMPT, D_MODEL), F32), jax.ShapeDtypeStruct((N_SAMPLE, D_MODEL), F32)]
    else:
        out_specs = [pl.BlockSpec((TD, D_MODEL), lambda i: (_tile_of_step(i), 0))]
        out_shape = [jax.ShapeDtypeStruct((N_TOK, D_MODEL), F32)]
    return pl.pallas_call(
        functools.partial(_ffn_kernel, n_out=len(out_specs)),
        grid=(NPRO + N_TILES,),
        in_specs=[
            pl.BlockSpec((HALO, D_MODEL), lambda i: (jnp.maximum(_tile_of_step(i) * per - 1, 0), 0)),
            pl.BlockSpec((TD, D_MODEL), lambda i: (_tile_of_step(i), 0)),
            pl.BlockSpec((HALO, D_MODEL), lambda i: (jnp.minimum((_tile_of_step(i) + 1) * per, last_blk), 0)),
            _mod_spec(l),
            pl.BlockSpec((None, 1, D_MODEL), lambda i: (l, 0, 0)),
            pl.BlockSpec((None, W_ROWS, 2 * D_FF), lambda i: (l, _chunk_of_step(i), 0)),
            pl.BlockSpec((None, CONV_WIDTH, 2 * D_FF), lambda i: (l, 0, 0)),
            pl.BlockSpec((None, 1, 2 * D_FF), lambda i: (l, 0, 0)),
            pl.BlockSpec((None, D_FF // NPRO, D_MODEL), lambda i: (l, _chunk_of_step(i), 0)),
        ],
        out_specs=out_specs,
        out_shape=out_shape,
        scratch_shapes=[pltpu.VMEM((FF_EXT, D_MODEL), F32),
                        pltpu.VMEM((D_MODEL, 2 * D_FF), BF), pltpu.VMEM((D_FF, D_MODEL), BF)]
                       + ([pltpu.VMEM((TD, D_MODEL), F32)] if split_out else []),
        compiler_params=_params(("arbitrary",)),
        name=f"ffn{l}",
    )(x, x, x, mod, norm_g, w_up, conv_w, conv_b, w_down)


def kernel(x_prompt, x_sample, cache_a_k, cache_a_v, cache_c_k, cache_c_v, state_hgrn, state_gla, c, c_ctx, w_ada, b_ada, norm1_g, norm2_g, w_in, a_qn_g, a_kn_g, c_qn_g, c_kn_g, c_lambda, c_subln_g, b_lb_logits, b_norm_g, d_alpha_w, d_alpha_b, d_norm_g, w_branch, w_out, w_up, conv_w, conv_b, w_down):
    xs = (x_prompt.reshape(N_PROMPT, D_MODEL), x_sample.reshape(N_SAMPLE, D_MODEL))
    w_in_t = jnp.swapaxes(w_in, 1, 2)

    cond8 = jnp.concatenate([c_ctx[None, :], c, jnp.zeros((8 - N_GROUPS, D_MODEL), F32)], axis=0)
    mod = _modulation(cond8, w_ada, b_ada)[:, :N_GROUPS].reshape(DEPTH, N_GROUPS, N_MOD, D_MODEL)

    tile4 = lambda g, n: jnp.tile(g, (1, n)).reshape(DEPTH, 1, -1)
    gains = (tile4(a_qn_g, A_HEADS), tile4(a_kn_g, A_KV_HEADS), tile4(c_qn_g, 2 * C_HEADS), tile4(c_kn_g, 2 * C_HEADS),
             c_subln_g.reshape(DEPTH, 1, 2 * C_HEAD_DIM))
    fmaj = lambda t: jnp.moveaxis(t, 2, -1).reshape(DEC_BATCH, DEPTH, -1, PAST_LEN)
    caches = tuple(fmaj(t) for t in (cache_a_k, cache_a_v, cache_c_k, cache_c_v))
    tables = _rope_tables(DEC_SEQ, A_HEAD_DIM, A_HEADS) + _rope_tables(DEC_SEQ, C_HEAD_DIM, 2 * C_HEADS)

    aw = jnp.zeros((DEPTH, 2, 128, 128), F32)
    aw = aw.at[:, 0, 0:D_GATE_RANK].set(d_alpha_w[:, 0]).at[:, 1, D_GATE_RANK:2 * D_GATE_RANK].set(d_alpha_w[:, 1])
    gla_small = (b_lb_logits, aw.astype(BF), d_alpha_b.reshape(DEPTH, 2, 1, 128),
                 tile4(b_norm_g, B_HEADS), tile4(d_norm_g, D_HEADS)) + _gla_constants() + (
                     _state_pattern(_B_BLOCKS), _state_pattern(_D_BLOCKS))

    n1 = norm1_g.reshape(DEPTH, 1, D_MODEL)
    n2 = norm2_g.reshape(DEPTH, 1, D_MODEL)
    cb = conv_b.reshape(DEPTH, 1, 2 * D_FF)

    carried = None
    for l in range(DEPTH):
        lam_init = 0.8 - 0.6 * math.exp(-0.3 * l)
        zac, zbd, h = _inproj(l, xs, mod, n1, w_in_t)
        y_p, g_p, carried = _ctx_mixers(l, zac, zbd, gains, c_lambda, gla_small, lam_init, carried)
        y_s = _attn_sample(l, zac, gains, c_lambda, caches, tables, lam_init)
        g_s = _gla_sample(l, zbd, gla_small, state_hgrn, state_gla)
        x1 = _mix(l, xs, h, y_p, y_s, g_p, g_s, mod, w_in_t, w_branch, w_out)
        xs = tuple(_ffn(l, x1, mod, n2, w_up, conv_w, cb, w_down, split_out=(l == DEPTH - 1)))

    y_prompt = xs[0].reshape(BATCH, SEQ, D_MODEL)
    y_sample = xs[1].reshape(DEC_BATCH, DEC_SEQ, D_MODEL)
    stacked = list(carried)
    feat_shapes = ((A_KV_HEADS, A_HEAD_DIM), (A_KV_HEADS, A_HEAD_DIM), (C_HEADS, 2, C_HEAD_DIM), (C_HEADS, 2 * C_HEAD_DIM))
    ctx = [jnp.moveaxis(t.reshape((BATCH, DEPTH) + fs + (SEQ,)), -1, 2) for t, fs in zip(stacked[:4], feat_shapes)]
    return (y_prompt, y_sample) + tuple(ctx) + tuple(stacked[4:])
```

```python
import functools
import math

import numpy as np
import jax
import jax.numpy as jnp
from jax import lax
from jax.experimental import pallas as pl
from jax.experimental.pallas import tpu as pltpu

F32 = jnp.float32
BF = jnp.bfloat16

D_MODEL = 1024
BATCH = 16
SEQ = 256
DEPTH = 2
DEC_BATCH = 2
DEC_SEQ = 1024
PAST_LEN = 512
GRID_W = 64
ROPE_THETA = 10000.0
EPS = 1e-6
LB_FLOOR = 1e-30
N_MOD = 6
N_BRANCH = 4
BRANCH_W = 256
A_HEADS, A_KV_HEADS, A_HEAD_DIM = 4, 2, 64
C_HEADS, C_HEAD_DIM = 4, 32
B_HEADS, B_KEY_DIM, B_VAL_DIM = 4, 64, 64
D_HEADS, D_KEY_DIM, D_VAL_DIM = 4, 32, 64
D_GATE_RANK = 16
D_GATE_TAU = 16.0
D_FF = 2816
CONV_WIDTH = 3

N_PROMPT = BATCH * SEQ
N_SAMPLE = DEC_BATCH * DEC_SEQ
N_TOK = N_PROMPT + N_SAMPLE
TM = 256
TQ = 512
TD = 2 * SEQ
N_TILES = N_TOK // TD
PROMPT_TILES = N_PROMPT // TD
SAMPLE_TILES_PER_SEQ = DEC_SEQ // TD
N_GROUPS = 1 + DEC_BATCH

AC_W = 1280
BD_W = 2080
BD_PAD = 2176
SMALL_W = AC_W + BD_PAD
MERGE_OFF = AC_W + BD_W
MERGE_ROWS = 560
HALO = 8
FF_EXT = TD + 2 * HALO
FF_CHUNK = 256
NPRO = 8
W_ROWS = D_MODEL // NPRO
LOG2E = 1.4426950408889634
VMEM_LIMIT = 56 * 1024 * 1024
TT = 128
N_LEVELS = 7
SCORE_ROWS = 16


def _dot(a, b):
    return jnp.dot(a, b, preferred_element_type=F32)


def _dot_nt(a, b):
    return lax.dot_general(a, b, (((1,), (1,)), ((), ())), preferred_element_type=F32)


def _dot_tn(a, b):
    return lax.dot_general(a, b, (((0,), (0,)), ((), ())), preferred_element_type=F32)


def _silu(x):
    return x * jax.nn.sigmoid(x)


def _log_sigmoid(x):
    return jnp.minimum(x, 0.0) - jnp.log1p(jnp.exp(-jnp.abs(x)))


def _rms(x, g):
    return x * lax.rsqrt(jnp.mean(x * x, axis=-1, keepdims=True) + EPS) * g


def _head_rms(x, head_dim, g):
    w = x.shape[-1]
    sh = int(math.log2(head_dim))
    r = lax.shift_right_logical(lax.broadcasted_iota(jnp.int32, (w, w), 0), sh)
    c = lax.shift_right_logical(lax.broadcasted_iota(jnp.int32, (w, w), 1), sh)
    bd = jnp.where(r == c, 1.0 / head_dim, 0.0).astype(BF)
    x2 = x * x
    hi = x2.astype(BF)
    lo = (x2 - hi.astype(F32)).astype(BF)
    ms = _dot(hi, bd) + _dot(lo, bd)
    return x * lax.rsqrt(ms + EPS) * g


def _group_of_tile(i):
    return jnp.where(i < PROMPT_TILES, 0, 1 + jnp.maximum(i - PROMPT_TILES, 0) // SAMPLE_TILES_PER_SEQ)


def _params(sem):
    return pltpu.CompilerParams(dimension_semantics=sem, vmem_limit_bytes=VMEM_LIMIT)


def _mod_kernel(cond_ref, w_ref, b_ref, o_ref):
    s = _silu(cond_ref[...])
    o_ref[...] = _dot(s.astype(BF), w_ref[...].astype(BF)) + b_ref[...]


def _modulation(cond8, w_ada, b_ada):
    nb = 1536
    return pl.pallas_call(
        _mod_kernel,
        grid=(DEPTH, N_MOD * D_MODEL // nb),
        in_specs=[
            pl.BlockSpec((8, D_MODEL), lambda l, j: (0, 0)),
            pl.BlockSpec((None, D_MODEL, nb), lambda l, j: (l, 0, j)),
            pl.BlockSpec((None, 1, nb), lambda l, j: (l, 0, j)),
        ],
        out_specs=pl.BlockSpec((None, 8, nb), lambda l, j: (l, 0, j)),
        out_shape=jax.ShapeDtypeStruct((DEPTH, 8, N_MOD * D_MODEL), F32),
        compiler_params=_params(("arbitrary", "arbitrary")),
        name="modulation",
    )(cond8, w_ada, b_ada.reshape(DEPTH, 1, N_MOD * D_MODEL))


def _tile_of_step(i):
    return jnp.maximum(i - NPRO, 0)


def _chunk_of_step(i):
    return jnp.minimum(i, NPRO - 1)


def _tok_specs(width, split):
    if not split:
        return [pl.BlockSpec((TD, width), lambda i: (_tile_of_step(i), 0))]
    return [pl.BlockSpec((TD, width), lambda i: (jnp.minimum(_tile_of_step(i), PROMPT_TILES - 1), 0)),
            pl.BlockSpec((TD, width), lambda i: (jnp.maximum(_tile_of_step(i) - PROMPT_TILES, 0), 0))]


def _tok_load(t, refs):
    if len(refs) == 1:
        return refs[0][...]
    return jnp.where(t < PROMPT_TILES, refs[0][...], refs[1][...])


def _mod_spec(l):
    return pl.BlockSpec((None, None, N_MOD, D_MODEL), lambda i: (l, _group_of_tile(_tile_of_step(i)), 0, 0))


def _stage_rows(i, n):
    return pl.ds(pl.multiple_of(i * n, n), n)


def _inproj_kernel(*refs, n_x):
    x_refs = refs[:n_x]
    mod_ref, g_ref, w_ref, zac_ref, zbd_ref, h_ref, w_s = refs[n_x:]
    i = pl.program_id(0)

    @pl.when(i < NPRO)
    def _():
        w_s[_stage_rows(i, SMALL_W // NPRO), :] = w_ref[...].astype(BF)

    @pl.when(i >= NPRO)
    def _():
        m = mod_ref[...]
        h = _rms(_tok_load(i - NPRO, x_refs), g_ref[...]) * (1.0 + m[1:2]) + m[0:1]
        hb = h.astype(BF)
        h_ref[...] = hb
        zac_ref[...] = _dot_nt(hb, w_s[:AC_W, :])
        zbd_ref[...] = _dot_nt(hb, w_s[AC_W:, :])


def _inproj(l, xs, mod, norm_g, w_in_t):
    split = len(xs) == 2
    return pl.pallas_call(
        functools.partial(_inproj_kernel, n_x=len(xs)),
        grid=(NPRO + N_TILES,),
        in_specs=_tok_specs(D_MODEL, split) + [
            _mod_spec(l),
            pl.BlockSpec((None, 1, D_MODEL), lambda i: (l, 0, 0)),
            pl.BlockSpec((None, SMALL_W // NPRO, D_MODEL), lambda i: (l, _chunk_of_step(i), 0)),
        ],
        out_specs=[
            pl.BlockSpec((TD, AC_W), lambda i: (_tile_of_step(i), 0)),
            pl.BlockSpec((TD, BD_PAD), lambda i: (_tile_of_step(i), 0)),
            pl.BlockSpec((TD, D_MODEL), lambda i: (_tile_of_step(i), 0)),
        ],
        out_shape=[
            jax.ShapeDtypeStruct((N_TOK, AC_W), F32),
            jax.ShapeDtypeStruct((N_TOK, BD_PAD), F32),
            jax.ShapeDtypeStruct((N_TOK, D_MODEL), BF),
        ],
        scratch_shapes=[pltpu.VMEM((SMALL_W, D_MODEL), BF)],
        compiler_params=_params(("arbitrary",)),
        name=f"inproj{l}",
    )(*xs, mod, norm_g, w_in_t)


def _rope_tables(n_tokens, head_dim, n_rep):
    rows = n_tokens // GRID_W
    row = np.repeat(np.arange(rows), GRID_W).astype(np.float64)
    col = np.tile(np.arange(GRID_W), rows).astype(np.float64)
    half = head_dim // 2
    q4 = head_dim // 4
    freqs = ROPE_THETA ** (-np.arange(0, half, 2, dtype=np.float64) / half)
    ang_r = row[:, None] * freqs
    ang_c = col[:, None] * freqs
    ang = np.concatenate([ang_r, ang_r, ang_c, ang_c], axis=-1)
    cos, sin = np.cos(ang), np.sin(ang)
    first = (np.arange(head_dim) % (2 * q4)) < q4
    s_dn = np.where(first, -sin, 0.0)
    s_up = np.where(first, 0.0, sin)
    return tuple(jnp.asarray(np.tile(t, (1, n_rep)), dtype=F32) for t in (cos, s_dn, s_up))


def _rope(x, cos, s_dn, s_up, q4):
    w = x.shape[-1]
    return x * cos + pltpu.roll(x, w - q4, 1) * s_dn + pltpu.roll(x, q4, 1) * s_up


def _softmax_pv_group(maps, outs):
    scores = [[_dot(q, k) if t else _dot_nt(q, k) for k, t in zip(ks, fm)] for q, ks, _, _, fm in maps]
    yield
    probs = []
    for ss in scores:
        m = ss[0].max(axis=-1, keepdims=True)
        for s in ss[1:]:
            m = jnp.maximum(m, s.max(axis=-1, keepdims=True))
        probs.append([jnp.exp2(s - m).astype(BF) for s in ss])
    yield
    for ps, (_, _, vexts, half, fm) in zip(probs, maps):
        o = None
        for p, v, t in zip(ps, vexts, fm):
            part = _dot_nt(p, v) if t else _dot(p, v)
            o = part if o is None else o + part
        outs.append(o[:, half * 64:(half + 1) * 64] / o[:, (1 - half) * 64:(1 - half) * 64 + 1])
    yield


def _with_ones(v, half, feature_major):
    idx = lax.broadcasted_iota(jnp.int32, v.shape, 0 if feature_major else 1)
    return jnp.where(lax.shift_right_logical(idx, 6) == half, v, jnp.ones_like(v))


def _attend_heads(aq, cq, ka, va, kc, vc, fm, lam, gsub, lam_init, y_ref, group):
    aqb = (aq * (A_HEAD_DIM ** -0.5 * LOG2E)).astype(BF)
    cqb = (cq * (C_HEAD_DIM ** -0.5 * LOG2E)).astype(BF)
    rep = A_HEADS // A_KV_HEADS

    def feat(x, t, sl):
        return x[sl, :] if t else x[:, sl]

    maps = []
    for h in range(A_HEADS):
        g = h // rep
        sl = slice(g * A_HEAD_DIM, (g + 1) * A_HEAD_DIM)
        maps.append((aqb[:, h * A_HEAD_DIM:(h + 1) * A_HEAD_DIM], [feat(k, t, sl) for k, t in zip(ka, fm)],
                     [_with_ones(v, g, t) for v, t in zip(va, fm)], g, fm))
    for h in range(C_HEADS):
        slab = slice((h // 2) * 128, (h // 2 + 1) * 128)
        vh = [_with_ones(feat(v, t, slab), h % 2, t) for v, t in zip(vc, fm)]
        for j in range(2):
            sl = slice((2 * h + j) * C_HEAD_DIM, (2 * h + j + 1) * C_HEAD_DIM)
            maps.append((cqb[:, sl], [feat(k, t, sl) for k, t in zip(kc, fm)], vh, h % 2, fm))
    outs = []
    for i in range(0, len(maps), group):
        yield from _softmax_pv_group(maps[i:i + group], outs)
    for h in range(A_HEADS):
        y_ref[:, h * A_HEAD_DIM:(h + 1) * A_HEAD_DIM] = outs[h]
    vd = 2 * C_HEAD_DIM
    for h in range(C_HEADS):
        d = outs[A_HEADS + 2 * h] - lam * outs[A_HEADS + 2 * h + 1]
        y_ref[:, BRANCH_W + h * vd:BRANCH_W + (h + 1) * vd] = _rms(d, gsub) * (1.0 - lam_init)


def _lambda(cl):
    s1 = jnp.sum(cl[0:1] * cl[1:2], axis=-1, keepdims=True)
    s2 = jnp.sum(cl[2:3] * cl[3:4], axis=-1, keepdims=True)
    return jnp.exp(s1) - jnp.exp(s2)


def _attn_prompt_kernel(z_ref, gaq, gak, gcq, gck, gsub, cl_ref, y_ref, oak, oav, ock, ocv, *, lam_init):
    z = z_ref[...]
    ak = _head_rms(z[:, 256:384], A_HEAD_DIM, gak[...])
    av = z[:, 384:512]
    ck = _head_rms(z[:, 768:1024], C_HEAD_DIM, gck[...])
    cv = z[:, 1024:1280]
    oak[...] = ak.T
    oav[...] = av.T
    ock[...] = ck.T
    ocv[...] = cv.T
    aq = _head_rms(z[:, 0:256], A_HEAD_DIM, gaq[...])
    cq = _head_rms(z[:, 512:768], C_HEAD_DIM, gcq[...])
    lam = _lambda(cl_ref[...]) + lam_init
    yield
    yield from _attend_heads(aq, cq, [ak.astype(BF)], [av.astype(BF)], [ck.astype(BF)], [cv.astype(BF)], [False],
                             lam, gsub[...], lam_init, y_ref, group=A_HEADS + 2 * C_HEADS)


def _attn_sample_kernel(z_ref, gaq, gak, gcq, gck, gsub, cl_ref, cak, cav, cck, ccv,
                        cosa, sda, sua, cosc, sdc, suc, y_ref, ka_s, va_s, kc_s, vc_s, *, lam_init):
    qi = pl.program_id(1)
    qa4, qc4 = A_HEAD_DIM // 4, C_HEAD_DIM // 4

    @pl.when(qi == 0)
    def _():
        ak = _head_rms(z_ref[:, 256:384], A_HEAD_DIM, gak[...])
        ka_s[...] = _rope(ak, cosa[:, :128], sda[:, :128], sua[:, :128], qa4).astype(BF)
        va_s[...] = z_ref[:, 384:512].astype(BF)
        ck = _head_rms(z_ref[:, 768:1024], C_HEAD_DIM, gck[...])
        kc_s[...] = _rope(ck, cosc[...], sdc[...], suc[...], qc4).astype(BF)
        vc_s[...] = z_ref[:, 1024:1280].astype(BF)

    rows = pl.ds(pl.multiple_of(qi * TQ, TQ), TQ)
    aq = _head_rms(z_ref[rows, 0:256], A_HEAD_DIM, gaq[...])
    aq = _rope(aq, cosa[rows, :], sda[rows, :], sua[rows, :], qa4)
    cq = _head_rms(z_ref[rows, 512:768], C_HEAD_DIM, gcq[...])
    cq = _rope(cq, cosc[rows, :], sdc[rows, :], suc[rows, :], qc4)
    lam = _lambda(cl_ref[...]) + lam_init
    yield from _attend_heads(aq, cq,
                             [cak[...].astype(BF), ka_s[...]], [cav[...].astype(BF), va_s[...]],
                             [cck[...].astype(BF), kc_s[...]], [ccv[...].astype(BF), vc_s[...]], [True, False],
                             lam, gsub[...], lam_init, y_ref, group=4)


def _gain_specs(l, nd):
    zeros = (0,) * (nd - 1)
    widths = (256, 128, 256, 256, 64)
    return [pl.BlockSpec((None, 1, w), lambda *a: (l, 0, 0)) for w in widths] + \
           [pl.BlockSpec((None, 4, C_HEAD_DIM), lambda *a: (l, 0, 0))]


N_VTILES = BRANCH_W // 128


def _state_blocks(nh, kd, vd):
    per = 128 // vd
    return [(h // per, slice((h * kd) % 128, (h * kd) % 128 + kd), slice((h % per) * vd, (h % per + 1) * vd))
            for h in range(nh)]


_B_BLOCKS = _state_blocks(B_HEADS, B_KEY_DIM, B_VAL_DIM)
_D_BLOCKS = _state_blocks(D_HEADS, D_KEY_DIM, D_VAL_DIM)


def _gla_constants():
    idx = np.arange(TT)
    scans, masks = [], []
    for rev in (False, True):
        eff = (TT - 1 - idx) if rev else idx
        et, eu = eff[:, None], eff[None, :]
        sc, mk = [], []
        for j in range(N_LEVELS):
            b = 1 << j
            start = et - et % b
            odd = (et // b) % 2 == 1
            sc.append(np.where(odd, (eu > start) & (eu <= et), (eu > et) & (eu <= start + b)))
            mk.append(((et // b) % 2 == 1) & (eu // b == et // b - 1))
        sc.append(eu <= et)
        sc.append(eu > et)
        mk.append(eu == et)
        scans.append(np.concatenate([np.concatenate(sc, axis=0)] * 2, axis=-1))
        masks.append(np.stack(mk))
    return tuple(jnp.asarray(np.stack(t), BF) for t in (scans, masks))


def _gla_prepare(q, k, v, la2, scan_ref, d, rev, use_state):
    tt = q.shape[0]
    la_hi = la2.astype(BF)
    la_lo = (la2 - la_hi.astype(F32)).astype(BF)
    la_split = jnp.concatenate([la_hi, la_lo], axis=0)

    sums = _dot(scan_ref[d], la_split)

    def factor(i):
        return jnp.exp2(sums[i * tt:(i + 1) * tt])

    qs, ks = [], []
    for j in range(N_LEVELS):
        f = factor(j)
        qs.append((q * f).astype(BF))
        ks.append((k * f).astype(BF).T)
    qs.append(q.astype(BF))
    ks.append(k.astype(BF).T)
    k_out = (k * factor(N_LEVELS + 1)).astype(BF)
    vb = v.astype(BF)
    q_in = d_tile = None
    if use_state:
        q_in = (q * factor(N_LEVELS)).astype(BF)
        ones = jnp.ones((2 * tt, 128), BF)
        d_tile = [jnp.exp2(_dot_tn(la_split[:, i * 128:(i + 1) * 128], ones)) for i in range(q.shape[1] // 128)]

    return dict(qs=qs, ks=ks, k_out=k_out, vb=vb, q_in=q_in, d_tile=d_tile)


def _lane_keep(x, lo, hi):
    lane = lax.broadcasted_iota(jnp.int32, x.shape, 1)
    return jnp.where((lane >= lo) & (lane < hi), x, jnp.zeros_like(x))


def _level_rows(j, rev):
    b = 1 << j
    if j == N_LEVELS or b < SCORE_ROWS:
        return None
    return [(i * b, (i + 1) * b) for i in range(TT // b) if (i % 2 == 1) != rev]


def _gla_scores(p, mask_ref, d, rev, nh, kd):
    out = []
    for h in range(nh):
        c, lo = divmod(h * kd, 128)
        blocks = [None] * (TT // SCORE_ROWS)

        def add(r0, val):
            for i in range(val.shape[0] // SCORE_ROWS):
                piece = val[i * SCORE_ROWS:(i + 1) * SCORE_ROWS]
                k = r0 // SCORE_ROWS + i
                blocks[k] = piece if blocks[k] is None else blocks[k] + piece

        for j in range(N_LEVELS + 1):
            kt = p["ks"][j]
            own = kt[c * 128 + lo:c * 128 + lo + kd, :]
            pieces = [jnp.zeros((lo, kt.shape[1]), BF)] if lo else []
            pieces.append(own)
            if lo + kd < 128:
                pieces.append(jnp.zeros((128 - lo - kd, kt.shape[1]), BF))
            kh = jnp.concatenate(pieces, axis=0) if len(pieces) > 1 else own
            qj = p["qs"][j][:, c * 128:(c + 1) * 128]
            ranges = _level_rows(j, rev)
            if ranges is None:
                add(0, _dot(qj, kh).astype(BF) * mask_ref[d, j])
            else:
                q_rows = jnp.concatenate([qj[a:b] for a, b in ranges], axis=0) if len(ranges) > 1 else qj[ranges[0][0]:ranges[0][1]]
                m_rows = [mask_ref[d, j, a:b, :] for a, b in ranges]
                t = _dot(q_rows, kh).astype(BF)
                off = 0
                for (a, b), m in zip(ranges, m_rows):
                    add(a, t[off:off + b - a] * m)
                    off += b - a
        zero = jnp.zeros((SCORE_ROWS, TT), BF)
        out.append(jnp.concatenate([z if z is not None else zero for z in blocks], axis=0))
    return out


def _gla_outputs(p, scs, blk_ref, st_ref, d, nh, kd, vd, use_state, o_ref, rows, col0, accumulate):
    per = 128 // vd
    for c in range(nh // per):
        vt = p["vb"][:, c * 128:(c + 1) * 128]
        kt = (c * per * kd) // 128
        ktile = slice(kt * 128, (kt + 1) * 128)
        o = None
        for i in range(per):
            t = _dot(scs[c * per + i], _lane_keep(vt, i * vd, (i + 1) * vd))
            o = t if o is None else o + t
        kv = _dot_tn(p["k_out"][:, ktile], vt) * blk_ref[c]
        if use_state:
            st = st_ref[d, c]
            o = o + _dot(p["q_in"][:, ktile], st.astype(BF))
            st_ref[d, c] = st * p["d_tile"][kt] + kv
        else:
            st_ref[d, c] = kv
        osl = slice(col0 + c * 128, col0 + (c + 1) * 128)
        if accumulate:
            o_ref[rows, osl] += o
        else:
            o_ref[rows, osl] = o


def _gla_kernel(*refs, layer, n_tiles, has_state, pump=None):
    pump = pump or (lambda: None)
    if has_state:
        (z_ref, lbl_ref, aw_ref, ab_ref, bng, dng, scan_ref, mask_ref, blkb_ref, blkd_ref, sh_in, sd_in,
         y_ref, o_scr, sth, std) = refs
        for packed, raw, blocks in ((sth, sh_in, _B_BLOCKS), (std, sd_in, _D_BLOCKS)):
            packed[...] = jnp.zeros(packed.shape, F32)
            for dd in range(2):
                for h, (c, r, ln) in enumerate(blocks):
                    packed[dd, c, r, ln] = raw[dd, h]
    else:
        (z_ref, lbl_ref, aw_ref, ab_ref, bng, dng, scan_ref, mask_ref, blkb_ref, blkd_ref,
         y_ref, sh_out, sd_out, o_scr, sth, std) = refs

    gates = []
    for d in range(2):
        logits = [lbl_ref[d, i:i + 1, :] for i in range(DEPTH)]
        mx = functools.reduce(jnp.maximum, logits)
        ex = [jnp.exp(t - mx) for t in logits]
        den = functools.reduce(lambda a, b: a + b, ex)
        ps = [t / den for t in ex]
        lb = functools.reduce(lambda a, b: a + b, ps[:layer + 1]) - ps[0]
        gates.append((lb, jnp.log(jnp.maximum(lb, LB_FLOOR)), jnp.log1p(-lb)))

    o_scr[...] = jnp.zeros(o_scr.shape, F32)

    def tile(i, use_state):
        preps = []
        for d in range(2):
            rev = d == 1
            lb, log_lb, log_1m = gates[d]
            j = (n_tiles - 1 - i) if rev else i
            rows = pl.ds(j * TT if isinstance(j, int) else pl.multiple_of(j * TT, TT), TT)
            bq = z_ref[rows, 0:256]
            zf = z_ref[rows, 768:1024] if rev else z_ref[rows, 512:768]
            b2 = log_1m + _log_sigmoid(zf)
            la = jnp.maximum(log_lb, b2) + jnp.log1p(jnp.exp(-jnp.abs(log_lb - b2)))
            kb = (1.0 - lb) * jax.nn.sigmoid(-zf)
            pre = _dot(z_ref[rows, 2048:2176].astype(BF), aw_ref[d]) + ab_ref[d]
            la_d = _log_sigmoid(pre) * (LOG2E / D_GATE_TAU)
            pb = _gla_prepare(_silu(bq), kb, z_ref[rows, 256:512], la * LOG2E, scan_ref, d, rev, use_state)
            pd = _gla_prepare(z_ref[rows, 1280:1408] * (D_KEY_DIM ** -0.5), z_ref[rows, 1408:1536],
                              z_ref[rows, 1536:1792], la_d, scan_ref, d, rev, use_state)
            preps.append((d, rows, pb, pd))
        pump()
        scores = [(_gla_scores(pb, mask_ref, d, d == 1, B_HEADS, B_KEY_DIM),
                   _gla_scores(pd, mask_ref, d, d == 1, D_HEADS, D_KEY_DIM)) for d, _, pb, pd in preps]
        pump()
        for (d, rows, pb, pd), (sb, sd) in zip(preps, scores):
            _gla_outputs(pb, sb, blkb_ref, sth, d, B_HEADS, B_KEY_DIM, B_VAL_DIM, use_state, o_scr, rows, 0, True)
            _gla_outputs(pd, sd, blkd_ref, std, d, D_HEADS, D_KEY_DIM, D_VAL_DIM, use_state, o_scr, rows, 256, True)
        pump()

    if has_state:
        def body(i, carry):
            tile(i, True)
            return carry

        lax.fori_loop(0, n_tiles, body, 0)
    else:
        for i in range(n_tiles):
            tile(i, i > 0)

    def finish(i, carry):
        rows = pl.ds(pl.multiple_of(i * TM, TM), TM)
        y_ref[rows, 0:256] = _head_rms(o_scr[rows, 0:256], B_VAL_DIM, bng[...]) * _silu(z_ref[rows, 1024:1280])
        y_ref[rows, 256:512] = _head_rms(o_scr[rows, 256:512], D_VAL_DIM, dng[...]) * _silu(z_ref[rows, 1792:2048])
        return carry

    lax.fori_loop(0, (n_tiles * TT) // TM, finish, 0)
    if not has_state:
        for packed, raw, blocks in ((sth, sh_out, _B_BLOCKS), (std, sd_out, _D_BLOCKS)):
            for dd in range(2):
                for h, (c, r, ln) in enumerate(blocks):
                    raw[dd, h] = packed[dd, c, r, ln]


def _gla_common_specs(l, nd):
    return [
        pl.BlockSpec((2, DEPTH, 256), lambda *a: (0, 0, 0)),
        pl.BlockSpec((None, 2, 128, 128), lambda *a: (l, 0, 0, 0)),
        pl.BlockSpec((None, 2, 1, 128), lambda *a: (l, 0, 0, 0)),
        pl.BlockSpec((None, 1, 256), lambda *a: (l, 0, 0)),
        pl.BlockSpec((None, 1, 256), lambda *a: (l, 0, 0)),
        pl.BlockSpec((2, (N_LEVELS + 2) * TT, 2 * TT), lambda *a: (0, 0, 0)),
        pl.BlockSpec((2, N_LEVELS + 1, TT, TT), lambda *a: (0, 0, 0, 0)),
        pl.BlockSpec((N_VTILES, 128, 128), lambda *a: (0, 0, 0)),
        pl.BlockSpec((N_VTILES, 128, 128), lambda *a: (0, 0, 0)),
    ]


_STATE_SCRATCH = [pltpu.VMEM((2, N_VTILES, 128, 128), F32), pltpu.VMEM((2, N_VTILES, 128, 128), F32)]


def _state_pattern(blocks):
    pat = np.zeros((N_VTILES, 128, 128), np.float32)
    for c, r, ln in blocks:
        pat[c, r, ln] = 1.0
    return jnp.asarray(pat)


N_ATTN_IN = 7
N_ATTN_OUT = 5


def _ctx_mixers_kernel(*refs, layer, lam_init, n_carried):
    n_in = N_ATTN_IN + 1 + len(_gla_common_specs(0, 1))
    a_in, g_in, outs = refs[:N_ATTN_IN], refs[N_ATTN_IN:n_in], refs[n_in + n_carried:]
    y_ac, caches, y_bd, states = outs[0], outs[1:N_ATTN_OUT], outs[N_ATTN_OUT], outs[N_ATTN_OUT + 1:]
    attn = _attn_prompt_kernel(*a_in, y_ac, *caches, lam_init=lam_init)
    _gla_kernel(*g_in, y_bd, *states, layer=layer, n_tiles=SEQ // TT, has_state=False,
                pump=lambda: next(attn, None))
    for _ in attn:
        pass


def _ctx_mixers(l, zac, zbd, gains, c_lambda, small, lam_init, carried):
    stacked = [(BATCH, DEPTH, w, SEQ) for w in (128, 128, 256, 256)] + [
        (BATCH, DEPTH, 2, B_HEADS, B_KEY_DIM, B_VAL_DIM), (BATCH, DEPTH, 2, D_HEADS, D_KEY_DIM, D_VAL_DIM)]
    seq = lambda w: pl.BlockSpec((SEQ, w), lambda b: (b, 0))
    layer_block = lambda s: pl.BlockSpec((None, None) + s[2:], lambda b: (b, l) + (0,) * (len(s) - 2))
    in_specs = [seq(AC_W)] + _gain_specs(l, 1) + [seq(BD_PAD)] + _gla_common_specs(l, 1)
    n_carried = 0 if carried is None else len(carried)
    aliases = {}
    if carried is not None:
        out_idx = [1, 2, 3, 4, 6, 7]
        aliases = {len(in_specs) + k: out_idx[k] for k in range(n_carried)}
        in_specs = in_specs + [pl.BlockSpec(memory_space=pl.ANY)] * n_carried
    out = pl.pallas_call(
        functools.partial(_ctx_mixers_kernel, layer=l, lam_init=lam_init, n_carried=n_carried),
        grid=(BATCH,),
        in_specs=in_specs,
        out_specs=[seq(2 * BRANCH_W)] + [layer_block(s) for s in stacked[:4]] + [seq(2 * BRANCH_W)]
                  + [layer_block(s) for s in stacked[4:]],
        out_shape=[jax.ShapeDtypeStruct((N_PROMPT, 2 * BRANCH_W), F32)]
                  + [jax.ShapeDtypeStruct(s, F32) for s in stacked[:4]]
                  + [jax.ShapeDtypeStruct((N_PROMPT, 2 * BRANCH_W), F32)]
                  + [jax.ShapeDtypeStruct(s, F32) for s in stacked[4:]],
        scratch_shapes=[pltpu.VMEM((SEQ, 2 * BRANCH_W), F32)] + _STATE_SCRATCH,
        input_output_aliases=aliases,
        compiler_params=_params(("arbitrary",)),
        name=f"ctx_mixers{l}",
    )(zac, *gains, c_lambda, zbd, *small, *(carried or ()))
    return out[0], out[5], out[1:5] + out[6:]


def _attn_sample_body(*refs, lam_init):
    for _ in _attn_sample_kernel(*refs, lam_init=lam_init):
        pass


def _attn_sample(l, zac, gains, c_lambda, caches, tables, lam_init):
    first_blk = N_PROMPT // DEC_SEQ
    cache_specs = [pl.BlockSpec((None, None, w, PAST_LEN), lambda b, q: (b, l, 0, 0)) for w in (128, 128, 256, 256)]
    table_specs = [pl.BlockSpec((DEC_SEQ, 256), lambda b, q: (0, 0)) for _ in range(6)]
    return pl.pallas_call(
        functools.partial(_attn_sample_body, lam_init=lam_init),
        grid=(DEC_BATCH, DEC_SEQ // TQ),
        in_specs=[pl.BlockSpec((DEC_SEQ, AC_W), lambda b, q: (first_blk + b, 0))] + _gain_specs(l, 2)
                 + cache_specs + table_specs,
        out_specs=pl.BlockSpec((TQ, 2 * BRANCH_W), lambda b, q: (b * (DEC_SEQ // TQ) + q, 0)),
        out_shape=jax.ShapeDtypeStruct((N_SAMPLE, 2 * BRANCH_W), F32),
        scratch_shapes=[pltpu.VMEM((DEC_SEQ, 128), BF), pltpu.VMEM((DEC_SEQ, 128), BF),
                        pltpu.VMEM((DEC_SEQ, 256), BF), pltpu.VMEM((DEC_SEQ, 256), BF)],
        compiler_params=_params(("arbitrary", "arbitrary")),
        name=f"attn_sample{l}",
    )(zac, *gains, c_lambda, *caches, *tables)


def _gla_sample(l, zbd, small, st_h, st_d):
    first_blk = N_PROMPT // DEC_SEQ
    return pl.pallas_call(
        functools.partial(_gla_kernel, layer=l, n_tiles=DEC_SEQ // TT, has_state=True),
        grid=(DEC_BATCH,),
        in_specs=[pl.BlockSpec((DEC_SEQ, BD_PAD), lambda b: (first_blk + b, 0))] + _gla_common_specs(l, 1) + [
            pl.BlockSpec((None, None, 2, B_HEADS, B_KEY_DIM, B_VAL_DIM), lambda b: (b, l, 0, 0, 0, 0)),
            pl.BlockSpec((None, None, 2, D_HEADS, D_KEY_DIM, D_VAL_DIM), lambda b: (b, l, 0, 0, 0, 0)),
        ],
        out_specs=pl.BlockSpec((DEC_SEQ, 2 * BRANCH_W), lambda b: (b, 0)),
        out_shape=jax.ShapeDtypeStruct((N_SAMPLE, 2 * BRANCH_W), F32),
        scratch_shapes=[pltpu.VMEM((DEC_SEQ, 2 * BRANCH_W), F32)] + _STATE_SCRATCH,
        compiler_params=_params(("arbitrary",)),
        name=f"gla_sample{l}",
    )(zbd, *small, st_h, st_d)


def _mix_kernel(*refs, n_x):
    x_refs = refs[:n_x]
    (h_ref, yp_ref, ys_ref, gp_ref, gs_ref, mod_ref, win_ref, wb_ref, wo_ref, o_ref, wm_s, wb_s, wo_s) = refs[n_x:]
    i = pl.program_id(0)

    @pl.when(i < NPRO)
    def _():
        rows = _stage_rows(i, W_ROWS)
        wm_s[_stage_rows(i, MERGE_ROWS), :] = win_ref[...].astype(BF)
        wb_s[rows, :] = wb_ref[...].astype(BF)
        wo_s[rows, :] = wo_ref[...].astype(BF)

    @pl.when(i >= NPRO)
    def _():
        t = i - NPRO
        hb = h_ref[...]
        yac = _tok_load(t, (yp_ref, ys_ref))
        ybd = _tok_load(t, (gp_ref, gs_ref))
        branches = (yac[:, :BRANCH_W], ybd[:, :BRANCH_W], yac[:, BRANCH_W:], ybd[:, BRANCH_W:])
        mixed = None
        for n, y in enumerate(branches):
            logits = _dot_nt(hb, wm_s[n * D_MODEL:(n + 1) * D_MODEL, :])
            term = jax.nn.sigmoid(logits) * _dot(y.astype(BF), wb_s[n * BRANCH_W:(n + 1) * BRANCH_W, :])
            mixed = term if mixed is None else mixed + term
        o_ref[...] = _tok_load(t, x_refs) + mod_ref[2:3, :] * _dot(mixed.astype(BF), wo_s[...])


def _mix(l, xs, h, y_p, y_s, g_p, g_s, mod, w_in_t, w_branch, w_out):
    chunk = pl.BlockSpec((None, W_ROWS, D_MODEL), lambda i: (l, _chunk_of_step(i), 0))
    return pl.pallas_call(
        functools.partial(_mix_kernel, n_x=len(xs)),
        grid=(NPRO + N_TILES,),
        in_specs=_tok_specs(D_MODEL, len(xs) == 2) + _tok_specs(D_MODEL, False)
                 + _tok_specs(2 * BRANCH_W, True) + _tok_specs(2 * BRANCH_W, True)
                 + [_mod_spec(l),
                    pl.BlockSpec((None, MERGE_ROWS, D_MODEL), lambda i: (l, MERGE_OFF // MERGE_ROWS + _chunk_of_step(i), 0)),
                    chunk, chunk],
        out_specs=pl.BlockSpec((TD, D_MODEL), lambda i: (_tile_of_step(i), 0)),
        out_shape=jax.ShapeDtypeStruct((N_TOK, D_MODEL), F32),
        scratch_shapes=[pltpu.VMEM((NPRO * MERGE_ROWS, D_MODEL), BF), pltpu.VMEM((D_MODEL, D_MODEL), BF),
                        pltpu.VMEM((D_MODEL, D_MODEL), BF)],
        compiler_params=_params(("arbitrary",)),
        name=f"mix{l}",
    )(*xs, h, y_p, y_s, g_p, g_s, mod, w_in_t, w_branch.reshape(DEPTH, N_BRANCH * BRANCH_W, D_MODEL), w_out)


def _ffn_kernel(*refs, n_out):
    (xp_ref, x_ref, xn_ref, mod_ref, g_ref, wup_ref, cw_ref, cb_ref, wdn_ref) = refs[:9]
    o_refs = refs[9:9 + n_out]
    hext, wup_s, wdn_s = refs[9 + n_out:12 + n_out]
    i = pl.program_id(0)

    @pl.when(i < NPRO)
    def _():
        wup_s[_stage_rows(i, W_ROWS), :] = wup_ref[...].astype(BF)
        wdn_s[_stage_rows(i, D_FF // NPRO), :] = wdn_ref[...].astype(BF)

    if n_out == 2:
        keep_s = refs[12 + n_out]

        @pl.when(i == 0)
        def _():
            keep_s[...] = jnp.zeros(keep_s.shape, F32)

    @pl.when(i >= NPRO)
    def _():
        t = i - NPRO
        ctx = t < PROMPT_TILES
        pos = jnp.maximum(t - PROMPT_TILES, 0) % SAMPLE_TILES_PER_SEQ
        seq_first = ctx | (pos == 0)
        seq_last = ctx | (pos == SAMPLE_TILES_PER_SEQ - 1)
        m = mod_ref[...]
        g = g_ref[...]

        def pre(x):
            return _rms(x, g) * (1.0 + m[4:5]) + m[3:4]

        sub = lax.broadcasted_iota(jnp.int32, (HALO, D_MODEL), 0)
        hext[0:TD, :] = pre(x_ref[...])
        hext[TD:TD + HALO, :] = jnp.where(seq_last | (sub != 0), 0.0, pre(xn_ref[...]))
        hext[TD + HALO:, :] = jnp.where(seq_first | (sub != HALO - 1), 0.0, pre(xp_ref[...]))
        hb = hext[...].astype(BF)

        n_tile, mid = TD // 8, SEQ // 8
        sub8 = lax.broadcasted_iota(jnp.int32, (1, 8, FF_CHUNK), 1)

        def up_conv_act(c):
            halves = []
            for off in (0, D_FF):
                cols = slice(off + c * FF_CHUNK, off + (c + 1) * FF_CHUNK)
                u = _dot(hb, wup_s[:, cols]).reshape(FF_EXT // 8, 8, FF_CHUNK)
                down = pltpu.roll(u, 1, axis=1)
                up = pltpu.roll(u, 7, axis=1)
                before = jnp.concatenate([down[-1:], down[:mid - 1], jnp.where(ctx, 0.0, down[mid - 1:mid]),
                                          down[mid:n_tile - 1]], axis=0)
                after = jnp.concatenate([up[1:mid], jnp.where(ctx, 0.0, up[mid:mid + 1]), up[mid + 1:n_tile + 1]],
                                        axis=0)
                prev = jnp.where(sub8 == 0, before, down[:n_tile])
                nxt = jnp.where(sub8 == 7, after, up[:n_tile])
                conv = (prev * cw_ref[0:1, cols] + u[:n_tile] * cw_ref[1:2, cols] + nxt * cw_ref[2:3, cols]
                        + cb_ref[:, cols])
                halves.append(conv.reshape(TD, FF_CHUNK))
            return (_silu(halves[1]) * halves[0]).astype(BF)

        n_chunks = D_FF // FF_CHUNK
        acc = _dot(jnp.concatenate([up_conv_act(c) for c in range(n_chunks)], axis=1), wdn_s[...])
        gate = m[5:6]
        res = x_ref[...] + gate * acc
        if n_out == 1:
            o_refs[0][...] = res
        else:
            kept = jnp.where(ctx, res, keep_s[...])
            keep_s[...] = kept
            o_refs[0][...] = kept
            o_refs[1][...] = res


def _ffn(l, x, mod, norm_g, w_up, conv_w, conv_b, w_down, split_out):
    per = TD // HALO
    last_blk = N_TOK // HALO - 1
    if split_out:
        out_specs = [pl.BlockSpec((TD, D_MODEL), lambda i: (jnp.minimum(_tile_of_step(i), PROMPT_TILES - 1), 0)),
                     pl.BlockSpec((TD, D_MODEL), lambda i: (jnp.maximum(_tile_of_step(i) - PROMPT_TILES, 0), 0))]
        out_shape = [jax.ShapeDtypeStruct((N_PROMPT, D_MODEL), F32), jax.ShapeDtypeStruct((N_SAMPLE, D_MODEL), F32)]
    else:
        out_specs = [pl.BlockSpec((TD, D_MODEL), lambda i: (_tile_of_step(i), 0))]
        out_shape = [jax.ShapeDtypeStruct((N_TOK, D_MODEL), F32)]
    return pl.pallas_call(
        functools.partial(_ffn_kernel, n_out=len(out_specs)),
        grid=(NPRO + N_TILES,),
        in_specs=[
            pl.BlockSpec((HALO, D_MODEL), lambda i: (jnp.maximum(_tile_of_step(i) * per - 1, 0), 0)),
            pl.BlockSpec((TD, D_MODEL), lambda i: (_tile_of_step(i), 0)),
            pl.BlockSpec((HALO, D_MODEL), lambda i: (jnp.minimum((_tile_of_step(i) + 1) * per, last_blk), 0)),
            _mod_spec(l),
            pl.BlockSpec((None, 1, D_MODEL), lambda i: (l, 0, 0)),
            pl.BlockSpec((None, W_ROWS, 2 * D_FF), lambda i: (l, _chunk_of_step(i), 0)),
            pl.BlockSpec((None, CONV_WIDTH, 2 * D_FF), lambda i: (l, 0, 0)),
            pl.BlockSpec((None, 1, 2 * D_FF), lambda i: (l, 0, 0)),
            pl.BlockSpec((None, D_FF // NPRO, D_MODEL), lambda i: (l, _chunk_of_step(i), 0)),
        ],
        out_specs=out_specs,
        out_shape=out_shape,
        scratch_shapes=[pltpu.VMEM((FF_EXT, D_MODEL), F32),
                        pltpu.VMEM((D_MODEL, 2 * D_FF), BF), pltpu.VMEM((D_FF, D_MODEL), BF)]
                       + ([pltpu.VMEM((TD, D_MODEL), F32)] if split_out else []),
        compiler_params=_params(("arbitrary",)),
        name=f"ffn{l}",
    )(x, x, x, mod, norm_g, w_up, conv_w, conv_b, w_down)


def kernel(x_prompt, x_sample, cache_a_k, cache_a_v, cache_c_k, cache_c_v, state_hgrn, state_gla, c, c_ctx, w_ada, b_ada, norm1_g, norm2_g, w_in, a_qn_g, a_kn_g, c_qn_g, c_kn_g, c_lambda, c_subln_g, b_lb_logits, b_norm_g, d_alpha_w, d_alpha_b, d_norm_g, w_branch, w_out, w_up, conv_w, conv_b, w_down):
    xs = (x_prompt.reshape(N_PROMPT, D_MODEL), x_sample.reshape(N_SAMPLE, D_MODEL))
    w_in_t = jnp.swapaxes(w_in, 1, 2)

    cond8 = jnp.concatenate([c_ctx[None, :], c, jnp.zeros((8 - N_GROUPS, D_MODEL), F32)], axis=0)
    mod = _modulation(cond8, w_ada, b_ada)[:, :N_GROUPS].reshape(DEPTH, N_GROUPS, N_MOD, D_MODEL)

    tile4 = lambda g, n: jnp.tile(g, (1, n)).reshape(DEPTH, 1, -1)
    gains = (tile4(a_qn_g, A_HEADS), tile4(a_kn_g, A_KV_HEADS), tile4(c_qn_g, 2 * C_HEADS), tile4(c_kn_g, 2 * C_HEADS),
             c_subln_g.reshape(DEPTH, 1, 2 * C_HEAD_DIM))
    fmaj = lambda t: jnp.moveaxis(t, 2, -1).reshape(DEC_BATCH, DEPTH, -1, PAST_LEN)
    caches = tuple(fmaj(t) for t in (cache_a_k, cache_a_v, cache_c_k, cache_c_v))
    tables = _rope_tables(DEC_SEQ, A_HEAD_DIM, A_HEADS) + _rope_tables(DEC_SEQ, C_HEAD_DIM, 2 * C_HEADS)

    aw = jnp.zeros((DEPTH, 2, 128, 128), F32)
    aw = aw.at[:, 0, 0:D_GATE_RANK].set(d_alpha_w[:, 0]).at[:, 1, D_GATE_RANK:2 * D_GATE_RANK].set(d_alpha_w[:, 1])
    gla_small = (b_lb_logits, aw.astype(BF), d_alpha_b.reshape(DEPTH, 2, 1, 128),
                 tile4(b_norm_g, B_HEADS), tile4(d_norm_g, D_HEADS)) + _gla_constants() + (
                     _state_pattern(_B_BLOCKS), _state_pattern(_D_BLOCKS))

    n1 = norm1_g.reshape(DEPTH, 1, D_MODEL)
    n2 = norm2_g.reshape(DEPTH, 1, D_MODEL)
    cb = conv_b.reshape(DEPTH, 1, 2 * D_FF)

    carried = None
    for l in range(DEPTH):
        lam_init = 0.8 - 0.6 * math.exp(-0.3 * l)
        zac, zbd, h = _inproj(l, xs, mod, n1, w_in_t)
        y_p, g_p, carried = _ctx_mixers(l, zac, zbd, gains, c_lambda, gla_small, lam_init, carried)
        y_s = _attn_sample(l, zac, gains, c_lambda, caches, tables, lam_init)
        g_s = _gla_sample(l, zbd, gla_small, state_hgrn, state_gla)
        x1 = _mix(l, xs, h, y_p, y_s, g_p, g_s, mod, w_in_t, w_branch, w_out)
        xs = tuple(_ffn(l, x1, mod, n2, w_up, conv_w, cb, w_down, split_out=(l == DEPTH - 1)))

    y_prompt = xs[0].reshape(BATCH, SEQ, D_MODEL)
    y_sample = xs[1].reshape(DEC_BATCH, DEC_SEQ, D_MODEL)
    stacked = list(carried)
    feat_shapes = ((A_KV_HEADS, A_HEAD_DIM), (A_KV_HEADS, A_HEAD_DIM), (C_HEADS, 2, C_HEAD_DIM), (C_HEADS, 2 * C_HEAD_DIM))
    ctx = [jnp.moveaxis(t.reshape((BATCH, DEPTH) + fs + (SEQ,)), -1, 2) for t, fs in zip(stacked[:4], feat_shapes)]
    return (y_prompt, y_sample) + tuple(ctx) + tuple(stacked[4:])
```

```python
import functools
import math

import numpy as np
import jax
import jax.numpy as jnp
from jax import lax
from jax.experimental import pallas as pl
from jax.experimental.pallas import tpu as pltpu

F32 = jnp.float32
BF = jnp.bfloat16

D_MODEL = 1024
BATCH = 16
SEQ = 256
DEPTH = 2
DEC_BATCH = 2
DEC_SEQ = 1024
PAST_LEN = 512
GRID_W = 64
ROPE_THETA = 10000.0
EPS = 1e-6
LB_FLOOR = 1e-30
N_MOD = 6
N_BRANCH = 4
BRANCH_W = 256
A_HEADS, A_KV_HEADS, A_HEAD_DIM = 4, 2, 64
C_HEADS, C_HEAD_DIM = 4, 32
B_HEADS, B_KEY_DIM, B_VAL_DIM = 4, 64, 64
D_HEADS, D_KEY_DIM, D_VAL_DIM = 4, 32, 64
D_GATE_RANK = 16
D_GATE_TAU = 16.0
D_FF = 2816
CONV_WIDTH = 3

N_PROMPT = BATCH * SEQ
N_SAMPLE = DEC_BATCH * DEC_SEQ
N_TOK = N_PROMPT + N_SAMPLE
TM = 256
TQ = 512
TD = 2 * SEQ
N_TILES = N_TOK // TD
PROMPT_TILES = N_PROMPT // TD
SAMPLE_TILES_PER_SEQ = DEC_SEQ // TD
N_GROUPS = 1 + DEC_BATCH

AC_W = 1280
BD_W = 2080
BD_PAD = 2176
SMALL_W = AC_W + BD_PAD
MERGE_OFF = AC_W + BD_W
MERGE_ROWS = 560
HALO = 8
FF_EXT = TD + 2 * HALO
FF_CHUNK = 256
NPRO = 8
W_ROWS = D_MODEL // NPRO
LOG2E = 1.4426950408889634
VMEM_LIMIT = 56 * 1024 * 1024
TT = 128
N_LEVELS = 7
SCORE_ROWS = 16


def _dot(a, b):
    return jnp.dot(a, b, preferred_element_type=F32)


def _dot_nt(a, b):
    return lax.dot_general(a, b, (((1,), (1,)), ((), ())), preferred_element_type=F32)


def _dot_tn(a, b):
    return lax.dot_general(a, b, (((0,), (0,)), ((), ())), preferred_element_type=F32)


def _silu(x):
    return x * jax.nn.sigmoid(x)


def _log_sigmoid(x):
    return jnp.minimum(x, 0.0) - jnp.log1p(jnp.exp(-jnp.abs(x)))


def _rms(x, g):
    return x * lax.rsqrt(jnp.mean(x * x, axis=-1, keepdims=True) + EPS) * g


def _head_rms(x, head_dim, g):
    w = x.shape[-1]
    sh = int(math.log2(head_dim))
    r = lax.shift_right_logical(lax.broadcasted_iota(jnp.int32, (w, w), 0), sh)
    c = lax.shift_right_logical(lax.broadcasted_iota(jnp.int32, (w, w), 1), sh)
    bd = jnp.where(r == c, 1.0 / head_dim, 0.0).astype(BF)
    x2 = x * x
    hi = x2.astype(BF)
    lo = (x2 - hi.astype(F32)).astype(BF)
    ms = _dot(hi, bd) + _dot(lo, bd)
    return x * lax.rsqrt(ms + EPS) * g


def _group_of_tile(i):
    return jnp.where(i < PROMPT_TILES, 0, 1 + jnp.maximum(i - PROMPT_TILES, 0) // SAMPLE_TILES_PER_SEQ)


def _params(sem):
    return pltpu.CompilerParams(dimension_semantics=sem, vmem_limit_bytes=VMEM_LIMIT)


def _mod_kernel(cond_ref, w_ref, b_ref, o_ref):
    s = _silu(cond_ref[...])
    o_ref[...] = _dot(s.astype(BF), w_ref[...].astype(BF)) + b_ref[...]


def _modulation(cond8, w_ada, b_ada):
    nb = 1536
    return pl.pallas_call(
        _mod_kernel,
        grid=(DEPTH, N_MOD * D_MODEL // nb),
        in_specs=[
            pl.BlockSpec((8, D_MODEL), lambda l, j: (0, 0)),
            pl.BlockSpec((None, D_MODEL, nb), lambda l, j: (l, 0, j)),
            pl.BlockSpec((None, 1, nb), lambda l, j: (l, 0, j)),
        ],
        out_specs=pl.BlockSpec((None, 8, nb), lambda l, j: (l, 0, j)),
        out_shape=jax.ShapeDtypeStruct((DEPTH, 8, N_MOD * D_MODEL), F32),
        compiler_params=_params(("arbitrary", "arbitrary")),
        name="modulation",
    )(cond8, w_ada, b_ada.reshape(DEPTH, 1, N_MOD * D_MODEL))


def _tile_of_step(i):
    return jnp.maximum(i - NPRO, 0)


def _chunk_of_step(i):
    return jnp.minimum(i, NPRO - 1)


def _tok_specs(width, split):
    if not split:
        return [pl.BlockSpec((TD, width), lambda i: (_tile_of_step(i), 0))]
    return [pl.BlockSpec((TD, width), lambda i: (jnp.minimum(_tile_of_step(i), PROMPT_TILES - 1), 0)),
            pl.BlockSpec((TD, width), lambda i: (jnp.maximum(_tile_of_step(i) - PROMPT_TILES, 0), 0))]


def _tok_load(t, refs):
    if len(refs) == 1:
        return refs[0][...]
    return jnp.where(t < PROMPT_TILES, refs[0][...], refs[1][...])


def _mod_spec(l):
    return pl.BlockSpec((None, None, N_MOD, D_MODEL), lambda i: (l, _group_of_tile(_tile_of_step(i)), 0, 0))


def _stage_rows(i, n):
    return pl.ds(pl.multiple_of(i * n, n), n)


def _inproj_kernel(*refs, n_x):
    x_refs = refs[:n_x]
    mod_ref, g_ref, w_ref, zac_ref, zbd_ref, h_ref, w_s = refs[n_x:]
    i = pl.program_id(0)

    @pl.when(i < NPRO)
    def _():
        w_s[_stage_rows(i, SMALL_W // NPRO), :] = w_ref[...].astype(BF)

    @pl.when(i >= NPRO)
    def _():
        m = mod_ref[...]
        h = _rms(_tok_load(i - NPRO, x_refs), g_ref[...]) * (1.0 + m[1:2]) + m[0:1]
        hb = h.astype(BF)
        h_ref[...] = hb
        zac_ref[...] = _dot_nt(hb, w_s[:AC_W, :])
        zbd_ref[...] = _dot_nt(hb, w_s[AC_W:, :])


def _inproj(l, xs, mod, norm_g, w_in_t):
    split = len(xs) == 2
    return pl.pallas_call(
        functools.partial(_inproj_kernel, n_x=len(xs)),
        grid=(NPRO + N_TILES,),
        in_specs=_tok_specs(D_MODEL, split) + [
            _mod_spec(l),
            pl.BlockSpec((None, 1, D_MODEL), lambda i: (l, 0, 0)),
            pl.BlockSpec((None, SMALL_W // NPRO, D_MODEL), lambda i: (l, _chunk_of_step(i), 0)),
        ],
        out_specs=[
            pl.BlockSpec((TD, AC_W), lambda i: (_tile_of_step(i), 0)),
            pl.BlockSpec((TD, BD_PAD), lambda i: (_tile_of_step(i), 0)),
            pl.BlockSpec((TD, D_MODEL), lambda i: (_tile_of_step(i), 0)),
        ],
        out_shape=[
            jax.ShapeDtypeStruct((N_TOK, AC_W), F32),
            jax.ShapeDtypeStruct((N_TOK, BD_PAD), F32),
            jax.ShapeDtypeStruct((N_TOK, D_MODEL), BF),
        ],
        scratch_shapes=[pltpu.VMEM((SMALL_W, D_MODEL), BF)],
        compiler_params=_params(("arbitrary",)),
        name=f"inproj{l}",
    )(*xs, mod, norm_g, w_in_t)


def _rope_tables(n_tokens, head_dim, n_rep):
    rows = n_tokens // GRID_W
    row = np.repeat(np.arange(rows), GRID_W).astype(np.float64)
    col = np.tile(np.arange(GRID_W), rows).astype(np.float64)
    half = head_dim // 2
    q4 = head_dim // 4
    freqs = ROPE_THETA ** (-np.arange(0, half, 2, dtype=np.float64) / half)
    ang_r = row[:, None] * freqs
    ang_c = col[:, None] * freqs
    ang = np.concatenate([ang_r, ang_r, ang_c, ang_c], axis=-1)
    cos, sin = np.cos(ang), np.sin(ang)
    first = (np.arange(head_dim) % (2 * q4)) < q4
    s_dn = np.where(first, -sin, 0.0)
    s_up = np.where(first, 0.0, sin)
    return tuple(jnp.asarray(np.tile(t, (1, n_rep)), dtype=F32) for t in (cos, s_dn, s_up))


def _rope(x, cos, s_dn, s_up, q4):
    w = x.shape[-1]
    return x * cos + pltpu.roll(x, w - q4, 1) * s_dn + pltpu.roll(x, q4, 1) * s_up


def _softmax_pv_group(maps, outs):
    scores = [[_dot(q, k) if t else _dot_nt(q, k) for k, t in zip(ks, fm)] for q, ks, _, _, fm in maps]
    yield
    probs = []
    for ss in scores:
        m = ss[0].max(axis=-1, keepdims=True)
        for s in ss[1:]:
            m = jnp.maximum(m, s.max(axis=-1, keepdims=True))
        probs.append([jnp.exp2(s - m).astype(BF) for s in ss])
    yield
    for ps, (_, _, vexts, half, fm) in zip(probs, maps):
        o = None
        for p, v, t in zip(ps, vexts, fm):
            part = _dot_nt(p, v) if t else _dot(p, v)
            o = part if o is None else o + part
        outs.append(o[:, half * 64:(half + 1) * 64] / o[:, (1 - half) * 64:(1 - half) * 64 + 1])
    yield


def _with_ones(v, half, feature_major):
    idx = lax.broadcasted_iota(jnp.int32, v.shape, 0 if feature_major else 1)
    return jnp.where(lax.shift_right_logical(idx, 6) == half, v, jnp.ones_like(v))


def _attend_heads(aq, cq, ka, va, kc, vc, fm, lam, gsub, lam_init, y_ref, group):
    aqb = (aq * (A_HEAD_DIM ** -0.5 * LOG2E)).astype(BF)
    cqb = (cq * (C_HEAD_DIM ** -0.5 * LOG2E)).astype(BF)
    rep = A_HEADS // A_KV_HEADS

    def feat(x, t, sl):
        return x[sl, :] if t else x[:, sl]

    maps = []
    for h in range(A_HEADS):
        g = h // rep
        sl = slice(g * A_HEAD_DIM, (g + 1) * A_HEAD_DIM)
        maps.append((aqb[:, h * A_HEAD_DIM:(h + 1) * A_HEAD_DIM], [feat(k, t, sl) for k, t in zip(ka, fm)],
                     [_with_ones(v, g, t) for v, t in zip(va, fm)], g, fm))
    for h in range(C_HEADS):
        slab = slice((h // 2) * 128, (h // 2 + 1) * 128)
        vh = [_with_ones(feat(v, t, slab), h % 2, t) for v, t in zip(vc, fm)]
        for j in range(2):
            sl = slice((2 * h + j) * C_HEAD_DIM, (2 * h + j + 1) * C_HEAD_DIM)
            maps.append((cqb[:, sl], [feat(k, t, sl) for k, t in zip(kc, fm)], vh, h % 2, fm))
    outs = []
    for i in range(0, len(maps), group):
        yield from _softmax_pv_group(maps[i:i + group], outs)
    for h in range(A_HEADS):
        y_ref[:, h * A_HEAD_DIM:(h + 1) * A_HEAD_DIM] = outs[h]
    vd = 2 * C_HEAD_DIM
    for h in range(C_HEADS):
        d = outs[A_HEADS + 2 * h] - lam * outs[A_HEADS + 2 * h + 1]
        y_ref[:, BRANCH_W + h * vd:BRANCH_W + (h + 1) * vd] = _rms(d, gsub) * (1.0 - lam_init)


def _lambda(cl):
    s1 = jnp.sum(cl[0:1] * cl[1:2], axis=-1, keepdims=True)
    s2 = jnp.sum(cl[2:3] * cl[3:4], axis=-1, keepdims=True)
    return jnp.exp(s1) - jnp.exp(s2)


def _attn_prompt_kernel(z_ref, gaq, gak, gcq, gck, gsub, cl_ref, y_ref, oak, oav, ock, ocv, *, lam_init):
    z = z_ref[...]
    ak = _head_rms(z[:, 256:384], A_HEAD_DIM, gak[...])
    av = z[:, 384:512]
    ck = _head_rms(z[:, 768:1024], C_HEAD_DIM, gck[...])
    cv = z[:, 1024:1280]
    oak[...] = ak.T
    oav[...] = av.T
    ock[...] = ck.T
    ocv[...] = cv.T
    aq = _head_rms(z[:, 0:256], A_HEAD_DIM, gaq[...])
    cq = _head_rms(z[:, 512:768], C_HEAD_DIM, gcq[...])
    lam = _lambda(cl_ref[...]) + lam_init
    yield
    yield from _attend_heads(aq, cq, [ak.astype(BF)], [av.astype(BF)], [ck.astype(BF)], [cv.astype(BF)], [False],
                             lam, gsub[...], lam_init, y_ref, group=A_HEADS + 2 * C_HEADS)


def _attn_sample_kernel(z_ref, gaq, gak, gcq, gck, gsub, cl_ref, cak, cav, cck, ccv,
                        cosa, sda, sua, cosc, sdc, suc, y_ref, ka_s, va_s, kc_s, vc_s, *, lam_init):
    qi = pl.program_id(1)
    qa4, qc4 = A_HEAD_DIM // 4, C_HEAD_DIM // 4

    @pl.when(qi == 0)
    def _():
        ak = _head_rms(z_ref[:, 256:384], A_HEAD_DIM, gak[...])
        ka_s[...] = _rope(ak, cosa[:, :128], sda[:, :128], sua[:, :128], qa4).astype(BF)
        va_s[...] = z_ref[:, 384:512].astype(BF)
        ck = _head_rms(z_ref[:, 768:1024], C_HEAD_DIM, gck[...])
        kc_s[...] = _rope(ck, cosc[...], sdc[...], suc[...], qc4).astype(BF)
        vc_s[...] = z_ref[:, 1024:1280].astype(BF)

    rows = pl.ds(pl.multiple_of(qi * TQ, TQ), TQ)
    aq = _head_rms(z_ref[rows, 0:256], A_HEAD_DIM, gaq[...])
    aq = _rope(aq, cosa[rows, :], sda[rows, :], sua[rows, :], qa4)
    cq = _head_rms(z_ref[rows, 512:768], C_HEAD_DIM, gcq[...])
    cq = _rope(cq, cosc[rows, :], sdc[rows, :], suc[rows, :], qc4)
    lam = _lambda(cl_ref[...]) + lam_init
    yield from _attend_heads(aq, cq,
                             [cak[...].astype(BF), ka_s[...]], [cav[...].astype(BF), va_s[...]],
                             [cck[...].astype(BF), kc_s[...]], [ccv[...].astype(BF), vc_s[...]], [True, False],
                             lam, gsub[...], lam_init, y_ref, group=4)


def _gain_specs(l, nd):
    zeros = (0,) * (nd - 1)
    widths = (256, 128, 256, 256, 64)
    return [pl.BlockSpec((None, 1, w), lambda *a: (l, 0, 0)) for w in widths] + \
           [pl.BlockSpec((None, 4, C_HEAD_DIM), lambda *a: (l, 0, 0))]


N_VTILES = BRANCH_W // 128


def _state_blocks(nh, kd, vd):
    per = 128 // vd
    return [(h // per, slice((h * kd) % 128, (h * kd) % 128 + kd), slice((h % per) * vd, (h % per + 1) * vd))
            for h in range(nh)]


_B_BLOCKS = _state_blocks(B_HEADS, B_KEY_DIM, B_VAL_DIM)
_D_BLOCKS = _state_blocks(D_HEADS, D_KEY_DIM, D_VAL_DIM)


def _gla_constants():
    idx = np.arange(TT)
    scans, masks = [], []
    for rev in (False, True):
        eff = (TT - 1 - idx) if rev else idx
        et, eu = eff[:, None], eff[None, :]
        sc, mk = [], []
        for j in range(N_LEVELS):
            b = 1 << j
            start = et - et % b
            odd = (et // b) % 2 == 1
            sc.append(np.where(odd, (eu > start) & (eu <= et), (eu > et) & (eu <= start + b)))
            mk.append(((et // b) % 2 == 1) & (eu // b == et // b - 1))
        sc.append(eu <= et)
        sc.append(eu > et)
        mk.append(eu == et)
        scans.append(np.concatenate([np.concatenate(sc, axis=0)] * 2, axis=-1))
        masks.append(np.stack(mk))
    return tuple(jnp.asarray(np.stack(t), BF) for t in (scans, masks))


def _gla_prepare(q, k, v, la2, scan_ref, d, rev, use_state):
    tt = q.shape[0]
    la_hi = la2.astype(BF)
    la_lo = (la2 - la_hi.astype(F32)).astype(BF)
    la_split = jnp.concatenate([la_hi, la_lo], axis=0)

    sums = _dot(scan_ref[d], la_split)

    def factor(i):
        return jnp.exp2(sums[i * tt:(i + 1) * tt])

    qs, ks = [], []
    for j in range(N_LEVELS):
        f = factor(j)
        qs.append((q * f).astype(BF))
        ks.append((k * f).astype(BF).T)
    qs.append(q.astype(BF))
    ks.append(k.astype(BF).T)
    k_out = (k * factor(N_LEVELS + 1)).astype(BF)
    vb = v.astype(BF)
    q_in = d_tile = None
    if use_state:
        q_in = (q * factor(N_LEVELS)).astype(BF)
        ones = jnp.ones((2 * tt, 128), BF)
        d_tile = [jnp.exp2(_dot_tn(la_split[:, i * 128:(i + 1) * 128], ones)) for i in range(q.shape[1] // 128)]

    return dict(qs=qs, ks=ks, k_out=k_out, vb=vb, q_in=q_in, d_tile=d_tile)


def _lane_keep(x, lo, hi):
    lane = lax.broadcasted_iota(jnp.int32, x.shape, 1)
    return jnp.where((lane >= lo) & (lane < hi), x, jnp.zeros_like(x))


def _level_rows(j, rev):
    b = 1 << j
    if j == N_LEVELS or b < SCORE_ROWS:
        return None
    return [(i * b, (i + 1) * b) for i in range(TT // b) if (i % 2 == 1) != rev]


def _gla_scores(p, mask_ref, d, rev, nh, kd):
    assert nh % 2 == 0 and 128 % (2 * kd) == 0
    out = []
    for h0 in range(0, nh, 2):
        c = (h0 * kd) // 128
        blocks = [[None] * (TT // SCORE_ROWS) for _ in range(2)]

        def add(r0, val):
            for i in range(val[0].shape[0] // SCORE_ROWS):
                k = r0 // SCORE_ROWS + i
                for n in range(2):
                    piece = val[n][i * SCORE_ROWS:(i + 1) * SCORE_ROWS]
                    blocks[n][k] = piece if blocks[n][k] is None else blocks[n][k] + piece

        for j in range(N_LEVELS + 1):
            kt = p["ks"][j]
            khs = []
            for h in (h0, h0 + 1):
                lo = (h * kd) % 128
                pieces = [jnp.zeros((lo, kt.shape[1]), BF)] if lo else []
                pieces.append(kt[c * 128 + lo:c * 128 + lo + kd, :])
                if lo + kd < 128:
                    pieces.append(jnp.zeros((128 - lo - kd, kt.shape[1]), BF))
                khs.append(jnp.concatenate(pieces, axis=0))
            kh = jnp.concatenate(khs, axis=1)
            qj = p["qs"][j][:, c * 128:(c + 1) * 128]
            ranges = _level_rows(j, rev)
            if ranges is None:
                t = _dot(qj, kh).astype(BF)
                m = mask_ref[d, j]
                add(0, [t[:, :TT] * m, t[:, TT:] * m])
            else:
                q_rows = jnp.concatenate([qj[a:b] for a, b in ranges], axis=0) if len(ranges) > 1 else qj[ranges[0][0]:ranges[0][1]]
                m_rows = [mask_ref[d, j, a:b, :] for a, b in ranges]
                t = _dot(q_rows, kh).astype(BF)
                off = 0
                for (a, b), m in zip(ranges, m_rows):
                    add(a, [t[off:off + b - a, :TT] * m, t[off:off + b - a, TT:] * m])
                    off += b - a
        zero = jnp.zeros((SCORE_ROWS, TT), BF)
        for n in range(2):
            out.append(jnp.concatenate([z if z is not None else zero for z in blocks[n]], axis=0))
    return out


def _gla_outputs(p, scs, blk_ref, st_ref, d, nh, kd, vd, use_state, o_ref, rows, col0, accumulate):
    per = 128 // vd
    for c in range(nh // per):
        vt = p["vb"][:, c * 128:(c + 1) * 128]
        kt = (c * per * kd) // 128
        ktile = slice(kt * 128, (kt + 1) * 128)
        o = None
        for i in range(per):
            t = _dot(scs[c * per + i], _lane_keep(vt, i * vd, (i + 1) * vd))
            o = t if o is None else o + t
        kv = _dot_tn(p["k_out"][:, ktile], vt) * blk_ref[c]
        if use_state:
            st = st_ref[d, c]
            o = o + _dot(p["q_in"][:, ktile], st.astype(BF))
            st_ref[d, c] = st * p["d_tile"][kt] + kv
        else:
            st_ref[d, c] = kv
        osl = slice(col0 + c * 128, col0 + (c + 1) * 128)
        if accumulate:
            o_ref[rows, osl] += o
        else:
            o_ref[rows, osl] = o


def _gla_kernel(*refs, layer, n_tiles, has_state, pump=None):
    pump = pump or (lambda: None)
    if has_state:
        (z_ref, lbl_ref, aw_ref, ab_ref, bng, dng, scan_ref, mask_ref, blkb_ref, blkd_ref, sh_in, sd_in,
         y_ref, o_scr, sth, std) = refs
        for packed, raw, blocks in ((sth, sh_in, _B_BLOCKS), (std, sd_in, _D_BLOCKS)):
            packed[...] = jnp.zeros(packed.shape, F32)
            for dd in range(2):
                for h, (c, r, ln) in enumerate(blocks):
                    packed[dd, c, r, ln] = raw[dd, h]
    else:
        (z_ref, lbl_ref, aw_ref, ab_ref, bng, dng, scan_ref, mask_ref, blkb_ref, blkd_ref,
         y_ref, sh_out, sd_out, o_scr, sth, std) = refs

    gates = []
    for d in range(2):
        logits = [lbl_ref[d, i:i + 1, :] for i in range(DEPTH)]
        mx = functools.reduce(jnp.maximum, logits)
        ex = [jnp.exp(t - mx) for t in logits]
        den = functools.reduce(lambda a, b: a + b, ex)
        ps = [t / den for t in ex]
        lb = functools.reduce(lambda a, b: a + b, ps[:layer + 1]) - ps[0]
        gates.append((lb, jnp.log(jnp.maximum(lb, LB_FLOOR)), jnp.log1p(-lb)))

    o_scr[...] = jnp.zeros(o_scr.shape, F32)

    def tile(i, use_state):
        preps = []
        for d in range(2):
            rev = d == 1
            lb, log_lb, log_1m = gates[d]
            j = (n_tiles - 1 - i) if rev else i
            rows = pl.ds(j * TT if isinstance(j, int) else pl.multiple_of(j * TT, TT), TT)
            bq = z_ref[rows, 0:256]
            zf = z_ref[rows, 768:1024] if rev else z_ref[rows, 512:768]
            b2 = log_1m + _log_sigmoid(zf)
            la = jnp.maximum(log_lb, b2) + jnp.log1p(jnp.exp(-jnp.abs(log_lb - b2)))
            kb = (1.0 - lb) * jax.nn.sigmoid(-zf)
            pre = _dot(z_ref[rows, 2048:2176].astype(BF), aw_ref[d]) + ab_ref[d]
            la_d = _log_sigmoid(pre) * (LOG2E / D_GATE_TAU)
            pb = _gla_prepare(_silu(bq), kb, z_ref[rows, 256:512], la * LOG2E, scan_ref, d, rev, use_state)
            pd = _gla_prepare(z_ref[rows, 1280:1408] * (D_KEY_DIM ** -0.5), z_ref[rows, 1408:1536],
                              z_ref[rows, 1536:1792], la_d, scan_ref, d, rev, use_state)
            preps.append((d, rows, pb, pd))
        pump()
        scores = [(_gla_scores(pb, mask_ref, d, d == 1, B_HEADS, B_KEY_DIM),
                   _gla_scores(pd, mask_ref, d, d == 1, D_HEADS, D_KEY_DIM)) for d, _, pb, pd in preps]
        pump()
        for (d, rows, pb, pd), (sb, sd) in zip(preps, scores):
            _gla_outputs(pb, sb, blkb_ref, sth, d, B_HEADS, B_KEY_DIM, B_VAL_DIM, use_state, o_scr, rows, 0, True)
            _gla_outputs(pd, sd, blkd_ref, std, d, D_HEADS, D_KEY_DIM, D_VAL_DIM, use_state, o_scr, rows, 256, True)
        pump()

    if has_state:
        def body(i, carry):
            tile(i, True)
            return carry

        lax.fori_loop(0, n_tiles, body, 0)
    else:
        for i in range(n_tiles):
            tile(i, i > 0)

    def finish(i, carry):
        rows = pl.ds(pl.multiple_of(i * TM, TM), TM)
        y_ref[rows, 0:256] = _head_rms(o_scr[rows, 0:256], B_VAL_DIM, bng[...]) * _silu(z_ref[rows, 1024:1280])
        y_ref[rows, 256:512] = _head_rms(o_scr[rows, 256:512], D_VAL_DIM, dng[...]) * _silu(z_ref[rows, 1792:2048])
        return carry

    lax.fori_loop(0, (n_tiles * TT) // TM, finish, 0)
    if not has_state:
        for packed, raw, blocks in ((sth, sh_out, _B_BLOCKS), (std, sd_out, _D_BLOCKS)):
            for dd in range(2):
                for h, (c, r, ln) in enumerate(blocks):
                    raw[dd, h] = packed[dd, c, r, ln]


def _gla_common_specs(l, nd):
    return [
        pl.BlockSpec((2, DEPTH, 256), lambda *a: (0, 0, 0)),
        pl.BlockSpec((None, 2, 128, 128), lambda *a: (l, 0, 0, 0)),
        pl.BlockSpec((None, 2, 1, 128), lambda *a: (l, 0, 0, 0)),
        pl.BlockSpec((None, 1, 256), lambda *a: (l, 0, 0)),
        pl.BlockSpec((None, 1, 256), lambda *a: (l, 0, 0)),
        pl.BlockSpec((2, (N_LEVELS + 2) * TT, 2 * TT), lambda *a: (0, 0, 0)),
        pl.BlockSpec((2, N_LEVELS + 1, TT, TT), lambda *a: (0, 0, 0, 0)),
        pl.BlockSpec((N_VTILES, 128, 128), lambda *a: (0, 0, 0)),
        pl.BlockSpec((N_VTILES, 128, 128), lambda *a: (0, 0, 0)),
    ]


_STATE_SCRATCH = [pltpu.VMEM((2, N_VTILES, 128, 128), F32), pltpu.VMEM((2, N_VTILES, 128, 128), F32)]


def _state_pattern(blocks):
    pat = np.zeros((N_VTILES, 128, 128), np.float32)
    for c, r, ln in blocks:
        pat[c, r, ln] = 1.0
    return jnp.asarray(pat)


N_ATTN_IN = 7
N_ATTN_OUT = 5


def _ctx_mixers_kernel(*refs, layer, lam_init, n_carried):
    n_in = N_ATTN_IN + 1 + len(_gla_common_specs(0, 1))
    a_in, g_in, outs = refs[:N_ATTN_IN], refs[N_ATTN_IN:n_in], refs[n_in + n_carried:]
    y_ac, caches, y_bd, states = outs[0], outs[1:N_ATTN_OUT], outs[N_ATTN_OUT], outs[N_ATTN_OUT + 1:]
    attn = _attn_prompt_kernel(*a_in, y_ac, *caches, lam_init=lam_init)
    _gla_kernel(*g_in, y_bd, *states, layer=layer, n_tiles=SEQ // TT, has_state=False,
                pump=lambda: next(attn, None))
    for _ in attn:
        pass


def _ctx_mixers(l, zac, zbd, gains, c_lambda, small, lam_init, carried):
    stacked = [(BATCH, DEPTH, w, SEQ) for w in (128, 128, 256, 256)] + [
        (BATCH, DEPTH, 2, B_HEADS, B_KEY_DIM, B_VAL_DIM), (BATCH, DEPTH, 2, D_HEADS, D_KEY_DIM, D_VAL_DIM)]
    seq = lambda w: pl.BlockSpec((SEQ, w), lambda b: (b, 0))
    layer_block = lambda s: pl.BlockSpec((None, None) + s[2:], lambda b: (b, l) + (0,) * (len(s) - 2))
    in_specs = [seq(AC_W)] + _gain_specs(l, 1) + [seq(BD_PAD)] + _gla_common_specs(l, 1)
    n_carried = 0 if carried is None else len(carried)
    aliases = {}
    if carried is not None:
        out_idx = [1, 2, 3, 4, 6, 7]
        aliases = {len(in_specs) + k: out_idx[k] for k in range(n_carried)}
        in_specs = in_specs + [pl.BlockSpec(memory_space=pl.ANY)] * n_carried
    out = pl.pallas_call(
        functools.partial(_ctx_mixers_kernel, layer=l, lam_init=lam_init, n_carried=n_carried),
        grid=(BATCH,),
        in_specs=in_specs,
        out_specs=[seq(2 * BRANCH_W)] + [layer_block(s) for s in stacked[:4]] + [seq(2 * BRANCH_W)]
                  + [layer_block(s) for s in stacked[4:]],
        out_shape=[jax.ShapeDtypeStruct((N_PROMPT, 2 * BRANCH_W), F32)]
                  + [jax.ShapeDtypeStruct(s, F32) for s in stacked[:4]]
                  + [jax.ShapeDtypeStruct((N_PROMPT, 2 * BRANCH_W), F32)]
                  + [jax.ShapeDtypeStruct(s, F32) for s in stacked[4:]],
        scratch_shapes=[pltpu.VMEM((SEQ, 2 * BRANCH_W), F32)] + _STATE_SCRATCH,
        input_output_aliases=aliases,
        compiler_params=_params(("arbitrary",)),
        name=f"ctx_mixers{l}",
    )(zac, *gains, c_lambda, zbd, *small, *(carried or ()))
    return out[0], out[5], out[1:5] + out[6:]


def _attn_sample_body(*refs, lam_init):
    for _ in _attn_sample_kernel(*refs, lam_init=lam_init):
        pass


def _attn_sample(l, zac, gains, c_lambda, caches, tables, lam_init):
    first_blk = N_PROMPT // DEC_SEQ
    cache_specs = [pl.BlockSpec((None, None, w, PAST_LEN), lambda b, q: (b, l, 0, 0)) for w in (128, 128, 256, 256)]
    table_specs = [pl.BlockSpec((DEC_SEQ, 256), lambda b, q: (0, 0)) for _ in range(6)]
    return pl.pallas_call(
        functools.partial(_attn_sample_body, lam_init=lam_init),
        grid=(DEC_BATCH, DEC_SEQ // TQ),
        in_specs=[pl.BlockSpec((DEC_SEQ, AC_W), lambda b, q: (first_blk + b, 0))] + _gain_specs(l, 2)
                 + cache_specs + table_specs,
        out_specs=pl.BlockSpec((TQ, 2 * BRANCH_W), lambda b, q: (b * (DEC_SEQ // TQ) + q, 0)),
        out_shape=jax.ShapeDtypeStruct((N_SAMPLE, 2 * BRANCH_W), F32),
        scratch_shapes=[pltpu.VMEM((DEC_SEQ, 128), BF), pltpu.VMEM((DEC_SEQ, 128), BF),
                        pltpu.VMEM((DEC_SEQ, 256), BF), pltpu.VMEM((DEC_SEQ, 256), BF)],
        compiler_params=_params(("arbitrary", "arbitrary")),
        name=f"attn_sample{l}",
    )(zac, *gains, c_lambda, *caches, *tables)


def _gla_sample(l, zbd, small, st_h, st_d):
    first_blk = N_PROMPT // DEC_SEQ
    return pl.pallas_call(
        functools.partial(_gla_kernel, layer=l, n_tiles=DEC_SEQ // TT, has_state=True),
        grid=(DEC_BATCH,),
        in_specs=[pl.BlockSpec((DEC_SEQ, BD_PAD), lambda b: (first_blk + b, 0))] + _gla_common_specs(l, 1) + [
            pl.BlockSpec((None, None, 2, B_HEADS, B_KEY_DIM, B_VAL_DIM), lambda b: (b, l, 0, 0, 0, 0)),
            pl.BlockSpec((None, None, 2, D_HEADS, D_KEY_DIM, D_VAL_DIM), lambda b: (b, l, 0, 0, 0, 0)),
        ],
        out_specs=pl.BlockSpec((DEC_SEQ, 2 * BRANCH_W), lambda b: (b, 0)),
        out_shape=jax.ShapeDtypeStruct((N_SAMPLE, 2 * BRANCH_W), F32),
        scratch_shapes=[pltpu.VMEM((DEC_SEQ, 2 * BRANCH_W), F32)] + _STATE_SCRATCH,
        compiler_params=_params(("arbitrary",)),
        name=f"gla_sample{l}",
    )(zbd, *small, st_h, st_d)


def _mix_kernel(*refs, n_x):
    x_refs = refs[:n_x]
    (h_ref, yp_ref, ys_ref, gp_ref, gs_ref, mod_ref, win_ref, wb_ref, wo_ref, o_ref, wm_s, wb_s, wo_s) = refs[n_x:]
    i = pl.program_id(0)

    @pl.when(i < NPRO)
    def _():
        rows = _stage_rows(i, W_ROWS)
        wm_s[_stage_rows(i, MERGE_ROWS), :] = win_ref[...].astype(BF)
        wb_s[rows, :] = wb_ref[...].astype(BF)
        wo_s[rows, :] = wo_ref[...].astype(BF)

    @pl.when(i >= NPRO)
    def _():
        t = i - NPRO
        hb = h_ref[...]
        yac = _tok_load(t, (yp_ref, ys_ref))
        ybd = _tok_load(t, (gp_ref, gs_ref))
        branches = (yac[:, :BRANCH_W], ybd[:, :BRANCH_W], yac[:, BRANCH_W:], ybd[:, BRANCH_W:])
        mixed = None
        for n, y in enumerate(branches):
            logits = _dot_nt(hb, wm_s[n * D_MODEL:(n + 1) * D_MODEL, :])
            term = jax.nn.sigmoid(logits) * _dot(y.astype(BF), wb_s[n * BRANCH_W:(n + 1) * BRANCH_W, :])
            mixed = term if mixed is None else mixed + term
        o_ref[...] = _tok_load(t, x_refs) + mod_ref[2:3, :] * _dot(mixed.astype(BF), wo_s[...])


def _mix(l, xs, h, y_p, y_s, g_p, g_s, mod, w_in_t, w_branch, w_out):
    chunk = pl.BlockSpec((None, W_ROWS, D_MODEL), lambda i: (l, _chunk_of_step(i), 0))
    return pl.pallas_call(
        functools.partial(_mix_kernel, n_x=len(xs)),
        grid=(NPRO + N_TILES,),
        in_specs=_tok_specs(D_MODEL, len(xs) == 2) + _tok_specs(D_MODEL, False)
                 + _tok_specs(2 * BRANCH_W, True) + _tok_specs(2 * BRANCH_W, True)
                 + [_mod_spec(l),
                    pl.BlockSpec((None, MERGE_ROWS, D_MODEL), lambda i: (l, MERGE_OFF // MERGE_ROWS + _chunk_of_step(i), 0)),
                    chunk, chunk],
        out_specs=pl.BlockSpec((TD, D_MODEL), lambda i: (_tile_of_step(i), 0)),
        out_shape=jax.ShapeDtypeStruct((N_TOK, D_MODEL), F32),
        scratch_shapes=[pltpu.VMEM((NPRO * MERGE_ROWS, D_MODEL), BF), pltpu.VMEM((D_MODEL, D_MODEL), BF),
                        pltpu.VMEM((D_MODEL, D_MODEL), BF)],
        compiler_params=_params(("arbitrary",)),
        name=f"mix{l}",
    )(*xs, h, y_p, y_s, g_p, g_s, mod, w_in_t, w_branch.reshape(DEPTH, N_BRANCH * BRANCH_W, D_MODEL), w_out)


def _ffn_kernel(*refs, n_out):
    (xp_ref, x_ref, xn_ref, mod_ref, g_ref, wup_ref, cw_ref, cb_ref, wdn_ref) = refs[:9]
    o_refs = refs[9:9 + n_out]
    hext, wup_s, wdn_s = refs[9 + n_out:12 + n_out]
    i = pl.program_id(0)

    @pl.when(i < NPRO)
    def _():
        wup_s[_stage_rows(i, W_ROWS), :] = wup_ref[...].astype(BF)
        wdn_s[_stage_rows(i, D_FF // NPRO), :] = wdn_ref[...].astype(BF)

    if n_out == 2:
        keep_s = refs[12 + n_out]

        @pl.when(i == 0)
        def _():
            keep_s[...] = jnp.zeros(keep_s.shape, F32)

    @pl.when(i >= NPRO)
    def _():
        t = i - NPRO
        ctx = t < PROMPT_TILES
        pos = jnp.maximum(t - PROMPT_TILES, 0) % SAMPLE_TILES_PER_SEQ
        seq_first = ctx | (pos == 0)
        seq_last = ctx | (pos == SAMPLE_TILES_PER_SEQ - 1)
        m = mod_ref[...]
        g = g_ref[...]

        def pre(x):
            return _rms(x, g) * (1.0 + m[4:5]) + m[3:4]

        sub = lax.broadcasted_iota(jnp.int32, (HALO, D_MODEL), 0)
        hext[0:TD, :] = pre(x_ref[...])
        hext[TD:TD + HALO, :] = jnp.where(seq_last | (sub != 0), 0.0, pre(xn_ref[...]))
        hext[TD + HALO:, :] = jnp.where(seq_first | (sub != HALO - 1), 0.0, pre(xp_ref[...]))
        hb = hext[...].astype(BF)

        n_tile, mid = TD // 8, SEQ // 8
        sub8 = lax.broadcasted_iota(jnp.int32, (1, 8, FF_CHUNK), 1)

        def up_conv_act(c):
            halves = []
            for off in (0, D_FF):
                cols = slice(off + c * FF_CHUNK, off + (c + 1) * FF_CHUNK)
                u = _dot(hb, wup_s[:, cols]).reshape(FF_EXT // 8, 8, FF_CHUNK)
                down = pltpu.roll(u, 1, axis=1)
                up = pltpu.roll(u, 7, axis=1)
                before = jnp.concatenate([down[-1:], down[:mid - 1], jnp.where(ctx, 0.0, down[mid - 1:mid]),
                                          down[mid:n_tile - 1]], axis=0)
                after = jnp.concatenate([up[1:mid], jnp.where(ctx, 0.0, up[mid:mid + 1]), up[mid + 1:n_tile + 1]],
                                        axis=0)
                prev = jnp.where(sub8 == 0, before, down[:n_tile])
                nxt = jnp.where(sub8 == 7, after, up[:n_tile])
                conv = (prev * cw_ref[0:1, cols] + u[:n_tile] * cw_ref[1:2, cols] + nxt * cw_ref[2:3, cols]
                        + cb_ref[:, cols])
                halves.append(conv.reshape(TD, FF_CHUNK))
            return (_silu(halves[1]) * halves[0]).astype(BF)

        n_chunks = D_FF // FF_CHUNK
        acc = _dot(jnp.concatenate([up_conv_act(c) for c in range(n_chunks)], axis=1), wdn_s[...])
        gate = m[5:6]
        res = x_ref[...] + gate * acc
        if n_out == 1:
            o_refs[0][...] = res
        else:
            kept = jnp.where(ctx, res, keep_s[...])
            keep_s[...] = kept
            o_refs[0][...] = kept
            o_refs[1][...] = res


def _ffn(l, x, mod, norm_g, w_up, conv_w, conv_b, w_down, split_out):
    per = TD // HALO
    last_blk = N_TOK // HALO - 1
    if split_out:
        out_specs = [pl.BlockSpec((TD, D_MODEL), lambda i: (jnp.minimum(_tile_of_step(i), PROMPT_TILES - 1), 0)),
                     pl.BlockSpec((TD, D_MODEL), lambda i: (jnp.maximum(_tile_of_step(i) - PROMPT_TILES, 0), 0))]
        out_shape = [jax.ShapeDtypeStruct((N_PROMPT, D_MODEL), F32), jax.ShapeDtypeStruct((N_SAMPLE, D_MODEL), F32)]
    else:
        out_specs = [pl.BlockSpec((TD, D_MODEL), lambda i: (_tile_of_step(i), 0))]
        out_shape = [jax.ShapeDtypeStruct((N_TOK, D_MODEL), F32)]
    return pl.pallas_call(
        functools.partial(_ffn_kernel, n_out=len(out_specs)),
        grid=(NPRO + N_TILES,),
        in_specs=[
            pl.BlockSpec((HALO, D_MODEL), lambda i: (jnp.maximum(_tile_of_step(i) * per - 1, 0), 0)),
            pl.BlockSpec((TD, D_MODEL), lambda i: (_tile_of_step(i), 0)),
            pl.BlockSpec((HALO, D_MODEL), lambda i: (jnp.minimum((_tile_of_step(i) + 1) * per, last_blk), 0)),
            _mod_spec(l),
            pl.BlockSpec((None, 1, D_MODEL), lambda i: (l, 0, 0)),
            pl.BlockSpec((None, W_ROWS, 2 * D_FF), lambda i: (l, _chunk_of_step(i), 0)),
            pl.BlockSpec((None, CONV_WIDTH, 2 * D_FF), lambda i: (l, 0, 0)),
            pl.BlockSpec((None, 1, 2 * D_FF), lambda i: (l, 0, 0)),
            pl.BlockSpec((None, D_FF // NPRO, D_MODEL), lambda i: (l, _chunk_of_step(i), 0)),
        ],
        out_specs=out_specs,
        out_shape=out_shape,
        scratch_shapes=[pltpu.VMEM((FF_EXT, D_MODEL), F32),
                        pltpu.VMEM((D_MODEL, 2 * D_FF), BF), pltpu.VMEM((D_FF, D_MODEL), BF)]
                       + ([pltpu.VMEM((TD, D_MODEL), F32)] if split_out else []),
        compiler_params=_params(("arbitrary",)),
        name=f"ffn{l}",
    )(x, x, x, mod, norm_g, w_up, conv_w, conv_b, w_down)


def kernel(x_prompt, x_sample, cache_a_k, cache_a_v, cache_c_k, cache_c_v, state_hgrn, state_gla, c, c_ctx, w_ada, b_ada, norm1_g, norm2_g, w_in, a_qn_g, a_kn_g, c_qn_g, c_kn_g, c_lambda, c_subln_g, b_lb_logits, b_norm_g, d_alpha_w, d_alpha_b, d_norm_g, w_branch, w_out, w_up, conv_w, conv_b, w_down):
    xs = (x_prompt.reshape(N_PROMPT, D_MODEL), x_sample.reshape(N_SAMPLE, D_MODEL))
    w_in_t = jnp.swapaxes(w_in, 1, 2)

    cond8 = jnp.concatenate([c_ctx[None, :], c, jnp.zeros((8 - N_GROUPS, D_MODEL), F32)], axis=0)
    mod = _modulation(cond8, w_ada, b_ada)[:, :N_GROUPS].reshape(DEPTH, N_GROUPS, N_MOD, D_MODEL)

    tile4 = lambda g, n: jnp.tile(g, (1, n)).reshape(DEPTH, 1, -1)
    gains = (tile4(a_qn_g, A_HEADS), tile4(a_kn_g, A_KV_HEADS), tile4(c_qn_g, 2 * C_HEADS), tile4(c_kn_g, 2 * C_HEADS),
             c_subln_g.reshape(DEPTH, 1, 2 * C_HEAD_DIM))
    fmaj = lambda t: jnp.moveaxis(t, 2, -1).reshape(DEC_BATCH, DEPTH, -1, PAST_LEN)
    caches = tuple(fmaj(t) for t in (cache_a_k, cache_a_v, cache_c_k, cache_c_v))
    tables = _rope_tables(DEC_SEQ, A_HEAD_DIM, A_HEADS) + _rope_tables(DEC_SEQ, C_HEAD_DIM, 2 * C_HEADS)

    aw = jnp.zeros((DEPTH, 2, 128, 128), F32)
    aw = aw.at[:, 0, 0:D_GATE_RANK].set(d_alpha_w[:, 0]).at[:, 1, D_GATE_RANK:2 * D_GATE_RANK].set(d_alpha_w[:, 1])
    gla_small = (b_lb_logits, aw.astype(BF), d_alpha_b.reshape(DEPTH, 2, 1, 128),
                 tile4(b_norm_g, B_HEADS), tile4(d_norm_g, D_HEADS)) + _gla_constants() + (
                     _state_pattern(_B_BLOCKS), _state_pattern(_D_BLOCKS))

    n1 = norm1_g.reshape(DEPTH, 1, D_MODEL)
    n2 = norm2_g.reshape(DEPTH, 1, D_MODEL)
    cb = conv_b.reshape(DEPTH, 1, 2 * D_FF)

    carried = None
    for l in range(DEPTH):
        lam_init = 0.8 - 0.6 * math.exp(-0.3 * l)
        zac, zbd, h = _inproj(l, xs, mod, n1, w_in_t)
        y_p, g_p, carried = _ctx_mixers(l, zac, zbd, gains, c_lambda, gla_small, lam_init, carried)
        y_s = _attn_sample(l, zac, gains, c_lambda, caches, tables, lam_init)
        g_s = _gla_sample(l, zbd, gla_small, state_hgrn, state_gla)
        x1 = _mix(l, xs, h, y_p, y_s, g_p, g_s, mod, w_in_t, w_branch, w_out)
        xs = tuple(_ffn(l, x1, mod, n2, w_up, conv_w, cb, w_down, split_out=(l == DEPTH - 1)))

    y_prompt = xs[0].reshape(BATCH, SEQ, D_MODEL)
    y_sample = xs[1].reshape(DEC_BATCH, DEC_SEQ, D_MODEL)
    stacked = list(carried)
    feat_shapes = ((A_KV_HEADS, A_HEAD_DIM), (A_KV_HEADS, A_HEAD_DIM), (C_HEADS, 2, C_HEAD_DIM), (C_HEADS, 2 * C_HEAD_DIM))
    ctx = [jnp.moveaxis(t.reshape((BATCH, DEPTH) + fs + (SEQ,)), -1, 2) for t, fs in zip(stacked[:4], feat_shapes)]
    return (y_prompt, y_sample) + tuple(ctx) + tuple(stacked[4:])
```

```python
import functools
import math

import numpy as np
import jax
import jax.numpy as jnp
from jax import lax
from jax.experimental import pallas as pl
from jax.experimental.pallas import tpu as pltpu

F32 = jnp.float32
BF = jnp.bfloat16

D_MODEL = 1024
BATCH = 16
SEQ = 256
DEPTH = 2
DEC_BATCH = 2
DEC_SEQ = 1024
PAST_LEN = 512
GRID_W = 64
ROPE_THETA = 10000.0
EPS = 1e-6
LB_FLOOR = 1e-30
N_MOD = 6
N_BRANCH = 4
BRANCH_W = 256
A_HEADS, A_KV_HEADS, A_HEAD_DIM = 4, 2, 64
C_HEADS, C_HEAD_DIM = 4, 32
B_HEADS, B_KEY_DIM, B_VAL_DIM = 4, 64, 64
D_HEADS, D_KEY_DIM, D_VAL_DIM = 4, 32, 64
D_GATE_RANK = 16
D_GATE_TAU = 16.0
D_FF = 2816
CONV_WIDTH = 3

N_PROMPT = BATCH * SEQ
N_SAMPLE = DEC_BATCH * DEC_SEQ
N_TOK = N_PROMPT + N_SAMPLE
TM = 256
TQ = 512
TD = 2 * SEQ
N_TILES = N_TOK // TD
PROMPT_TILES = N_PROMPT // TD
SAMPLE_TILES_PER_SEQ = DEC_SEQ // TD
N_GROUPS = 1 + DEC_BATCH

AC_W = 1280
BD_W = 2080
BD_PAD = 2176
SMALL_W = AC_W + BD_PAD
MERGE_OFF = AC_W + BD_W
MERGE_ROWS = 560
HALO = 8
FF_EXT = TD + 2 * HALO
FF_CHUNK = 256
NPRO = 8
W_ROWS = D_MODEL // NPRO
LOG2E = 1.4426950408889634
VMEM_LIMIT = 56 * 1024 * 1024
TT = 128
N_LEVELS = 7
SCORE_ROWS = 16


def _dot(a, b):
    return jnp.dot(a, b, preferred_element_type=F32)


def _dot_nt(a, b):
    return lax.dot_general(a, b, (((1,), (1,)), ((), ())), preferred_element_type=F32)


def _dot_tn(a, b):
    return lax.dot_general(a, b, (((0,), (0,)), ((), ())), preferred_element_type=F32)


def _silu(x):
    return x * jax.nn.sigmoid(x)


def _log_sigmoid(x):
    return jnp.minimum(x, 0.0) - jnp.log1p(jnp.exp(-jnp.abs(x)))


def _rms(x, g):
    return x * lax.rsqrt(jnp.mean(x * x, axis=-1, keepdims=True) + EPS) * g


def _head_rms(x, head_dim, g):
    w = x.shape[-1]
    sh = int(math.log2(head_dim))
    r = lax.shift_right_logical(lax.broadcasted_iota(jnp.int32, (w, w), 0), sh)
    c = lax.shift_right_logical(lax.broadcasted_iota(jnp.int32, (w, w), 1), sh)
    bd = jnp.where(r == c, 1.0 / head_dim, 0.0).astype(BF)
    x2 = x * x
    hi = x2.astype(BF)
    lo = (x2 - hi.astype(F32)).astype(BF)
    ms = _dot(hi, bd) + _dot(lo, bd)
    return x * lax.rsqrt(ms + EPS) * g


def _group_of_tile(i):
    return jnp.where(i < PROMPT_TILES, 0, 1 + jnp.maximum(i - PROMPT_TILES, 0) // SAMPLE_TILES_PER_SEQ)


def _params(sem):
    return pltpu.CompilerParams(dimension_semantics=sem, vmem_limit_bytes=VMEM_LIMIT)


def _mod_kernel(cond_ref, w_ref, b_ref, o_ref):
    s = _silu(cond_ref[...])
    o_ref[...] = _dot(s.astype(BF), w_ref[...].astype(BF)) + b_ref[...]


def _modulation(cond8, w_ada, b_ada):
    nb = 1536
    return pl.pallas_call(
        _mod_kernel,
        grid=(DEPTH, N_MOD * D_MODEL // nb),
        in_specs=[
            pl.BlockSpec((8, D_MODEL), lambda l, j: (0, 0)),
            pl.BlockSpec((None, D_MODEL, nb), lambda l, j: (l, 0, j)),
            pl.BlockSpec((None, 1, nb), lambda l, j: (l, 0, j)),
        ],
        out_specs=pl.BlockSpec((None, 8, nb), lambda l, j: (l, 0, j)),
        out_shape=jax.ShapeDtypeStruct((DEPTH, 8, N_MOD * D_MODEL), F32),
        compiler_params=_params(("arbitrary", "arbitrary")),
        name="modulation",
    )(cond8, w_ada, b_ada.reshape(DEPTH, 1, N_MOD * D_MODEL))


def _tile_of_step(i):
    return jnp.maximum(i - NPRO, 0)


def _chunk_of_step(i):
    return jnp.minimum(i, NPRO - 1)


def _tok_specs(width, split):
    if not split:
        return [pl.BlockSpec((TD, width), lambda i: (_tile_of_step(i), 0))]
    return [pl.BlockSpec((TD, width), lambda i: (jnp.minimum(_tile_of_step(i), PROMPT_TILES - 1), 0)),
            pl.BlockSpec((TD, width), lambda i: (jnp.maximum(_tile_of_step(i) - PROMPT_TILES, 0), 0))]


def _tok_load(t, refs):
    if len(refs) == 1:
        return refs[0][...]
    return jnp.where(t < PROMPT_TILES, refs[0][...], refs[1][...])


def _mod_spec(l):
    return pl.BlockSpec((None, None, N_MOD, D_MODEL), lambda i: (l, _group_of_tile(_tile_of_step(i)), 0, 0))


def _stage_rows(i, n):
    return pl.ds(pl.multiple_of(i * n, n), n)


def _inproj_kernel(*refs, n_x):
    x_refs = refs[:n_x]
    mod_ref, g_ref, w_ref, zac_ref, zbd_ref, h_ref, w_s = refs[n_x:]
    i = pl.program_id(0)

    @pl.when(i < NPRO)
    def _():
        w_s[_stage_rows(i, SMALL_W // NPRO), :] = w_ref[...].astype(BF)

    @pl.when(i >= NPRO)
    def _():
        m = mod_ref[...]
        h = _rms(_tok_load(i - NPRO, x_refs), g_ref[...]) * (1.0 + m[1:2]) + m[0:1]
        hb = h.astype(BF)
        h_ref[...] = hb
        zac_ref[...] = _dot_nt(hb, w_s[:AC_W, :])
        zbd_ref[...] = _dot_nt(hb, w_s[AC_W:, :])


def _inproj(l, xs, mod, norm_g, w_in_t):
    split = len(xs) == 2
    return pl.pallas_call(
        functools.partial(_inproj_kernel, n_x=len(xs)),
        grid=(NPRO + N_TILES,),
        in_specs=_tok_specs(D_MODEL, split) + [
            _mod_spec(l),
            pl.BlockSpec((None, 1, D_MODEL), lambda i: (l, 0, 0)),
            pl.BlockSpec((None, SMALL_W // NPRO, D_MODEL), lambda i: (l, _chunk_of_step(i), 0)),
        ],
        out_specs=[
            pl.BlockSpec((TD, AC_W), lambda i: (_tile_of_step(i), 0)),
            pl.BlockSpec((TD, BD_PAD), lambda i: (_tile_of_step(i), 0)),
            pl.BlockSpec((TD, D_MODEL), lambda i: (_tile_of_step(i), 0)),
        ],
        out_shape=[
            jax.ShapeDtypeStruct((N_TOK, AC_W), F32),
            jax.ShapeDtypeStruct((N_TOK, BD_PAD), F32),
            jax.ShapeDtypeStruct((N_TOK, D_MODEL), BF),
        ],
        scratch_shapes=[pltpu.VMEM((SMALL_W, D_MODEL), BF)],
        compiler_params=_params(("arbitrary",)),
        name=f"inproj{l}",
    )(*xs, mod, norm_g, w_in_t)


def _rope_tables(n_tokens, head_dim, n_rep):
    rows = n_tokens // GRID_W
    row = np.repeat(np.arange(rows), GRID_W).astype(np.float64)
    col = np.tile(np.arange(GRID_W), rows).astype(np.float64)
    half = head_dim // 2
    q4 = head_dim // 4
    freqs = ROPE_THETA ** (-np.arange(0, half, 2, dtype=np.float64) / half)
    ang_r = row[:, None] * freqs
    ang_c = col[:, None] * freqs
    ang = np.concatenate([ang_r, ang_r, ang_c, ang_c], axis=-1)
    cos, sin = np.cos(ang), np.sin(ang)
    first = (np.arange(head_dim) % (2 * q4)) < q4
    s_dn = np.where(first, -sin, 0.0)
    s_up = np.where(first, 0.0, sin)
    return tuple(jnp.asarray(np.tile(t, (1, n_rep)), dtype=F32) for t in (cos, s_dn, s_up))


def _rope(x, cos, s_dn, s_up, q4):
    w = x.shape[-1]
    return x * cos + pltpu.roll(x, w - q4, 1) * s_dn + pltpu.roll(x, q4, 1) * s_up


def _softmax_pv_group(maps, outs):
    scores = [[_dot(q, k) if t else _dot_nt(q, k) for k, t in zip(ks, fm)] for q, ks, _, _, fm in maps]
    yield
    probs = []
    for ss in scores:
        m = ss[0].max(axis=-1, keepdims=True)
        for s in ss[1:]:
            m = jnp.maximum(m, s.max(axis=-1, keepdims=True))
        probs.append([jnp.exp2(s - m).astype(BF) for s in ss])
    yield
    for ps, (_, _, vexts, half, fm) in zip(probs, maps):
        o = None
        for p, v, t in zip(ps, vexts, fm):
            part = _dot_nt(p, v) if t else _dot(p, v)
            o = part if o is None else o + part
        outs.append(o[:, half * 64:(half + 1) * 64] / o[:, (1 - half) * 64:(1 - half) * 64 + 1])
    yield


def _with_ones(v, half, feature_major):
    idx = lax.broadcasted_iota(jnp.int32, v.shape, 0 if feature_major else 1)
    return jnp.where(lax.shift_right_logical(idx, 6) == half, v, jnp.ones_like(v))


def _attend_heads(aq, cq, ka, va, kc, vc, fm, lam, gsub, lam_init, y_ref, group):
    aqb = (aq * (A_HEAD_DIM ** -0.5 * LOG2E)).astype(BF)
    cqb = (cq * (C_HEAD_DIM ** -0.5 * LOG2E)).astype(BF)
    rep = A_HEADS // A_KV_HEADS

    def feat(x, t, sl):
        return x[sl, :] if t else x[:, sl]

    maps = []
    for h in range(A_HEADS):
        g = h // rep
        sl = slice(g * A_HEAD_DIM, (g + 1) * A_HEAD_DIM)
        maps.append((aqb[:, h * A_HEAD_DIM:(h + 1) * A_HEAD_DIM], [feat(k, t, sl) for k, t in zip(ka, fm)],
                     [_with_ones(v, g, t) for v, t in zip(va, fm)], g, fm))
    for h in range(C_HEADS):
        slab = slice((h // 2) * 128, (h // 2 + 1) * 128)
        vh = [_with_ones(feat(v, t, slab), h % 2, t) for v, t in zip(vc, fm)]
        for j in range(2):
            sl = slice((2 * h + j) * C_HEAD_DIM, (2 * h + j + 1) * C_HEAD_DIM)
            maps.append((cqb[:, sl], [feat(k, t, sl) for k, t in zip(kc, fm)], vh, h % 2, fm))
    outs = []
    for i in range(0, len(maps), group):
        yield from _softmax_pv_group(maps[i:i + group], outs)
    for h in range(A_HEADS):
        y_ref[:, h * A_HEAD_DIM:(h + 1) * A_HEAD_DIM] = outs[h]
    vd = 2 * C_HEAD_DIM
    for h in range(C_HEADS):
        d = outs[A_HEADS + 2 * h] - lam * outs[A_HEADS + 2 * h + 1]
        y_ref[:, BRANCH_W + h * vd:BRANCH_W + (h + 1) * vd] = _rms(d, gsub) * (1.0 - lam_init)


def _lambda(cl):
    s1 = jnp.sum(cl[0:1] * cl[1:2], axis=-1, keepdims=True)
    s2 = jnp.sum(cl[2:3] * cl[3:4], axis=-1, keepdims=True)
    return jnp.exp(s1) - jnp.exp(s2)


def _attn_prompt_kernel(z_ref, gaq, gak, gcq, gck, gsub, cl_ref, y_ref, oak, oav, ock, ocv, *, lam_init):
    z = z_ref[...]
    ak = _head_rms(z[:, 256:384], A_HEAD_DIM, gak[...])
    av = z[:, 384:512]
    ck = _head_rms(z[:, 768:1024], C_HEAD_DIM, gck[...])
    cv = z[:, 1024:1280]
    oak[...] = ak.T
    oav[...] = av.T
    ock[...] = ck.T
    ocv[...] = cv.T
    aq = _head_rms(z[:, 0:256], A_HEAD_DIM, gaq[...])
    cq = _head_rms(z[:, 512:768], C_HEAD_DIM, gcq[...])
    lam = _lambda(cl_ref[...]) + lam_init
    yield
    yield from _attend_heads(aq, cq, [ak.astype(BF)], [av.astype(BF)], [ck.astype(BF)], [cv.astype(BF)], [False],
                             lam, gsub[...], lam_init, y_ref, group=A_HEADS + 2 * C_HEADS)


def _attn_sample_kernel(z_ref, gaq, gak, gcq, gck, gsub, cl_ref, cak, cav, cck, ccv,
                        cosa, sda, sua, cosc, sdc, suc, y_ref, ka_s, va_s, kc_s, vc_s, *, lam_init):
    qi = pl.program_id(1)
    qa4, qc4 = A_HEAD_DIM // 4, C_HEAD_DIM // 4

    @pl.when(qi == 0)
    def _():
        ak = _head_rms(z_ref[:, 256:384], A_HEAD_DIM, gak[...])
        ka_s[...] = _rope(ak, cosa[:, :128], sda[:, :128], sua[:, :128], qa4).astype(BF)
        va_s[...] = z_ref[:, 384:512].astype(BF)
        ck = _head_rms(z_ref[:, 768:1024], C_HEAD_DIM, gck[...])
        kc_s[...] = _rope(ck, cosc[...], sdc[...], suc[...], qc4).astype(BF)
        vc_s[...] = z_ref[:, 1024:1280].astype(BF)

    rows = pl.ds(pl.multiple_of(qi * TQ, TQ), TQ)
    aq = _head_rms(z_ref[rows, 0:256], A_HEAD_DIM, gaq[...])
    aq = _rope(aq, cosa[rows, :], sda[rows, :], sua[rows, :], qa4)
    cq = _head_rms(z_ref[rows, 512:768], C_HEAD_DIM, gcq[...])
    cq = _rope(cq, cosc[rows, :], sdc[rows, :], suc[rows, :], qc4)
    lam = _lambda(cl_ref[...]) + lam_init
    yield from _attend_heads(aq, cq,
                             [cak[...].astype(BF), ka_s[...]], [cav[...].astype(BF), va_s[...]],
                             [cck[...].astype(BF), kc_s[...]], [ccv[...].astype(BF), vc_s[...]], [True, False],
                             lam, gsub[...], lam_init, y_ref, group=4)


def _gain_specs(l, nd):
    zeros = (0,) * (nd - 1)
    widths = (256, 128, 256, 256, 64)
    return [pl.BlockSpec((None, 1, w), lambda *a: (l, 0, 0)) for w in widths] + \
           [pl.BlockSpec((None, 4, C_HEAD_DIM), lambda *a: (l, 0, 0))]


N_VTILES = BRANCH_W // 128


def _state_blocks(nh, kd, vd):
    per = 128 // vd
    return [(h // per, slice((h * kd) % 128, (h * kd) % 128 + kd), slice((h % per) * vd, (h % per + 1) * vd))
            for h in range(nh)]


_B_BLOCKS = _state_blocks(B_HEADS, B_KEY_DIM, B_VAL_DIM)
_D_BLOCKS = _state_blocks(D_HEADS, D_KEY_DIM, D_VAL_DIM)


def _gla_constants():
    idx = np.arange(TT)
    scans, masks = [], []
    for rev in (False, True):
        eff = (TT - 1 - idx) if rev else idx
        et, eu = eff[:, None], eff[None, :]
        sc, mk = [], []
        for j in range(N_LEVELS):
            b = 1 << j
            start = et - et % b
            odd = (et // b) % 2 == 1
            sc.append(np.where(odd, (eu > start) & (eu <= et), (eu > et) & (eu <= start + b)))
            mk.append(((et // b) % 2 == 1) & (eu // b == et // b - 1))
        sc.append(eu <= et)
        sc.append(eu > et)
        mk.append(eu == et)
        scans.append(np.concatenate([np.concatenate(sc, axis=0)] * 2, axis=-1))
        masks.append(np.stack(mk))
    return tuple(jnp.asarray(np.stack(t), BF) for t in (scans, masks))


def _gla_prepare(q, k, v, la2, scan_ref, d, rev, use_state):
    tt = q.shape[0]
    la_hi = la2.astype(BF)
    la_lo = (la2 - la_hi.astype(F32)).astype(BF)
    la_split = jnp.concatenate([la_hi, la_lo], axis=0)

    sums = _dot(scan_ref[d], la_split)

    def factor(i):
        return jnp.exp2(sums[i * tt:(i + 1) * tt])

    qs, ks = [], []
    for j in range(N_LEVELS):
        f = factor(j)
        qs.append((q * f).astype(BF))
        ks.append((k * f).astype(BF).T)
    qs.append(q.astype(BF))
    ks.append(k.astype(BF).T)
    k_out = (k * factor(N_LEVELS + 1)).astype(BF)
    vb = v.astype(BF)
    q_in = d_tile = None
    if use_state:
        q_in = (q * factor(N_LEVELS)).astype(BF)
        ones = jnp.ones((2 * tt, 128), BF)
        d_tile = [jnp.exp2(_dot_tn(la_split[:, i * 128:(i + 1) * 128], ones)) for i in range(q.shape[1] // 128)]

    return dict(qs=qs, ks=ks, k_out=k_out, vb=vb, q_in=q_in, d_tile=d_tile)


def _lane_keep(x, lo, hi):
    lane = lax.broadcasted_iota(jnp.int32, x.shape, 1)
    return jnp.where((lane >= lo) & (lane < hi), x, jnp.zeros_like(x))


def _level_rows(j, rev):
    b = 1 << j
    if j == N_LEVELS or b < SCORE_ROWS:
        return None
    return [(i * b, (i + 1) * b) for i in range(TT // b) if (i % 2 == 1) != rev]


def _gla_scores(p, mask_ref, d, rev, nh, kd):
    assert nh % 2 == 0 and 128 % (2 * kd) == 0
    out = []
    for h0 in range(0, nh, 2):
        c = (h0 * kd) // 128
        blocks = [[None] * (TT // SCORE_ROWS) for _ in range(2)]

        def add(r0, val):
            for i in range(val[0].shape[0] // SCORE_ROWS):
                k = r0 // SCORE_ROWS + i
                for n in range(2):
                    piece = val[n][i * SCORE_ROWS:(i + 1) * SCORE_ROWS]
                    blocks[n][k] = piece if blocks[n][k] is None else blocks[n][k] + piece

        for j in range(N_LEVELS + 1):
            kt = p["ks"][j]
            khs = []
            for h in (h0, h0 + 1):
                lo = (h * kd) % 128
                pieces = [jnp.zeros((lo, kt.shape[1]), BF)] if lo else []
                pieces.append(kt[c * 128 + lo:c * 128 + lo + kd, :])
                if lo + kd < 128:
                    pieces.append(jnp.zeros((128 - lo - kd, kt.shape[1]), BF))
                khs.append(jnp.concatenate(pieces, axis=0))
            kh = jnp.concatenate(khs, axis=1)
            qj = p["qs"][j][:, c * 128:(c + 1) * 128]
            ranges = _level_rows(j, rev)
            if ranges is None:
                t = _dot(qj, kh).astype(BF)
                m = mask_ref[d, j]
                add(0, [t[:, :TT] * m, t[:, TT:] * m])
            else:
                q_rows = jnp.concatenate([qj[a:b] for a, b in ranges], axis=0) if len(ranges) > 1 else qj[ranges[0][0]:ranges[0][1]]
                m_rows = [mask_ref[d, j, a:b, :] for a, b in ranges]
                t = _dot(q_rows, kh).astype(BF)
                off = 0
                for (a, b), m in zip(ranges, m_rows):
                    add(a, [t[off:off + b - a, :TT] * m, t[off:off + b - a, TT:] * m])
                    off += b - a
        zero = jnp.zeros((SCORE_ROWS, TT), BF)
        for n in range(2):
            out.append(jnp.concatenate([z if z is not None else zero for z in blocks[n]], axis=0))
    return out


def _gla_outputs(p, scs, blk_ref, st_ref, d, nh, kd, vd, use_state, o_ref, rows, col0, accumulate):
    per = 128 // vd
    for c in range(nh // per):
        vt = p["vb"][:, c * 128:(c + 1) * 128]
        kt = (c * per * kd) // 128
        ktile = slice(kt * 128, (kt + 1) * 128)
        o = _dot(jnp.concatenate([scs[c * per + i] for i in range(per)], axis=1),
                 jnp.concatenate([_lane_keep(vt, i * vd, (i + 1) * vd) for i in range(per)], axis=0))
        kv = _dot_tn(p["k_out"][:, ktile], vt) * blk_ref[c]
        if use_state:
            st = st_ref[d, c]
            o = o + _dot(p["q_in"][:, ktile], st.astype(BF))
            st_ref[d, c] = st * p["d_tile"][kt] + kv
        else:
            st_ref[d, c] = kv
        osl = slice(col0 + c * 128, col0 + (c + 1) * 128)
        if accumulate:
            o_ref[rows, osl] += o
        else:
            o_ref[rows, osl] = o


def _gla_kernel(*refs, layer, n_tiles, has_state, pump=None):
    pump = pump or (lambda: None)
    if has_state:
        (z_ref, lbl_ref, aw_ref, ab_ref, bng, dng, scan_ref, mask_ref, blkb_ref, blkd_ref, sh_in, sd_in,
         y_ref, o_scr, sth, std) = refs
        for packed, raw, blocks in ((sth, sh_in, _B_BLOCKS), (std, sd_in, _D_BLOCKS)):
            packed[...] = jnp.zeros(packed.shape, F32)
            for dd in range(2):
                for h, (c, r, ln) in enumerate(blocks):
                    packed[dd, c, r, ln] = raw[dd, h]
    else:
        (z_ref, lbl_ref, aw_ref, ab_ref, bng, dng, scan_ref, mask_ref, blkb_ref, blkd_ref,
         y_ref, sh_out, sd_out, o_scr, sth, std) = refs

    gates = []
    for d in range(2):
        logits = [lbl_ref[d, i:i + 1, :] for i in range(DEPTH)]
        mx = functools.reduce(jnp.maximum, logits)
        ex = [jnp.exp(t - mx) for t in logits]
        den = functools.reduce(lambda a, b: a + b, ex)
        ps = [t / den for t in ex]
        lb = functools.reduce(lambda a, b: a + b, ps[:layer + 1]) - ps[0]
        gates.append((lb, jnp.log(jnp.maximum(lb, LB_FLOOR)), jnp.log1p(-lb)))

    o_scr[...] = jnp.zeros(o_scr.shape, F32)

    def tile(i, use_state):
        preps = []
        for d in range(2):
            rev = d == 1
            lb, log_lb, log_1m = gates[d]
            j = (n_tiles - 1 - i) if rev else i
            rows = pl.ds(j * TT if isinstance(j, int) else pl.multiple_of(j * TT, TT), TT)
            bq = z_ref[rows, 0:256]
            zf = z_ref[rows, 768:1024] if rev else z_ref[rows, 512:768]
            b2 = log_1m + _log_sigmoid(zf)
            la = jnp.maximum(log_lb, b2) + jnp.log1p(jnp.exp(-jnp.abs(log_lb - b2)))
            kb = (1.0 - lb) * jax.nn.sigmoid(-zf)
            pre = _dot(z_ref[rows, 2048:2176].astype(BF), aw_ref[d]) + ab_ref[d]
            la_d = _log_sigmoid(pre) * (LOG2E / D_GATE_TAU)
            pb = _gla_prepare(_silu(bq), kb, z_ref[rows, 256:512], la * LOG2E, scan_ref, d, rev, use_state)
            pd = _gla_prepare(z_ref[rows, 1280:1408] * (D_KEY_DIM ** -0.5), z_ref[rows, 1408:1536],
                              z_ref[rows, 1536:1792], la_d, scan_ref, d, rev, use_state)
            preps.append((d, rows, pb, pd))
        pump()
        scores = [(_gla_scores(pb, mask_ref, d, d == 1, B_HEADS, B_KEY_DIM),
                   _gla_scores(pd, mask_ref, d, d == 1, D_HEADS, D_KEY_DIM)) for d, _, pb, pd in preps]
        pump()
        for (d, rows, pb, pd), (sb, sd) in zip(preps, scores):
            _gla_outputs(pb, sb, blkb_ref, sth, d, B_HEADS, B_KEY_DIM, B_VAL_DIM, use_state, o_scr, rows, 0, True)
            _gla_outputs(pd, sd, blkd_ref, std, d, D_HEADS, D_KEY_DIM, D_VAL_DIM, use_state, o_scr, rows, 256, True)
        pump()

    if has_state:
        def body(i, carry):
            tile(i, True)
            return carry

        lax.fori_loop(0, n_tiles, body, 0)
    else:
        for i in range(n_tiles):
            tile(i, i > 0)

    def finish(i, carry):
        rows = pl.ds(pl.multiple_of(i * TM, TM), TM)
        y_ref[rows, 0:256] = _head_rms(o_scr[rows, 0:256], B_VAL_DIM, bng[...]) * _silu(z_ref[rows, 1024:1280])
        y_ref[rows, 256:512] = _head_rms(o_scr[rows, 256:512], D_VAL_DIM, dng[...]) * _silu(z_ref[rows, 1792:2048])
        return carry

    lax.fori_loop(0, (n_tiles * TT) // TM, finish, 0)
    if not has_state:
        for packed, raw, blocks in ((sth, sh_out, _B_BLOCKS), (std, sd_out, _D_BLOCKS)):
            for dd in range(2):
                for h, (c, r, ln) in enumerate(blocks):
                    raw[dd, h] = packed[dd, c, r, ln]


def _gla_common_specs(l, nd):
    return [
        pl.BlockSpec((2, DEPTH, 256), lambda *a: (0, 0, 0)),
        pl.BlockSpec((None, 2, 128, 128), lambda *a: (l, 0, 0, 0)),
        pl.BlockSpec((None, 2, 1, 128), lambda *a: (l, 0, 0, 0)),
        pl.BlockSpec((None, 1, 256), lambda *a: (l, 0, 0)),
        pl.BlockSpec((None, 1, 256), lambda *a: (l, 0, 0)),
        pl.BlockSpec((2, (N_LEVELS + 2) * TT, 2 * TT), lambda *a: (0, 0, 0)),
        pl.BlockSpec((2, N_LEVELS + 1, TT, TT), lambda *a: (0, 0, 0, 0)),
        pl.BlockSpec((N_VTILES, 128, 128), lambda *a: (0, 0, 0)),
        pl.BlockSpec((N_VTILES, 128, 128), lambda *a: (0, 0, 0)),
    ]


_STATE_SCRATCH = [pltpu.VMEM((2, N_VTILES, 128, 128), F32), pltpu.VMEM((2, N_VTILES, 128, 128), F32)]


def _state_pattern(blocks):
    pat = np.zeros((N_VTILES, 128, 128), np.float32)
    for c, r, ln in blocks:
        pat[c, r, ln] = 1.0
    return jnp.asarray(pat)


N_ATTN_IN = 7
N_ATTN_OUT = 5


def _ctx_mixers_kernel(*refs, layer, lam_init, n_carried):
    n_in = N_ATTN_IN + 1 + len(_gla_common_specs(0, 1))
    a_in, g_in, outs = refs[:N_ATTN_IN], refs[N_ATTN_IN:n_in], refs[n_in + n_carried:]
    y_ac, caches, y_bd, states = outs[0], outs[1:N_ATTN_OUT], outs[N_ATTN_OUT], outs[N_ATTN_OUT + 1:]
    attn = _attn_prompt_kernel(*a_in, y_ac, *caches, lam_init=lam_init)
    _gla_kernel(*g_in, y_bd, *states, layer=layer, n_tiles=SEQ // TT, has_state=False,
                pump=lambda: next(attn, None))
    for _ in attn:
        pass


def _ctx_mixers(l, zac, zbd, gains, c_lambda, small, lam_init, carried):
    stacked = [(BATCH, DEPTH, w, SEQ) for w in (128, 128, 256, 256)] + [
        (BATCH, DEPTH, 2, B_HEADS, B_KEY_DIM, B_VAL_DIM), (BATCH, DEPTH, 2, D_HEADS, D_KEY_DIM, D_VAL_DIM)]
    seq = lambda w: pl.BlockSpec((SEQ, w), lambda b: (b, 0))
    layer_block = lambda s: pl.BlockSpec((None, None) + s[2:], lambda b: (b, l) + (0,) * (len(s) - 2))
    in_specs = [seq(AC_W)] + _gain_specs(l, 1) + [seq(BD_PAD)] + _gla_common_specs(l, 1)
    n_carried = 0 if carried is None else len(carried)
    aliases = {}
    if carried is not None:
        out_idx = [1, 2, 3, 4, 6, 7]
        aliases = {len(in_specs) + k: out_idx[k] for k in range(n_carried)}
        in_specs = in_specs + [pl.BlockSpec(memory_space=pl.ANY)] * n_carried
    out = pl.pallas_call(
        functools.partial(_ctx_mixers_kernel, layer=l, lam_init=lam_init, n_carried=n_carried),
        grid=(BATCH,),
        in_specs=in_specs,
        out_specs=[seq(2 * BRANCH_W)] + [layer_block(s) for s in stacked[:4]] + [seq(2 * BRANCH_W)]
                  + [layer_block(s) for s in stacked[4:]],
        out_shape=[jax.ShapeDtypeStruct((N_PROMPT, 2 * BRANCH_W), F32)]
                  + [jax.ShapeDtypeStruct(s, F32) for s in stacked[:4]]
                  + [jax.ShapeDtypeStruct((N_PROMPT, 2 * BRANCH_W), F32)]
                  + [jax.ShapeDtypeStruct(s, F32) for s in stacked[4:]],
        scratch_shapes=[pltpu.VMEM((SEQ, 2 * BRANCH_W), F32)] + _STATE_SCRATCH,
        input_output_aliases=aliases,
        compiler_params=_params(("arbitrary",)),
        name=f"ctx_mixers{l}",
    )(zac, *gains, c_lambda, zbd, *small, *(carried or ()))
    return out[0], out[5], out[1:5] + out[6:]


def _attn_sample_body(*refs, lam_init):
    for _ in _attn_sample_kernel(*refs, lam_init=lam_init):
        pass


def _attn_sample(l, zac, gains, c_lambda, caches, tables, lam_init):
    first_blk = N_PROMPT // DEC_SEQ
    cache_specs = [pl.BlockSpec((None, None, w, PAST_LEN), lambda b, q: (b, l, 0, 0)) for w in (128, 128, 256, 256)]
    table_specs = [pl.BlockSpec((DEC_SEQ, 256), lambda b, q: (0, 0)) for _ in range(6)]
    return pl.pallas_call(
        functools.partial(_attn_sample_body, lam_init=lam_init),
        grid=(DEC_BATCH, DEC_SEQ // TQ),
        in_specs=[pl.BlockSpec((DEC_SEQ, AC_W), lambda b, q: (first_blk + b, 0))] + _gain_specs(l, 2)
                 + cache_specs + table_specs,
        out_specs=pl.BlockSpec((TQ, 2 * BRANCH_W), lambda b, q: (b * (DEC_SEQ // TQ) + q, 0)),
        out_shape=jax.ShapeDtypeStruct((N_SAMPLE, 2 * BRANCH_W), F32),
        scratch_shapes=[pltpu.VMEM((DEC_SEQ, 128), BF), pltpu.VMEM((DEC_SEQ, 128), BF),
                        pltpu.VMEM((DEC_SEQ, 256), BF), pltpu.VMEM((DEC_SEQ, 256), BF)],
        compiler_params=_params(("arbitrary", "arbitrary")),
        name=f"attn_sample{l}",
    )(zac, *gains, c_lambda, *caches, *tables)


def _gla_sample(l, zbd, small, st_h, st_d):
    first_blk = N_PROMPT // DEC_SEQ
    return pl.pallas_call(
        functools.partial(_gla_kernel, layer=l, n_tiles=DEC_SEQ // TT, has_state=True),
        grid=(DEC_BATCH,),
        in_specs=[pl.BlockSpec((DEC_SEQ, BD_PAD), lambda b: (first_blk + b, 0))] + _gla_common_specs(l, 1) + [
            pl.BlockSpec((None, None, 2, B_HEADS, B_KEY_DIM, B_VAL_DIM), lambda b: (b, l, 0, 0, 0, 0)),
            pl.BlockSpec((None, None, 2, D_HEADS, D_KEY_DIM, D_VAL_DIM), lambda b: (b, l, 0, 0, 0, 0)),
        ],
        out_specs=pl.BlockSpec((DEC_SEQ, 2 * BRANCH_W), lambda b: (b, 0)),
        out_shape=jax.ShapeDtypeStruct((N_SAMPLE, 2 * BRANCH_W), F32),
        scratch_shapes=[pltpu.VMEM((DEC_SEQ, 2 * BRANCH_W), F32)] + _STATE_SCRATCH,
        compiler_params=_params(("arbitrary",)),
        name=f"gla_sample{l}",
    )(zbd, *small, st_h, st_d)


def _mix_kernel(*refs, n_x):
    x_refs = refs[:n_x]
    (h_ref, yp_ref, ys_ref, gp_ref, gs_ref, mod_ref, win_ref, wb_ref, wo_ref, o_ref, wm_s, wb_s, wo_s) = refs[n_x:]
    i = pl.program_id(0)

    @pl.when(i < NPRO)
    def _():
        rows = _stage_rows(i, W_ROWS)
        wm_s[_stage_rows(i, MERGE_ROWS), :] = win_ref[...].astype(BF)
        wb_s[rows, :] = wb_ref[...].astype(BF)
        wo_s[rows, :] = wo_ref[...].astype(BF)

    @pl.when(i >= NPRO)
    def _():
        t = i - NPRO
        hb = h_ref[...]
        yac = _tok_load(t, (yp_ref, ys_ref))
        ybd = _tok_load(t, (gp_ref, gs_ref))
        branches = (yac[:, :BRANCH_W], ybd[:, :BRANCH_W], yac[:, BRANCH_W:], ybd[:, BRANCH_W:])
        mixed = None
        for n, y in enumerate(branches):
            logits = _dot_nt(hb, wm_s[n * D_MODEL:(n + 1) * D_MODEL, :])
            term = jax.nn.sigmoid(logits) * _dot(y.astype(BF), wb_s[n * BRANCH_W:(n + 1) * BRANCH_W, :])
            mixed = term if mixed is None else mixed + term
        o_ref[...] = _tok_load(t, x_refs) + mod_ref[2:3, :] * _dot(mixed.astype(BF), wo_s[...])


def _mix(l, xs, h, y_p, y_s, g_p, g_s, mod, w_in_t, w_branch, w_out):
    chunk = pl.BlockSpec((None, W_ROWS, D_MODEL), lambda i: (l, _chunk_of_step(i), 0))
    return pl.pallas_call(
        functools.partial(_mix_kernel, n_x=len(xs)),
        grid=(NPRO + N_TILES,),
        in_specs=_tok_specs(D_MODEL, len(xs) == 2) + _tok_specs(D_MODEL, False)
                 + _tok_specs(2 * BRANCH_W, True) + _tok_specs(2 * BRANCH_W, True)
                 + [_mod_spec(l),
                    pl.BlockSpec((None, MERGE_ROWS, D_MODEL), lambda i: (l, MERGE_OFF // MERGE_ROWS + _chunk_of_step(i), 0)),
                    chunk, chunk],
        out_specs=pl.BlockSpec((TD, D_MODEL), lambda i: (_tile_of_step(i), 0)),
        out_shape=jax.ShapeDtypeStruct((N_TOK, D_MODEL), F32),
        scratch_shapes=[pltpu.VMEM((NPRO * MERGE_ROWS, D_MODEL), BF), pltpu.VMEM((D_MODEL, D_MODEL), BF),
                        pltpu.VMEM((D_MODEL, D_MODEL), BF)],
        compiler_params=_params(("arbitrary",)),
        name=f"mix{l}",
    )(*xs, h, y_p, y_s, g_p, g_s, mod, w_in_t, w_branch.reshape(DEPTH, N_BRANCH * BRANCH_W, D_MODEL), w_out)


def _ffn_kernel(*refs, n_out):
    (xp_ref, x_ref, xn_ref, mod_ref, g_ref, wup_ref, cw_ref, cb_ref, wdn_ref) = refs[:9]
    o_refs = refs[9:9 + n_out]
    hext, wup_s, wdn_s = refs[9 + n_out:12 + n_out]
    i = pl.program_id(0)

    @pl.when(i < NPRO)
    def _():
        wup_s[_stage_rows(i, W_ROWS), :] = wup_ref[...].astype(BF)
        wdn_s[_stage_rows(i, D_FF // NPRO), :] = wdn_ref[...].astype(BF)

    if n_out == 2:
        keep_s = refs[12 + n_out]

        @pl.when(i == 0)
        def _():
            keep_s[...] = jnp.zeros(keep_s.shape, F32)

    @pl.when(i >= NPRO)
    def _():
        t = i - NPRO
        ctx = t < PROMPT_TILES
        pos = jnp.maximum(t - PROMPT_TILES, 0) % SAMPLE_TILES_PER_SEQ
        seq_first = ctx | (pos == 0)
        seq_last = ctx | (pos == SAMPLE_TILES_PER_SEQ - 1)
        m = mod_ref[...]
        g = g_ref[...]

        def pre(x):
            return _rms(x, g) * (1.0 + m[4:5]) + m[3:4]

        sub = lax.broadcasted_iota(jnp.int32, (HALO, D_MODEL), 0)
        hext[0:TD, :] = pre(x_ref[...])
        hext[TD:TD + HALO, :] = jnp.where(seq_last | (sub != 0), 0.0, pre(xn_ref[...]))
        hext[TD + HALO:, :] = jnp.where(seq_first | (sub != HALO - 1), 0.0, pre(xp_ref[...]))
        hb = hext[...].astype(BF)

        n_tile, mid = TD // 8, SEQ // 8
        sub8 = lax.broadcasted_iota(jnp.int32, (1, 8, FF_CHUNK), 1)

        def up_conv_act(c):
            halves = []
            for off in (0, D_FF):
                cols = slice(off + c * FF_CHUNK, off + (c + 1) * FF_CHUNK)
                u = _dot(hb, wup_s[:, cols]).reshape(FF_EXT // 8, 8, FF_CHUNK)
                down = pltpu.roll(u, 1, axis=1)
                up = pltpu.roll(u, 7, axis=1)
                before = jnp.concatenate([down[-1:], down[:mid - 1], jnp.where(ctx, 0.0, down[mid - 1:mid]),
                                          down[mid:n_tile - 1]], axis=0)
                after = jnp.concatenate([up[1:mid], jnp.where(ctx, 0.0, up[mid:mid + 1]), up[mid + 1:n_tile + 1]],
                                        axis=0)
                prev = jnp.where(sub8 == 0, before, down[:n_tile])
                nxt = jnp.where(sub8 == 7, after, up[:n_tile])
                conv = (prev * cw_ref[0:1, cols] + u[:n_tile] * cw_ref[1:2, cols] + nxt * cw_ref[2:3, cols]
                        + cb_ref[:, cols])
                halves.append(conv.reshape(TD, FF_CHUNK))
            return (_silu(halves[1]) * halves[0]).astype(BF)

        n_chunks = D_FF // FF_CHUNK
        acc = _dot(jnp.concatenate([up_conv_act(c) for c in range(n_chunks)], axis=1), wdn_s[...])
        gate = m[5:6]
        res = x_ref[...] + gate * acc
        if n_out == 1:
            o_refs[0][...] = res
        else:
            kept = jnp.where(ctx, res, keep_s[...])
            keep_s[...] = kept
            o_refs[0][...] = kept
            o_refs[1][...] = res


def _ffn(l, x, mod, norm_g, w_up, conv_w, conv_b, w_down, split_out):
    per = TD // HALO
    last_blk = N_TOK // HALO - 1
    if split_out:
        out_specs = [pl.BlockSpec((TD, D_MODEL), lambda i: (jnp.minimum(_tile_of_step(i), PROMPT_TILES - 1), 0)),
                     pl.BlockSpec((TD, D_MODEL), lambda i: (jnp.maximum(_tile_of_step(i) - PROMPT_TILES, 0), 0))]
        out_shape = [jax.ShapeDtypeStruct((N_PROMPT, D_MODEL), F32), jax.ShapeDtypeStruct((N_SAMPLE, D_MODEL), F32)]
    else:
        out_specs = [pl.BlockSpec((TD, D_MODEL), lambda i: (_tile_of_step(i), 0))]
        out_shape = [jax.ShapeDtypeStruct((N_TOK, D_MODEL), F32)]
    return pl.pallas_call(
        functools.partial(_ffn_kernel, n_out=len(out_specs)),
        grid=(NPRO + N_TILES,),
        in_specs=[
            pl.BlockSpec((HALO, D_MODEL), lambda i: (jnp.maximum(_tile_of_step(i) * per - 1, 0), 0)),
            pl.BlockSpec((TD, D_MODEL), lambda i: (_tile_of_step(i), 0)),
            pl.BlockSpec((HALO, D_MODEL), lambda i: (jnp.minimum((_tile_of_step(i) + 1) * per, last_blk), 0)),
            _mod_spec(l),
            pl.BlockSpec((None, 1, D_MODEL), lambda i: (l, 0, 0)),
            pl.BlockSpec((None, W_ROWS, 2 * D_FF), lambda i: (l, _chunk_of_step(i), 0)),
            pl.BlockSpec((None, CONV_WIDTH, 2 * D_FF), lambda i: (l, 0, 0)),
            pl.BlockSpec((None, 1, 2 * D_FF), lambda i: (l, 0, 0)),
            pl.BlockSpec((None, D_FF // NPRO, D_MODEL), lambda i: (l, _chunk_of_step(i), 0)),
        ],
        out_specs=out_specs,
        out_shape=out_shape,
        scratch_shapes=[pltpu.VMEM((FF_EXT, D_MODEL), F32),
                        pltpu.VMEM((D_MODEL, 2 * D_FF), BF), pltpu.VMEM((D_FF, D_MODEL), BF)]
                       + ([pltpu.VMEM((TD, D_MODEL), F32)] if split_out else []),
        compiler_params=_params(("arbitrary",)),
        name=f"ffn{l}",
    )(x, x, x, mod, norm_g, w_up, conv_w, conv_b, w_down)


def kernel(x_prompt, x_sample, cache_a_k, cache_a_v, cache_c_k, cache_c_v, state_hgrn, state_gla, c, c_ctx, w_ada, b_ada, norm1_g, norm2_g, w_in, a_qn_g, a_kn_g, c_qn_g, c_kn_g, c_lambda, c_subln_g, b_lb_logits, b_norm_g, d_alpha_w, d_alpha_b, d_norm_g, w_branch, w_out, w_up, conv_w, conv_b, w_down):
    xs = (x_prompt.reshape(N_PROMPT, D_MODEL), x_sample.reshape(N_SAMPLE, D_MODEL))
    w_in_t = jnp.swapaxes(w_in, 1, 2)

    cond8 = jnp.concatenate([c_ctx[None, :], c, jnp.zeros((8 - N_GROUPS, D_MODEL), F32)], axis=0)
    mod = _modulation(cond8, w_ada, b_ada)[:, :N_GROUPS].reshape(DEPTH, N_GROUPS, N_MOD, D_MODEL)

    tile4 = lambda g, n: jnp.tile(g, (1, n)).reshape(DEPTH, 1, -1)
    gains = (tile4(a_qn_g, A_HEADS), tile4(a_kn_g, A_KV_HEADS), tile4(c_qn_g, 2 * C_HEADS), tile4(c_kn_g, 2 * C_HEADS),
             c_subln_g.reshape(DEPTH, 1, 2 * C_HEAD_DIM))
    fmaj = lambda t: jnp.moveaxis(t, 2, -1).reshape(DEC_BATCH, DEPTH, -1, PAST_LEN)
    caches = tuple(fmaj(t) for t in (cache_a_k, cache_a_v, cache_c_k, cache_c_v))
    tables = _rope_tables(DEC_SEQ, A_HEAD_DIM, A_HEADS) + _rope_tables(DEC_SEQ, C_HEAD_DIM, 2 * C_HEADS)

    aw = jnp.zeros((DEPTH, 2, 128, 128), F32)
    aw = aw.at[:, 0, 0:D_GATE_RANK].set(d_alpha_w[:, 0]).at[:, 1, D_GATE_RANK:2 * D_GATE_RANK].set(d_alpha_w[:, 1])
    gla_small = (b_lb_logits, aw.astype(BF), d_alpha_b.reshape(DEPTH, 2, 1, 128),
                 tile4(b_norm_g, B_HEADS), tile4(d_norm_g, D_HEADS)) + _gla_constants() + (
                     _state_pattern(_B_BLOCKS), _state_pattern(_D_BLOCKS))

    n1 = norm1_g.reshape(DEPTH, 1, D_MODEL)
    n2 = norm2_g.reshape(DEPTH, 1, D_MODEL)
    cb = conv_b.reshape(DEPTH, 1, 2 * D_FF)

    carried = None
    for l in range(DEPTH):
        lam_init = 0.8 - 0.6 * math.exp(-0.3 * l)
        zac, zbd, h = _inproj(l, xs, mod, n1, w_in_t)
        y_p, g_p, carried = _ctx_mixers(l, zac, zbd, gains, c_lambda, gla_small, lam_init, carried)
        y_s = _attn_sample(l, zac, gains, c_lambda, caches, tables, lam_init)
        g_s = _gla_sample(l, zbd, gla_small, state_hgrn, state_gla)
        x1 = _mix(l, xs, h, y_p, y_s, g_p, g_s, mod, w_in_t, w_branch, w_out)
        xs = tuple(_ffn(l, x1, mod, n2, w_up, conv_w, cb, w_down, split_out=(l == DEPTH - 1)))

    y_prompt = xs[0].reshape(BATCH, SEQ, D_MODEL)
    y_sample = xs[1].reshape(DEC_BATCH, DEC_SEQ, D_MODEL)
    stacked = list(carried)
    feat_shapes = ((A_KV_HEADS, A_HEAD_DIM), (A_KV_HEADS, A_HEAD_DIM), (C_HEADS, 2, C_HEAD_DIM), (C_HEADS, 2 * C_HEAD_DIM))
    ctx = [jnp.moveaxis(t.reshape((BATCH, DEPTH) + fs + (SEQ,)), -1, 2) for t, fs in zip(stacked[:4], feat_shapes)]
    return (y_prompt, y_sample) + tuple(ctx) + tuple(stacked[4:])
```

```python
import functools
import math

import numpy as np
import jax
import jax.numpy as jnp
from jax import lax
from jax.experimental import pallas as pl
from jax.experimental.pallas import tpu as pltpu

F32 = jnp.float32
BF = jnp.bfloat16

D_MODEL = 1024
BATCH = 16
SEQ = 256
DEPTH = 2
DEC_BATCH = 2
DEC_SEQ = 1024
PAST_LEN = 512
GRID_W = 64
ROPE_THETA = 10000.0
EPS = 1e-6
LB_FLOOR = 1e-30
N_MOD = 6
N_BRANCH = 4
BRANCH_W = 256
A_HEADS, A_KV_HEADS, A_HEAD_DIM = 4, 2, 64
C_HEADS, C_HEAD_DIM = 4, 32
B_HEADS, B_KEY_DIM, B_VAL_DIM = 4, 64, 64
D_HEADS, D_KEY_DIM, D_VAL_DIM = 4, 32, 64
D_GATE_RANK = 16
D_GATE_TAU = 16.0
D_FF = 2816
CONV_WIDTH = 3

N_PROMPT = BATCH * SEQ
N_SAMPLE = DEC_BATCH * DEC_SEQ
N_TOK = N_PROMPT + N_SAMPLE
TM = 256
TQ = 512
TD = 2 * SEQ
N_TILES = N_TOK // TD
PROMPT_TILES = N_PROMPT // TD
SAMPLE_TILES_PER_SEQ = DEC_SEQ // TD
N_GROUPS = 1 + DEC_BATCH

AC_W = 1280
BD_W = 2080
BD_PAD = 2176
SMALL_W = AC_W + BD_PAD
MERGE_OFF = AC_W + BD_W
MERGE_ROWS = 560
HALO = 8
FF_EXT = TD + 2 * HALO
FF_CHUNK = 256
NPRO = 8
W_ROWS = D_MODEL // NPRO
LOG2E = 1.4426950408889634
VMEM_LIMIT = 56 * 1024 * 1024
TT = 128
N_LEVELS = 7
SCORE_ROWS = 16


def _dot(a, b):
    return jnp.dot(a, b, preferred_element_type=F32)


def _dot_nt(a, b):
    return lax.dot_general(a, b, (((1,), (1,)), ((), ())), preferred_element_type=F32)


def _dot_tn(a, b):
    return lax.dot_general(a, b, (((0,), (0,)), ((), ())), preferred_element_type=F32)


def _silu(x):
    return x * jax.nn.sigmoid(x)


def _log_sigmoid(x):
    return jnp.minimum(x, 0.0) - jnp.log1p(jnp.exp(-jnp.abs(x)))


def _rms(x, g):
    return x * lax.rsqrt(jnp.mean(x * x, axis=-1, keepdims=True) + EPS) * g


def _head_rms(x, head_dim, g):
    w = x.shape[-1]
    sh = int(math.log2(head_dim))
    r = lax.shift_right_logical(lax.broadcasted_iota(jnp.int32, (w, w), 0), sh)
    c = lax.shift_right_logical(lax.broadcasted_iota(jnp.int32, (w, w), 1), sh)
    bd = jnp.where(r == c, 1.0 / head_dim, 0.0).astype(BF)
    x2 = x * x
    hi = x2.astype(BF)
    lo = (x2 - hi.astype(F32)).astype(BF)
    ms = _dot(hi, bd) + _dot(lo, bd)
    return x * lax.rsqrt(ms + EPS) * g


def _group_of_tile(i):
    return jnp.where(i < PROMPT_TILES, 0, 1 + jnp.maximum(i - PROMPT_TILES, 0) // SAMPLE_TILES_PER_SEQ)


def _params(sem):
    return pltpu.CompilerParams(dimension_semantics=sem, vmem_limit_bytes=VMEM_LIMIT)


def _mod_kernel(cond_ref, w_ref, b_ref, o_ref):
    s = _silu(cond_ref[...])
    o_ref[...] = _dot(s.astype(BF), w_ref[...].astype(BF)) + b_ref[...]


def _modulation(cond8, w_ada, b_ada):
    nb = 768
    return pl.pallas_call(
        _mod_kernel,
        grid=(DEPTH, N_MOD * D_MODEL // nb),
        in_specs=[
            pl.BlockSpec((8, D_MODEL), lambda l, j: (0, 0)),
            pl.BlockSpec((None, D_MODEL, nb), lambda l, j: (l, 0, j)),
            pl.BlockSpec((None, 1, nb), lambda l, j: (l, 0, j)),
        ],
        out_specs=pl.BlockSpec((None, 8, nb), lambda l, j: (l, 0, j)),
        out_shape=jax.ShapeDtypeStruct((DEPTH, 8, N_MOD * D_MODEL), F32),
        compiler_params=_params(("arbitrary", "arbitrary")),
        name="modulation",
    )(cond8, w_ada, b_ada.reshape(DEPTH, 1, N_MOD * D_MODEL))


def _tile_of_step(i):
    return jnp.maximum(i - NPRO, 0)


def _chunk_of_step(i):
    return jnp.minimum(i, NPRO - 1)


def _tok_specs(width, split):
    if not split:
        return [pl.BlockSpec((TD, width), lambda i: (_tile_of_step(i), 0))]
    return [pl.BlockSpec((TD, width), lambda i: (jnp.minimum(_tile_of_step(i), PROMPT_TILES - 1), 0)),
            pl.BlockSpec((TD, width), lambda i: (jnp.maximum(_tile_of_step(i) - PROMPT_TILES, 0), 0))]


def _tok_load(t, refs):
    if len(refs) == 1:
        return refs[0][...]
    return jnp.where(t < PROMPT_TILES, refs[0][...], refs[1][...])


def _mod_spec(l):
    return pl.BlockSpec((None, None, N_MOD, D_MODEL), lambda i: (l, _group_of_tile(_tile_of_step(i)), 0, 0))


def _stage_rows(i, n):
    return pl.ds(pl.multiple_of(i * n, n), n)


def _inproj_kernel(*refs, n_x):
    x_refs = refs[:n_x]
    mod_ref, g_ref, w_ref, zac_ref, zbd_ref, h_ref, w_s = refs[n_x:]
    i = pl.program_id(0)

    @pl.when(i < NPRO)
    def _():
        w_s[_stage_rows(i, SMALL_W // NPRO), :] = w_ref[...].astype(BF)

    @pl.when(i >= NPRO)
    def _():
        m = mod_ref[...]
        h = _rms(_tok_load(i - NPRO, x_refs), g_ref[...]) * (1.0 + m[1:2]) + m[0:1]
        hb = h.astype(BF)
        h_ref[...] = hb
        zac_ref[...] = _dot_nt(hb, w_s[:AC_W, :])
        zbd_ref[...] = _dot_nt(hb, w_s[AC_W:, :])


def _inproj(l, xs, mod, norm_g, w_in_t):
    split = len(xs) == 2
    return pl.pallas_call(
        functools.partial(_inproj_kernel, n_x=len(xs)),
        grid=(NPRO + N_TILES,),
        in_specs=_tok_specs(D_MODEL, split) + [
            _mod_spec(l),
            pl.BlockSpec((None, 1, D_MODEL), lambda i: (l, 0, 0)),
            pl.BlockSpec((None, SMALL_W // NPRO, D_MODEL), lambda i: (l, _chunk_of_step(i), 0)),
        ],
        out_specs=[
            pl.BlockSpec((TD, AC_W), lambda i: (_tile_of_step(i), 0)),
            pl.BlockSpec((TD, BD_PAD), lambda i: (_tile_of_step(i), 0)),
            pl.BlockSpec((TD, D_MODEL), lambda i: (_tile_of_step(i), 0)),
        ],
        out_shape=[
            jax.ShapeDtypeStruct((N_TOK, AC_W), F32),
            jax.ShapeDtypeStruct((N_TOK, BD_PAD), F32),
            jax.ShapeDtypeStruct((N_TOK, D_MODEL), BF),
        ],
        scratch_shapes=[pltpu.VMEM((SMALL_W, D_MODEL), BF)],
        compiler_params=_params(("arbitrary",)),
        name=f"inproj{l}",
    )(*xs, mod, norm_g, w_in_t)


def _rope_tables(n_tokens, head_dim, n_rep):
    rows = n_tokens // GRID_W
    row = np.repeat(np.arange(rows), GRID_W).astype(np.float64)
    col = np.tile(np.arange(GRID_W), rows).astype(np.float64)
    half = head_dim // 2
    q4 = head_dim // 4
    freqs = ROPE_THETA ** (-np.arange(0, half, 2, dtype=np.float64) / half)
    ang_r = row[:, None] * freqs
    ang_c = col[:, None] * freqs
    ang = np.concatenate([ang_r, ang_r, ang_c, ang_c], axis=-1)
    cos, sin = np.cos(ang), np.sin(ang)
    first = (np.arange(head_dim) % (2 * q4)) < q4
    s_dn = np.where(first, -sin, 0.0)
    s_up = np.where(first, 0.0, sin)
    return tuple(jnp.asarray(np.tile(t, (1, n_rep)), dtype=F32) for t in (cos, s_dn, s_up))


def _rope(x, cos, s_dn, s_up, q4):
    w = x.shape[-1]
    return x * cos + pltpu.roll(x, w - q4, 1) * s_dn + pltpu.roll(x, q4, 1) * s_up


def _softmax_pv_group(maps, outs):
    scores = [[_dot(q, k) if t else _dot_nt(q, k) for k, t in zip(ks, fm)] for q, ks, _, _, fm in maps]
    yield
    probs = []
    for ss in scores:
        m = ss[0].max(axis=-1, keepdims=True)
        for s in ss[1:]:
            m = jnp.maximum(m, s.max(axis=-1, keepdims=True))
        probs.append([jnp.exp2(s - m).astype(BF) for s in ss])
    yield
    for ps, (_, _, vexts, half, fm) in zip(probs, maps):
        o = None
        for p, v, t in zip(ps, vexts, fm):
            part = _dot_nt(p, v) if t else _dot(p, v)
            o = part if o is None else o + part
        outs.append(o[:, half * 64:(half + 1) * 64] / o[:, (1 - half) * 64:(1 - half) * 64 + 1])
    yield


def _with_ones(v, half, feature_major):
    idx = lax.broadcasted_iota(jnp.int32, v.shape, 0 if feature_major else 1)
    return jnp.where(lax.shift_right_logical(idx, 6) == half, v, jnp.ones_like(v))


def _attend_heads(aq, cq, ka, va, kc, vc, fm, lam, gsub, lam_init, y_ref, group):
    aqb = (aq * (A_HEAD_DIM ** -0.5 * LOG2E)).astype(BF)
    cqb = (cq * (C_HEAD_DIM ** -0.5 * LOG2E)).astype(BF)
    rep = A_HEADS // A_KV_HEADS

    def feat(x, t, sl):
        return x[sl, :] if t else x[:, sl]

    maps = []
    for h in range(A_HEADS):
        g = h // rep
        sl = slice(g * A_HEAD_DIM, (g + 1) * A_HEAD_DIM)
        maps.append((aqb[:, h * A_HEAD_DIM:(h + 1) * A_HEAD_DIM], [feat(k, t, sl) for k, t in zip(ka, fm)],
                     [_with_ones(v, g, t) for v, t in zip(va, fm)], g, fm))
    for h in range(C_HEADS):
        slab = slice((h // 2) * 128, (h // 2 + 1) * 128)
        vh = [_with_ones(feat(v, t, slab), h % 2, t) for v, t in zip(vc, fm)]
        for j in range(2):
            sl = slice((2 * h + j) * C_HEAD_DIM, (2 * h + j + 1) * C_HEAD_DIM)
            maps.append((cqb[:, sl], [feat(k, t, sl) for k, t in zip(kc, fm)], vh, h % 2, fm))
    outs = []
    for i in range(0, len(maps), group):
        yield from _softmax_pv_group(maps[i:i + group], outs)
    for h in range(A_HEADS):
        y_ref[:, h * A_HEAD_DIM:(h + 1) * A_HEAD_DIM] = outs[h]
    vd = 2 * C_HEAD_DIM
    for h in range(C_HEADS):
        d = outs[A_HEADS + 2 * h] - lam * outs[A_HEADS + 2 * h + 1]
        y_ref[:, BRANCH_W + h * vd:BRANCH_W + (h + 1) * vd] = _rms(d, gsub) * (1.0 - lam_init)


def _lambda(cl):
    s1 = jnp.sum(cl[0:1] * cl[1:2], axis=-1, keepdims=True)
    s2 = jnp.sum(cl[2:3] * cl[3:4], axis=-1, keepdims=True)
    return jnp.exp(s1) - jnp.exp(s2)


def _attn_prompt_kernel(z_ref, gaq, gak, gcq, gck, gsub, cl_ref, y_ref, oak, oav, ock, ocv, *, lam_init):
    z = z_ref[...]
    ak = _head_rms(z[:, 256:384], A_HEAD_DIM, gak[...])
    av = z[:, 384:512]
    ck = _head_rms(z[:, 768:1024], C_HEAD_DIM, gck[...])
    cv = z[:, 1024:1280]
    oak[...] = ak.T
    oav[...] = av.T
    ock[...] = ck.T
    ocv[...] = cv.T
    aq = _head_rms(z[:, 0:256], A_HEAD_DIM, gaq[...])
    cq = _head_rms(z[:, 512:768], C_HEAD_DIM, gcq[...])
    lam = _lambda(cl_ref[...]) + lam_init
    yield
    yield from _attend_heads(aq, cq, [ak.astype(BF)], [av.astype(BF)], [ck.astype(BF)], [cv.astype(BF)], [False],
                             lam, gsub[...], lam_init, y_ref, group=A_HEADS + 2 * C_HEADS)


def _attn_sample_kernel(z_ref, gaq, gak, gcq, gck, gsub, cl_ref, cak, cav, cck, ccv,
                        cosa, sda, sua, cosc, sdc, suc, y_ref, ka_s, va_s, kc_s, vc_s, *, lam_init):
    qi = pl.program_id(1)
    qa4, qc4 = A_HEAD_DIM // 4, C_HEAD_DIM // 4

    @pl.when(qi == 0)
    def _():
        ak = _head_rms(z_ref[:, 256:384], A_HEAD_DIM, gak[...])
        ka_s[...] = _rope(ak, cosa[:, :128], sda[:, :128], sua[:, :128], qa4).astype(BF)
        va_s[...] = z_ref[:, 384:512].astype(BF)
        ck = _head_rms(z_ref[:, 768:1024], C_HEAD_DIM, gck[...])
        kc_s[...] = _rope(ck, cosc[...], sdc[...], suc[...], qc4).astype(BF)
        vc_s[...] = z_ref[:, 1024:1280].astype(BF)

    rows = pl.ds(pl.multiple_of(qi * TQ, TQ), TQ)
    aq = _head_rms(z_ref[rows, 0:256], A_HEAD_DIM, gaq[...])
    aq = _rope(aq, cosa[rows, :], sda[rows, :], sua[rows, :], qa4)
    cq = _head_rms(z_ref[rows, 512:768], C_HEAD_DIM, gcq[...])
    cq = _rope(cq, cosc[rows, :], sdc[rows, :], suc[rows, :], qc4)
    lam = _lambda(cl_ref[...]) + lam_init
    yield from _attend_heads(aq, cq,
                             [cak[...].astype(BF), ka_s[...]], [cav[...].astype(BF), va_s[...]],
                             [cck[...].astype(BF), kc_s[...]], [ccv[...].astype(BF), vc_s[...]], [True, False],
                             lam, gsub[...], lam_init, y_ref, group=4)


def _gain_specs(l, nd):
    zeros = (0,) * (nd - 1)
    widths = (256, 128, 256, 256, 64)
    return [pl.BlockSpec((None, 1, w), lambda *a: (l, 0, 0)) for w in widths] + \
           [pl.BlockSpec((None, 4, C_HEAD_DIM), lambda *a: (l, 0, 0))]


N_VTILES = BRANCH_W // 128


def _state_blocks(nh, kd, vd):
    per = 128 // vd
    return [(h // per, slice((h * kd) % 128, (h * kd) % 128 + kd), slice((h % per) * vd, (h % per + 1) * vd))
            for h in range(nh)]


_B_BLOCKS = _state_blocks(B_HEADS, B_KEY_DIM, B_VAL_DIM)
_D_BLOCKS = _state_blocks(D_HEADS, D_KEY_DIM, D_VAL_DIM)


def _gla_constants():
    idx = np.arange(TT)
    scans, masks = [], []
    for rev in (False, True):
        eff = (TT - 1 - idx) if rev else idx
        et, eu = eff[:, None], eff[None, :]
        sc, mk = [], []
        for j in range(N_LEVELS):
            b = 1 << j
            start = et - et % b
            odd = (et // b) % 2 == 1
            sc.append(np.where(odd, (eu > start) & (eu <= et), (eu > et) & (eu <= start + b)))
            mk.append(((et // b) % 2 == 1) & (eu // b == et // b - 1))
        sc.append(eu <= et)
        sc.append(eu > et)
        mk.append(eu == et)
        scans.append(np.concatenate([np.concatenate(sc, axis=0)] * 2, axis=-1))
        masks.append(np.stack(mk))
    return tuple(jnp.asarray(np.stack(t), BF) for t in (scans, masks))


def _gla_prepare(q, k, v, la2, scan_ref, d, rev, use_state):
    tt = q.shape[0]
    la_hi = la2.astype(BF)
    la_lo = (la2 - la_hi.astype(F32)).astype(BF)
    la_split = jnp.concatenate([la_hi, la_lo], axis=0)

    sums = _dot(scan_ref[d], la_split)

    def factor(i):
        return jnp.exp2(sums[i * tt:(i + 1) * tt])

    qs, ks = [], []
    for j in range(N_LEVELS):
        f = factor(j)
        qs.append((q * f).astype(BF))
        ks.append((k * f).astype(BF).T)
    qs.append(q.astype(BF))
    ks.append(k.astype(BF).T)
    k_out = (k * factor(N_LEVELS + 1)).astype(BF)
    vb = v.astype(BF)
    q_in = d_tile = None
    if use_state:
        q_in = (q * factor(N_LEVELS)).astype(BF)
        ones = jnp.ones((2 * tt, 128), BF)
        d_tile = [jnp.exp2(_dot_tn(la_split[:, i * 128:(i + 1) * 128], ones)) for i in range(q.shape[1] // 128)]

    return dict(qs=qs, ks=ks, k_out=k_out, vb=vb, q_in=q_in, d_tile=d_tile)


def _lane_keep(x, lo, hi):
    lane = lax.broadcasted_iota(jnp.int32, x.shape, 1)
    return jnp.where((lane >= lo) & (lane < hi), x, jnp.zeros_like(x))


def _level_rows(j, rev):
    b = 1 << j
    if j == N_LEVELS or b < SCORE_ROWS:
        return None
    return [(i * b, (i + 1) * b) for i in range(TT // b) if (i % 2 == 1) != rev]


def _gla_scores(p, mask_ref, d, rev, nh, kd):
    assert nh % 2 == 0 and 128 % (2 * kd) == 0
    out = []
    for h0 in range(0, nh, 2):
        c = (h0 * kd) // 128
        blocks = [[None] * (TT // SCORE_ROWS) for _ in range(2)]

        def add(r0, val):
            for i in range(val[0].shape[0] // SCORE_ROWS):
                k = r0 // SCORE_ROWS + i
                for n in range(2):
                    piece = val[n][i * SCORE_ROWS:(i + 1) * SCORE_ROWS]
                    blocks[n][k] = piece if blocks[n][k] is None else blocks[n][k] + piece

        for j in range(N_LEVELS + 1):
            kt = p["ks"][j]
            khs = []
            for h in (h0, h0 + 1):
                lo = (h * kd) % 128
                pieces = [jnp.zeros((lo, kt.shape[1]), BF)] if lo else []
                pieces.append(kt[c * 128 + lo:c * 128 + lo + kd, :])
                if lo + kd < 128:
                    pieces.append(jnp.zeros((128 - lo - kd, kt.shape[1]), BF))
                khs.append(jnp.concatenate(pieces, axis=0))
            kh = jnp.concatenate(khs, axis=1)
            qj = p["qs"][j][:, c * 128:(c + 1) * 128]
            ranges = _level_rows(j, rev)
            if ranges is None:
                t = _dot(qj, kh).astype(BF)
                m = mask_ref[d, j]
                add(0, [t[:, :TT] * m, t[:, TT:] * m])
            else:
                q_rows = jnp.concatenate([qj[a:b] for a, b in ranges], axis=0) if len(ranges) > 1 else qj[ranges[0][0]:ranges[0][1]]
                m_rows = [mask_ref[d, j, a:b, :] for a, b in ranges]
                t = _dot(q_rows, kh).astype(BF)
                off = 0
                for (a, b), m in zip(ranges, m_rows):
                    add(a, [t[off:off + b - a, :TT] * m, t[off:off + b - a, TT:] * m])
                    off += b - a
        zero = jnp.zeros((SCORE_ROWS, TT), BF)
        for n in range(2):
            out.append(jnp.concatenate([z if z is not None else zero for z in blocks[n]], axis=0))
    return out


def _gla_outputs(p, scs, blk_ref, st_ref, d, nh, kd, vd, use_state, o_ref, rows, col0, accumulate):
    per = 128 // vd
    for c in range(nh // per):
        vt = p["vb"][:, c * 128:(c + 1) * 128]
        kt = (c * per * kd) // 128
        ktile = slice(kt * 128, (kt + 1) * 128)
        o = _dot(jnp.concatenate([scs[c * per + i] for i in range(per)], axis=1),
                 jnp.concatenate([_lane_keep(vt, i * vd, (i + 1) * vd) for i in range(per)], axis=0))
        kv = _dot_tn(p["k_out"][:, ktile], vt) * blk_ref[c]
        if use_state:
            st = st_ref[d, c]
            o = o + _dot(p["q_in"][:, ktile], st.astype(BF))
            st_ref[d, c] = st * p["d_tile"][kt] + kv
        else:
            st_ref[d, c] = kv
        osl = slice(col0 + c * 128, col0 + (c + 1) * 128)
        if accumulate:
            o_ref[rows, osl] += o
        else:
            o_ref[rows, osl] = o


def _gla_kernel(*refs, layer, n_tiles, has_state, pump=None):
    pump = pump or (lambda: None)
    if has_state:
        (z_ref, lbl_ref, aw_ref, ab_ref, bng, dng, scan_ref, mask_ref, blkb_ref, blkd_ref, sh_in, sd_in,
         y_ref, o_scr, sth, std) = refs
        for packed, raw, blocks in ((sth, sh_in, _B_BLOCKS), (std, sd_in, _D_BLOCKS)):
            packed[...] = jnp.zeros(packed.shape, F32)
            for dd in range(2):
                for h, (c, r, ln) in enumerate(blocks):
                    packed[dd, c, r, ln] = raw[dd, h]
    else:
        (z_ref, lbl_ref, aw_ref, ab_ref, bng, dng, scan_ref, mask_ref, blkb_ref, blkd_ref,
         y_ref, sh_out, sd_out, o_scr, sth, std) = refs

    gates = []
    for d in range(2):
        logits = [lbl_ref[d, i:i + 1, :] for i in range(DEPTH)]
        mx = functools.reduce(jnp.maximum, logits)
        ex = [jnp.exp(t - mx) for t in logits]
        den = functools.reduce(lambda a, b: a + b, ex)
        ps = [t / den for t in ex]
        lb = functools.reduce(lambda a, b: a + b, ps[:layer + 1]) - ps[0]
        gates.append((lb, jnp.log(jnp.maximum(lb, LB_FLOOR)), jnp.log1p(-lb)))

    o_scr[...] = jnp.zeros(o_scr.shape, F32)

    def tile(i, use_state):
        preps = []
        for d in range(2):
            rev = d == 1
            lb, log_lb, log_1m = gates[d]
            j = (n_tiles - 1 - i) if rev else i
            rows = pl.ds(j * TT if isinstance(j, int) else pl.multiple_of(j * TT, TT), TT)
            bq = z_ref[rows, 0:256]
            zf = z_ref[rows, 768:1024] if rev else z_ref[rows, 512:768]
            b2 = log_1m + _log_sigmoid(zf)
            la = jnp.maximum(log_lb, b2) + jnp.log1p(jnp.exp(-jnp.abs(log_lb - b2)))
            kb = (1.0 - lb) * jax.nn.sigmoid(-zf)
            pre = _dot(z_ref[rows, 2048:2176].astype(BF), aw_ref[d]) + ab_ref[d]
            la_d = _log_sigmoid(pre) * (LOG2E / D_GATE_TAU)
            pb = _gla_prepare(_silu(bq), kb, z_ref[rows, 256:512], la * LOG2E, scan_ref, d, rev, use_state)
            pd = _gla_prepare(z_ref[rows, 1280:1408] * (D_KEY_DIM ** -0.5), z_ref[rows, 1408:1536],
                              z_ref[rows, 1536:1792], la_d, scan_ref, d, rev, use_state)
            preps.append((d, rows, pb, pd))
        pump()
        scores = [(_gla_scores(pb, mask_ref, d, d == 1, B_HEADS, B_KEY_DIM),
                   _gla_scores(pd, mask_ref, d, d == 1, D_HEADS, D_KEY_DIM)) for d, _, pb, pd in preps]
        pump()
        for (d, rows, pb, pd), (sb, sd) in zip(preps, scores):
            _gla_outputs(pb, sb, blkb_ref, sth, d, B_HEADS, B_KEY_DIM, B_VAL_DIM, use_state, o_scr, rows, 0, True)
            _gla_outputs(pd, sd, blkd_ref, std, d, D_HEADS, D_KEY_DIM, D_VAL_DIM, use_state, o_scr, rows, 256, True)
        pump()

    if has_state:
        def body(i, carry):
            tile(i, True)
            return carry

        lax.fori_loop(0, n_tiles, body, 0)
    else:
        for i in range(n_tiles):
            tile(i, i > 0)

    def finish(i, carry):
        rows = pl.ds(pl.multiple_of(i * TM, TM), TM)
        y_ref[rows, 0:256] = _head_rms(o_scr[rows, 0:256], B_VAL_DIM, bng[...]) * _silu(z_ref[rows, 1024:1280])
        y_ref[rows, 256:512] = _head_rms(o_scr[rows, 256:512], D_VAL_DIM, dng[...]) * _silu(z_ref[rows, 1792:2048])
        return carry

    lax.fori_loop(0, (n_tiles * TT) // TM, finish, 0)
    if not has_state:
        for packed, raw, blocks in ((sth, sh_out, _B_BLOCKS), (std, sd_out, _D_BLOCKS)):
            for dd in range(2):
                for h, (c, r, ln) in enumerate(blocks):
                    raw[dd, h] = packed[dd, c, r, ln]


def _gla_common_specs(l, nd):
    return [
        pl.BlockSpec((2, DEPTH, 256), lambda *a: (0, 0, 0)),
        pl.BlockSpec((None, 2, 128, 128), lambda *a: (l, 0, 0, 0)),
        pl.BlockSpec((None, 2, 1, 128), lambda *a: (l, 0, 0, 0)),
        pl.BlockSpec((None, 1, 256), lambda *a: (l, 0, 0)),
        pl.BlockSpec((None, 1, 256), lambda *a: (l, 0, 0)),
        pl.BlockSpec((2, (N_LEVELS + 2) * TT, 2 * TT), lambda *a: (0, 0, 0)),
        pl.BlockSpec((2, N_LEVELS + 1, TT, TT), lambda *a: (0, 0, 0, 0)),
        pl.BlockSpec((N_VTILES, 128, 128), lambda *a: (0, 0, 0)),
        pl.BlockSpec((N_VTILES, 128, 128), lambda *a: (0, 0, 0)),
    ]


_STATE_SCRATCH = [pltpu.VMEM((2, N_VTILES, 128, 128), F32), pltpu.VMEM((2, N_VTILES, 128, 128), F32)]


def _state_pattern(blocks):
    pat = np.zeros((N_VTILES, 128, 128), np.float32)
    for c, r, ln in blocks:
        pat[c, r, ln] = 1.0
    return jnp.asarray(pat)


N_ATTN_IN = 7
N_ATTN_OUT = 5


def _ctx_mixers_kernel(*refs, layer, lam_init, n_carried):
    n_in = N_ATTN_IN + 1 + len(_gla_common_specs(0, 1))
    a_in, g_in, outs = refs[:N_ATTN_IN], refs[N_ATTN_IN:n_in], refs[n_in + n_carried:]
    y_ac, caches, y_bd, states = outs[0], outs[1:N_ATTN_OUT], outs[N_ATTN_OUT], outs[N_ATTN_OUT + 1:]
    attn = _attn_prompt_kernel(*a_in, y_ac, *caches, lam_init=lam_init)
    _gla_kernel(*g_in, y_bd, *states, layer=layer, n_tiles=SEQ // TT, has_state=False,
                pump=lambda: next(attn, None))
    for _ in attn:
        pass


def _ctx_mixers(l, zac, zbd, gains, c_lambda, small, lam_init, carried):
    stacked = [(BATCH, DEPTH, w, SEQ) for w in (128, 128, 256, 256)] + [
        (BATCH, DEPTH, 2, B_HEADS, B_KEY_DIM, B_VAL_DIM), (BATCH, DEPTH, 2, D_HEADS, D_KEY_DIM, D_VAL_DIM)]
    seq = lambda w: pl.BlockSpec((SEQ, w), lambda b: (b, 0))
    layer_block = lambda s: pl.BlockSpec((None, None) + s[2:], lambda b: (b, l) + (0,) * (len(s) - 2))
    in_specs = [seq(AC_W)] + _gain_specs(l, 1) + [seq(BD_PAD)] + _gla_common_specs(l, 1)
    n_carried = 0 if carried is None else len(carried)
    aliases = {}
    if carried is not None:
        out_idx = [1, 2, 3, 4, 6, 7]
        aliases = {len(in_specs) + k: out_idx[k] for k in range(n_carried)}
        in_specs = in_specs + [pl.BlockSpec(memory_space=pl.ANY)] * n_carried
    out = pl.pallas_call(
        functools.partial(_ctx_mixers_kernel, layer=l, lam_init=lam_init, n_carried=n_carried),
        grid=(BATCH,),
        in_specs=in_specs,
        out_specs=[seq(2 * BRANCH_W)] + [layer_block(s) for s in stacked[:4]] + [seq(2 * BRANCH_W)]
                  + [layer_block(s) for s in stacked[4:]],
        out_shape=[jax.ShapeDtypeStruct((N_PROMPT, 2 * BRANCH_W), F32)]
                  + [jax.ShapeDtypeStruct(s, F32) for s in stacked[:4]]
                  + [jax.ShapeDtypeStruct((N_PROMPT, 2 * BRANCH_W), F32)]
                  + [jax.ShapeDtypeStruct(s, F32) for s in stacked[4:]],
        scratch_shapes=[pltpu.VMEM((SEQ, 2 * BRANCH_W), F32)] + _STATE_SCRATCH,
        input_output_aliases=aliases,
        compiler_params=_params(("arbitrary",)),
        name=f"ctx_mixers{l}",
    )(zac, *gains, c_lambda, zbd, *small, *(carried or ()))
    return out[0], out[5], out[1:5] + out[6:]


def _attn_sample_body(*refs, lam_init):
    for _ in _attn_sample_kernel(*refs, lam_init=lam_init):
        pass


def _attn_sample(l, zac, gains, c_lambda, caches, tables, lam_init):
    first_blk = N_PROMPT // DEC_SEQ
    cache_specs = [pl.BlockSpec((None, None, w, PAST_LEN), lambda b, q: (b, l, 0, 0)) for w in (128, 128, 256, 256)]
    table_specs = [pl.BlockSpec((DEC_SEQ, 256), lambda b, q: (0, 0)) for _ in range(6)]
    return pl.pallas_call(
        functools.partial(_attn_sample_body, lam_init=lam_init),
        grid=(DEC_BATCH, DEC_SEQ // TQ),
        in_specs=[pl.BlockSpec((DEC_SEQ, AC_W), lambda b, q: (first_blk + b, 0))] + _gain_specs(l, 2)
                 + cache_specs + table_specs,
        out_specs=pl.BlockSpec((TQ, 2 * BRANCH_W), lambda b, q: (b * (DEC_SEQ // TQ) + q, 0)),
        out_shape=jax.ShapeDtypeStruct((N_SAMPLE, 2 * BRANCH_W), F32),
        scratch_shapes=[pltpu.VMEM((DEC_SEQ, 128), BF), pltpu.VMEM((DEC_SEQ, 128), BF),
                        pltpu.VMEM((DEC_SEQ, 256), BF), pltpu.VMEM((DEC_SEQ, 256), BF)],
        compiler_params=_params(("arbitrary", "arbitrary")),
        name=f"attn_sample{l}",
    )(zac, *gains, c_lambda, *caches, *tables)


def _gla_sample(l, zbd, small, st_h, st_d):
    first_blk = N_PROMPT // DEC_SEQ
    return pl.pallas_call(
        functools.partial(_gla_kernel, layer=l, n_tiles=DEC_SEQ // TT, has_state=True),
        grid=(DEC_BATCH,),
        in_specs=[pl.BlockSpec((DEC_SEQ, BD_PAD), lambda b: (first_blk + b, 0))] + _gla_common_specs(l, 1) + [
            pl.BlockSpec((None, None, 2, B_HEADS, B_KEY_DIM, B_VAL_DIM), lambda b: (b, l, 0, 0, 0, 0)),
            pl.BlockSpec((None, None, 2, D_HEADS, D_KEY_DIM, D_VAL_DIM), lambda b: (b, l, 0, 0, 0, 0)),
        ],
        out_specs=pl.BlockSpec((DEC_SEQ, 2 * BRANCH_W), lambda b: (b, 0)),
        out_shape=jax.ShapeDtypeStruct((N_SAMPLE, 2 * BRANCH_W), F32),
        scratch_shapes=[pltpu.VMEM((DEC_SEQ, 2 * BRANCH_W), F32)] + _STATE_SCRATCH,
        compiler_params=_params(("arbitrary",)),
        name=f"gla_sample{l}",
    )(zbd, *small, st_h, st_d)


def _mix_kernel(*refs, n_x):
    x_refs = refs[:n_x]
    (h_ref, yp_ref, ys_ref, gp_ref, gs_ref, mod_ref, win_ref, wb_ref, wo_ref, o_ref, wm_s, wb_s, wo_s) = refs[n_x:]
    i = pl.program_id(0)

    @pl.when(i < NPRO)
    def _():
        rows = _stage_rows(i, W_ROWS)
        wm_s[_stage_rows(i, MERGE_ROWS), :] = win_ref[...].astype(BF)
        wb_s[rows, :] = wb_ref[...].astype(BF)
        wo_s[rows, :] = wo_ref[...].astype(BF)

    @pl.when(i >= NPRO)
    def _():
        t = i - NPRO
        hb = h_ref[...]
        yac = _tok_load(t, (yp_ref, ys_ref))
        ybd = _tok_load(t, (gp_ref, gs_ref))
        branches = (yac[:, :BRANCH_W], ybd[:, :BRANCH_W], yac[:, BRANCH_W:], ybd[:, BRANCH_W:])
        mixed = None
        for n, y in enumerate(branches):
            logits = _dot_nt(hb, wm_s[n * D_MODEL:(n + 1) * D_MODEL, :])
            term = jax.nn.sigmoid(logits) * _dot(y.astype(BF), wb_s[n * BRANCH_W:(n + 1) * BRANCH_W, :])
            mixed = term if mixed is None else mixed + term
        o_ref[...] = _tok_load(t, x_refs) + mod_ref[2:3, :] * _dot(mixed.astype(BF), wo_s[...])


def _mix(l, xs, h, y_p, y_s, g_p, g_s, mod, w_in_t, w_branch, w_out):
    chunk = pl.BlockSpec((None, W_ROWS, D_MODEL), lambda i: (l, _chunk_of_step(i), 0))
    return pl.pallas_call(
        functools.partial(_mix_kernel, n_x=len(xs)),
        grid=(NPRO + N_TILES,),
        in_specs=_tok_specs(D_MODEL, len(xs) == 2) + _tok_specs(D_MODEL, False)
                 + _tok_specs(2 * BRANCH_W, True) + _tok_specs(2 * BRANCH_W, True)
                 + [_mod_spec(l),
                    pl.BlockSpec((None, MERGE_ROWS, D_MODEL), lambda i: (l, MERGE_OFF // MERGE_ROWS + _chunk_of_step(i), 0)),
                    chunk, chunk],
        out_specs=pl.BlockSpec((TD, D_MODEL), lambda i: (_tile_of_step(i), 0)),
        out_shape=jax.ShapeDtypeStruct((N_TOK, D_MODEL), F32),
        scratch_shapes=[pltpu.VMEM((NPRO * MERGE_ROWS, D_MODEL), BF), pltpu.VMEM((D_MODEL, D_MODEL), BF),
                        pltpu.VMEM((D_MODEL, D_MODEL), BF)],
        compiler_params=_params(("arbitrary",)),
        name=f"mix{l}",
    )(*xs, h, y_p, y_s, g_p, g_s, mod, w_in_t, w_branch.reshape(DEPTH, N_BRANCH * BRANCH_W, D_MODEL), w_out)


def _ffn_kernel(*refs, n_out):
    (xp_ref, x_ref, xn_ref, mod_ref, g_ref, wup_ref, cw_ref, cb_ref, wdn_ref) = refs[:9]
    o_refs = refs[9:9 + n_out]
    hext, wup_s, wdn_s = refs[9 + n_out:12 + n_out]
    i = pl.program_id(0)

    @pl.when(i < NPRO)
    def _():
        wup_s[_stage_rows(i, W_ROWS), :] = wup_ref[...].astype(BF)
        wdn_s[_stage_rows(i, D_FF // NPRO), :] = wdn_ref[...].astype(BF)

    if n_out == 2:
        keep_s = refs[12 + n_out]

        @pl.when(i == 0)
        def _():
            keep_s[...] = jnp.zeros(keep_s.shape, F32)

    @pl.when(i >= NPRO)
    def _():
        t = i - NPRO
        ctx = t < PROMPT_TILES
        pos = jnp.maximum(t - PROMPT_TILES, 0) % SAMPLE_TILES_PER_SEQ
        seq_first = ctx | (pos == 0)
        seq_last = ctx | (pos == SAMPLE_TILES_PER_SEQ - 1)
        m = mod_ref[...]
        g = g_ref[...]

        def pre(x):
            return _rms(x, g) * (1.0 + m[4:5]) + m[3:4]

        sub = lax.broadcasted_iota(jnp.int32, (HALO, D_MODEL), 0)
        hext[0:TD, :] = pre(x_ref[...])
        hext[TD:TD + HALO, :] = jnp.where(seq_last | (sub != 0), 0.0, pre(xn_ref[...]))
        hext[TD + HALO:, :] = jnp.where(seq_first | (sub != HALO - 1), 0.0, pre(xp_ref[...]))
        hb = hext[...].astype(BF)

        n_tile, mid = TD // 8, SEQ // 8
        sub8 = lax.broadcasted_iota(jnp.int32, (1, 8, FF_CHUNK), 1)

        def up_conv_act(c):
            halves = []
            for off in (0, D_FF):
                cols = slice(off + c * FF_CHUNK, off + (c + 1) * FF_CHUNK)
                u = _dot(hb, wup_s[:, cols]).reshape(FF_EXT // 8, 8, FF_CHUNK)
                down = pltpu.roll(u, 1, axis=1)
                up = pltpu.roll(u, 7, axis=1)
                before = jnp.concatenate([down[-1:], down[:mid - 1], jnp.where(ctx, 0.0, down[mid - 1:mid]),
                                          down[mid:n_tile - 1]], axis=0)
                after = jnp.concatenate([up[1:mid], jnp.where(ctx, 0.0, up[mid:mid + 1]), up[mid + 1:n_tile + 1]],
                                        axis=0)
                prev = jnp.where(sub8 == 0, before, down[:n_tile])
                nxt = jnp.where(sub8 == 7, after, up[:n_tile])
                conv = (prev * cw_ref[0:1, cols] + u[:n_tile] * cw_ref[1:2, cols] + nxt * cw_ref[2:3, cols]
                        + cb_ref[:, cols])
                halves.append(conv.reshape(TD, FF_CHUNK))
            return (_silu(halves[1]) * halves[0]).astype(BF)

        n_chunks = D_FF // FF_CHUNK
        acc = _dot(jnp.concatenate([up_conv_act(c) for c in range(n_chunks)], axis=1), wdn_s[...])
        gate = m[5:6]
        res = x_ref[...] + gate * acc
        if n_out == 1:
            o_refs[0][...] = res
        else:
            kept = jnp.where(ctx, res, keep_s[...])
            keep_s[...] = kept
            o_refs[0][...] = kept
            o_refs[1][...] = res


def _ffn(l, x, mod, norm_g, w_up, conv_w, conv_b, w_down, split_out):
    per = TD // HALO
    last_blk = N_TOK // HALO - 1
    if split_out:
        out_specs = [pl.BlockSpec((TD, D_MODEL), lambda i: (jnp.minimum(_tile_of_step(i), PROMPT_TILES - 1), 0)),
                     pl.BlockSpec((TD, D_MODEL), lambda i: (jnp.maximum(_tile_of_step(i) - PROMPT_TILES, 0), 0))]
        out_shape = [jax.ShapeDtypeStruct((N_PROMPT, D_MODEL), F32), jax.ShapeDtypeStruct((N_SAMPLE, D_MODEL), F32)]
    else:
        out_specs = [pl.BlockSpec((TD, D_MODEL), lambda i: (_tile_of_step(i), 0))]
        out_shape = [jax.ShapeDtypeStruct((N_TOK, D_MODEL), F32)]
    return pl.pallas_call(
        functools.partial(_ffn_kernel, n_out=len(out_specs)),
        grid=(NPRO + N_TILES,),
        in_specs=[
            pl.BlockSpec((HALO, D_MODEL), lambda i: (jnp.maximum(_tile_of_step(i) * per - 1, 0), 0)),
            pl.BlockSpec((TD, D_MODEL), lambda i: (_tile_of_step(i), 0)),
            pl.BlockSpec((HALO, D_MODEL), lambda i: (jnp.minimum((_tile_of_step(i) + 1) * per, last_blk), 0)),
            _mod_spec(l),
            pl.BlockSpec((None, 1, D_MODEL), lambda i: (l, 0, 0)),
            pl.BlockSpec((None, W_ROWS, 2 * D_FF), lambda i: (l, _chunk_of_step(i), 0)),
            pl.BlockSpec((None, CONV_WIDTH, 2 * D_FF), lambda i: (l, 0, 0)),
            pl.BlockSpec((None, 1, 2 * D_FF), lambda i: (l, 0, 0)),
            pl.BlockSpec((None, D_FF // NPRO, D_MODEL), lambda i: (l, _chunk_of_step(i), 0)),
        ],
        out_specs=out_specs,
        out_shape=out_shape,
        scratch_shapes=[pltpu.VMEM((FF_EXT, D_MODEL), F32),
                        pltpu.VMEM((D_MODEL, 2 * D_FF), BF), pltpu.VMEM((D_FF, D_MODEL), BF)]
                       + ([pltpu.VMEM((TD, D_MODEL), F32)] if split_out else []),
        compiler_params=_params(("arbitrary",)),
        name=f"ffn{l}",
    )(x, x, x, mod, norm_g, w_up, conv_w, conv_b, w_down)


def kernel(x_prompt, x_sample, cache_a_k, cache_a_v, cache_c_k, cache_c_v, state_hgrn, state_gla, c, c_ctx, w_ada, b_ada, norm1_g, norm2_g, w_in, a_qn_g, a_kn_g, c_qn_g, c_kn_g, c_lambda, c_subln_g, b_lb_logits, b_norm_g, d_alpha_w, d_alpha_b, d_norm_g, w_branch, w_out, w_up, conv_w, conv_b, w_down):
    xs = (x_prompt.reshape(N_PROMPT, D_MODEL), x_sample.reshape(N_SAMPLE, D_MODEL))
    w_in_t = jnp.swapaxes(w_in, 1, 2)

    cond8 = jnp.concatenate([c_ctx[None, :], c, jnp.zeros((8 - N_GROUPS, D_MODEL), F32)], axis=0)
    mod = _modulation(cond8, w_ada, b_ada)[:, :N_GROUPS].reshape(DEPTH, N_GROUPS, N_MOD, D_MODEL)

    tile4 = lambda g, n: jnp.tile(g, (1, n)).reshape(DEPTH, 1, -1)
    gains = (tile4(a_qn_g, A_HEADS), tile4(a_kn_g, A_KV_HEADS), tile4(c_qn_g, 2 * C_HEADS), tile4(c_kn_g, 2 * C_HEADS),
             c_subln_g.reshape(DEPTH, 1, 2 * C_HEAD_DIM))
    fmaj = lambda t: jnp.moveaxis(t, 2, -1).reshape(DEC_BATCH, DEPTH, -1, PAST_LEN)
    caches = tuple(fmaj(t) for t in (cache_a_k, cache_a_v, cache_c_k, cache_c_v))
    tables = _rope_tables(DEC_SEQ, A_HEAD_DIM, A_HEADS) + _rope_tables(DEC_SEQ, C_HEAD_DIM, 2 * C_HEADS)

    aw = jnp.zeros((DEPTH, 2, 128, 128), F32)
    aw = aw.at[:, 0, 0:D_GATE_RANK].set(d_alpha_w[:, 0]).at[:, 1, D_GATE_RANK:2 * D_GATE_RANK].set(d_alpha_w[:, 1])
    gla_small = (b_lb_logits, aw.astype(BF), d_alpha_b.reshape(DEPTH, 2, 1, 128),
                 tile4(b_norm_g, B_HEADS), tile4(d_norm_g, D_HEADS)) + _gla_constants() + (
                     _state_pattern(_B_BLOCKS), _state_pattern(_D_BLOCKS))

    n1 = norm1_g.reshape(DEPTH, 1, D_MODEL)
    n2 = norm2_g.reshape(DEPTH, 1, D_MODEL)
    cb = conv_b.reshape(DEPTH, 1, 2 * D_FF)

    carried = None
    for l in range(DEPTH):
        lam_init = 0.8 - 0.6 * math.exp(-0.3 * l)
        zac, zbd, h = _inproj(l, xs, mod, n1, w_in_t)
        y_p, g_p, carried = _ctx_mixers(l, zac, zbd, gains, c_lambda, gla_small, lam_init, carried)
        y_s = _attn_sample(l, zac, gains, c_lambda, caches, tables, lam_init)
        g_s = _gla_sample(l, zbd, gla_small, state_hgrn, state_gla)
        x1 = _mix(l, xs, h, y_p, y_s, g_p, g_s, mod, w_in_t, w_branch, w_out)
        xs = tuple(_ffn(l, x1, mod, n2, w_up, conv_w, cb, w_down, split_out=(l == DEPTH - 1)))

    y_prompt = xs[0].reshape(BATCH, SEQ, D_MODEL)
    y_sample = xs[1].reshape(DEC_BATCH, DEC_SEQ, D_MODEL)
    stacked = list(carried)
    feat_shapes = ((A_KV_HEADS, A_HEAD_DIM), (A_KV_HEADS, A_HEAD_DIM), (C_HEADS, 2, C_HEAD_DIM), (C_HEADS, 2 * C_HEAD_DIM))
    ctx = [jnp.moveaxis(t.reshape((BATCH, DEPTH) + fs + (SEQ,)), -1, 2) for t, fs in zip(stacked[:4], feat_shapes)]
    return (y_prompt, y_sample) + tuple(ctx) + tuple(stacked[4:])
```

```python
import functools
import math

import numpy as np
import jax
import jax.numpy as jnp
from jax import lax
from jax.experimental import pallas as pl
from jax.experimental.pallas import tpu as pltpu

F32 = jnp.float32
BF = jnp.bfloat16

D_MODEL = 1024
BATCH = 16
SEQ = 256
DEPTH = 2
DEC_BATCH = 2
DEC_SEQ = 1024
PAST_LEN = 512
GRID_W = 64
ROPE_THETA = 10000.0
EPS = 1e-6
LB_FLOOR = 1e-30
N_MOD = 6
N_BRANCH = 4
BRANCH_W = 256
A_HEADS, A_KV_HEADS, A_HEAD_DIM = 4, 2, 64
C_HEADS, C_HEAD_DIM = 4, 32
B_HEADS, B_KEY_DIM, B_VAL_DIM = 4, 64, 64
D_HEADS, D_KEY_DIM, D_VAL_DIM = 4, 32, 64
D_GATE_RANK = 16
D_GATE_TAU = 16.0
D_FF = 2816
CONV_WIDTH = 3

N_PROMPT = BATCH * SEQ
N_SAMPLE = DEC_BATCH * DEC_SEQ
N_TOK = N_PROMPT + N_SAMPLE
TM = 256
TQ = 512
TD = 2 * SEQ
N_TILES = N_TOK // TD
PROMPT_TILES = N_PROMPT // TD
SAMPLE_TILES_PER_SEQ = DEC_SEQ // TD
N_GROUPS = 1 + DEC_BATCH

AC_W = 1280
BD_W = 2080
BD_PAD = 2176
SMALL_W = AC_W + BD_PAD
MERGE_OFF = AC_W + BD_W
MERGE_ROWS = 560
HALO = 8
FF_EXT = TD + 2 * HALO
FF_CHUNK = 256
NPRO = 8
W_ROWS = D_MODEL // NPRO
LOG2E = 1.4426950408889634
VMEM_LIMIT = 56 * 1024 * 1024
TT = 128
N_LEVELS = 7
SCORE_ROWS = 16


def _dot(a, b):
    return jnp.dot(a, b, preferred_element_type=F32)


def _dot_nt(a, b):
    return lax.dot_general(a, b, (((1,), (1,)), ((), ())), preferred_element_type=F32)


def _dot_tn(a, b):
    return lax.dot_general(a, b, (((0,), (0,)), ((), ())), preferred_element_type=F32)


def _silu(x):
    return x * jax.nn.sigmoid(x)


def _log_sigmoid(x):
    return jnp.minimum(x, 0.0) - jnp.log1p(jnp.exp(-jnp.abs(x)))


def _rms(x, g):
    return x * lax.rsqrt(jnp.mean(x * x, axis=-1, keepdims=True) + EPS) * g


def _head_rms(x, head_dim, g):
    w = x.shape[-1]
    sh = int(math.log2(head_dim))
    r = lax.shift_right_logical(lax.broadcasted_iota(jnp.int32, (w, w), 0), sh)
    c = lax.shift_right_logical(lax.broadcasted_iota(jnp.int32, (w, w), 1), sh)
    bd = jnp.where(r == c, 1.0 / head_dim, 0.0).astype(BF)
    x2 = x * x
    hi = x2.astype(BF)
    lo = (x2 - hi.astype(F32)).astype(BF)
    ms = _dot(hi, bd) + _dot(lo, bd)
    return x * lax.rsqrt(ms + EPS) * g


def _group_of_tile(i):
    return jnp.where(i < PROMPT_TILES, 0, 1 + jnp.maximum(i - PROMPT_TILES, 0) // SAMPLE_TILES_PER_SEQ)


def _params(sem):
    return pltpu.CompilerParams(dimension_semantics=sem, vmem_limit_bytes=VMEM_LIMIT)


def _mod_kernel(cond_ref, w_ref, b_ref, o_ref):
    s = _silu(cond_ref[...])
    o_ref[...] = _dot(s.astype(BF), w_ref[...].astype(BF)) + b_ref[...]


def _modulation(cond8, w_ada, b_ada):
    nb = 1536
    return pl.pallas_call(
        _mod_kernel,
        grid=(DEPTH, N_MOD * D_MODEL // nb),
        in_specs=[
            pl.BlockSpec((8, D_MODEL), lambda l, j: (0, 0)),
            pl.BlockSpec((None, D_MODEL, nb), lambda l, j: (l, 0, j)),
            pl.BlockSpec((None, 1, nb), lambda l, j: (l, 0, j)),
        ],
        out_specs=pl.BlockSpec((None, 8, nb), lambda l, j: (l, 0, j)),
        out_shape=jax.ShapeDtypeStruct((DEPTH, 8, N_MOD * D_MODEL), F32),
        compiler_params=_params(("arbitrary", "arbitrary")),
        name="modulation",
    )(cond8, w_ada, b_ada.reshape(DEPTH, 1, N_MOD * D_MODEL))


def _tile_of_step(i):
    return jnp.maximum(i - NPRO, 0)


def _chunk_of_step(i):
    return jnp.minimum(i, NPRO - 1)


def _tok_specs(width, split):
    if not split:
        return [pl.BlockSpec((TD, width), lambda i: (_tile_of_step(i), 0))]
    return [pl.BlockSpec((TD, width), lambda i: (jnp.minimum(_tile_of_step(i), PROMPT_TILES - 1), 0)),
            pl.BlockSpec((TD, width), lambda i: (jnp.maximum(_tile_of_step(i) - PROMPT_TILES, 0), 0))]


def _tok_load(t, refs):
    if len(refs) == 1:
        return refs[0][...]
    return jnp.where(t < PROMPT_TILES, refs[0][...], refs[1][...])


def _mod_spec(l):
    return pl.BlockSpec((None, None, N_MOD, D_MODEL), lambda i: (l, _group_of_tile(_tile_of_step(i)), 0, 0))


def _stage_rows(i, n):
    return pl.ds(pl.multiple_of(i * n, n), n)


def _inproj_kernel(*refs, n_x):
    x_refs = refs[:n_x]
    mod_ref, g_ref, w_ref, zac_ref, zbd_ref, h_ref, w_s = refs[n_x:]
    i = pl.program_id(0)

    @pl.when(i < NPRO)
    def _():
        w_s[_stage_rows(i, SMALL_W // NPRO), :] = w_ref[...].astype(BF)

    @pl.when(i >= NPRO)
    def _():
        m = mod_ref[...]
        h = _rms(_tok_load(i - NPRO, x_refs), g_ref[...]) * (1.0 + m[1:2]) + m[0:1]
        hb = h.astype(BF)
        h_ref[...] = hb
        zac_ref[...] = _dot_nt(hb, w_s[:AC_W, :])
        zbd_ref[...] = _dot_nt(hb, w_s[AC_W:, :])


def _inproj(l, xs, mod, norm_g, w_in_t):
    split = len(xs) == 2
    return pl.pallas_call(
        functools.partial(_inproj_kernel, n_x=len(xs)),
        grid=(NPRO + N_TILES,),
        in_specs=_tok_specs(D_MODEL, split) + [
            _mod_spec(l),
            pl.BlockSpec((None, 1, D_MODEL), lambda i: (l, 0, 0)),
            pl.BlockSpec((None, SMALL_W // NPRO, D_MODEL), lambda i: (l, _chunk_of_step(i), 0)),
        ],
        out_specs=[
            pl.BlockSpec((TD, AC_W), lambda i: (_tile_of_step(i), 0)),
            pl.BlockSpec((TD, BD_PAD), lambda i: (_tile_of_step(i), 0)),
            pl.BlockSpec((TD, D_MODEL), lambda i: (_tile_of_step(i), 0)),
        ],
        out_shape=[
            jax.ShapeDtypeStruct((N_TOK, AC_W), F32),
            jax.ShapeDtypeStruct((N_TOK, BD_PAD), F32),
            jax.ShapeDtypeStruct((N_TOK, D_MODEL), BF),
        ],
        scratch_shapes=[pltpu.VMEM((SMALL_W, D_MODEL), BF)],
        compiler_params=_params(("arbitrary",)),
        name=f"inproj{l}",
    )(*xs, mod, norm_g, w_in_t)


def _rope_tables(n_tokens, head_dim, n_rep):
    rows = n_tokens // GRID_W
    row = np.repeat(np.arange(rows), GRID_W).astype(np.float64)
    col = np.tile(np.arange(GRID_W), rows).astype(np.float64)
    half = head_dim // 2
    q4 = head_dim // 4
    freqs = ROPE_THETA ** (-np.arange(0, half, 2, dtype=np.float64) / half)
    ang_r = row[:, None] * freqs
    ang_c = col[:, None] * freqs
    ang = np.concatenate([ang_r, ang_r, ang_c, ang_c], axis=-1)
    cos, sin = np.cos(ang), np.sin(ang)
    first = (np.arange(head_dim) % (2 * q4)) < q4
    s_dn = np.where(first, -sin, 0.0)
    s_up = np.where(first, 0.0, sin)
    return tuple(jnp.asarray(np.tile(t, (1, n_rep)), dtype=F32) for t in (cos, s_dn, s_up))


def _rope(x, cos, s_dn, s_up, q4):
    w = x.shape[-1]
    return x * cos + pltpu.roll(x, w - q4, 1) * s_dn + pltpu.roll(x, q4, 1) * s_up


def _softmax_pv_group(maps, outs):
    scores = [[_dot(q, k) if t else _dot_nt(q, k) for k, t in zip(ks, fm)] for q, ks, _, _, fm in maps]
    yield
    probs = []
    for ss in scores:
        m = ss[0].max(axis=-1, keepdims=True)
        for s in ss[1:]:
            m = jnp.maximum(m, s.max(axis=-1, keepdims=True))
        probs.append([jnp.exp2(s - m).astype(BF) for s in ss])
    yield
    for ps, (_, _, vexts, half, fm) in zip(probs, maps):
        o = None
        for p, v, t in zip(ps, vexts, fm):
            part = _dot_nt(p, v) if t else _dot(p, v)
            o = part if o is None else o + part
        outs.append(o[:, half * 64:(half + 1) * 64] / o[:, (1 - half) * 64:(1 - half) * 64 + 1])
    yield


def _with_ones(v, half, feature_major):
    idx = lax.broadcasted_iota(jnp.int32, v.shape, 0 if feature_major else 1)
    return jnp.where(lax.shift_right_logical(idx, 6) == half, v, jnp.ones_like(v))


def _attend_heads(aq, cq, ka, va, kc, vc, fm, lam, gsub, lam_init, y_ref, group):
    aqb = (aq * (A_HEAD_DIM ** -0.5 * LOG2E)).astype(BF)
    cqb = (cq * (C_HEAD_DIM ** -0.5 * LOG2E)).astype(BF)
    rep = A_HEADS // A_KV_HEADS

    def feat(x, t, sl):
        return x[sl, :] if t else x[:, sl]

    maps = []
    for h in range(A_HEADS):
        g = h // rep
        sl = slice(g * A_HEAD_DIM, (g + 1) * A_HEAD_DIM)
        maps.append((aqb[:, h * A_HEAD_DIM:(h + 1) * A_HEAD_DIM], [feat(k, t, sl) for k, t in zip(ka, fm)],
                     [_with_ones(v, g, t) for v, t in zip(va, fm)], g, fm))
    for h in range(C_HEADS):
        slab = slice((h // 2) * 128, (h // 2 + 1) * 128)
        vh = [_with_ones(feat(v, t, slab), h % 2, t) for v, t in zip(vc, fm)]
        for j in range(2):
            sl = slice((2 * h + j) * C_HEAD_DIM, (2 * h + j + 1) * C_HEAD_DIM)
            maps.append((cqb[:, sl], [feat(k, t, sl) for k, t in zip(kc, fm)], vh, h % 2, fm))
    outs = []
    for i in range(0, len(maps), group):
        yield from _softmax_pv_group(maps[i:i + group], outs)
    for h in range(A_HEADS):
        y_ref[:, h * A_HEAD_DIM:(h + 1) * A_HEAD_DIM] = outs[h]
    vd = 2 * C_HEAD_DIM
    for h in range(C_HEADS):
        d = outs[A_HEADS + 2 * h] - lam * outs[A_HEADS + 2 * h + 1]
        y_ref[:, BRANCH_W + h * vd:BRANCH_W + (h + 1) * vd] = _rms(d, gsub) * (1.0 - lam_init)


def _lambda(cl):
    s1 = jnp.sum(cl[0:1] * cl[1:2], axis=-1, keepdims=True)
    s2 = jnp.sum(cl[2:3] * cl[3:4], axis=-1, keepdims=True)
    return jnp.exp(s1) - jnp.exp(s2)


def _attn_prompt_kernel(z_ref, gaq, gak, gcq, gck, gsub, cl_ref, y_ref, oak, oav, ock, ocv, *, lam_init):
    z = z_ref[...]
    ak = _head_rms(z[:, 256:384], A_HEAD_DIM, gak[...])
    av = z[:, 384:512]
    ck = _head_rms(z[:, 768:1024], C_HEAD_DIM, gck[...])
    cv = z[:, 1024:1280]
    oak[...] = ak.T
    oav[...] = av.T
    ock[...] = ck.T
    ocv[...] = cv.T
    aq = _head_rms(z[:, 0:256], A_HEAD_DIM, gaq[...])
    cq = _head_rms(z[:, 512:768], C_HEAD_DIM, gcq[...])
    lam = _lambda(cl_ref[...]) + lam_init
    yield
    yield from _attend_heads(aq, cq, [ak.astype(BF)], [av.astype(BF)], [ck.astype(BF)], [cv.astype(BF)], [False],
                             lam, gsub[...], lam_init, y_ref, group=A_HEADS + 2 * C_HEADS)


def _attn_sample_kernel(z_ref, gaq, gak, gcq, gck, gsub, cl_ref, cak, cav, cck, ccv,
                        cosa, sda, sua, cosc, sdc, suc, y_ref, ka_s, va_s, kc_s, vc_s, *, lam_init):
    qi = pl.program_id(1)
    qa4, qc4 = A_HEAD_DIM // 4, C_HEAD_DIM // 4

    @pl.when(qi == 0)
    def _():
        ak = _head_rms(z_ref[:, 256:384], A_HEAD_DIM, gak[...])
        ka_s[...] = _rope(ak, cosa[:, :128], sda[:, :128], sua[:, :128], qa4).astype(BF)
        va_s[...] = z_ref[:, 384:512].astype(BF)
        ck = _head_rms(z_ref[:, 768:1024], C_HEAD_DIM, gck[...])
        kc_s[...] = _rope(ck, cosc[...], sdc[...], suc[...], qc4).astype(BF)
        vc_s[...] = z_ref[:, 1024:1280].astype(BF)

    rows = pl.ds(pl.multiple_of(qi * TQ, TQ), TQ)
    aq = _head_rms(z_ref[rows, 0:256], A_HEAD_DIM, gaq[...])
    aq = _rope(aq, cosa[rows, :], sda[rows, :], sua[rows, :], qa4)
    cq = _head_rms(z_ref[rows, 512:768], C_HEAD_DIM, gcq[...])
    cq = _rope(cq, cosc[rows, :], sdc[rows, :], suc[rows, :], qc4)
    lam = _lambda(cl_ref[...]) + lam_init
    yield from _attend_heads(aq, cq,
                             [cak[...].astype(BF), ka_s[...]], [cav[...].astype(BF), va_s[...]],
                             [cck[...].astype(BF), kc_s[...]], [ccv[...].astype(BF), vc_s[...]], [True, False],
                             lam, gsub[...], lam_init, y_ref, group=4)


def _gain_specs(l, nd):
    zeros = (0,) * (nd - 1)
    widths = (256, 128, 256, 256, 64)
    return [pl.BlockSpec((None, 1, w), lambda *a: (l, 0, 0)) for w in widths] + \
           [pl.BlockSpec((None, 4, C_HEAD_DIM), lambda *a: (l, 0, 0))]


N_VTILES = BRANCH_W // 128


def _state_blocks(nh, kd, vd):
    per = 128 // vd
    return [(h // per, slice((h * kd) % 128, (h * kd) % 128 + kd), slice((h % per) * vd, (h % per + 1) * vd))
            for h in range(nh)]


_B_BLOCKS = _state_blocks(B_HEADS, B_KEY_DIM, B_VAL_DIM)
_D_BLOCKS = _state_blocks(D_HEADS, D_KEY_DIM, D_VAL_DIM)


def _gla_constants():
    idx = np.arange(TT)
    scans, masks = [], []
    for rev in (False, True):
        eff = (TT - 1 - idx) if rev else idx
        et, eu = eff[:, None], eff[None, :]
        sc, mk = [], []
        for j in range(N_LEVELS):
            b = 1 << j
            start = et - et % b
            odd = (et // b) % 2 == 1
            sc.append(np.where(odd, (eu > start) & (eu <= et), (eu > et) & (eu <= start + b)))
            mk.append(((et // b) % 2 == 1) & (eu // b == et // b - 1))
        sc.append(eu <= et)
        mk.append(eu == et)
        scans.append(np.concatenate([np.concatenate(sc, axis=0)] * 2, axis=-1))
        masks.append(np.stack(mk))
    return tuple(jnp.asarray(np.stack(t), BF) for t in (scans, masks))


def _gla_prepare(q, k, v, la2, scan_ref, d, rev, use_state):
    tt = q.shape[0]
    la_hi = la2.astype(BF)
    la_lo = (la2 - la_hi.astype(F32)).astype(BF)
    la_split = jnp.concatenate([la_hi, la_lo], axis=0)

    sums = _dot(scan_ref[d], la_split)

    def factor(i):
        return jnp.exp2(sums[i * tt:(i + 1) * tt])

    qs, ks = [], []
    for j in range(N_LEVELS):
        f = factor(j)
        qs.append((q * f).astype(BF))
        ks.append((k * f).astype(BF).T)
    qs.append(q.astype(BF))
    ks.append(k.astype(BF).T)
    prefix = sums[N_LEVELS * tt:(N_LEVELS + 1) * tt]
    end = 0 if rev else tt - 1
    k_out = (k * jnp.exp2(jnp.minimum(prefix[end:end + 1] - prefix, 0.0))).astype(BF)
    vb = v.astype(BF)
    q_in = d_tile = None
    if use_state:
        q_in = (q * factor(N_LEVELS)).astype(BF)
        ones = jnp.ones((2 * tt, 128), BF)
        d_tile = [jnp.exp2(_dot_tn(la_split[:, i * 128:(i + 1) * 128], ones)) for i in range(q.shape[1] // 128)]

    return dict(qs=qs, ks=ks, k_out=k_out, vb=vb, q_in=q_in, d_tile=d_tile)


def _lane_keep(x, lo, hi):
    lane = lax.broadcasted_iota(jnp.int32, x.shape, 1)
    return jnp.where((lane >= lo) & (lane < hi), x, jnp.zeros_like(x))


def _level_rows(j, rev):
    b = 1 << j
    if j == N_LEVELS or b < SCORE_ROWS:
        return None
    return [(i * b, (i + 1) * b) for i in range(TT // b) if (i % 2 == 1) != rev]


def _gla_scores(p, mask_ref, d, rev, nh, kd):
    assert nh % 2 == 0 and 128 % (2 * kd) == 0
    out = []
    for h0 in range(0, nh, 2):
        c = (h0 * kd) // 128
        blocks = [[None] * (TT // SCORE_ROWS) for _ in range(2)]

        def add(r0, val):
            for i in range(val[0].shape[0] // SCORE_ROWS):
                k = r0 // SCORE_ROWS + i
                for n in range(2):
                    piece = val[n][i * SCORE_ROWS:(i + 1) * SCORE_ROWS]
                    blocks[n][k] = piece if blocks[n][k] is None else blocks[n][k] + piece

        for j in range(N_LEVELS + 1):
            kt = p["ks"][j]
            khs = []
            for h in (h0, h0 + 1):
                lo = (h * kd) % 128
                pieces = [jnp.zeros((lo, kt.shape[1]), BF)] if lo else []
                pieces.append(kt[c * 128 + lo:c * 128 + lo + kd, :])
                if lo + kd < 128:
                    pieces.append(jnp.zeros((128 - lo - kd, kt.shape[1]), BF))
                khs.append(jnp.concatenate(pieces, axis=0))
            kh = jnp.concatenate(khs, axis=1)
            qj = p["qs"][j][:, c * 128:(c + 1) * 128]
            ranges = _level_rows(j, rev)
            if ranges is None:
                t = _dot(qj, kh).astype(BF)
                m = mask_ref[d, j]
                add(0, [t[:, :TT] * m, t[:, TT:] * m])
            else:
                q_rows = jnp.concatenate([qj[a:b] for a, b in ranges], axis=0) if len(ranges) > 1 else qj[ranges[0][0]:ranges[0][1]]
                m_rows = [mask_ref[d, j, a:b, :] for a, b in ranges]
                t = _dot(q_rows, kh).astype(BF)
                off = 0
                for (a, b), m in zip(ranges, m_rows):
                    add(a, [t[off:off + b - a, :TT] * m, t[off:off + b - a, TT:] * m])
                    off += b - a
        zero = jnp.zeros((SCORE_ROWS, TT), BF)
        for n in range(2):
            out.append(jnp.concatenate([z if z is not None else zero for z in blocks[n]], axis=0))
    return out


def _gla_outputs(p, scs, blk_ref, st_ref, d, nh, kd, vd, use_state, o_ref, rows, col0, accumulate):
    per = 128 // vd
    for c in range(nh // per):
        vt = p["vb"][:, c * 128:(c + 1) * 128]
        kt = (c * per * kd) // 128
        ktile = slice(kt * 128, (kt + 1) * 128)
        o = _dot(jnp.concatenate([scs[c * per + i] for i in range(per)], axis=1),
                 jnp.concatenate([_lane_keep(vt, i * vd, (i + 1) * vd) for i in range(per)], axis=0))
        kv = _dot_tn(p["k_out"][:, ktile], vt) * blk_ref[c]
        if use_state:
            st = st_ref[d, c]
            o = o + _dot(p["q_in"][:, ktile], st.astype(BF))
            st_ref[d, c] = st * p["d_tile"][kt] + kv
        else:
            st_ref[d, c] = kv
        osl = slice(col0 + c * 128, col0 + (c + 1) * 128)
        if accumulate:
            o_ref[rows, osl] += o
        else:
            o_ref[rows, osl] = o


def _gla_kernel(*refs, layer, n_tiles, has_state, pump=None):
    pump = pump or (lambda: None)
    if has_state:
        (z_ref, lbl_ref, aw_ref, ab_ref, bng, dng, scan_ref, mask_ref, blkb_ref, blkd_ref, sh_in, sd_in,
         y_ref, o_scr, sth, std) = refs
        for packed, raw, blocks in ((sth, sh_in, _B_BLOCKS), (std, sd_in, _D_BLOCKS)):
            packed[...] = jnp.zeros(packed.shape, F32)
            for dd in range(2):
                for h, (c, r, ln) in enumerate(blocks):
                    packed[dd, c, r, ln] = raw[dd, h]
    else:
        (z_ref, lbl_ref, aw_ref, ab_ref, bng, dng, scan_ref, mask_ref, blkb_ref, blkd_ref,
         y_ref, sh_out, sd_out, o_scr, sth, std) = refs

    gates = []
    for d in range(2):
        logits = [lbl_ref[d, i:i + 1, :] for i in range(DEPTH)]
        mx = functools.reduce(jnp.maximum, logits)
        ex = [jnp.exp(t - mx) for t in logits]
        den = functools.reduce(lambda a, b: a + b, ex)
        ps = [t / den for t in ex]
        lb = functools.reduce(lambda a, b: a + b, ps[:layer + 1]) - ps[0]
        gates.append((lb, jnp.log(jnp.maximum(lb, LB_FLOOR)), jnp.log1p(-lb)))

    o_scr[...] = jnp.zeros(o_scr.shape, F32)

    def tile(i, use_state):
        preps = []
        for d in range(2):
            rev = d == 1
            lb, log_lb, log_1m = gates[d]
            j = (n_tiles - 1 - i) if rev else i
            rows = pl.ds(j * TT if isinstance(j, int) else pl.multiple_of(j * TT, TT), TT)
            bq = z_ref[rows, 0:256]
            zf = z_ref[rows, 768:1024] if rev else z_ref[rows, 512:768]
            b2 = log_1m + _log_sigmoid(zf)
            la = jnp.maximum(log_lb, b2) + jnp.log1p(jnp.exp(-jnp.abs(log_lb - b2)))
            kb = (1.0 - lb) * jax.nn.sigmoid(-zf)
            pre = _dot(z_ref[rows, 2048:2176].astype(BF), aw_ref[d]) + ab_ref[d]
            la_d = _log_sigmoid(pre) * (LOG2E / D_GATE_TAU)
            pb = _gla_prepare(_silu(bq), kb, z_ref[rows, 256:512], la * LOG2E, scan_ref, d, rev, use_state)
            pd = _gla_prepare(z_ref[rows, 1280:1408] * (D_KEY_DIM ** -0.5), z_ref[rows, 1408:1536],
                              z_ref[rows, 1536:1792], la_d, scan_ref, d, rev, use_state)
            preps.append((d, rows, pb, pd))
        pump()
        scores = [(_gla_scores(pb, mask_ref, d, d == 1, B_HEADS, B_KEY_DIM),
                   _gla_scores(pd, mask_ref, d, d == 1, D_HEADS, D_KEY_DIM)) for d, _, pb, pd in preps]
        pump()
        for (d, rows, pb, pd), (sb, sd) in zip(preps, scores):
            _gla_outputs(pb, sb, blkb_ref, sth, d, B_HEADS, B_KEY_DIM, B_VAL_DIM, use_state, o_scr, rows, 0, True)
            _gla_outputs(pd, sd, blkd_ref, std, d, D_HEADS, D_KEY_DIM, D_VAL_DIM, use_state, o_scr, rows, 256, True)
        pump()

    if has_state:
        def body(i, carry):
            tile(i, True)
            return carry

        lax.fori_loop(0, n_tiles, body, 0)
    else:
        for i in range(n_tiles):
            tile(i, i > 0)

    def finish(i, carry):
        rows = pl.ds(pl.multiple_of(i * TM, TM), TM)
        y_ref[rows, 0:256] = _head_rms(o_scr[rows, 0:256], B_VAL_DIM, bng[...]) * _silu(z_ref[rows, 1024:1280])
        y_ref[rows, 256:512] = _head_rms(o_scr[rows, 256:512], D_VAL_DIM, dng[...]) * _silu(z_ref[rows, 1792:2048])
        return carry

    lax.fori_loop(0, (n_tiles * TT) // TM, finish, 0)
    if not has_state:
        for packed, raw, blocks in ((sth, sh_out, _B_BLOCKS), (std, sd_out, _D_BLOCKS)):
            for dd in range(2):
                for h, (c, r, ln) in enumerate(blocks):
                    raw[dd, h] = packed[dd, c, r, ln]


def _gla_common_specs(l, nd):
    return [
        pl.BlockSpec((2, DEPTH, 256), lambda *a: (0, 0, 0)),
        pl.BlockSpec((None, 2, 128, 128), lambda *a: (l, 0, 0, 0)),
        pl.BlockSpec((None, 2, 1, 128), lambda *a: (l, 0, 0, 0)),
        pl.BlockSpec((None, 1, 256), lambda *a: (l, 0, 0)),
        pl.BlockSpec((None, 1, 256), lambda *a: (l, 0, 0)),
        pl.BlockSpec((2, (N_LEVELS + 1) * TT, 2 * TT), lambda *a: (0, 0, 0)),
        pl.BlockSpec((2, N_LEVELS + 1, TT, TT), lambda *a: (0, 0, 0, 0)),
        pl.BlockSpec((N_VTILES, 128, 128), lambda *a: (0, 0, 0)),
        pl.BlockSpec((N_VTILES, 128, 128), lambda *a: (0, 0, 0)),
    ]


_STATE_SCRATCH = [pltpu.VMEM((2, N_VTILES, 128, 128), F32), pltpu.VMEM((2, N_VTILES, 128, 128), F32)]


def _state_pattern(blocks):
    pat = np.zeros((N_VTILES, 128, 128), np.float32)
    for c, r, ln in blocks:
        pat[c, r, ln] = 1.0
    return jnp.asarray(pat)


N_ATTN_IN = 7
N_ATTN_OUT = 5


def _ctx_mixers_kernel(*refs, layer, lam_init, n_carried):
    n_in = N_ATTN_IN + 1 + len(_gla_common_specs(0, 1))
    a_in, g_in, outs = refs[:N_ATTN_IN], refs[N_ATTN_IN:n_in], refs[n_in + n_carried:]
    y_ac, caches, y_bd, states = outs[0], outs[1:N_ATTN_OUT], outs[N_ATTN_OUT], outs[N_ATTN_OUT + 1:]
    attn = _attn_prompt_kernel(*a_in, y_ac, *caches, lam_init=lam_init)
    _gla_kernel(*g_in, y_bd, *states, layer=layer, n_tiles=SEQ // TT, has_state=False,
                pump=lambda: next(attn, None))
    for _ in attn:
        pass


def _ctx_mixers(l, zac, zbd, gains, c_lambda, small, lam_init, carried):
    stacked = [(BATCH, DEPTH, w, SEQ) for w in (128, 128, 256, 256)] + [
        (BATCH, DEPTH, 2, B_HEADS, B_KEY_DIM, B_VAL_DIM), (BATCH, DEPTH, 2, D_HEADS, D_KEY_DIM, D_VAL_DIM)]
    seq = lambda w: pl.BlockSpec((SEQ, w), lambda b: (b, 0))
    layer_block = lambda s: pl.BlockSpec((None, None) + s[2:], lambda b: (b, l) + (0,) * (len(s) - 2))
    in_specs = [seq(AC_W)] + _gain_specs(l, 1) + [seq(BD_PAD)] + _gla_common_specs(l, 1)
    n_carried = 0 if carried is None else len(carried)
    aliases = {}
    if carried is not None:
        out_idx = [1, 2, 3, 4, 6, 7]
        aliases = {len(in_specs) + k: out_idx[k] for k in range(n_carried)}
        in_specs = in_specs + [pl.BlockSpec(memory_space=pl.ANY)] * n_carried
    out = pl.pallas_call(
        functools.partial(_ctx_mixers_kernel, layer=l, lam_init=lam_init, n_carried=n_carried),
        grid=(BATCH,),
        in_specs=in_specs,
        out_specs=[seq(2 * BRANCH_W)] + [layer_block(s) for s in stacked[:4]] + [seq(2 * BRANCH_W)]
                  + [layer_block(s) for s in stacked[4:]],
        out_shape=[jax.ShapeDtypeStruct((N_PROMPT, 2 * BRANCH_W), F32)]
                  + [jax.ShapeDtypeStruct(s, F32) for s in stacked[:4]]
                  + [jax.ShapeDtypeStruct((N_PROMPT, 2 * BRANCH_W), F32)]
                  + [jax.ShapeDtypeStruct(s, F32) for s in stacked[4:]],
        scratch_shapes=[pltpu.VMEM((SEQ, 2 * BRANCH_W), F32)] + _STATE_SCRATCH,
        input_output_aliases=aliases,
        compiler_params=_params(("arbitrary",)),
        name=f"ctx_mixers{l}",
    )(zac, *gains, c_lambda, zbd, *small, *(carried or ()))
    return out[0], out[5], out[1:5] + out[6:]


def _attn_sample_body(*refs, lam_init):
    for _ in _attn_sample_kernel(*refs, lam_init=lam_init):
        pass


def _attn_sample(l, zac, gains, c_lambda, caches, tables, lam_init):
    first_blk = N_PROMPT // DEC_SEQ
    cache_specs = [pl.BlockSpec((None, None, w, PAST_LEN), lambda b, q: (b, l, 0, 0)) for w in (128, 128, 256, 256)]
    table_specs = [pl.BlockSpec((DEC_SEQ, 256), lambda b, q: (0, 0)) for _ in range(6)]
    return pl.pallas_call(
        functools.partial(_attn_sample_body, lam_init=lam_init),
        grid=(DEC_BATCH, DEC_SEQ // TQ),
        in_specs=[pl.BlockSpec((DEC_SEQ, AC_W), lambda b, q: (first_blk + b, 0))] + _gain_specs(l, 2)
                 + cache_specs + table_specs,
        out_specs=pl.BlockSpec((TQ, 2 * BRANCH_W), lambda b, q: (b * (DEC_SEQ // TQ) + q, 0)),
        out_shape=jax.ShapeDtypeStruct((N_SAMPLE, 2 * BRANCH_W), F32),
        scratch_shapes=[pltpu.VMEM((DEC_SEQ, 128), BF), pltpu.VMEM((DEC_SEQ, 128), BF),
                        pltpu.VMEM((DEC_SEQ, 256), BF), pltpu.VMEM((DEC_SEQ, 256), BF)],
        compiler_params=_params(("arbitrary", "arbitrary")),
        name=f"attn_sample{l}",
    )(zac, *gains, c_lambda, *caches, *tables)


def _gla_sample(l, zbd, small, st_h, st_d):
    first_blk = N_PROMPT // DEC_SEQ
    return pl.pallas_call(
        functools.partial(_gla_kernel, layer=l, n_tiles=DEC_SEQ // TT, has_state=True),
        grid=(DEC_BATCH,),
        in_specs=[pl.BlockSpec((DEC_SEQ, BD_PAD), lambda b: (first_blk + b, 0))] + _gla_common_specs(l, 1) + [
            pl.BlockSpec((None, None, 2, B_HEADS, B_KEY_DIM, B_VAL_DIM), lambda b: (b, l, 0, 0, 0, 0)),
            pl.BlockSpec((None, None, 2, D_HEADS, D_KEY_DIM, D_VAL_DIM), lambda b: (b, l, 0, 0, 0, 0)),
        ],
        out_specs=pl.BlockSpec((DEC_SEQ, 2 * BRANCH_W), lambda b: (b, 0)),
        out_shape=jax.ShapeDtypeStruct((N_SAMPLE, 2 * BRANCH_W), F32),
        scratch_shapes=[pltpu.VMEM((DEC_SEQ, 2 * BRANCH_W), F32)] + _STATE_SCRATCH,
        compiler_params=_params(("arbitrary",)),
        name=f"gla_sample{l}",
    )(zbd, *small, st_h, st_d)


def _mix_kernel(*refs, n_x):
    x_refs = refs[:n_x]
    (h_ref, yp_ref, ys_ref, gp_ref, gs_ref, mod_ref, win_ref, wb_ref, wo_ref, o_ref, wm_s, wb_s, wo_s) = refs[n_x:]
    i = pl.program_id(0)

    @pl.when(i < NPRO)
    def _():
        rows = _stage_rows(i, W_ROWS)
        wm_s[_stage_rows(i, MERGE_ROWS), :] = win_ref[...].astype(BF)
        wb_s[rows, :] = wb_ref[...].astype(BF)
        wo_s[rows, :] = wo_ref[...].astype(BF)

    @pl.when(i >= NPRO)
    def _():
        t = i - NPRO
        hb = h_ref[...]
        yac = _tok_load(t, (yp_ref, ys_ref))
        ybd = _tok_load(t, (gp_ref, gs_ref))
        branches = (yac[:, :BRANCH_W], ybd[:, :BRANCH_W], yac[:, BRANCH_W:], ybd[:, BRANCH_W:])
        mixed = None
        for n, y in enumerate(branches):
            logits = _dot_nt(hb, wm_s[n * D_MODEL:(n + 1) * D_MODEL, :])
            term = jax.nn.sigmoid(logits) * _dot(y.astype(BF), wb_s[n * BRANCH_W:(n + 1) * BRANCH_W, :])
            mixed = term if mixed is None else mixed + term
        o_ref[...] = _tok_load(t, x_refs) + mod_ref[2:3, :] * _dot(mixed.astype(BF), wo_s[...])


def _mix(l, xs, h, y_p, y_s, g_p, g_s, mod, w_in_t, w_branch, w_out):
    chunk = pl.BlockSpec((None, W_ROWS, D_MODEL), lambda i: (l, _chunk_of_step(i), 0))
    return pl.pallas_call(
        functools.partial(_mix_kernel, n_x=len(xs)),
        grid=(NPRO + N_TILES,),
        in_specs=_tok_specs(D_MODEL, len(xs) == 2) + _tok_specs(D_MODEL, False)
                 + _tok_specs(2 * BRANCH_W, True) + _tok_specs(2 * BRANCH_W, True)
                 + [_mod_spec(l),
                    pl.BlockSpec((None, MERGE_ROWS, D_MODEL), lambda i: (l, MERGE_OFF // MERGE_ROWS + _chunk_of_step(i), 0)),
                    chunk, chunk],
        out_specs=pl.BlockSpec((TD, D_MODEL), lambda i: (_tile_of_step(i), 0)),
        out_shape=jax.ShapeDtypeStruct((N_TOK, D_MODEL), F32),
        scratch_shapes=[pltpu.VMEM((NPRO * MERGE_ROWS, D_MODEL), BF), pltpu.VMEM((D_MODEL, D_MODEL), BF),
                        pltpu.VMEM((D_MODEL, D_MODEL), BF)],
        compiler_params=_params(("arbitrary",)),
        name=f"mix{l}",
    )(*xs, h, y_p, y_s, g_p, g_s, mod, w_in_t, w_branch.reshape(DEPTH, N_BRANCH * BRANCH_W, D_MODEL), w_out)


def _ffn_kernel(*refs, n_out):
    (xp_ref, x_ref, xn_ref, mod_ref, g_ref, wup_ref, cw_ref, cb_ref, wdn_ref) = refs[:9]
    o_refs = refs[9:9 + n_out]
    hext, wup_s, wdn_s = refs[9 + n_out:12 + n_out]
    i = pl.program_id(0)

    @pl.when(i < NPRO)
    def _():
        wup_s[_stage_rows(i, W_ROWS), :] = wup_ref[...].astype(BF)
        wdn_s[_stage_rows(i, D_FF // NPRO), :] = wdn_ref[...].astype(BF)

    if n_out == 2:
        keep_s = refs[12 + n_out]

        @pl.when(i == 0)
        def _():
            keep_s[...] = jnp.zeros(keep_s.shape, F32)

    @pl.when(i >= NPRO)
    def _():
        t = i - NPRO
        ctx = t < PROMPT_TILES
        pos = jnp.maximum(t - PROMPT_TILES, 0) % SAMPLE_TILES_PER_SEQ
        seq_first = ctx | (pos == 0)
        seq_last = ctx | (pos == SAMPLE_TILES_PER_SEQ - 1)
        m = mod_ref[...]
        g = g_ref[...]

        def pre(x):
            return _rms(x, g) * (1.0 + m[4:5]) + m[3:4]

        sub = lax.broadcasted_iota(jnp.int32, (HALO, D_MODEL), 0)
        hext[0:TD, :] = pre(x_ref[...])
        hext[TD:TD + HALO, :] = jnp.where(seq_last | (sub != 0), 0.0, pre(xn_ref[...]))
        hext[TD + HALO:, :] = jnp.where(seq_first | (sub != HALO - 1), 0.0, pre(xp_ref[...]))
        hb = hext[...].astype(BF)

        n_tile, mid = TD // 8, SEQ // 8
        sub8 = lax.broadcasted_iota(jnp.int32, (1, 8, FF_CHUNK), 1)

        def up_conv_act(c):
            halves = []
            for off in (0, D_FF):
                cols = slice(off + c * FF_CHUNK, off + (c + 1) * FF_CHUNK)
                u = _dot(hb, wup_s[:, cols]).reshape(FF_EXT // 8, 8, FF_CHUNK)
                down = pltpu.roll(u, 1, axis=1)
                up = pltpu.roll(u, 7, axis=1)
                before = jnp.concatenate([down[-1:], down[:mid - 1], jnp.where(ctx, 0.0, down[mid - 1:mid]),
                                          down[mid:n_tile - 1]], axis=0)
                after = jnp.concatenate([up[1:mid], jnp.where(ctx, 0.0, up[mid:mid + 1]), up[mid + 1:n_tile + 1]],
                                        axis=0)
                prev = jnp.where(sub8 == 0, before, down[:n_tile])
                nxt = jnp.where(sub8 == 7, after, up[:n_tile])
                conv = (prev * cw_ref[0:1, cols] + u[:n_tile] * cw_ref[1:2, cols] + nxt * cw_ref[2:3, cols]
                        + cb_ref[:, cols])
                halves.append(conv.reshape(TD, FF_CHUNK))
            return (_silu(halves[1]) * halves[0]).astype(BF)

        n_chunks = D_FF // FF_CHUNK
        acc = _dot(jnp.concatenate([up_conv_act(c) for c in range(n_chunks)], axis=1), wdn_s[...])
        gate = m[5:6]
        res = x_ref[...] + gate * acc
        if n_out == 1:
            o_refs[0][...] = res
        else:
            kept = jnp.where(ctx, res, keep_s[...])
            keep_s[...] = kept
            o_refs[0][...] = kept
            o_refs[1][...] = res


def _ffn(l, x, mod, norm_g, w_up, conv_w, conv_b, w_down, split_out):
    per = TD // HALO
    last_blk = N_TOK // HALO - 1
    if split_out:
        out_specs = [pl.BlockSpec((TD, D_MODEL), lambda i: (jnp.minimum(_tile_of_step(i), PROMPT_TILES - 1), 0)),
                     pl.BlockSpec((TD, D_MODEL), lambda i: (jnp.maximum(_tile_of_step(i) - PROMPT_TILES, 0), 0))]
        out_shape = [jax.ShapeDtypeStruct((N_PROMPT, D_MODEL), F32), jax.ShapeDtypeStruct((N_SAMPLE, D_MODEL), F32)]
    else:
        out_specs = [pl.BlockSpec((TD, D_MODEL), lambda i: (_tile_of_step(i), 0))]
        out_shape = [jax.ShapeDtypeStruct((N_TOK, D_MODEL), F32)]
    return pl.pallas_call(
        functools.partial(_ffn_kernel, n_out=len(out_specs)),
        grid=(NPRO + N_TILES,),
        in_specs=[
            pl.BlockSpec((HALO, D_MODEL), lambda i: (jnp.maximum(_tile_of_step(i) * per - 1, 0), 0)),
            pl.BlockSpec((TD, D_MODEL), lambda i: (_tile_of_step(i), 0)),
            pl.BlockSpec((HALO, D_MODEL), lambda i: (jnp.minimum((_tile_of_step(i) + 1) * per, last_blk), 0)),
            _mod_spec(l),
            pl.BlockSpec((None, 1, D_MODEL), lambda i: (l, 0, 0)),
            pl.BlockSpec((None, W_ROWS, 2 * D_FF), lambda i: (l, _chunk_of_step(i), 0)),
            pl.BlockSpec((None, CONV_WIDTH, 2 * D_FF), lambda i: (l, 0, 0)),
            pl.BlockSpec((None, 1, 2 * D_FF), lambda i: (l, 0, 0)),
            pl.BlockSpec((None, D_FF // NPRO, D_MODEL), lambda i: (l, _chunk_of_step(i), 0)),
        ],
        out_specs=out_specs,
        out_shape=out_shape,
        scratch_shapes=[pltpu.VMEM((FF_EXT, D_MODEL), F32),
                        pltpu.VMEM((D_MODEL, 2 * D_FF), BF), pltpu.VMEM((D_FF, D_MODEL), BF)]
                       + ([pltpu.VMEM((TD, D_MODEL), F32)] if split_out else []),
        compiler_params=_params(("arbitrary",)),
        name=f"ffn{l}",
    )(x, x, x, mod, norm_g, w_up, conv_w, conv_b, w_down)


def kernel(x_prompt, x_sample, cache_a_k, cache_a_v, cache_c_k, cache_c_v, state_hgrn, state_gla, c, c_ctx, w_ada, b_ada, norm1_g, norm2_g, w_in, a_qn_g, a_kn_g, c_qn_g, c_kn_g, c_lambda, c_subln_g, b_lb_logits, b_norm_g, d_alpha_w, d_alpha_b, d_norm_g, w_branch, w_out, w_up, conv_w, conv_b, w_down):
    xs = (x_prompt.reshape(N_PROMPT, D_MODEL), x_sample.reshape(N_SAMPLE, D_MODEL))
    w_in_t = jnp.swapaxes(w_in, 1, 2)

    cond8 = jnp.concatenate([c_ctx[None, :], c, jnp.zeros((8 - N_GROUPS, D_MODEL), F32)], axis=0)
    mod = _modulation(cond8, w_ada, b_ada)[:, :N_GROUPS].reshape(DEPTH, N_GROUPS, N_MOD, D_MODEL)

    tile4 = lambda g, n: jnp.tile(g, (1, n)).reshape(DEPTH, 1, -1)
    gains = (tile4(a_qn_g, A_HEADS), tile4(a_kn_g, A_KV_HEADS), tile4(c_qn_g, 2 * C_HEADS), tile4(c_kn_g, 2 * C_HEADS),
             c_subln_g.reshape(DEPTH, 1, 2 * C_HEAD_DIM))
    fmaj = lambda t: jnp.moveaxis(t, 2, -1).reshape(DEC_BATCH, DEPTH, -1, PAST_LEN)
    caches = tuple(fmaj(t) for t in (cache_a_k, cache_a_v, cache_c_k, cache_c_v))
    tables = _rope_tables(DEC_SEQ, A_HEAD_DIM, A_HEADS) + _rope_tables(DEC_SEQ, C_HEAD_DIM, 2 * C_HEADS)

    aw = jnp.zeros((DEPTH, 2, 128, 128), F32)
    aw = aw.at[:, 0, 0:D_GATE_RANK].set(d_alpha_w[:, 0]).at[:, 1, D_GATE_RANK:2 * D_GATE_RANK].set(d_alpha_w[:, 1])
    gla_small = (b_lb_logits, aw.astype(BF), d_alpha_b.reshape(DEPTH, 2, 1, 128),
                 tile4(b_norm_g, B_HEADS), tile4(d_norm_g, D_HEADS)) + _gla_constants() + (
                     _state_pattern(_B_BLOCKS), _state_pattern(_D_BLOCKS))

    n1 = norm1_g.reshape(DEPTH, 1, D_MODEL)
    n2 = norm2_g.reshape(DEPTH, 1, D_MODEL)
    cb = conv_b.reshape(DEPTH, 1, 2 * D_FF)

    carried = None
    for l in range(DEPTH):
        lam_init = 0.8 - 0.6 * math.exp(-0.3 * l)
        zac, zbd, h = _inproj(l, xs, mod, n1, w_in_t)
        y_p, g_p, carried = _ctx_mixers(l, zac, zbd, gains, c_lambda, gla_small, lam_init, carried)
        y_s = _attn_sample(l, zac, gains, c_lambda, caches, tables, lam_init)
        g_s = _gla_sample(l, zbd, gla_small, state_hgrn, state_gla)
        x1 = _mix(l, xs, h, y_p, y_s, g_p, g_s, mod, w_in_t, w_branch, w_out)
        xs = tuple(_ffn(l, x1, mod, n2, w_up, conv_w, cb, w_down, split_out=(l == DEPTH - 1)))

    y_prompt = xs[0].reshape(BATCH, SEQ, D_MODEL)
    y_sample = xs[1].reshape(DEC_BATCH, DEC_SEQ, D_MODEL)
    stacked = list(carried)
    feat_shapes = ((A_KV_HEADS, A_HEAD_DIM), (A_KV_HEADS, A_HEAD_DIM), (C_HEADS, 2, C_HEAD_DIM), (C_HEADS, 2 * C_HEAD_DIM))
    ctx = [jnp.moveaxis(t.reshape((BATCH, DEPTH) + fs + (SEQ,)), -1, 2) for t, fs in zip(stacked[:4], feat_shapes)]
    return (y_prompt, y_sample) + tuple(ctx) + tuple(stacked[4:])
```
